```python
import jax
import jax.numpy as jnp
from jax import lax
import numpy as np

D_MODEL = 1024
BATCH = 16
SEQ = 256
DEPTH = 2
DEC_BATCH = 4
DEC_SEQ = 1024
PAST_LEN = 256

GRID_W = 64
N_HEADS = 8
HEAD_K = D_MODEL // N_HEADS
HEAD_V = D_MODEL // N_HEADS
HGRN_K = N_HEADS * HEAD_K
HGRN_V = N_HEADS * HEAD_V
CHUNK = 16
CONV_DIM = D_MODEL // 2
CONV_WIDTH = 31
N_EXPERTS = 64
N_GROUPS = 8
TOPK_GROUPS = 4
TOP_K = 8
EXPERT_FF = 256
SHARED_FF = 256
ROUTED_SCALE = 2.5
N_MOD = 6
EPS = 1e-6
IN_COLS = 3 * HGRN_K + 2 * HGRN_V + 2 * CONV_DIM + 2 * D_MODEL

kernel_name = 'hybrid_hgrn2_conformer_moe_prefix_diffusion'


def _rmsnorm(x, g):
    xf = x.astype(jnp.float32)
    return xf * lax.rsqrt(jnp.mean(xf * xf, axis=-1, keepdims=True) + EPS) * g.astype(jnp.float32)


def _hgrn_direction(q, k, v, log_f, s0):
    bsz, length, heads, _ = q.shape
    dv = v.shape[-1]
    n = length // CHUNK

    def to_chunks(t):
        return t.reshape(bsz, n, CHUNK, heads, t.shape[-1]).transpose(0, 1, 3, 2, 4)

    q, k, v, log_f = to_chunks(q), to_chunks(k), to_chunks(v), to_chunks(log_f)
    b = jnp.cumsum(log_f, axis=3)
    mask = jnp.tril(jnp.ones((CHUNK, CHUNK), dtype=bool))[:, :, None]
    diff = b[:, :, :, :, None, :] - b[:, :, :, None, :, :]
    decay = jnp.exp(jnp.where(mask, diff, -jnp.inf))
    scores = jnp.einsum('bnhtd,bnhtsd,bnhsd->bnhts', q, decay, k)
    o_intra = jnp.einsum('bnhts,bnhse->bnhte', scores, v)
    b_last = b[:, :, :, -1, :]
    kv = jnp.einsum('bnhsd,bnhse->bnhde', k * jnp.exp(b_last[:, :, :, None, :] - b), v)

    def step(s, inp):
        dec, kv_c = inp
        return dec[..., None] * s + kv_c, s

    s_final, s_start = lax.scan(step, s0, (jnp.moveaxis(jnp.exp(b_last), 1, 0), jnp.moveaxis(kv, 1, 0)))
    s_start = jnp.moveaxis(s_start, 0, 1)
    o_inter = jnp.einsum('bnhtd,bnhde->bnhte', q * jnp.exp(b), s_start)
    o = (o_intra + o_inter).transpose(0, 1, 3, 2, 4).reshape(bsz, length, heads, dv)
    return o, s_final


def _forget(z, lb):
    lb = lb.astype(jnp.float32).reshape(N_HEADS, HEAD_K)
    log_f = jnp.logaddexp(jnp.log(lb), jnp.log1p(-lb) + jax.nn.log_sigmoid(z))
    k = (1.0 - lb) * jax.nn.sigmoid(-z)
    return log_f, k


def _mixers(h, s0_f, s0_b, lb_f, lb_b, w_in, hgrn_norm_g, w_proj_hgrn, conv_dw_w, conv_dw_b,
            conv_norm_g, conv_norm_b, w_proj_conv, w_out):
    bsz, length, _ = h.shape
    z = jnp.einsum('bld,dc->blc', h, w_in).astype(jnp.float32)
    splits = [HGRN_K, HGRN_K + HGRN_V, 2 * HGRN_K + HGRN_V, 3 * HGRN_K + HGRN_V,
              3 * HGRN_K + 2 * HGRN_V, 3 * HGRN_K + 2 * HGRN_V + 2 * CONV_DIM]
    q, v, zf, zb, og, u, gz = jnp.split(z, splits, axis=-1)

    def heads(t):
        return t.reshape(bsz, length, N_HEADS, t.shape[-1] // N_HEADS)

    def flip(t):
        return jnp.flip(t, axis=1)

    q, v = heads(q), heads(v)
    log_ff, kf = _forget(heads(zf), lb_f)
    log_fb, kb = _forget(heads(zb), lb_b)
    o_f, s_f = _hgrn_direction(q, kf, v, log_ff, s0_f)
    o_b, s_b = _hgrn_direction(flip(q), flip(kb), flip(v), flip(log_fb), s0_b)
    o = o_f + flip(o_b)
    o = o * lax.rsqrt(jnp.mean(o * o, axis=-1, keepdims=True) + EPS)
    o = o.reshape(bsz, length, HGRN_V) * hgrn_norm_g * jax.nn.silu(og)
    y_a = o @ w_proj_hgrn

    u = u[..., :CONV_DIM] * jax.nn.sigmoid(u[..., CONV_DIM:])
    u = lax.conv_general_dilated(u, conv_dw_w.astype(u.dtype)[:, None, :], window_strides=(1,),
                                 padding=[(CONV_WIDTH // 2, CONV_WIDTH // 2)],
                                 dimension_numbers=('NWC', 'WIO', 'NWC'),
                                 feature_group_count=CONV_DIM) + conv_dw_b
    mu = jnp.mean(u, axis=-1, keepdims=True)
    var = jnp.mean(jnp.square(u - mu), axis=-1, keepdims=True)
    u = (u - mu) * lax.rsqrt(var + EPS) * conv_norm_g + conv_norm_b
    y_b = jax.nn.silu(u) @ w_proj_conv

    g_a, g_b = jnp.split(jax.nn.sigmoid(gz), 2, axis=-1)
    out = (g_a * y_a + g_b * y_b) @ w_out
    return out.astype(jnp.float32), s_f, s_b


def _moe(h, w_router, router_bias, w_expert_gate_up, w_expert_down, w_shared_gate_up, w_shared_down):
    bsz, length, d = h.shape
    t = h.reshape(bsz * length, d)
    scores = jax.nn.sigmoid((t @ w_router).astype(jnp.float32))
    sel = scores + router_bias.astype(jnp.float32)
    per_group = N_EXPERTS // N_GROUPS
    grp = jnp.sum(lax.top_k(sel.reshape(-1, N_GROUPS, per_group), 2)[0], axis=-1)
    _, gi = lax.top_k(grp, TOPK_GROUPS)
    gmask = jnp.any(gi[..., None] == jnp.arange(N_GROUPS), axis=-2)
    sel = jnp.where(jnp.repeat(gmask, per_group, axis=-1), sel, -jnp.inf)
    _, ei = lax.top_k(sel, TOP_K)
    w = jnp.take_along_axis(scores, ei, axis=-1)
    w = w / jnp.sum(w, axis=-1, keepdims=True) * ROUTED_SCALE
    combine = jnp.sum(jax.nn.one_hot(ei, N_EXPERTS, dtype=jnp.float32) * w[..., None], axis=1)
    gu = jnp.einsum('td,edf->etf', t, w_expert_gate_up)
    g, u = jnp.split(gu, 2, axis=-1)
    act = jax.nn.silu(g) * u * combine.T[:, :, None]
    routed = jnp.einsum('etf,efd->td', act, w_expert_down)
    sg, su = jnp.split(t @ w_shared_gate_up, 2, axis=-1)
    shared = (jax.nn.silu(sg) * su) @ w_shared_down
    return (routed + shared).astype(jnp.float32).reshape(bsz, length, d)


def _layer(x, mod, s0_f, s0_b, lb_f, lb_b, p):
    (norm1_g, w_in, hgrn_norm_g, w_proj_hgrn, conv_dw_w, conv_dw_b, conv_norm_g, conv_norm_b,
     w_proj_conv, w_out, norm2_g, w_router, router_bias, w_expert_gate_up, w_expert_down,
     w_shared_gate_up, w_shared_down) = p
    shift1, scale1, gate1, shift2, scale2, gate2 = jnp.split(mod.astype(jnp.float32)[:, None, :], N_MOD, axis=-1)
    h = _rmsnorm(x, norm1_g) * (1.0 + scale1) + shift1
    mix, s_f, s_b = _mixers(h, s0_f, s0_b, lb_f, lb_b, w_in, hgrn_norm_g, w_proj_hgrn, conv_dw_w,
                            conv_dw_b, conv_norm_g, conv_norm_b, w_proj_conv, w_out)
    xf = x.astype(jnp.float32) + gate1 * mix
    h = _rmsnorm(xf, norm2_g) * (1.0 + scale2) + shift2
    xf = xf + gate2 * _moe(h, w_router, router_bias, w_expert_gate_up, w_expert_down,
                           w_shared_gate_up, w_shared_down)
    return xf.astype(x.dtype), s_f, s_b


def setup_inputs(seed: int = 0) -> dict:
    key = jax.random.key(seed)
    ks = jax.random.split(key, 32)

    def nrm(k, shape, scale):
        return jax.random.normal(k, shape, jnp.float32) * scale

    d = D_MODEL
    return {
        'x_prompt': nrm(ks[0], (BATCH, SEQ, d), 1.0),
        'x_sample': nrm(ks[1], (DEC_BATCH, DEC_SEQ, d), 1.0),
        'state_hgrn': nrm(ks[2], (DEC_BATCH, DEPTH, 2, N_HEADS, HEAD_K, HEAD_V), 0.3),
        'c': nrm(ks[3], (DEC_BATCH, d), 1.0),
        'c_ctx': nrm(ks[4], (d,), 1.0),
        'w_ada': nrm(ks[5], (DEPTH, d, N_MOD * d), 0.5 * d ** -0.5),
        'b_ada': nrm(ks[6], (DEPTH, N_MOD * d), 0.02),
        'norm1_g': 1.0 + nrm(ks[7], (DEPTH, d), 0.02),
        'w_in': nrm(ks[8], (DEPTH, d, IN_COLS), d ** -0.5),
        'hgrn_lb_logits': nrm(ks[9], (DEPTH, 2, HGRN_K), 0.5),
        'hgrn_norm_g': 1.0 + nrm(ks[10], (DEPTH, HGRN_V), 0.02),
        'w_proj_hgrn': nrm(ks[11], (DEPTH, HGRN_V, d), HGRN_V ** -0.5),
        'conv_dw_w': nrm(ks[12], (DEPTH, CONV_WIDTH, CONV_DIM), CONV_WIDTH ** -0.5),
        'conv_dw_b': nrm(ks[13], (DEPTH, CONV_DIM), 0.02),
        'conv_norm_g': 1.0 + nrm(ks[14], (DEPTH, CONV_DIM), 0.02),
        'conv_norm_b': nrm(ks[15], (DEPTH, CONV_DIM), 0.02),
        'w_proj_conv': nrm(ks[16], (DEPTH, CONV_DIM, d), CONV_DIM ** -0.5),
        'w_out': nrm(ks[17], (DEPTH, d, d), d ** -0.5),
        'norm2_g': 1.0 + nrm(ks[18], (DEPTH, d), 0.02),
        'w_router': nrm(ks[19], (DEPTH, d, N_EXPERTS), d ** -0.5),
        'router_bias': nrm(ks[20], (DEPTH, N_EXPERTS), 0.01),
        'w_expert_gate_up': nrm(ks[21], (DEPTH, N_EXPERTS, d, 2 * EXPERT_FF), d ** -0.5),
        'w_expert_down': nrm(ks[22], (DEPTH, N_EXPERTS, EXPERT_FF, d), EXPERT_FF ** -0.5),
        'w_shared_gate_up': nrm(ks[23], (DEPTH, d, 2 * SHARED_FF), d ** -0.5),
        'w_shared_down': nrm(ks[24], (DEPTH, SHARED_FF, d), SHARED_FF ** -0.5),
        'final_norm_g': 1.0 + nrm(ks[25], (d,), 0.02),
    }


def reference(x_prompt, x_sample, state_hgrn, c, c_ctx, w_ada, b_ada, norm1_g, w_in, hgrn_lb_logits,
              hgrn_norm_g, w_proj_hgrn, conv_dw_w, conv_dw_b, conv_norm_g, conv_norm_b, w_proj_conv,
              w_out, norm2_g, w_router, router_bias, w_expert_gate_up, w_expert_down,
              w_shared_gate_up, w_shared_down, final_norm_g):
    n_lat = x_sample.shape[1]
    rows = n_lat // GRID_W
    if rows * GRID_W != n_lat:
        raise ValueError('latent token count must fill whole grid rows')

    lb = jnp.cumsum(jax.nn.softmax(hgrn_lb_logits.astype(jnp.float32), axis=0), axis=0)
    lb = lb - lb[:1]

    layers = [(norm1_g[l], w_in[l], hgrn_norm_g[l], w_proj_hgrn[l], conv_dw_w[l], conv_dw_b[l],
               conv_norm_g[l], conv_norm_b[l], w_proj_conv[l], w_out[l], norm2_g[l], w_router[l],
               router_bias[l], w_expert_gate_up[l], w_expert_down[l], w_shared_gate_up[l],
               w_shared_down[l]) for l in range(DEPTH)]

    xp = x_prompt
    zeros = jnp.zeros((x_prompt.shape[0], N_HEADS, HEAD_K, HEAD_V), jnp.float32)
    ctx_states = []
    for l in range(DEPTH):
        mod = (jax.nn.silu(c_ctx.astype(jnp.float32)) @ w_ada[l] + b_ada[l])[None, :]
        xp, s_f, s_b = _layer(xp, mod, zeros, zeros, lb[l, 0], lb[l, 1], layers[l])
        ctx_states.append(jnp.stack([s_f, s_b], axis=1))
    new_state_hgrn = jnp.stack(ctx_states, axis=1).astype(x_prompt.dtype)

    xs = x_sample
    for l in range(DEPTH):
        mod = jax.nn.silu(c.astype(jnp.float32)) @ w_ada[l] + b_ada[l]
        xs, _, _ = _layer(xs, mod, state_hgrn[:, l, 0].astype(jnp.float32),
                          state_hgrn[:, l, 1].astype(jnp.float32), lb[l, 0], lb[l, 1], layers[l])

    y_prompt = _rmsnorm(xp, final_norm_g).astype(x_prompt.dtype)
    y_sample = _rmsnorm(xs, final_norm_g).astype(x_sample.dtype)
    return (y_prompt, y_sample, new_state_hgrn)
```

```python
import functools

import numpy as np
import jax
import jax.numpy as jnp
from jax import lax
from jax.experimental import pallas as pl
from jax.experimental.pallas import tpu as pltpu

F32 = jnp.float32
BF16 = jnp.bfloat16
HIGHEST = lax.Precision.HIGHEST

D_MODEL = 1024
N_HEADS = 8
HEAD_DIM = 128
CONV_DIM = 512
CONV_WIDTH = 31
CONV_HALO = 16
N_EXPERTS = 64
N_GROUPS = 8
GROUP_SIZE = N_EXPERTS // N_GROUPS
TOPK_GROUPS = 4
TOP_K = 8
EXPERT_FF = 256
ROUTED_SCALE = 2.5
N_MOD = 6
EPS = 1e-6

CHUNK = 128
N_LEVELS = 7
STEP = 1024
POST_TILE = 256
MOE_TILE = 2048
ROW_TILE = 256
HEAD_COLS = 5 * HEAD_DIM
VMEM_LIMIT = 52 * 1024 * 1024


def _dot(a, b):
    return jnp.dot(a, b, preferred_element_type=F32)


def _dot_nt(a, b):
    return lax.dot_general(a, b, (((1,), (1,)), ((), ())), preferred_element_type=F32)


def _dot_tn(a, b):
    return lax.dot_general(a, b, (((0,), (0,)), ((), ())), preferred_element_type=F32)


def _sigmoid(x):
    return 1.0 / (1.0 + jnp.exp(-x))


def _silu(x):
    return x * _sigmoid(x)


def _rms_mod(x, g, scale, shift):
    ms = jnp.mean(x * x, axis=-1, keepdims=True)
    return x * lax.rsqrt(ms + EPS) * g * (1.0 + scale) + shift


def _ada_kernel(c_ref, w_ref, b_ref, o_ref):
    cc = c_ref[...]
    o_ref[0] = jnp.dot(_silu(cc), w_ref[0], preferred_element_type=F32, precision=HIGHEST) + b_ref[0]


def _ada(cc, w_ada, b_ada):
    depth, d, n = w_ada.shape
    tn = 1536
    return pl.pallas_call(
        _ada_kernel,
        grid=(depth, n // tn),
        in_specs=[
            pl.BlockSpec((8, d), lambda l, j: (0, 0)),
            pl.BlockSpec((1, d, tn), lambda l, j: (l, 0, j)),
            pl.BlockSpec((1, 1, tn), lambda l, j: (l, 0, j)),
        ],
        out_specs=pl.BlockSpec((1, 8, tn), lambda l, j: (l, 0, j)),
        out_shape=jax.ShapeDtypeStruct((depth, 8, n), F32),
        compiler_params=pltpu.CompilerParams(vmem_limit_bytes=VMEM_LIMIT),
        name="ada_mod",
    )(cc, w_ada, b_ada.reshape(depth, 1, n))


def _forget_gate(z, log_lb, log1m_lb, one_m_lb):
    e = jnp.exp(-jnp.abs(z))
    r = 1.0 / (1.0 + e)
    log_sig = jnp.minimum(z, 0.0) - jnp.log(1.0 + e)
    k = one_m_lb * jnp.where(z > 0, e * r, r)
    b = log1m_lb + log_sig
    log_f = jnp.maximum(log_lb, b) + jnp.log(1.0 + jnp.exp(-jnp.abs(log_lb - b)))
    return log_f, k


def _level_reference(cum_ref, blk, fwd):
    half = blk // 2
    pieces = []
    if blk >= 8:
        for i in range(CHUNK // blk):
            row = i * blk + (half - 1 if fwd else half)
            pieces.append(jnp.broadcast_to(cum_ref[row:row + 1, :], (blk, HEAD_DIM)))
    else:
        sub = lax.broadcasted_iota(jnp.int32, (8, HEAD_DIM), 0)
        for i in range(CHUNK // 8):
            lo_row = 8 * i + (1 if fwd else 2)
            hi_row = 8 * i + (5 if fwd else 6)
            lo = jnp.broadcast_to(cum_ref[lo_row:lo_row + 1, :], (8, HEAD_DIM))
            hi = jnp.broadcast_to(cum_ref[hi_row:hi_row + 1, :], (8, HEAD_DIM))
            pieces.append(jnp.where(sub >= 4, hi, lo))
    return pieces[0] if len(pieces) == 1 else jnp.concatenate(pieces, axis=0)


def _chunk_step(q, k, v, log_f, st, lvl, cum_ref, fwd):
    r_idx = lax.broadcasted_iota(jnp.int32, (CHUNK, CHUNK), 0)
    c_idx = lax.broadcasted_iota(jnp.int32, (CHUNK, CHUNK), 1)
    tri = jnp.where((r_idx >= c_idx) if fwd else (r_idx <= c_idx), 1.0, 0.0).astype(F32)
    cum = jnp.dot(tri, log_f, preferred_element_type=F32, precision=HIGHEST)
    cum_ref[...] = cum
    row = lax.broadcasted_iota(jnp.int32, (CHUNK, HEAD_DIM), 0)

    scores = jnp.where(lvl == 0, _dot_nt(q.astype(BF16), k.astype(BF16)), 0.0)
    for lev in range(1, N_LEVELS + 1):
        blk = 1 << lev
        half = blk >> 1
        is_query = ((row & half) != 0) if fwd else ((row & half) == 0)
        if lev == 1:
            eq = jnp.where(is_query, jnp.exp(log_f), 0.0)
            ek = jnp.where(is_query, 0.0, 1.0)
        else:
            d = cum - _level_reference(cum_ref, blk, fwd)
            e = jnp.exp(jnp.where(is_query, d, -d))
            eq = jnp.where(is_query, e, 0.0)
            ek = jnp.where(is_query, 0.0, e)
        s = _dot_nt((q * eq).astype(BF16), (k * ek).astype(BF16))
        scores = jnp.where(lvl == lev, s, scores)

    total = cum_ref[CHUNK - 1:CHUNK, :] if fwd else cum_ref[0:1, :]
    st_b = st.astype(BF16)
    o = _dot(scores.astype(BF16), v.astype(BF16)) + _dot_nt((q * jnp.exp(cum)).astype(BF16), st_b)
    k_st = k * jnp.exp(total - cum)
    st_new = st * jnp.exp(total) + _dot_tn(v.astype(BF16), k_st.astype(BF16))
    return o, st_new


def _scan_kernel(nc_ref, x_ref, mod_ref, g1_ref, w_ref, gp_ref, s0_ref, lvl_ref,
                 o_out, st_out, h_ref, z_ref, o_ref, cum_ref):
    step = pl.program_id(0)
    head = pl.program_id(1)
    n_seq_chunks = nc_ref[step]
    n_chunks = STEP // CHUNK

    @pl.when(head == 0)
    def _():
        g1 = g1_ref[...]
        shift = mod_ref[0, 0:1, :]
        scale = mod_ref[0, 1:2, :]

        def body(i, carry):
            rows = pl.ds(pl.multiple_of(i * CHUNK, CHUNK), CHUNK)
            h_ref[rows, :] = _rms_mod(x_ref[rows, :], g1, scale, shift).astype(BF16)
            return carry

        lax.fori_loop(0, n_chunks, body, 0)

    z_ref[...] = _dot(h_ref[...], w_ref[0])
    st_out[...] = jnp.zeros(st_out.shape, F32)
    lvl = lvl_ref[...]

    def direction(fwd):
        d = 0 if fwd else 1
        zcol = (2 if fwd else 3) * HEAD_DIM
        log_lb = gp_ref[0, 3 * d + 0:3 * d + 1, :]
        log1m_lb = gp_ref[0, 3 * d + 1:3 * d + 2, :]
        one_m_lb = gp_ref[0, 3 * d + 2:3 * d + 3, :]
        norm_g = gp_ref[0, 6:7, :]
        st0 = s0_ref[0, d, 0].T

        def body(i, st):
            c = i if fwd else n_chunks - 1 - i
            pos = lax.rem(c, n_seq_chunks)
            first = (pos == 0) if fwd else (pos == n_seq_chunks - 1)
            last = (pos == n_seq_chunks - 1) if fwd else (pos == 0)
            rows = pl.ds(pl.multiple_of(c * CHUNK, CHUNK), CHUNK)
            q = z_ref[rows, 0:HEAD_DIM]
            v = z_ref[rows, HEAD_DIM:2 * HEAD_DIM]
            log_f, k = _forget_gate(z_ref[rows, zcol:zcol + HEAD_DIM], log_lb, log1m_lb, one_m_lb)
            st = jnp.where(first, st0, st)
            o, st = _chunk_step(q, k, v, log_f, st, lvl, cum_ref, fwd)

            @pl.when(last)
            def _():
                st_out[0, lax.div(c, n_seq_chunks), d, 0] = st.T

            if fwd:
                o_ref[rows, :] = o
            else:
                o = o + o_ref[rows, :]
                o = o * lax.rsqrt(jnp.mean(o * o, axis=-1, keepdims=True) + EPS)
                og = z_ref[rows, 4 * HEAD_DIM:5 * HEAD_DIM]
                o_out[rows, :] = (o * norm_g * _silu(og)).astype(BF16)
            return st

        lax.fori_loop(0, n_chunks, body, st0)

    direction(True)
    direction(False)


def _scan(nc, x_all, mod_step, g1, w_hg, gp, s0, lvl):
    n_tok = x_all.shape[0]
    n_steps = n_tok // STEP
    grid_spec = pltpu.PrefetchScalarGridSpec(
        num_scalar_prefetch=1,
        grid=(n_steps, N_HEADS),
        in_specs=[
            pl.BlockSpec((STEP, D_MODEL), lambda s, h, nc: (s, 0)),
            pl.BlockSpec((1, 8, D_MODEL), lambda s, h, nc: (s, 0, 0)),
            pl.BlockSpec((1, D_MODEL), lambda s, h, nc: (0, 0)),
            pl.BlockSpec((1, D_MODEL, HEAD_COLS), lambda s, h, nc: (h, 0, 0)),
            pl.BlockSpec((1, 8, HEAD_DIM), lambda s, h, nc: (h, 0, 0)),
            pl.BlockSpec((1, 2, 1, HEAD_DIM, HEAD_DIM), lambda s, h, nc: (s, 0, h, 0, 0)),
            pl.BlockSpec((CHUNK, CHUNK), lambda s, h, nc: (0, 0)),
        ],
        out_specs=[
            pl.BlockSpec((STEP, HEAD_DIM), lambda s, h, nc: (s, h)),
            pl.BlockSpec((1, 4, 2, 1, HEAD_DIM, HEAD_DIM), lambda s, h, nc: (s, 0, 0, h, 0, 0)),
        ],
        scratch_shapes=[
            pltpu.VMEM((STEP, D_MODEL), BF16),
            pltpu.VMEM((STEP, HEAD_COLS), F32),
            pltpu.VMEM((STEP, HEAD_DIM), F32),
            pltpu.VMEM((CHUNK, HEAD_DIM), F32),
        ],
    )
    return pl.pallas_call(
        _scan_kernel,
        grid_spec=grid_spec,
        out_shape=[
            jax.ShapeDtypeStruct((n_tok, D_MODEL), BF16),
            jax.ShapeDtypeStruct((n_steps, 4, 2, N_HEADS, HEAD_DIM, HEAD_DIM), F32),
        ],
        compiler_params=pltpu.CompilerParams(
            dimension_semantics=("arbitrary", "arbitrary"), vmem_limit_bytes=VMEM_LIMIT),
        name="hgrn_scan",
    )(nc, x_all, mod_step, g1, w_hg, gp, s0, lvl)


def _post_kernel(pv_ref, nv_ref, x_ref, xp_ref, xn_ref, mod_ref, o_ref, wu_ref, wgz_ref, wph_ref,
                 wpc_ref, wo_ref, wr_ref, g1_ref, cw_ref, cb_ref, lg_ref, lb_ref, g2_ref,
                 x1_out, h2_out, lg_out, cu_ref):
    i = pl.program_id(0)
    g1 = g1_ref[...]
    shift1 = mod_ref[0, 0:1, :]
    scale1 = mod_ref[0, 1:2, :]
    gate1 = mod_ref[0, 2:3, :]
    shift2 = mod_ref[0, 3:4, :]
    scale2 = mod_ref[0, 4:5, :]

    def glu(xv):
        h = _rms_mod(xv, g1, scale1, shift1).astype(BF16)
        u = _dot(h, wu_ref[...])
        return h, u[:, :CONV_DIM] * _sigmoid(u[:, CONV_DIM:])

    x = x_ref[...]
    h, glu_mid = glu(x)
    _, glu_prev = glu(xp_ref[...])
    _, glu_next = glu(xn_ref[...])
    cu_ref[0:CONV_HALO, :] = glu_prev * pv_ref[i].astype(F32)
    cu_ref[CONV_HALO:CONV_HALO + POST_TILE, :] = glu_mid
    cu_ref[CONV_HALO + POST_TILE:, :] = glu_next * nv_ref[i].astype(F32)

    off = CONV_HALO - CONV_WIDTH // 2
    acc = jnp.zeros((POST_TILE, CONV_DIM), F32) + cb_ref[...]
    for j in range(CONV_WIDTH):
        acc = acc + cu_ref[off + j:off + j + POST_TILE, :] * cw_ref[j:j + 1, :]
    mu = jnp.mean(acc, axis=-1, keepdims=True)
    cen = acc - mu
    var = jnp.mean(cen * cen, axis=-1, keepdims=True)
    cv = cen * lax.rsqrt(var + EPS) * lg_ref[...] + lb_ref[...]
    y_b = _dot(_silu(cv).astype(BF16), wpc_ref[...])

    y_a = _dot(o_ref[...], wph_ref[...])
    gz = _sigmoid(_dot(h, wgz_ref[...]))
    merged = gz[:, :D_MODEL] * y_a + gz[:, D_MODEL:] * y_b
    x1 = x + gate1 * _dot(merged.astype(BF16), wo_ref[...])
    x1_out[...] = x1
    h2 = _rms_mod(x1, g2_ref[...], scale2, shift2)
    h2_out[...] = h2.astype(BF16)
    lg_out[...] = lax.dot_general(wr_ref[...], h2, (((1,), (1,)), ((), ())),
                                  preferred_element_type=F32, precision=HIGHEST)


def _post(pv, nv, x_all, mod_tile, o_all, w_u, w_gz, w_ph, w_pc, w_o, w_rt, g1, cw, cb, lg, lb, g2):
    n_tok = x_all.shape[0]
    n_tiles = n_tok // POST_TILE
    halo_per_tile = POST_TILE // CONV_HALO
    n_halo_blocks = n_tok // CONV_HALO

    def full(a):
        return pl.BlockSpec(a.shape, lambda i, pv, nv: (0,) * a.ndim)

    grid_spec = pltpu.PrefetchScalarGridSpec(
        num_scalar_prefetch=2,
        grid=(n_tiles,),
        in_specs=[
            pl.BlockSpec((POST_TILE, D_MODEL), lambda i, pv, nv: (i, 0)),
            pl.BlockSpec((CONV_HALO, D_MODEL), lambda i, pv, nv: (jnp.maximum(i * halo_per_tile - 1, 0), 0)),
            pl.BlockSpec((CONV_HALO, D_MODEL),
                         lambda i, pv, nv: (jnp.minimum((i + 1) * halo_per_tile, n_halo_blocks - 1), 0)),
            pl.BlockSpec((1, 8, D_MODEL), lambda i, pv, nv: (i, 0, 0)),
            pl.BlockSpec((POST_TILE, D_MODEL), lambda i, pv, nv: (i, 0)),
            full(w_u), full(w_gz), full(w_ph), full(w_pc), full(w_o), full(w_rt),
            full(g1), full(cw), full(cb), full(lg), full(lb), full(g2),
        ],
        out_specs=[
            pl.BlockSpec((POST_TILE, D_MODEL), lambda i, pv, nv: (i, 0)),
            pl.BlockSpec((POST_TILE, D_MODEL), lambda i, pv, nv: (i, 0)),
            pl.BlockSpec((N_EXPERTS, POST_TILE), lambda i, pv, nv: (0, i)),
        ],
        scratch_shapes=[pltpu.VMEM((POST_TILE + 2 * CONV_HALO, CONV_DIM), F32)],
    )
    return pl.pallas_call(
        _post_kernel,
        grid_spec=grid_spec,
        out_shape=[
            jax.ShapeDtypeStruct((n_tok, D_MODEL), F32),
            jax.ShapeDtypeStruct((n_tok, D_MODEL), BF16),
            jax.ShapeDtypeStruct((N_EXPERTS, n_tok), F32),
        ],
        compiler_params=pltpu.CompilerParams(
            dimension_semantics=("arbitrary",), vmem_limit_bytes=VMEM_LIMIT),
        name="post_mixer",
    )(pv, nv, x_all, x_all, x_all, mod_tile, o_all, w_u, w_gz, w_ph, w_pc, w_o, w_rt, g1, cw, cb, lg, lb, g2)


def _route_kernel(lg_ref, bias_ref, comb_out):
    n = lg_ref.shape[1]
    scores = _sigmoid(lg_ref[...])
    sel = scores + bias_ref[...]
    neg = jnp.float32(-jnp.inf)

    sel3 = sel.reshape(N_GROUPS, GROUP_SIZE, n)
    m1 = jnp.max(sel3, axis=1, keepdims=True)
    is_m1 = sel3 == m1
    n_m1 = jnp.sum(is_m1.astype(F32), axis=1, keepdims=True)
    m2 = jnp.max(jnp.where(is_m1, neg, sel3), axis=1, keepdims=True)
    grp = (m1 + jnp.where(n_m1 > 1.5, m1, m2)).reshape(N_GROUPS, n)

    gidx = lax.broadcasted_iota(jnp.int32, (N_GROUPS, n), 0)
    rank = jnp.zeros((N_GROUPS, n), F32)
    for g in range(N_GROUPS):
        other = grp[g:g + 1, :]
        ahead = (other > grp) | ((other == grp) & (g < gidx))
        rank = rank + ahead.astype(F32)
    keep_g = rank < TOPK_GROUPS - 0.5
    keep = jnp.broadcast_to(keep_g.reshape(N_GROUPS, 1, n), (N_GROUPS, GROUP_SIZE, n)).reshape(N_EXPERTS, n)
    cand = jnp.where(keep, sel, neg)

    eidx = lax.broadcasted_iota(jnp.int32, (N_EXPERTS, n), 0)
    chosen = jnp.zeros((N_EXPERTS, n), F32)
    for _ in range(TOP_K):
        best = jnp.max(cand, axis=0, keepdims=True)
        first = jnp.min(jnp.where(cand == best, eidx, N_EXPERTS), axis=0, keepdims=True)
        hit = eidx == first
        chosen = jnp.where(hit, 1.0, chosen)
        cand = jnp.where(hit, neg, cand)
    w = scores * chosen
    comb_out[...] = w / jnp.sum(w, axis=0, keepdims=True) * ROUTED_SCALE


def _route(logits_t, bias):
    n_tok = logits_t.shape[1]
    tile = 512
    return pl.pallas_call(
        _route_kernel,
        grid=(n_tok // tile,),
        in_specs=[
            pl.BlockSpec((N_EXPERTS, tile), lambda i: (0, i)),
            pl.BlockSpec((N_EXPERTS, 1), lambda i: (0, 0)),
        ],
        out_specs=pl.BlockSpec((N_EXPERTS, tile), lambda i: (0, i)),
        out_shape=jax.ShapeDtypeStruct((N_EXPERTS, n_tok), F32),
        name="router",
    )(logits_t, bias)


def _moe_kernel(h_ref, wgu_ref, wd_ref, comb_ref, wsgu_ref, wsd_ref, acc_out):
    e = pl.program_id(1)
    n_rows = MOE_TILE // ROW_TILE

    @pl.when(e == 0)
    def _():
        def body(r, carry):
            rows = pl.ds(pl.multiple_of(r * ROW_TILE, ROW_TILE), ROW_TILE)
            gu = _dot(h_ref[rows, :], wsgu_ref[...])
            act = _silu(gu[:, :EXPERT_FF]) * gu[:, EXPERT_FF:]
            acc_out[rows, :] = _dot(act.astype(BF16), wsd_ref[...])
            return carry

        lax.fori_loop(0, n_rows, body, 0)

    def body(r, carry):
        rows = pl.ds(pl.multiple_of(r * ROW_TILE, ROW_TILE), ROW_TILE)
        gu = _dot(h_ref[rows, :], wgu_ref[0])
        act = _silu(gu[:, :EXPERT_FF]) * gu[:, EXPERT_FF:] * comb_ref[0, rows, :]
        acc_out[rows, :] += _dot(act.astype(BF16), wd_ref[0])
        return carry

    lax.fori_loop(0, n_rows, body, 0)


def _moe(h2, w_gu, w_d, comb3, ws_gu, ws_d):
    n_tok = h2.shape[0]
    return pl.pallas_call(
        _moe_kernel,
        grid=(n_tok // MOE_TILE, N_EXPERTS),
        in_specs=[
            pl.BlockSpec((MOE_TILE, D_MODEL), lambda i, e: (i, 0)),
            pl.BlockSpec((1, D_MODEL, 2 * EXPERT_FF), lambda i, e: (e, 0, 0)),
            pl.BlockSpec((1, EXPERT_FF, D_MODEL), lambda i, e: (e, 0, 0)),
            pl.BlockSpec((1, MOE_TILE, 1), lambda i, e: (e, i, 0)),
            pl.BlockSpec(ws_gu.shape, lambda i, e: (0, 0)),
            pl.BlockSpec(ws_d.shape, lambda i, e: (0, 0)),
        ],
        out_specs=pl.BlockSpec((MOE_TILE, D_MODEL), lambda i, e: (i, 0)),
        out_shape=jax.ShapeDtypeStruct((n_tok, D_MODEL), F32),
        compiler_params=pltpu.CompilerParams(
            dimension_semantics=("arbitrary", "arbitrary"), vmem_limit_bytes=VMEM_LIMIT),
        name="moe_dense",
    )(h2, w_gu, w_d, comb3, ws_gu, ws_d)


def _residual_kernel(x_ref, m_ref, mod_ref, fg_ref, o_ref, *, final):
    x2 = x_ref[...] + mod_ref[0, 5:6, :] * m_ref[...]
    if final:
        ms = jnp.mean(x2 * x2, axis=-1, keepdims=True)
        x2 = x2 * lax.rsqrt(ms + EPS) * fg_ref[...]
    o_ref[...] = x2


def _residual(x1, moe, mod_tile, final_g, final):
    n_tok = x1.shape[0]
    return pl.pallas_call(
        functools.partial(_residual_kernel, final=final),
        grid=(n_tok // POST_TILE,),
        in_specs=[
            pl.BlockSpec((POST_TILE, D_MODEL), lambda i: (i, 0)),
            pl.BlockSpec((POST_TILE, D_MODEL), lambda i: (i, 0)),
            pl.BlockSpec((1, 8, D_MODEL), lambda i: (i, 0, 0)),
            pl.BlockSpec((1, D_MODEL), lambda i: (0, 0)),
        ],
        out_specs=pl.BlockSpec((POST_TILE, D_MODEL), lambda i: (i, 0)),
        out_shape=jax.ShapeDtypeStruct((n_tok, D_MODEL), F32),
        name="residual",
    )(x1, moe, mod_tile, final_g)


def _level_ids():
    t = np.arange(CHUNK)
    x = t[:, None] ^ t[None, :]
    lvl = np.zeros((CHUNK, CHUNK), np.int32)
    nz = x > 0
    lvl[nz] = np.floor(np.log2(x[nz])).astype(np.int32) + 1
    return jnp.asarray(lvl)


def kernel(x_prompt, x_sample, state_hgrn, c, c_ctx, w_ada, b_ada, norm1_g, w_in, hgrn_lb_logits, hgrn_norm_g, w_proj_hgrn, conv_dw_w, conv_dw_b, conv_norm_g, conv_norm_b, w_proj_conv, w_out, norm2_g, w_router, router_bias, w_expert_gate_up, w_expert_down, w_shared_gate_up, w_shared_down, final_norm_g):
    n_ctx, ctx_len, d = x_prompt.shape
    n_lat, lat_len, _ = x_sample.shape
    depth = w_ada.shape[0]
    assert d == D_MODEL and ctx_len * 4 == STEP and lat_len == STEP and n_ctx % 4 == 0
    ctx_steps = n_ctx * ctx_len // STEP
    n_steps = ctx_steps + n_lat
    n_tok = n_steps * STEP
    tiles_per_step = STEP // POST_TILE

    x_all = jnp.concatenate([x_prompt.reshape(-1, d), x_sample.reshape(-1, d)], axis=0)

    cc = jnp.zeros((8, d), F32).at[:n_lat].set(c.astype(F32)).at[n_lat].set(c_ctx.astype(F32))
    mod = _ada(cc, w_ada, b_ada).reshape(depth, 8, N_MOD, d)
    step_src = np.array([n_lat] * ctx_steps + list(range(n_lat)))
    mod_step = jnp.pad(mod[:, step_src], ((0, 0), (0, 0), (0, 8 - N_MOD), (0, 0)))
    mod_tile = jnp.repeat(mod_step, tiles_per_step, axis=1)

    nc = jnp.asarray([ctx_len // CHUNK] * ctx_steps + [lat_len // CHUNK] * n_lat, jnp.int32)
    tile_pos = np.arange(n_lat * tiles_per_step) % tiles_per_step
    pv = jnp.asarray(np.concatenate([np.zeros(ctx_steps * tiles_per_step), tile_pos > 0]), jnp.int32)
    nv = jnp.asarray(np.concatenate([np.zeros(ctx_steps * tiles_per_step), tile_pos < tiles_per_step - 1]), jnp.int32)

    lbv = jnp.cumsum(jax.nn.softmax(hgrn_lb_logits.astype(F32), axis=0), axis=0)
    lbv = (lbv - lbv[:1]).reshape(depth, 2, N_HEADS, HEAD_DIM)
    lvl = _level_ids()

    xs = x_all
    ctx_states = []
    for l in range(depth):
        wl = w_in[l]
        hk = N_HEADS * HEAD_DIM
        parts = [wl[:, j * hk:(j + 1) * hk].reshape(d, N_HEADS, HEAD_DIM) for j in range(5)]
        w_hg = jnp.concatenate(parts, axis=-1).transpose(1, 0, 2).astype(BF16)
        w_u = wl[:, 5 * hk:5 * hk + 2 * CONV_DIM].astype(BF16)
        w_gz = wl[:, 5 * hk + 2 * CONV_DIM:].astype(BF16)
        lb = lbv[l]
        gp = jnp.stack([jnp.log(lb[0]), jnp.log1p(-lb[0]), 1.0 - lb[0],
                        jnp.log(lb[1]), jnp.log1p(-lb[1]), 1.0 - lb[1],
                        hgrn_norm_g[l].reshape(N_HEADS, HEAD_DIM).astype(F32),
                        jnp.zeros((N_HEADS, HEAD_DIM), F32)], axis=1)
        s0 = jnp.concatenate([jnp.zeros((ctx_steps, 2, N_HEADS, HEAD_DIM, HEAD_DIM), F32),
                              state_hgrn[:, l].astype(F32)], axis=0)
        g1 = norm1_g[l].reshape(1, d).astype(F32)

        o_all, states = _scan(nc, xs, mod_step[l], g1, w_hg, gp, s0, lvl)
        ctx_states.append(states[:ctx_steps].reshape(n_ctx, 2, N_HEADS, HEAD_DIM, HEAD_DIM))

        cw = jnp.pad(conv_dw_w[l].astype(F32), ((0, 1), (0, 0)))
        x1, h2, logits_t = _post(
            pv, nv, xs, mod_tile[l], o_all, w_u, w_gz, w_proj_hgrn[l].astype(BF16),
            w_proj_conv[l].astype(BF16), w_out[l].astype(BF16), w_router[l].T.astype(F32), g1, cw,
            conv_dw_b[l].reshape(1, -1).astype(F32), conv_norm_g[l].reshape(1, -1).astype(F32),
            conv_norm_b[l].reshape(1, -1).astype(F32), norm2_g[l].reshape(1, d).astype(F32))

        comb_t = _route(logits_t, router_bias[l].reshape(N_EXPERTS, 1).astype(F32))
        moe = _moe(h2, w_expert_gate_up[l].astype(BF16), w_expert_down[l].astype(BF16),
                   comb_t.reshape(N_EXPERTS, n_tok, 1),
                   w_shared_gate_up[l].astype(BF16), w_shared_down[l].astype(BF16))
        xs = _residual(x1, moe, mod_tile[l], final_norm_g.reshape(1, d).astype(F32), final=(l == depth - 1))

    n_ctx_tok = n_ctx * ctx_len
    y_prompt = xs[:n_ctx_tok].reshape(x_prompt.shape).astype(x_prompt.dtype)
    y_sample = xs[n_ctx_tok:].reshape(x_sample.shape).astype(x_sample.dtype)
    new_state = jnp.stack(ctx_states, axis=1).astype(x_prompt.dtype)
    return (y_prompt, y_sample, new_state)
```

```python
import functools

import numpy as np
import jax
import jax.numpy as jnp
from jax import lax
from jax.experimental import pallas as pl
from jax.experimental.pallas import tpu as pltpu

F32 = jnp.float32
BF16 = jnp.bfloat16
HIGHEST = lax.Precision.HIGHEST

D_MODEL = 1024
N_HEADS = 8
HEAD_DIM = 128
CONV_DIM = 512
CONV_WIDTH = 31
CONV_HALO = 16
N_EXPERTS = 64
N_GROUPS = 8
GROUP_SIZE = N_EXPERTS // N_GROUPS
TOPK_GROUPS = 4
TOP_K = 8
EXPERT_FF = 256
ROUTED_SCALE = 2.5
N_MOD = 6
EPS = 1e-6

CHUNK = 128
N_LEVELS = 7
STEP = 1024
POST_TILE = 256
MOE_BLOCK = 256
UNIT = 16
FFN_TILE = 256
UNITS_PER_TILE = FFN_TILE // UNIT
SLOT_MAX = MOE_BLOCK * TOP_K + N_EXPERTS * UNIT
UNITS_PER_BLOCK = SLOT_MAX // UNIT
GROUP_UNITS = 32
GROUP_SLOTS = GROUP_UNITS * UNIT
GROUPS_PER_BLOCK = UNITS_PER_BLOCK // GROUP_UNITS
N_TOKENS = 8192
N_TILES_MAX = (N_TOKENS * TOP_K // UNIT + (N_TOKENS // MOE_BLOCK) * N_EXPERTS) // UNITS_PER_TILE + N_EXPERTS
HEAD_COLS = 5 * HEAD_DIM
VMEM_LIMIT = 52 * 1024 * 1024


def _dot(a, b):
    return jnp.dot(a, b, preferred_element_type=F32)


def _dot_nt(a, b):
    return lax.dot_general(a, b, (((1,), (1,)), ((), ())), preferred_element_type=F32)


def _dot_tn(a, b):
    return lax.dot_general(a, b, (((0,), (0,)), ((), ())), preferred_element_type=F32)


def _sigmoid(x):
    return 1.0 / (1.0 + jnp.exp(-x))


def _silu(x):
    return x * _sigmoid(x)


def _rms_mod(x, g, scale, shift):
    ms = jnp.mean(x * x, axis=-1, keepdims=True)
    return x * lax.rsqrt(ms + EPS) * g * (1.0 + scale) + shift


def _ada_kernel(c_ref, w_ref, b_ref, o_ref):
    cc = c_ref[...]
    o_ref[0] = jnp.dot(_silu(cc), w_ref[0], preferred_element_type=F32, precision=HIGHEST) + b_ref[0]


def _ada(cc, w_ada, b_ada):
    depth, d, n = w_ada.shape
    tn = 1536
    return pl.pallas_call(
        _ada_kernel,
        grid=(depth, n // tn),
        in_specs=[
            pl.BlockSpec((8, d), lambda l, j: (0, 0)),
            pl.BlockSpec((1, d, tn), lambda l, j: (l, 0, j)),
            pl.BlockSpec((1, 1, tn), lambda l, j: (l, 0, j)),
        ],
        out_specs=pl.BlockSpec((1, 8, tn), lambda l, j: (l, 0, j)),
        out_shape=jax.ShapeDtypeStruct((depth, 8, n), F32),
        compiler_params=pltpu.CompilerParams(vmem_limit_bytes=VMEM_LIMIT),
        name="ada_mod",
    )(cc, w_ada, b_ada.reshape(depth, 1, n))


def _forget_gate(z, log_lb, log1m_lb, one_m_lb):
    e = jnp.exp(-jnp.abs(z))
    r = 1.0 / (1.0 + e)
    log_sig = jnp.minimum(z, 0.0) - jnp.log(1.0 + e)
    k = one_m_lb * jnp.where(z > 0, e * r, r)
    b = log1m_lb + log_sig
    log_f = jnp.maximum(log_lb, b) + jnp.log(1.0 + jnp.exp(-jnp.abs(log_lb - b)))
    return log_f, k


def _level_reference(cum_ref, blk, fwd):
    half = blk // 2
    pieces = []
    if blk >= 8:
        for i in range(CHUNK // blk):
            row = i * blk + (half - 1 if fwd else half)
            pieces.append(jnp.broadcast_to(cum_ref[row:row + 1, :], (blk, HEAD_DIM)))
    else:
        sub = lax.broadcasted_iota(jnp.int32, (8, HEAD_DIM), 0)
        for i in range(CHUNK // 8):
            lo_row = 8 * i + (1 if fwd else 2)
            hi_row = 8 * i + (5 if fwd else 6)
            lo = jnp.broadcast_to(cum_ref[lo_row:lo_row + 1, :], (8, HEAD_DIM))
            hi = jnp.broadcast_to(cum_ref[hi_row:hi_row + 1, :], (8, HEAD_DIM))
            pieces.append(jnp.where(sub >= 4, hi, lo))
    return pieces[0] if len(pieces) == 1 else jnp.concatenate(pieces, axis=0)


def _chunk_step(q, k, v, log_f, st, lvl, cum_ref, fwd):
    r_idx = lax.broadcasted_iota(jnp.int32, (CHUNK, CHUNK), 0)
    c_idx = lax.broadcasted_iota(jnp.int32, (CHUNK, CHUNK), 1)
    tri = jnp.where((r_idx >= c_idx) if fwd else (r_idx <= c_idx), 1.0, 0.0).astype(F32)
    cum = jnp.dot(tri, log_f, preferred_element_type=F32, precision=HIGHEST)
    cum_ref[...] = cum
    row = lax.broadcasted_iota(jnp.int32, (CHUNK, HEAD_DIM), 0)

    scores = jnp.where(lvl == 0, _dot_nt(q.astype(BF16), k.astype(BF16)), 0.0)
    for lev in range(1, N_LEVELS + 1):
        blk = 1 << lev
        half = blk >> 1
        is_query = ((row & half) != 0) if fwd else ((row & half) == 0)
        if lev == 1:
            eq = jnp.where(is_query, jnp.exp(log_f), 0.0)
            ek = jnp.where(is_query, 0.0, 1.0)
        else:
            d = cum - _level_reference(cum_ref, blk, fwd)
            e = jnp.exp(jnp.where(is_query, d, -d))
            eq = jnp.where(is_query, e, 0.0)
            ek = jnp.where(is_query, 0.0, e)
        s = _dot_nt((q * eq).astype(BF16), (k * ek).astype(BF16))
        scores = jnp.where(lvl == lev, s, scores)

    total = cum_ref[CHUNK - 1:CHUNK, :] if fwd else cum_ref[0:1, :]
    st_b = st.astype(BF16)
    o = _dot(scores.astype(BF16), v.astype(BF16)) + _dot_nt((q * jnp.exp(cum)).astype(BF16), st_b)
    k_st = k * jnp.exp(total - cum)
    st_new = st * jnp.exp(total) + _dot_tn(v.astype(BF16), k_st.astype(BF16))
    return o, st_new


def _scan_kernel(nc_ref, x_ref, mod_ref, g1_ref, w_ref, gp_ref, s0_ref, lvl_ref,
                 o_out, st_out, h_ref, z_ref, o_ref, cum_ref):
    step = pl.program_id(0)
    head = pl.program_id(1)
    n_seq_chunks = nc_ref[step]
    n_chunks = STEP // CHUNK

    @pl.when(head == 0)
    def _():
        g1 = g1_ref[...]
        shift = mod_ref[0, 0:1, :]
        scale = mod_ref[0, 1:2, :]

        def body(i, carry):
            rows = pl.ds(pl.multiple_of(i * CHUNK, CHUNK), CHUNK)
            h_ref[rows, :] = _rms_mod(x_ref[rows, :], g1, scale, shift).astype(BF16)
            return carry

        lax.fori_loop(0, n_chunks, body, 0)

    z_ref[...] = _dot(h_ref[...], w_ref[0])
    st_out[...] = jnp.zeros(st_out.shape, F32)
    lvl = lvl_ref[...]

    def direction(fwd):
        d = 0 if fwd else 1
        zcol = (2 if fwd else 3) * HEAD_DIM
        log_lb = gp_ref[0, 3 * d + 0:3 * d + 1, :]
        log1m_lb = gp_ref[0, 3 * d + 1:3 * d + 2, :]
        one_m_lb = gp_ref[0, 3 * d + 2:3 * d + 3, :]
        norm_g = gp_ref[0, 6:7, :]
        st0 = s0_ref[0, d, 0].T

        def body(i, st):
            c = i if fwd else n_chunks - 1 - i
            pos = lax.rem(c, n_seq_chunks)
            first = (pos == 0) if fwd else (pos == n_seq_chunks - 1)
            last = (pos == n_seq_chunks - 1) if fwd else (pos == 0)
            rows = pl.ds(pl.multiple_of(c * CHUNK, CHUNK), CHUNK)
            q = z_ref[rows, 0:HEAD_DIM]
            v = z_ref[rows, HEAD_DIM:2 * HEAD_DIM]
            log_f, k = _forget_gate(z_ref[rows, zcol:zcol + HEAD_DIM], log_lb, log1m_lb, one_m_lb)
            st = jnp.where(first, st0, st)
            o, st = _chunk_step(q, k, v, log_f, st, lvl, cum_ref, fwd)

            @pl.when(last)
            def _():
                st_out[0, lax.div(c, n_seq_chunks), d, 0] = st.T

            if fwd:
                o_ref[rows, :] = o
            else:
                o = o + o_ref[rows, :]
                o = o * lax.rsqrt(jnp.mean(o * o, axis=-1, keepdims=True) + EPS)
                og = z_ref[rows, 4 * HEAD_DIM:5 * HEAD_DIM]
                o_out[rows, :] = (o * norm_g * _silu(og)).astype(BF16)
            return st

        lax.fori_loop(0, n_chunks, body, st0)

    direction(True)
    direction(False)


def _scan(nc, x_all, mod_step, g1, w_hg, gp, s0, lvl):
    n_tok = x_all.shape[0]
    n_steps = n_tok // STEP
    grid_spec = pltpu.PrefetchScalarGridSpec(
        num_scalar_prefetch=1,
        grid=(n_steps, N_HEADS),
        in_specs=[
            pl.BlockSpec((STEP, D_MODEL), lambda s, h, nc: (s, 0)),
            pl.BlockSpec((1, 8, D_MODEL), lambda s, h, nc: (s, 0, 0)),
            pl.BlockSpec((1, D_MODEL), lambda s, h, nc: (0, 0)),
            pl.BlockSpec((1, D_MODEL, HEAD_COLS), lambda s, h, nc: (h, 0, 0)),
            pl.BlockSpec((1, 8, HEAD_DIM), lambda s, h, nc: (h, 0, 0)),
            pl.BlockSpec((1, 2, 1, HEAD_DIM, HEAD_DIM), lambda s, h, nc: (s, 0, h, 0, 0)),
            pl.BlockSpec((CHUNK, CHUNK), lambda s, h, nc: (0, 0)),
        ],
        out_specs=[
            pl.BlockSpec((STEP, HEAD_DIM), lambda s, h, nc: (s, h)),
            pl.BlockSpec((1, 4, 2, 1, HEAD_DIM, HEAD_DIM), lambda s, h, nc: (s, 0, 0, h, 0, 0)),
        ],
        scratch_shapes=[
            pltpu.VMEM((STEP, D_MODEL), BF16),
            pltpu.VMEM((STEP, HEAD_COLS), F32),
            pltpu.VMEM((STEP, HEAD_DIM), F32),
            pltpu.VMEM((CHUNK, HEAD_DIM), F32),
        ],
    )
    return pl.pallas_call(
        _scan_kernel,
        grid_spec=grid_spec,
        out_shape=[
            jax.ShapeDtypeStruct((n_tok, D_MODEL), BF16),
            jax.ShapeDtypeStruct((n_steps, 4, 2, N_HEADS, HEAD_DIM, HEAD_DIM), F32),
        ],
        compiler_params=pltpu.CompilerParams(
            dimension_semantics=("arbitrary", "arbitrary"), vmem_limit_bytes=VMEM_LIMIT),
        name="hgrn_scan",
    )(nc, x_all, mod_step, g1, w_hg, gp, s0, lvl)


def _post_kernel(pv_ref, nv_ref, x_ref, xp_ref, xn_ref, mod_ref, o_ref, wu_ref, wgz_ref, wph_ref,
                 wpc_ref, wo_ref, wr_ref, g1_ref, cw_ref, cb_ref, lg_ref, lb_ref, g2_ref,
                 x1_out, h2_out, lg_out, cu_ref):
    i = pl.program_id(0)
    g1 = g1_ref[...]
    shift1 = mod_ref[0, 0:1, :]
    scale1 = mod_ref[0, 1:2, :]
    gate1 = mod_ref[0, 2:3, :]
    shift2 = mod_ref[0, 3:4, :]
    scale2 = mod_ref[0, 4:5, :]

    def glu(xv):
        h = _rms_mod(xv, g1, scale1, shift1).astype(BF16)
        u = _dot(h, wu_ref[...])
        return h, u[:, :CONV_DIM] * _sigmoid(u[:, CONV_DIM:])

    x = x_ref[...]
    h, glu_mid = glu(x)
    _, glu_prev = glu(xp_ref[...])
    _, glu_next = glu(xn_ref[...])
    cu_ref[0:CONV_HALO, :] = glu_prev * pv_ref[i].astype(F32)
    cu_ref[CONV_HALO:CONV_HALO + POST_TILE, :] = glu_mid
    cu_ref[CONV_HALO + POST_TILE:, :] = glu_next * nv_ref[i].astype(F32)

    off = CONV_HALO - CONV_WIDTH // 2
    acc = jnp.zeros((POST_TILE, CONV_DIM), F32) + cb_ref[...]
    for j in range(CONV_WIDTH):
        acc = acc + cu_ref[off + j:off + j + POST_TILE, :] * cw_ref[j:j + 1, :]
    mu = jnp.mean(acc, axis=-1, keepdims=True)
    cen = acc - mu
    var = jnp.mean(cen * cen, axis=-1, keepdims=True)
    cv = cen * lax.rsqrt(var + EPS) * lg_ref[...] + lb_ref[...]
    y_b = _dot(_silu(cv).astype(BF16), wpc_ref[...])

    y_a = _dot(o_ref[...], wph_ref[...])
    gz = _sigmoid(_dot(h, wgz_ref[...]))
    merged = gz[:, :D_MODEL] * y_a + gz[:, D_MODEL:] * y_b
    x1 = x + gate1 * _dot(merged.astype(BF16), wo_ref[...])
    x1_out[...] = x1
    h2 = _rms_mod(x1, g2_ref[...], scale2, shift2)
    h2_out[...] = h2.astype(BF16)
    lg_out[...] = lax.dot_general(wr_ref[...], h2, (((1,), (1,)), ((), ())),
                                  preferred_element_type=F32, precision=HIGHEST)


def _post(pv, nv, x_all, mod_tile, o_all, w_u, w_gz, w_ph, w_pc, w_o, w_rt, g1, cw, cb, lg, lb, g2):
    n_tok = x_all.shape[0]
    n_tiles = n_tok // POST_TILE
    halo_per_tile = POST_TILE // CONV_HALO
    n_halo_blocks = n_tok // CONV_HALO

    def full(a):
        return pl.BlockSpec(a.shape, lambda i, pv, nv: (0,) * a.ndim)

    grid_spec = pltpu.PrefetchScalarGridSpec(
        num_scalar_prefetch=2,
        grid=(n_tiles,),
        in_specs=[
            pl.BlockSpec((POST_TILE, D_MODEL), lambda i, pv, nv: (i, 0)),
            pl.BlockSpec((CONV_HALO, D_MODEL), lambda i, pv, nv: (jnp.maximum(i * halo_per_tile - 1, 0), 0)),
            pl.BlockSpec((CONV_HALO, D_MODEL),
                         lambda i, pv, nv: (jnp.minimum((i + 1) * halo_per_tile, n_halo_blocks - 1), 0)),
            pl.BlockSpec((1, 8, D_MODEL), lambda i, pv, nv: (i, 0, 0)),
            pl.BlockSpec((POST_TILE, D_MODEL), lambda i, pv, nv: (i, 0)),
            full(w_u), full(w_gz), full(w_ph), full(w_pc), full(w_o), full(w_rt),
            full(g1), full(cw), full(cb), full(lg), full(lb), full(g2),
        ],
        out_specs=[
            pl.BlockSpec((POST_TILE, D_MODEL), lambda i, pv, nv: (i, 0)),
            pl.BlockSpec((POST_TILE, D_MODEL), lambda i, pv, nv: (i, 0)),
            pl.BlockSpec((N_EXPERTS, POST_TILE), lambda i, pv, nv: (0, i)),
        ],
        scratch_shapes=[pltpu.VMEM((POST_TILE + 2 * CONV_HALO, CONV_DIM), F32)],
    )
    return pl.pallas_call(
        _post_kernel,
        grid_spec=grid_spec,
        out_shape=[
            jax.ShapeDtypeStruct((n_tok, D_MODEL), F32),
            jax.ShapeDtypeStruct((n_tok, D_MODEL), BF16),
            jax.ShapeDtypeStruct((N_EXPERTS, n_tok), F32),
        ],
        compiler_params=pltpu.CompilerParams(
            dimension_semantics=("arbitrary",), vmem_limit_bytes=VMEM_LIMIT),
        name="post_mixer",
    )(pv, nv, x_all, x_all, x_all, mod_tile, o_all, w_u, w_gz, w_ph, w_pc, w_o, w_rt, g1, cw, cb, lg, lb, g2)


def _route_kernel(lg_ref, bias_ref, comb_out, chosen_out):
    n = lg_ref.shape[1]
    scores = _sigmoid(lg_ref[...])
    sel = scores + bias_ref[...]
    neg = jnp.float32(-jnp.inf)

    sel3 = sel.reshape(N_GROUPS, GROUP_SIZE, n)
    m1 = jnp.max(sel3, axis=1, keepdims=True)
    is_m1 = sel3 == m1
    n_m1 = jnp.sum(is_m1.astype(F32), axis=1, keepdims=True)
    m2 = jnp.max(jnp.where(is_m1, neg, sel3), axis=1, keepdims=True)
    grp = (m1 + jnp.where(n_m1 > 1.5, m1, m2)).reshape(N_GROUPS, n)

    gidx = lax.broadcasted_iota(jnp.int32, (N_GROUPS, n), 0)
    rank = jnp.zeros((N_GROUPS, n), F32)
    for g in range(N_GROUPS):
        other = grp[g:g + 1, :]
        ahead = (other > grp) | ((other == grp) & (g < gidx))
        rank = rank + ahead.astype(F32)
    keep_g = rank < TOPK_GROUPS - 0.5
    keep = jnp.broadcast_to(keep_g.reshape(N_GROUPS, 1, n), (N_GROUPS, GROUP_SIZE, n)).reshape(N_EXPERTS, n)
    cand = jnp.where(keep, sel, neg)

    eidx = lax.broadcasted_iota(jnp.int32, (N_EXPERTS, n), 0)
    chosen = jnp.zeros((N_EXPERTS, n), F32)
    for _ in range(TOP_K):
        best = jnp.max(cand, axis=0, keepdims=True)
        first = jnp.min(jnp.where(cand == best, eidx, N_EXPERTS), axis=0, keepdims=True)
        hit = eidx == first
        chosen = jnp.where(hit, 1.0, chosen)
        cand = jnp.where(hit, neg, cand)
    w = scores * chosen
    comb_out[...] = w / jnp.sum(w, axis=0, keepdims=True) * ROUTED_SCALE
    chosen_out[...] = chosen


def _route(logits_t, bias):
    n_tok = logits_t.shape[1]
    tile = 512
    return pl.pallas_call(
        _route_kernel,
        grid=(n_tok // tile,),
        in_specs=[
            pl.BlockSpec((N_EXPERTS, tile), lambda i: (0, i)),
            pl.BlockSpec((N_EXPERTS, 1), lambda i: (0, 0)),
        ],
        out_specs=[pl.BlockSpec((N_EXPERTS, tile), lambda i: (0, i)),
                   pl.BlockSpec((N_EXPERTS, tile), lambda i: (0, i))],
        out_shape=[jax.ShapeDtypeStruct((N_EXPERTS, n_tok), F32),
                   jax.ShapeDtypeStruct((N_EXPERTS, n_tok), F32)],
        name="router",
    )(logits_t, bias)


def _routing_plan(chosen):
    n_tok = chosen.shape[1]
    nb = n_tok // MOE_BLOCK
    chi = chosen.reshape(N_EXPERTS, nb, MOE_BLOCK).astype(jnp.int32)
    cnt = chi.sum(-1)
    units = (cnt + UNIT - 1) // UNIT
    rank = jnp.where(chi > 0, jnp.cumsum(chi, -1) - chi, -1).astype(BF16)
    rank_b = rank.transpose(1, 0, 2)
    rank_bt = rank.transpose(1, 2, 0)

    ub = units.T
    incl_b = jnp.cumsum(ub, 1)
    start_b = incl_b - ub
    used_units_b = incl_b[:, -1]
    v = jnp.arange(UNITS_PER_BLOCK)
    e_unit = (incl_b[:, None, :] <= v[None, :, None]).sum(-1)
    valid_unit = e_unit < N_EXPERTS
    e_c = jnp.minimum(e_unit, N_EXPERTS - 1)
    start_u = jnp.take_along_axis(start_b, e_c, 1)
    cnt_u = jnp.take_along_axis(cnt.T, e_c, 1)
    r = ((v[None, :] - start_u) * UNIT)[..., None] + jnp.arange(UNIT)
    valid = valid_unit[..., None] & (r < cnt_u[..., None])
    e_slot = jnp.where(valid, e_c[..., None], -1).reshape(nb, SLOT_MAX).astype(jnp.int32)
    r_slot = jnp.where(valid, r, -2).reshape(nb, SLOT_MAX).astype(F32)

    tot = units.sum(1)
    tiles_e = (tot + UNITS_PER_TILE - 1) // UNITS_PER_TILE
    incl_t = jnp.cumsum(tiles_e)
    start_t = incl_t - tiles_e
    n_used = incl_t[-1]
    incl_ub = jnp.cumsum(units, 1)
    cum_b = incl_ub - units
    j = jnp.arange(N_TILES_MAX)
    e_tile = jnp.minimum((incl_t[None, :] <= j[:, None]).sum(-1), N_EXPERTS - 1)
    p = jnp.arange(N_TILES_MAX * UNITS_PER_TILE)
    e_p = jnp.repeat(e_tile, UNITS_PER_TILE)
    q = p - start_t[e_p] * UNITS_PER_TILE
    valid_p = (p // UNITS_PER_TILE < n_used) & (q < tot[e_p])
    b_p = jnp.minimum((incl_ub[e_p] <= q[:, None]).sum(-1), nb - 1)
    src = b_p * UNITS_PER_BLOCK + start_b[b_p, e_p] + (q - cum_b[e_p, b_p])
    src = jnp.where(valid_p, src, 0).astype(jnp.int32)
    pos = start_t[e_c] * UNITS_PER_TILE + jnp.take_along_axis(cum_b.T, e_c, 1) + (v[None, :] - start_u)
    ysrc = jnp.where(valid_unit, pos, 0).reshape(-1).astype(jnp.int32)

    used_tiles_b = ((used_units_b * UNIT + FFN_TILE - 1) // FFN_TILE).astype(jnp.int32)
    used_groups_b = ((used_units_b + GROUP_UNITS - 1) // GROUP_UNITS).astype(jnp.int32)
    return dict(rank_b=rank_b, rank_bt=rank_bt, e_slot=e_slot, r_slot=r_slot, src=src,
                e_tile=e_tile.astype(jnp.int32), n_used=n_used.reshape(1).astype(jnp.int32), ysrc=ysrc,
                used_tiles_b=used_tiles_b, used_groups_b=used_groups_b)


def _permute_kernel(ut_ref, h_ref, rank_ref, es_ref, rs_ref, xp_out):
    n_used = ut_ref[pl.program_id(0)]
    lane_e = lax.broadcasted_iota(jnp.int32, (FFN_TILE, N_EXPERTS), 1)

    def body(i, carry):
        rows = pl.ds(pl.multiple_of(i * FFN_TILE, FFN_TILE), FFN_TILE)

        @pl.when(i < n_used)
        def _():
            onehot_e = jnp.where(lane_e == es_ref[0, rows, :], 1.0, 0.0).astype(BF16)
            slot_rank = _dot(onehot_e, rank_ref[0])
            pick = jnp.where(slot_rank == rs_ref[0, rows, :], 1.0, 0.0).astype(BF16)
            xp_out[0, rows, :] = _dot(pick, h_ref[...]).astype(BF16)

        @pl.when(i >= n_used)
        def _():
            xp_out[0, rows, :] = jnp.zeros((FFN_TILE, D_MODEL), BF16)

        return carry

    lax.fori_loop(0, SLOT_MAX // FFN_TILE, body, 0)


def _permute(plan, h2):
    nb = h2.shape[0] // MOE_BLOCK
    grid_spec = pltpu.PrefetchScalarGridSpec(
        num_scalar_prefetch=1,
        grid=(nb,),
        in_specs=[
            pl.BlockSpec((MOE_BLOCK, D_MODEL), lambda b, ut: (b, 0)),
            pl.BlockSpec((1, N_EXPERTS, MOE_BLOCK), lambda b, ut: (b, 0, 0)),
            pl.BlockSpec((1, SLOT_MAX, 1), lambda b, ut: (b, 0, 0)),
            pl.BlockSpec((1, SLOT_MAX, 1), lambda b, ut: (b, 0, 0)),
        ],
        out_specs=pl.BlockSpec((1, SLOT_MAX, D_MODEL), lambda b, ut: (b, 0, 0)),
    )
    return pl.pallas_call(
        _permute_kernel,
        grid_spec=grid_spec,
        out_shape=jax.ShapeDtypeStruct((nb, SLOT_MAX, D_MODEL), BF16),
        compiler_params=pltpu.CompilerParams(
            dimension_semantics=("arbitrary",), vmem_limit_bytes=VMEM_LIMIT),
        name="moe_permute",
    )(plan["used_tiles_b"], h2, plan["rank_b"], plan["e_slot"][..., None], plan["r_slot"][..., None])


def _ffn_kernel(src_ref, et_ref, nu_ref, *refs):
    x_refs = refs[:UNITS_PER_TILE]
    wgu_ref, wd_ref, y_out, wgu_b, wd_b, xt_ref = refs[UNITS_PER_TILE:]
    j = pl.program_id(0)

    @pl.when(j < nu_ref[0])
    def _():
        @pl.when((j == 0) | (et_ref[j] != et_ref[jnp.maximum(j - 1, 0)]))
        def _():
            wgu_b[...] = wgu_ref[0].astype(BF16)
            wd_b[...] = wd_ref[0].astype(BF16)

        for u in range(UNITS_PER_TILE):
            xt_ref[u * UNIT:(u + 1) * UNIT, :] = x_refs[u][0]
        gu = _dot(xt_ref[...], wgu_b[...])
        act = _silu(gu[:, :EXPERT_FF]) * gu[:, EXPERT_FF:]
        y_out[...] = _dot(act.astype(BF16), wd_b[...]).astype(BF16)

    @pl.when(j >= nu_ref[0])
    def _():
        y_out[...] = jnp.zeros(y_out.shape, BF16)


def _expert_ffn(plan, xp, w_gu, w_d):
    xp_units = xp.reshape(-1, UNIT, D_MODEL)

    def unit_spec(u):
        return pl.BlockSpec((1, UNIT, D_MODEL),
                            lambda j, src, et, nu, u=u: (src[j * UNITS_PER_TILE + u], 0, 0))

    grid_spec = pltpu.PrefetchScalarGridSpec(
        num_scalar_prefetch=3,
        grid=(N_TILES_MAX,),
        in_specs=[unit_spec(u) for u in range(UNITS_PER_TILE)] + [
            pl.BlockSpec((1, D_MODEL, 2 * EXPERT_FF), lambda j, src, et, nu: (et[j], 0, 0)),
            pl.BlockSpec((1, EXPERT_FF, D_MODEL), lambda j, src, et, nu: (et[j], 0, 0)),
        ],
        out_specs=pl.BlockSpec((FFN_TILE, D_MODEL), lambda j, src, et, nu: (j, 0)),
        scratch_shapes=[
            pltpu.VMEM((D_MODEL, 2 * EXPERT_FF), BF16),
            pltpu.VMEM((EXPERT_FF, D_MODEL), BF16),
            pltpu.VMEM((FFN_TILE, D_MODEL), BF16),
        ],
    )
    return pl.pallas_call(
        _ffn_kernel,
        grid_spec=grid_spec,
        out_shape=jax.ShapeDtypeStruct((N_TILES_MAX * FFN_TILE, D_MODEL), BF16),
        compiler_params=pltpu.CompilerParams(
            dimension_semantics=("arbitrary",), vmem_limit_bytes=VMEM_LIMIT),
        name="moe_expert_ffn",
    )(plan["src"], plan["e_tile"], plan["n_used"], *([xp_units] * UNITS_PER_TILE), w_gu, w_d)


def _combine_kernel(ysrc_ref, ug_ref, *refs, final):
    y_refs = refs[:GROUP_UNITS]
    (rank_ref, comb_ref, er_ref, rr_ref, h_ref, x1_ref, mod_ref, wsgu_ref, wsd_ref, fg_ref,
     o_ref, acc_ref, yt_ref) = refs[GROUP_UNITS:]
    b = pl.program_id(0)
    g = pl.program_id(1)

    @pl.when(g == 0)
    def _():
        gu = _dot(h_ref[...], wsgu_ref[...])
        act = _silu(gu[:, :EXPERT_FF]) * gu[:, EXPERT_FF:]
        acc_ref[...] = _dot(act.astype(BF16), wsd_ref[...])

    @pl.when(g < ug_ref[b])
    def _():
        sub_e = lax.broadcasted_iota(jnp.int32, (N_EXPERTS, GROUP_SLOTS), 0)
        onehot_e = jnp.where(sub_e == er_ref[0], 1.0, 0.0).astype(BF16)
        slot_rank = _dot(rank_ref[0], onehot_e)
        slot_w = _dot(comb_ref[0], onehot_e)
        wc = jnp.where(slot_rank == rr_ref[0], slot_w, 0.0).astype(BF16)
        for u in range(GROUP_UNITS):
            yt_ref[u * UNIT:(u + 1) * UNIT, :] = y_refs[u][0]
        acc_ref[...] += _dot(wc, yt_ref[...])

    @pl.when(g == GROUPS_PER_BLOCK - 1)
    def _():
        x2 = x1_ref[...] + mod_ref[0, 5:6, :] * acc_ref[...]
        if final:
            ms = jnp.mean(x2 * x2, axis=-1, keepdims=True)
            x2 = x2 * lax.rsqrt(ms + EPS) * fg_ref[...]
        o_ref[...] = x2


def _combine(plan, y_sorted, comb_bt, h2, x1, mod_tile, ws_gu, ws_d, final_g, final):
    n_tok = h2.shape[0]
    nb = n_tok // MOE_BLOCK
    y_units = y_sorted.reshape(-1, UNIT, D_MODEL)

    def unit_spec(u):
        return pl.BlockSpec(
            (1, UNIT, D_MODEL),
            lambda b, g, ysrc, ug, u=u: (ysrc[(b * GROUPS_PER_BLOCK + g) * GROUP_UNITS + u], 0, 0))

    def full(a):
        return pl.BlockSpec(a.shape, lambda b, g, ysrc, ug: (0,) * a.ndim)

    grid_spec = pltpu.PrefetchScalarGridSpec(
        num_scalar_prefetch=2,
        grid=(nb, GROUPS_PER_BLOCK),
        in_specs=[unit_spec(u) for u in range(GROUP_UNITS)] + [
            pl.BlockSpec((1, MOE_BLOCK, N_EXPERTS), lambda b, g, ysrc, ug: (b, 0, 0)),
            pl.BlockSpec((1, MOE_BLOCK, N_EXPERTS), lambda b, g, ysrc, ug: (b, 0, 0)),
            pl.BlockSpec((1, 1, GROUP_SLOTS), lambda b, g, ysrc, ug: (b, 0, g)),
            pl.BlockSpec((1, 1, GROUP_SLOTS), lambda b, g, ysrc, ug: (b, 0, g)),
            pl.BlockSpec((MOE_BLOCK, D_MODEL), lambda b, g, ysrc, ug: (b, 0)),
            pl.BlockSpec((MOE_BLOCK, D_MODEL), lambda b, g, ysrc, ug: (b, 0)),
            pl.BlockSpec((1, 8, D_MODEL), lambda b, g, ysrc, ug: (b, 0, 0)),
            full(ws_gu), full(ws_d), full(final_g),
        ],
        out_specs=pl.BlockSpec((MOE_BLOCK, D_MODEL), lambda b, g, ysrc, ug: (b, 0)),
        scratch_shapes=[
            pltpu.VMEM((MOE_BLOCK, D_MODEL), F32),
            pltpu.VMEM((GROUP_SLOTS, D_MODEL), BF16),
        ],
    )
    return pl.pallas_call(
        functools.partial(_combine_kernel, final=final),
        grid_spec=grid_spec,
        out_shape=jax.ShapeDtypeStruct((n_tok, D_MODEL), F32),
        compiler_params=pltpu.CompilerParams(
            dimension_semantics=("arbitrary", "arbitrary"), vmem_limit_bytes=VMEM_LIMIT),
        name="moe_combine",
    )(plan["ysrc"], plan["used_groups_b"], *([y_units] * GROUP_UNITS), plan["rank_bt"], comb_bt,
      plan["e_slot"][:, None, :], plan["r_slot"][:, None, :], h2, x1, mod_tile, ws_gu, ws_d, final_g)


def _level_ids():
    t = np.arange(CHUNK)
    x = t[:, None] ^ t[None, :]
    lvl = np.zeros((CHUNK, CHUNK), np.int32)
    nz = x > 0
    lvl[nz] = np.floor(np.log2(x[nz])).astype(np.int32) + 1
    return jnp.asarray(lvl)


def kernel(x_prompt, x_sample, state_hgrn, c, c_ctx, w_ada, b_ada, norm1_g, w_in, hgrn_lb_logits, hgrn_norm_g, w_proj_hgrn, conv_dw_w, conv_dw_b, conv_norm_g, conv_norm_b, w_proj_conv, w_out, norm2_g, w_router, router_bias, w_expert_gate_up, w_expert_down, w_shared_gate_up, w_shared_down, final_norm_g):
    n_ctx, ctx_len, d = x_prompt.shape
    n_lat, lat_len, _ = x_sample.shape
    depth = w_ada.shape[0]
    assert d == D_MODEL and ctx_len * 4 == STEP and lat_len == STEP and n_ctx % 4 == 0
    ctx_steps = n_ctx * ctx_len // STEP
    n_steps = ctx_steps + n_lat
    n_tok = n_steps * STEP
    assert n_tok == N_TOKENS and MOE_BLOCK == POST_TILE
    tiles_per_step = STEP // POST_TILE

    x_all = jnp.concatenate([x_prompt.reshape(-1, d), x_sample.reshape(-1, d)], axis=0)

    cc = jnp.zeros((8, d), F32).at[:n_lat].set(c.astype(F32)).at[n_lat].set(c_ctx.astype(F32))
    mod = _ada(cc, w_ada, b_ada).reshape(depth, 8, N_MOD, d)
    step_src = np.array([n_lat] * ctx_steps + list(range(n_lat)))
    mod_step = jnp.pad(mod[:, step_src], ((0, 0), (0, 0), (0, 8 - N_MOD), (0, 0)))
    mod_tile = jnp.repeat(mod_step, tiles_per_step, axis=1)

    nc = jnp.asarray([ctx_len // CHUNK] * ctx_steps + [lat_len // CHUNK] * n_lat, jnp.int32)
    tile_pos = np.arange(n_lat * tiles_per_step) % tiles_per_step
    pv = jnp.asarray(np.concatenate([np.zeros(ctx_steps * tiles_per_step), tile_pos > 0]), jnp.int32)
    nv = jnp.asarray(np.concatenate([np.zeros(ctx_steps * tiles_per_step), tile_pos < tiles_per_step - 1]), jnp.int32)

    lbv = jnp.cumsum(jax.nn.softmax(hgrn_lb_logits.astype(F32), axis=0), axis=0)
    lbv = (lbv - lbv[:1]).reshape(depth, 2, N_HEADS, HEAD_DIM)
    lvl = _level_ids()

    xs = x_all
    ctx_states = []
    for l in range(depth):
        wl = w_in[l]
        hk = N_HEADS * HEAD_DIM
        parts = [wl[:, j * hk:(j + 1) * hk].reshape(d, N_HEADS, HEAD_DIM) for j in range(5)]
        w_hg = jnp.concatenate(parts, axis=-1).transpose(1, 0, 2).astype(BF16)
        w_u = wl[:, 5 * hk:5 * hk + 2 * CONV_DIM].astype(BF16)
        w_gz = wl[:, 5 * hk + 2 * CONV_DIM:].astype(BF16)
        lb = lbv[l]
        gp = jnp.stack([jnp.log(lb[0]), jnp.log1p(-lb[0]), 1.0 - lb[0],
                        jnp.log(lb[1]), jnp.log1p(-lb[1]), 1.0 - lb[1],
                        hgrn_norm_g[l].reshape(N_HEADS, HEAD_DIM).astype(F32),
                        jnp.zeros((N_HEADS, HEAD_DIM), F32)], axis=1)
        s0 = jnp.concatenate([jnp.zeros((ctx_steps, 2, N_HEADS, HEAD_DIM, HEAD_DIM), F32),
                              state_hgrn[:, l].astype(F32)], axis=0)
        g1 = norm1_g[l].reshape(1, d).astype(F32)

        o_all, states = _scan(nc, xs, mod_step[l], g1, w_hg, gp, s0, lvl)
        ctx_states.append(states[:ctx_steps].reshape(n_ctx, 2, N_HEADS, HEAD_DIM, HEAD_DIM))

        cw = jnp.pad(conv_dw_w[l].astype(F32), ((0, 1), (0, 0)))
        x1, h2, logits_t = _post(
            pv, nv, xs, mod_tile[l], o_all, w_u, w_gz, w_proj_hgrn[l].astype(BF16),
            w_proj_conv[l].astype(BF16), w_out[l].astype(BF16), w_router[l].T.astype(F32), g1, cw,
            conv_dw_b[l].reshape(1, -1).astype(F32), conv_norm_g[l].reshape(1, -1).astype(F32),
            conv_norm_b[l].reshape(1, -1).astype(F32), norm2_g[l].reshape(1, d).astype(F32))

        comb_t, chosen_t = _route(logits_t, router_bias[l].reshape(N_EXPERTS, 1).astype(F32))
        plan = _routing_plan(chosen_t)
        comb_bt = comb_t.reshape(N_EXPERTS, n_tok // MOE_BLOCK, MOE_BLOCK).transpose(1, 2, 0).astype(BF16)
        xp = _permute(plan, h2)
        y_sorted = _expert_ffn(plan, xp, w_expert_gate_up[l], w_expert_down[l])
        xs = _combine(plan, y_sorted, comb_bt, h2, x1, mod_tile[l], w_shared_gate_up[l].astype(BF16),
                      w_shared_down[l].astype(BF16), final_norm_g.reshape(1, d).astype(F32),
                      final=(l == depth - 1))

    n_ctx_tok = n_ctx * ctx_len
    y_prompt = xs[:n_ctx_tok].reshape(x_prompt.shape).astype(x_prompt.dtype)
    y_sample = xs[n_ctx_tok:].reshape(x_sample.shape).astype(x_sample.dtype)
    new_state = jnp.stack(ctx_states, axis=1).astype(x_prompt.dtype)
    return (y_prompt, y_sample, new_state)
```

```python
import functools

import numpy as np
import jax
import jax.numpy as jnp
from jax import lax
from jax.experimental import pallas as pl
from jax.experimental.pallas import tpu as pltpu

F32 = jnp.float32
BF16 = jnp.bfloat16
HIGHEST = lax.Precision.HIGHEST

D_MODEL = 1024
N_HEADS = 8
HEAD_DIM = 128
CONV_DIM = 512
CONV_WIDTH = 31
CONV_HALO = 16
N_EXPERTS = 64
N_GROUPS = 8
GROUP_SIZE = N_EXPERTS // N_GROUPS
TOPK_GROUPS = 4
TOP_K = 8
EXPERT_FF = 256
ROUTED_SCALE = 2.5
N_MOD = 6
EPS = 1e-6

CHUNK = 128
N_LEVELS = 7
STEP = 1024
POST_TILE = 256
MOE_BLOCK = 256
UNIT = 16
FFN_TILE = 256
UNITS_PER_TILE = FFN_TILE // UNIT
SLOT_MAX = MOE_BLOCK * TOP_K + N_EXPERTS * UNIT
UNITS_PER_BLOCK = SLOT_MAX // UNIT
GROUP_UNITS = 32
GROUP_SLOTS = GROUP_UNITS * UNIT
GROUPS_PER_BLOCK = UNITS_PER_BLOCK // GROUP_UNITS
N_TOKENS = 8192
N_TILES_MAX = (N_TOKENS * TOP_K // UNIT + (N_TOKENS // MOE_BLOCK) * N_EXPERTS) // UNITS_PER_TILE + N_EXPERTS
HEAD_COLS = 5 * HEAD_DIM
VMEM_LIMIT = 52 * 1024 * 1024


def _dot(a, b):
    return jnp.dot(a, b, preferred_element_type=F32)


def _dot_nt(a, b):
    return lax.dot_general(a, b, (((1,), (1,)), ((), ())), preferred_element_type=F32)


def _dot_tn(a, b):
    return lax.dot_general(a, b, (((0,), (0,)), ((), ())), preferred_element_type=F32)


def _sigmoid(x):
    return 1.0 / (1.0 + jnp.exp(-x))


def _silu(x):
    return x * _sigmoid(x)


def _rms_mod(x, g, scale, shift):
    ms = jnp.mean(x * x, axis=-1, keepdims=True)
    return x * lax.rsqrt(ms + EPS) * g * (1.0 + scale) + shift


def _ada_kernel(c_ref, w_ref, b_ref, o_ref):
    cc = c_ref[...]
    o_ref[0] = jnp.dot(_silu(cc), w_ref[0], preferred_element_type=F32, precision=HIGHEST) + b_ref[0]


def _ada(cc, w_ada, b_ada):
    depth, d, n = w_ada.shape
    tn = 1536
    return pl.pallas_call(
        _ada_kernel,
        grid=(depth, n // tn),
        in_specs=[
            pl.BlockSpec((8, d), lambda l, j: (0, 0)),
            pl.BlockSpec((1, d, tn), lambda l, j: (l, 0, j)),
            pl.BlockSpec((1, 1, tn), lambda l, j: (l, 0, j)),
        ],
        out_specs=pl.BlockSpec((1, 8, tn), lambda l, j: (l, 0, j)),
        out_shape=jax.ShapeDtypeStruct((depth, 8, n), F32),
        compiler_params=pltpu.CompilerParams(vmem_limit_bytes=VMEM_LIMIT),
        name="ada_mod",
    )(cc, w_ada, b_ada.reshape(depth, 1, n))


def _forget_gate(z, log_lb, log1m_lb, one_m_lb):
    e = jnp.exp(-jnp.abs(z))
    r = 1.0 / (1.0 + e)
    log_sig = jnp.minimum(z, 0.0) - jnp.log(1.0 + e)
    k = one_m_lb * jnp.where(z > 0, e * r, r)
    b = log1m_lb + log_sig
    log_f = jnp.maximum(log_lb, b) + jnp.log(1.0 + jnp.exp(-jnp.abs(log_lb - b)))
    return log_f, k


def _level_reference(cum_ref, blk, fwd):
    half = blk // 2
    pieces = []
    if blk >= 8:
        for i in range(CHUNK // blk):
            row = i * blk + (half - 1 if fwd else half)
            pieces.append(jnp.broadcast_to(cum_ref[row:row + 1, :], (blk, HEAD_DIM)))
    else:
        sub = lax.broadcasted_iota(jnp.int32, (8, HEAD_DIM), 0)
        for i in range(CHUNK // 8):
            lo_row = 8 * i + (1 if fwd else 2)
            hi_row = 8 * i + (5 if fwd else 6)
            lo = jnp.broadcast_to(cum_ref[lo_row:lo_row + 1, :], (8, HEAD_DIM))
            hi = jnp.broadcast_to(cum_ref[hi_row:hi_row + 1, :], (8, HEAD_DIM))
            pieces.append(jnp.where(sub >= 4, hi, lo))
    return pieces[0] if len(pieces) == 1 else jnp.concatenate(pieces, axis=0)


def _chunk_step(q, k, v, log_f, st, lvl, cum_ref, fwd):
    r_idx = lax.broadcasted_iota(jnp.int32, (CHUNK, CHUNK), 0)
    c_idx = lax.broadcasted_iota(jnp.int32, (CHUNK, CHUNK), 1)
    tri = jnp.where((r_idx >= c_idx) if fwd else (r_idx <= c_idx), 1.0, 0.0).astype(F32)
    cum = jnp.dot(tri, log_f, preferred_element_type=F32, precision=HIGHEST)
    cum_ref[...] = cum
    row = lax.broadcasted_iota(jnp.int32, (CHUNK, HEAD_DIM), 0)

    scores = jnp.where(lvl == 0, _dot_nt(q.astype(BF16), k.astype(BF16)), 0.0)
    for lev in range(1, N_LEVELS + 1):
        blk = 1 << lev
        half = blk >> 1
        is_query = ((row & half) != 0) if fwd else ((row & half) == 0)
        if lev == 1:
            eq = jnp.where(is_query, jnp.exp(log_f), 0.0)
            ek = jnp.where(is_query, 0.0, 1.0)
        else:
            d = cum - _level_reference(cum_ref, blk, fwd)
            e = jnp.exp(jnp.where(is_query, d, -d))
            eq = jnp.where(is_query, e, 0.0)
            ek = jnp.where(is_query, 0.0, e)
        s = _dot_nt((q * eq).astype(BF16), (k * ek).astype(BF16))
        scores = jnp.where(lvl == lev, s, scores)

    total = cum_ref[CHUNK - 1:CHUNK, :] if fwd else cum_ref[0:1, :]
    st_b = st.astype(BF16)
    o = _dot(scores.astype(BF16), v.astype(BF16)) + _dot_nt((q * jnp.exp(cum)).astype(BF16), st_b)
    k_st = k * jnp.exp(total - cum)
    st_new = st * jnp.exp(total) + _dot_tn(v.astype(BF16), k_st.astype(BF16))
    return o, st_new


def _scan_kernel(nc_ref, x_ref, mod_ref, g1_ref, w_ref, gp_ref, s0_ref, lvl_ref,
                 o_out, st_out, h_ref, z_ref, o_ref, cum_ref):
    step = pl.program_id(0)
    head = pl.program_id(1)
    n_seq_chunks = nc_ref[step]
    n_chunks = STEP // CHUNK

    @pl.when(head == 0)
    def _():
        g1 = g1_ref[...]
        shift = mod_ref[0, 0:1, :]
        scale = mod_ref[0, 1:2, :]

        def body(i, carry):
            rows = pl.ds(pl.multiple_of(i * CHUNK, CHUNK), CHUNK)
            h_ref[rows, :] = _rms_mod(x_ref[rows, :], g1, scale, shift).astype(BF16)
            return carry

        lax.fori_loop(0, n_chunks, body, 0)

    z_ref[...] = _dot(h_ref[...], w_ref[0])
    st_out[...] = jnp.zeros(st_out.shape, F32)
    lvl = lvl_ref[...]

    def direction(fwd):
        d = 0 if fwd else 1
        zcol = (2 if fwd else 3) * HEAD_DIM
        log_lb = gp_ref[0, 3 * d + 0:3 * d + 1, :]
        log1m_lb = gp_ref[0, 3 * d + 1:3 * d + 2, :]
        one_m_lb = gp_ref[0, 3 * d + 2:3 * d + 3, :]
        norm_g = gp_ref[0, 6:7, :]
        st0 = s0_ref[0, d, 0].T

        def body(i, st):
            c = i if fwd else n_chunks - 1 - i
            pos = lax.rem(c, n_seq_chunks)
            first = (pos == 0) if fwd else (pos == n_seq_chunks - 1)
            last = (pos == n_seq_chunks - 1) if fwd else (pos == 0)
            rows = pl.ds(pl.multiple_of(c * CHUNK, CHUNK), CHUNK)
            q = z_ref[rows, 0:HEAD_DIM]
            v = z_ref[rows, HEAD_DIM:2 * HEAD_DIM]
            log_f, k = _forget_gate(z_ref[rows, zcol:zcol + HEAD_DIM], log_lb, log1m_lb, one_m_lb)
            st = jnp.where(first, st0, st)
            o, st = _chunk_step(q, k, v, log_f, st, lvl, cum_ref, fwd)

            @pl.when(last)
            def _():
                st_out[0, lax.div(c, n_seq_chunks), d, 0] = st.T

            if fwd:
                o_ref[rows, :] = o
            else:
                o = o + o_ref[rows, :]
                o = o * lax.rsqrt(jnp.mean(o * o, axis=-1, keepdims=True) + EPS)
                og = z_ref[rows, 4 * HEAD_DIM:5 * HEAD_DIM]
                o_out[rows, :] = (o * norm_g * _silu(og)).astype(BF16)
            return st

        lax.fori_loop(0, n_chunks, body, st0)

    direction(True)
    direction(False)


def _scan(nc, x_all, mod_step, g1, w_hg, gp, s0, lvl):
    n_tok = x_all.shape[0]
    n_steps = n_tok // STEP
    grid_spec = pltpu.PrefetchScalarGridSpec(
        num_scalar_prefetch=1,
        grid=(n_steps, N_HEADS),
        in_specs=[
            pl.BlockSpec((STEP, D_MODEL), lambda s, h, nc: (s, 0)),
            pl.BlockSpec((1, 8, D_MODEL), lambda s, h, nc: (s, 0, 0)),
            pl.BlockSpec((1, D_MODEL), lambda s, h, nc: (0, 0)),
            pl.BlockSpec((1, D_MODEL, HEAD_COLS), lambda s, h, nc: (h, 0, 0)),
            pl.BlockSpec((1, 8, HEAD_DIM), lambda s, h, nc: (h, 0, 0)),
            pl.BlockSpec((1, 2, 1, HEAD_DIM, HEAD_DIM), lambda s, h, nc: (s, 0, h, 0, 0)),
            pl.BlockSpec((CHUNK, CHUNK), lambda s, h, nc: (0, 0)),
        ],
        out_specs=[
            pl.BlockSpec((STEP, HEAD_DIM), lambda s, h, nc: (s, h)),
            pl.BlockSpec((1, 4, 2, 1, HEAD_DIM, HEAD_DIM), lambda s, h, nc: (s, 0, 0, h, 0, 0)),
        ],
        scratch_shapes=[
            pltpu.VMEM((STEP, D_MODEL), BF16),
            pltpu.VMEM((STEP, HEAD_COLS), F32),
            pltpu.VMEM((STEP, HEAD_DIM), F32),
            pltpu.VMEM((CHUNK, HEAD_DIM), F32),
        ],
    )
    return pl.pallas_call(
        _scan_kernel,
        grid_spec=grid_spec,
        out_shape=[
            jax.ShapeDtypeStruct((n_tok, D_MODEL), BF16),
            jax.ShapeDtypeStruct((n_steps, 4, 2, N_HEADS, HEAD_DIM, HEAD_DIM), F32),
        ],
        compiler_params=pltpu.CompilerParams(
            dimension_semantics=("arbitrary", "arbitrary"), vmem_limit_bytes=VMEM_LIMIT),
        name="hgrn_scan",
    )(nc, x_all, mod_step, g1, w_hg, gp, s0, lvl)


def _post_kernel(pv_ref, nv_ref, x_ref, xp_ref, xn_ref, mod_ref, o_ref, wu_ref, wgz_ref, wph_ref,
                 wpc_ref, wo_ref, wr_ref, g1_ref, cw_ref, cb_ref, lg_ref, lb_ref, g2_ref,
                 x1_out, h2_out, lg_out, cu_ref):
    i = pl.program_id(0)
    g1 = g1_ref[...]
    shift1 = mod_ref[0, 0:1, :]
    scale1 = mod_ref[0, 1:2, :]
    gate1 = mod_ref[0, 2:3, :]
    shift2 = mod_ref[0, 3:4, :]
    scale2 = mod_ref[0, 4:5, :]

    def glu(xv):
        h = _rms_mod(xv, g1, scale1, shift1).astype(BF16)
        u = _dot(h, wu_ref[...])
        return h, u[:, :CONV_DIM] * _sigmoid(u[:, CONV_DIM:])

    x = x_ref[...]
    h, glu_mid = glu(x)
    _, glu_prev = glu(xp_ref[...])
    _, glu_next = glu(xn_ref[...])
    cu_ref[0:CONV_HALO, :] = glu_prev * pv_ref[i].astype(F32)
    cu_ref[CONV_HALO:CONV_HALO + POST_TILE, :] = glu_mid
    cu_ref[CONV_HALO + POST_TILE:, :] = glu_next * nv_ref[i].astype(F32)

    off = CONV_HALO - CONV_WIDTH // 2
    acc = jnp.zeros((POST_TILE, CONV_DIM), F32) + cb_ref[...]
    for j in range(CONV_WIDTH):
        acc = acc + cu_ref[off + j:off + j + POST_TILE, :] * cw_ref[j:j + 1, :]
    mu = jnp.mean(acc, axis=-1, keepdims=True)
    cen = acc - mu
    var = jnp.mean(cen * cen, axis=-1, keepdims=True)
    cv = cen * lax.rsqrt(var + EPS) * lg_ref[...] + lb_ref[...]
    y_b = _dot(_silu(cv).astype(BF16), wpc_ref[...])

    y_a = _dot(o_ref[...], wph_ref[...])
    gz = _sigmoid(_dot(h, wgz_ref[...]))
    merged = gz[:, :D_MODEL] * y_a + gz[:, D_MODEL:] * y_b
    x1 = x + gate1 * _dot(merged.astype(BF16), wo_ref[...])
    x1_out[...] = x1
    h2 = _rms_mod(x1, g2_ref[...], scale2, shift2)
    h2_out[...] = h2.astype(BF16)
    lg_out[...] = lax.dot_general(wr_ref[...], h2, (((1,), (1,)), ((), ())),
                                  preferred_element_type=F32, precision=HIGHEST)


def _post(pv, nv, x_all, mod_tile, o_all, w_u, w_gz, w_ph, w_pc, w_o, w_rt, g1, cw, cb, lg, lb, g2):
    n_tok = x_all.shape[0]
    n_tiles = n_tok // POST_TILE
    halo_per_tile = POST_TILE // CONV_HALO
    n_halo_blocks = n_tok // CONV_HALO

    def full(a):
        return pl.BlockSpec(a.shape, lambda i, pv, nv: (0,) * a.ndim)

    grid_spec = pltpu.PrefetchScalarGridSpec(
        num_scalar_prefetch=2,
        grid=(n_tiles,),
        in_specs=[
            pl.BlockSpec((POST_TILE, D_MODEL), lambda i, pv, nv: (i, 0)),
            pl.BlockSpec((CONV_HALO, D_MODEL), lambda i, pv, nv: (jnp.maximum(i * halo_per_tile - 1, 0), 0)),
            pl.BlockSpec((CONV_HALO, D_MODEL),
                         lambda i, pv, nv: (jnp.minimum((i + 1) * halo_per_tile, n_halo_blocks - 1), 0)),
            pl.BlockSpec((1, 8, D_MODEL), lambda i, pv, nv: (i, 0, 0)),
            pl.BlockSpec((POST_TILE, D_MODEL), lambda i, pv, nv: (i, 0)),
            full(w_u), full(w_gz), full(w_ph), full(w_pc), full(w_o), full(w_rt),
            full(g1), full(cw), full(cb), full(lg), full(lb), full(g2),
        ],
        out_specs=[
            pl.BlockSpec((POST_TILE, D_MODEL), lambda i, pv, nv: (i, 0)),
            pl.BlockSpec((POST_TILE, D_MODEL), lambda i, pv, nv: (i, 0)),
            pl.BlockSpec((N_EXPERTS, POST_TILE), lambda i, pv, nv: (0, i)),
        ],
        scratch_shapes=[pltpu.VMEM((POST_TILE + 2 * CONV_HALO, CONV_DIM), F32)],
    )
    return pl.pallas_call(
        _post_kernel,
        grid_spec=grid_spec,
        out_shape=[
            jax.ShapeDtypeStruct((n_tok, D_MODEL), F32),
            jax.ShapeDtypeStruct((n_tok, D_MODEL), BF16),
            jax.ShapeDtypeStruct((N_EXPERTS, n_tok), F32),
        ],
        compiler_params=pltpu.CompilerParams(
            dimension_semantics=("arbitrary",), vmem_limit_bytes=VMEM_LIMIT),
        name="post_mixer",
    )(pv, nv, x_all, x_all, x_all, mod_tile, o_all, w_u, w_gz, w_ph, w_pc, w_o, w_rt, g1, cw, cb, lg, lb, g2)


def _route_kernel(lg_ref, bias_ref, comb_out, chosen_out):
    n = lg_ref.shape[1]
    scores = _sigmoid(lg_ref[...])
    sel = scores + bias_ref[...]
    neg = jnp.float32(-jnp.inf)

    sel3 = sel.reshape(N_GROUPS, GROUP_SIZE, n)
    m1 = jnp.max(sel3, axis=1, keepdims=True)
    is_m1 = sel3 == m1
    n_m1 = jnp.sum(is_m1.astype(F32), axis=1, keepdims=True)
    m2 = jnp.max(jnp.where(is_m1, neg, sel3), axis=1, keepdims=True)
    grp = (m1 + jnp.where(n_m1 > 1.5, m1, m2)).reshape(N_GROUPS, n)

    gidx = lax.broadcasted_iota(jnp.int32, (N_GROUPS, n), 0)
    rank = jnp.zeros((N_GROUPS, n), F32)
    for g in range(N_GROUPS):
        other = grp[g:g + 1, :]
        ahead = (other > grp) | ((other == grp) & (g < gidx))
        rank = rank + ahead.astype(F32)
    keep_g = rank < TOPK_GROUPS - 0.5
    keep = jnp.broadcast_to(keep_g.reshape(N_GROUPS, 1, n), (N_GROUPS, GROUP_SIZE, n)).reshape(N_EXPERTS, n)
    cand = jnp.where(keep, sel, neg)

    eidx = lax.broadcasted_iota(jnp.int32, (N_EXPERTS, n), 0)
    chosen = jnp.zeros((N_EXPERTS, n), F32)
    for _ in range(TOP_K):
        best = jnp.max(cand, axis=0, keepdims=True)
        first = jnp.min(jnp.where(cand == best, eidx, N_EXPERTS), axis=0, keepdims=True)
        hit = eidx == first
        chosen = jnp.where(hit, 1.0, chosen)
        cand = jnp.where(hit, neg, cand)
    w = scores * chosen
    comb_out[...] = w / jnp.sum(w, axis=0, keepdims=True) * ROUTED_SCALE
    chosen_out[...] = chosen


def _route(logits_t, bias):
    n_tok = logits_t.shape[1]
    tile = 512
    return pl.pallas_call(
        _route_kernel,
        grid=(n_tok // tile,),
        in_specs=[
            pl.BlockSpec((N_EXPERTS, tile), lambda i: (0, i)),
            pl.BlockSpec((N_EXPERTS, 1), lambda i: (0, 0)),
        ],
        out_specs=[pl.BlockSpec((N_EXPERTS, tile), lambda i: (0, i)),
                   pl.BlockSpec((N_EXPERTS, tile), lambda i: (0, i))],
        out_shape=[jax.ShapeDtypeStruct((N_EXPERTS, n_tok), F32),
                   jax.ShapeDtypeStruct((N_EXPERTS, n_tok), F32)],
        name="router",
    )(logits_t, bias)


def _count_le(bounds, idx):
    return jnp.sum(jnp.where(bounds <= idx, 1.0, 0.0), axis=0, keepdims=True)


def _pick_row(onehot, col):
    return jnp.sum(jnp.where(onehot, col, 0.0), axis=0, keepdims=True)


def _plan_kernel(ch_ref, rankt_out, es_out, rs_out, src_out, et_out, ysrc_out, misc_out):
    nb = N_TOKENS // MOE_BLOCK
    n_src = N_TILES_MAX * UNITS_PER_TILE
    e_col = lax.broadcasted_iota(jnp.int32, (N_EXPERTS, 1), 0).astype(F32)
    blk_lane = lax.broadcasted_iota(jnp.int32, (N_EXPERTS, 128), 1)
    t_r = lax.broadcasted_iota(jnp.int32, (MOE_BLOCK, MOE_BLOCK), 0)
    t_c = lax.broadcasted_iota(jnp.int32, (MOE_BLOCK, MOE_BLOCK), 1)
    earlier = jnp.where(t_c < t_r, 1.0, 0.0).astype(BF16)
    e_r = lax.broadcasted_iota(jnp.int32, (N_EXPERTS, N_EXPERTS), 0)
    e_c = lax.broadcasted_iota(jnp.int32, (N_EXPERTS, N_EXPERTS), 1)
    upto = jnp.where(e_c <= e_r, 1.0, 0.0).astype(F32)
    slot = lax.broadcasted_iota(jnp.int32, (1, SLOT_MAX), 1).astype(F32)

    def col_cumsum(col):
        wide = jnp.broadcast_to(col, (N_EXPERTS, 128))
        return jnp.dot(upto, wide, preferred_element_type=F32, precision=HIGHEST)[:, 0:1]

    def block_stats(b, carry):
        units_mat, start_mat = carry
        cb = ch_ref[:, pl.ds(pl.multiple_of(b * MOE_BLOCK, MOE_BLOCK), MOE_BLOCK)]
        rank_t = _dot_nt(earlier, cb.astype(BF16))
        cb_t = cb.T
        rankt_out[b] = jnp.where(cb_t > 0, rank_t, -1.0).astype(BF16)
        cnt = jnp.sum(cb, axis=1, keepdims=True)
        units = jnp.floor((cnt + (UNIT - 1)) * (1.0 / UNIT))
        incl = col_cumsum(units)
        start = incl - units
        e_slot = _count_le(incl * UNIT, slot)
        onehot = e_col == e_slot
        r = slot - _pick_row(onehot, start * UNIT)
        valid = (e_slot < N_EXPERTS - 0.5) & (r < _pick_row(onehot, cnt))
        es_out[b] = jnp.where(valid, e_slot, -1.0).astype(jnp.int32)
        rs_out[b] = jnp.where(valid, r, -2.0)
        units_mat = jnp.where(blk_lane == b, units, units_mat)
        start_mat = jnp.where(blk_lane == b, start, start_mat)
        return units_mat, start_mat

    zeros = jnp.zeros((N_EXPERTS, 128), F32)
    units_mat, start_mat = lax.fori_loop(0, nb, block_stats, (zeros, zeros))

    tot = jnp.sum(units_mat, axis=1, keepdims=True)
    tiles_e = jnp.floor((tot + (UNITS_PER_TILE - 1)) * (1.0 / UNITS_PER_TILE))
    incl_t = col_cumsum(tiles_e)
    start_t = incl_t - tiles_e
    n_used = incl_t[N_EXPERTS - 1:N_EXPERTS, :]
    b_r = lax.broadcasted_iota(jnp.int32, (128, 128), 0)
    b_c = lax.broadcasted_iota(jnp.int32, (128, 128), 1)
    before = jnp.where(b_r < b_c, 1.0, 0.0).astype(F32)
    cum_b = jnp.dot(units_mat, before, preferred_element_type=F32, precision=HIGHEST)
    run_pos = start_t * UNITS_PER_TILE + cum_b

    tile_idx = lax.broadcasted_iota(jnp.int32, (1, 512), 1).astype(F32)
    et_out[...] = jnp.minimum(_count_le(incl_t, tile_idx), N_EXPERTS - 1.0).astype(jnp.int32)

    incl_ub_t = (cum_b + units_mat).T
    cum_b_t = cum_b.T
    start_mat_t = start_mat.T
    b_col = lax.broadcasted_iota(jnp.int32, (128, 1), 0)
    chunk = 1024

    def src_chunk(c, carry):
        off = pl.multiple_of(c * chunk, chunk)
        p = (lax.broadcasted_iota(jnp.int32, (1, chunk), 1) + off).astype(F32)
        e_p = jnp.minimum(_count_le(incl_t * UNITS_PER_TILE, p), N_EXPERTS - 1.0)
        onehot_e = e_col == e_p
        q = p - _pick_row(onehot_e, start_t * UNITS_PER_TILE)
        valid = (p < n_used * UNITS_PER_TILE) & (q < _pick_row(onehot_e, tot))
        sel = jnp.where(onehot_e, 1.0, 0.0).astype(F32)
        incl_sel = jnp.dot(incl_ub_t, sel, preferred_element_type=F32, precision=HIGHEST)
        b_p = jnp.sum(jnp.where((incl_sel <= q) & (b_col < nb), 1.0, 0.0), axis=0, keepdims=True)
        b_p = jnp.minimum(b_p, nb - 1.0)
        onehot_b = b_col.astype(F32) == b_p
        cum_sel = jnp.dot(cum_b_t, sel, preferred_element_type=F32, precision=HIGHEST)
        start_sel = jnp.dot(start_mat_t, sel, preferred_element_type=F32, precision=HIGHEST)
        unit = b_p * UNITS_PER_BLOCK + _pick_row(onehot_b, start_sel) + q - _pick_row(onehot_b, cum_sel)
        src_out[:, pl.ds(off, chunk)] = jnp.where(valid, unit, 0.0).astype(jnp.int32)
        return carry

    lax.fori_loop(0, n_src // chunk, src_chunk, 0)

    unit_idx = lax.broadcasted_iota(jnp.int32, (1, 256), 1).astype(F32)
    misc_lane = lax.broadcasted_iota(jnp.int32, (1, 128), 1)

    def block_units(b, used):
        here = blk_lane == b
        units = jnp.sum(jnp.where(here, units_mat, 0.0), axis=1, keepdims=True)
        start = jnp.sum(jnp.where(here, start_mat, 0.0), axis=1, keepdims=True)
        pos = jnp.sum(jnp.where(here, run_pos, 0.0), axis=1, keepdims=True)
        incl = start + units
        e_unit = _count_le(incl, unit_idx)
        onehot = e_col == e_unit
        val = _pick_row(onehot, pos - start) + unit_idx
        ysrc_out[b] = jnp.where(e_unit < N_EXPERTS - 0.5, val, 0.0).astype(jnp.int32)
        return jnp.where(misc_lane == b, incl[N_EXPERTS - 1:N_EXPERTS, :], used)

    used = lax.fori_loop(0, nb, block_units, jnp.zeros((1, 128), F32))
    misc_out[0:1, :] = used.astype(jnp.int32)
    misc_out[1:2, :] = jnp.broadcast_to(n_used, (1, 128)).astype(jnp.int32)
    misc_out[2:8, :] = jnp.zeros((6, 128), jnp.int32)


def _routing_plan(chosen):
    nb = N_TOKENS // MOE_BLOCK
    n_src = N_TILES_MAX * UNITS_PER_TILE
    assert n_src % 1024 == 0 and N_TILES_MAX <= 512 and nb <= 128 and UNITS_PER_BLOCK <= 256
    rank_bt, e_slot, r_slot, src, e_tile, ysrc, misc = pl.pallas_call(
        _plan_kernel,
        out_shape=[
            jax.ShapeDtypeStruct((nb, MOE_BLOCK, N_EXPERTS), BF16),
            jax.ShapeDtypeStruct((nb, 1, SLOT_MAX), jnp.int32),
            jax.ShapeDtypeStruct((nb, 1, SLOT_MAX), F32),
            jax.ShapeDtypeStruct((1, n_src), jnp.int32),
            jax.ShapeDtypeStruct((1, 512), jnp.int32),
            jax.ShapeDtypeStruct((nb, 1, 256), jnp.int32),
            jax.ShapeDtypeStruct((8, 128), jnp.int32),
        ],
        compiler_params=pltpu.CompilerParams(vmem_limit_bytes=VMEM_LIMIT),
        name="moe_plan",
    )(chosen)
    used_units_b = misc[0, :nb]
    return dict(rank_bt=rank_bt, e_slot=e_slot, r_slot=r_slot, src=src.reshape(-1),
                e_tile=e_tile[0, :N_TILES_MAX], n_used=misc[1, :1],
                ysrc=ysrc[:, 0, :UNITS_PER_BLOCK].reshape(-1),
                used_tiles_b=(used_units_b * UNIT + FFN_TILE - 1) // FFN_TILE,
                used_groups_b=(used_units_b + GROUP_UNITS - 1) // GROUP_UNITS)


def _permute_kernel(ut_ref, h_ref, rank_ref, es_ref, rs_ref, xp_out):
    n_used = ut_ref[pl.program_id(0)]
    sub_e = lax.broadcasted_iota(jnp.int32, (N_EXPERTS, FFN_TILE), 0)

    def body(i, carry):
        rows = pl.ds(pl.multiple_of(i * FFN_TILE, FFN_TILE), FFN_TILE)

        @pl.when(i < n_used)
        def _():
            onehot_e = jnp.where(sub_e == es_ref[0, :, rows], 1.0, 0.0).astype(BF16)
            slot_rank = _dot(rank_ref[0], onehot_e)
            pick = jnp.where(slot_rank == rs_ref[0, :, rows], 1.0, 0.0).astype(BF16)
            xp_out[0, rows, :] = _dot_tn(pick, h_ref[...]).astype(BF16)

        @pl.when(i >= n_used)
        def _():
            xp_out[0, rows, :] = jnp.zeros((FFN_TILE, D_MODEL), BF16)

        return carry

    lax.fori_loop(0, SLOT_MAX // FFN_TILE, body, 0)


def _permute(plan, h2):
    nb = h2.shape[0] // MOE_BLOCK
    grid_spec = pltpu.PrefetchScalarGridSpec(
        num_scalar_prefetch=1,
        grid=(nb,),
        in_specs=[
            pl.BlockSpec((MOE_BLOCK, D_MODEL), lambda b, ut: (b, 0)),
            pl.BlockSpec((1, MOE_BLOCK, N_EXPERTS), lambda b, ut: (b, 0, 0)),
            pl.BlockSpec((1, 1, SLOT_MAX), lambda b, ut: (b, 0, 0)),
            pl.BlockSpec((1, 1, SLOT_MAX), lambda b, ut: (b, 0, 0)),
        ],
        out_specs=pl.BlockSpec((1, SLOT_MAX, D_MODEL), lambda b, ut: (b, 0, 0)),
    )
    return pl.pallas_call(
        _permute_kernel,
        grid_spec=grid_spec,
        out_shape=jax.ShapeDtypeStruct((nb, SLOT_MAX, D_MODEL), BF16),
        compiler_params=pltpu.CompilerParams(
            dimension_semantics=("arbitrary",), vmem_limit_bytes=VMEM_LIMIT),
        name="moe_permute",
    )(plan["used_tiles_b"], h2, plan["rank_bt"], plan["e_slot"], plan["r_slot"])


def _ffn_kernel(src_ref, et_ref, nu_ref, *refs):
    x_refs = refs[:UNITS_PER_TILE]
    wgu_ref, wd_ref, y_out, wgu_b, wd_b, xt_ref = refs[UNITS_PER_TILE:]
    j = pl.program_id(0)

    @pl.when(j < nu_ref[0])
    def _():
        @pl.when((j == 0) | (et_ref[j] != et_ref[jnp.maximum(j - 1, 0)]))
        def _():
            wgu_b[...] = wgu_ref[0].astype(BF16)
            wd_b[...] = wd_ref[0].astype(BF16)

        for u in range(UNITS_PER_TILE):
            xt_ref[u * UNIT:(u + 1) * UNIT, :] = x_refs[u][0]
        gu = _dot(xt_ref[...], wgu_b[...])
        act = _silu(gu[:, :EXPERT_FF]) * gu[:, EXPERT_FF:]
        y_out[...] = _dot(act.astype(BF16), wd_b[...]).astype(BF16)

    @pl.when(j >= nu_ref[0])
    def _():
        y_out[...] = jnp.zeros(y_out.shape, BF16)


def _expert_ffn(plan, xp, w_gu, w_d):
    xp_units = xp.reshape(-1, UNIT, D_MODEL)

    def unit_spec(u):
        return pl.BlockSpec((1, UNIT, D_MODEL),
                            lambda j, src, et, nu, u=u: (src[j * UNITS_PER_TILE + u], 0, 0))

    grid_spec = pltpu.PrefetchScalarGridSpec(
        num_scalar_prefetch=3,
        grid=(N_TILES_MAX,),
        in_specs=[unit_spec(u) for u in range(UNITS_PER_TILE)] + [
            pl.BlockSpec((1, D_MODEL, 2 * EXPERT_FF), lambda j, src, et, nu: (et[j], 0, 0)),
            pl.BlockSpec((1, EXPERT_FF, D_MODEL), lambda j, src, et, nu: (et[j], 0, 0)),
        ],
        out_specs=pl.BlockSpec((FFN_TILE, D_MODEL), lambda j, src, et, nu: (j, 0)),
        scratch_shapes=[
            pltpu.VMEM((D_MODEL, 2 * EXPERT_FF), BF16),
            pltpu.VMEM((EXPERT_FF, D_MODEL), BF16),
            pltpu.VMEM((FFN_TILE, D_MODEL), BF16),
        ],
    )
    return pl.pallas_call(
        _ffn_kernel,
        grid_spec=grid_spec,
        out_shape=jax.ShapeDtypeStruct((N_TILES_MAX * FFN_TILE, D_MODEL), BF16),
        compiler_params=pltpu.CompilerParams(
            dimension_semantics=("arbitrary",), vmem_limit_bytes=VMEM_LIMIT),
        name="moe_expert_ffn",
    )(plan["src"], plan["e_tile"], plan["n_used"], *([xp_units] * UNITS_PER_TILE), w_gu, w_d)


def _combine_kernel(ysrc_ref, ug_ref, *refs, final):
    y_refs = refs[:GROUP_UNITS]
    (rank_ref, comb_ref, er_ref, rr_ref, h_ref, x1_ref, mod_ref, wsgu_ref, wsd_ref, fg_ref,
     o_ref, acc_ref, yt_ref) = refs[GROUP_UNITS:]
    b = pl.program_id(0)
    g = pl.program_id(1)

    @pl.when(g == 0)
    def _():
        gu = _dot(h_ref[...], wsgu_ref[...])
        act = _silu(gu[:, :EXPERT_FF]) * gu[:, EXPERT_FF:]
        acc_ref[...] = _dot(act.astype(BF16), wsd_ref[...])

    @pl.when(g < ug_ref[b])
    def _():
        sub_e = lax.broadcasted_iota(jnp.int32, (N_EXPERTS, GROUP_SLOTS), 0)
        onehot_e = jnp.where(sub_e == er_ref[0], 1.0, 0.0).astype(BF16)
        slot_rank = _dot(rank_ref[0], onehot_e)
        slot_w = _dot(comb_ref[0], onehot_e)
        wc = jnp.where(slot_rank == rr_ref[0], slot_w, 0.0).astype(BF16)
        for u in range(GROUP_UNITS):
            yt_ref[u * UNIT:(u + 1) * UNIT, :] = y_refs[u][0]
        acc_ref[...] += _dot(wc, yt_ref[...])

    @pl.when(g == GROUPS_PER_BLOCK - 1)
    def _():
        x2 = x1_ref[...] + mod_ref[0, 5:6, :] * acc_ref[...]
        if final:
            ms = jnp.mean(x2 * x2, axis=-1, keepdims=True)
            x2 = x2 * lax.rsqrt(ms + EPS) * fg_ref[...]
        o_ref[...] = x2


def _combine(plan, y_sorted, comb_bt, h2, x1, mod_tile, ws_gu, ws_d, final_g, final):
    n_tok = h2.shape[0]
    nb = n_tok // MOE_BLOCK
    y_units = y_sorted.reshape(-1, UNIT, D_MODEL)

    def unit_spec(u):
        return pl.BlockSpec(
            (1, UNIT, D_MODEL),
            lambda b, g, ysrc, ug, u=u: (ysrc[(b * GROUPS_PER_BLOCK + g) * GROUP_UNITS + u], 0, 0))

    def full(a):
        return pl.BlockSpec(a.shape, lambda b, g, ysrc, ug: (0,) * a.ndim)

    grid_spec = pltpu.PrefetchScalarGridSpec(
        num_scalar_prefetch=2,
        grid=(nb, GROUPS_PER_BLOCK),
        in_specs=[unit_spec(u) for u in range(GROUP_UNITS)] + [
            pl.BlockSpec((1, MOE_BLOCK, N_EXPERTS), lambda b, g, ysrc, ug: (b, 0, 0)),
            pl.BlockSpec((1, MOE_BLOCK, N_EXPERTS), lambda b, g, ysrc, ug: (b, 0, 0)),
            pl.BlockSpec((1, 1, GROUP_SLOTS), lambda b, g, ysrc, ug: (b, 0, g)),
            pl.BlockSpec((1, 1, GROUP_SLOTS), lambda b, g, ysrc, ug: (b, 0, g)),
            pl.BlockSpec((MOE_BLOCK, D_MODEL), lambda b, g, ysrc, ug: (b, 0)),
            pl.BlockSpec((MOE_BLOCK, D_MODEL), lambda b, g, ysrc, ug: (b, 0)),
            pl.BlockSpec((1, 8, D_MODEL), lambda b, g, ysrc, ug: (b, 0, 0)),
            full(ws_gu), full(ws_d), full(final_g),
        ],
        out_specs=pl.BlockSpec((MOE_BLOCK, D_MODEL), lambda b, g, ysrc, ug: (b, 0)),
        scratch_shapes=[
            pltpu.VMEM((MOE_BLOCK, D_MODEL), F32),
            pltpu.VMEM((GROUP_SLOTS, D_MODEL), BF16),
        ],
    )
    return pl.pallas_call(
        functools.partial(_combine_kernel, final=final),
        grid_spec=grid_spec,
        out_shape=jax.ShapeDtypeStruct((n_tok, D_MODEL), F32),
        compiler_params=pltpu.CompilerParams(
            dimension_semantics=("arbitrary", "arbitrary"), vmem_limit_bytes=VMEM_LIMIT),
        name="moe_combine",
    )(plan["ysrc"], plan["used_groups_b"], *([y_units] * GROUP_UNITS), plan["rank_bt"], comb_bt,
      plan["e_slot"], plan["r_slot"], h2, x1, mod_tile, ws_gu, ws_d, final_g)


def _level_ids():
    t = np.arange(CHUNK)
    x = t[:, None] ^ t[None, :]
    lvl = np.zeros((CHUNK, CHUNK), np.int32)
    nz = x > 0
    lvl[nz] = np.floor(np.log2(x[nz])).astype(np.int32) + 1
    return jnp.asarray(lvl)


def kernel(x_prompt, x_sample, state_hgrn, c, c_ctx, w_ada, b_ada, norm1_g, w_in, hgrn_lb_logits, hgrn_norm_g, w_proj_hgrn, conv_dw_w, conv_dw_b, conv_norm_g, conv_norm_b, w_proj_conv, w_out, norm2_g, w_router, router_bias, w_expert_gate_up, w_expert_down, w_shared_gate_up, w_shared_down, final_norm_g):
    n_ctx, ctx_len, d = x_prompt.shape
    n_lat, lat_len, _ = x_sample.shape
    depth = w_ada.shape[0]
    assert d == D_MODEL and ctx_len * 4 == STEP and lat_len == STEP and n_ctx % 4 == 0
    ctx_steps = n_ctx * ctx_len // STEP
    n_steps = ctx_steps + n_lat
    n_tok = n_steps * STEP
    assert n_tok == N_TOKENS and MOE_BLOCK == POST_TILE
    tiles_per_step = STEP // POST_TILE

    x_all = jnp.concatenate([x_prompt.reshape(-1, d), x_sample.reshape(-1, d)], axis=0)

    cc = jnp.zeros((8, d), F32).at[:n_lat].set(c.astype(F32)).at[n_lat].set(c_ctx.astype(F32))
    mod = _ada(cc, w_ada, b_ada).reshape(depth, 8, N_MOD, d)
    step_src = np.array([n_lat] * ctx_steps + list(range(n_lat)))
    mod_step = jnp.pad(mod[:, step_src], ((0, 0), (0, 0), (0, 8 - N_MOD), (0, 0)))
    mod_tile = jnp.repeat(mod_step, tiles_per_step, axis=1)

    nc = jnp.asarray([ctx_len // CHUNK] * ctx_steps + [lat_len // CHUNK] * n_lat, jnp.int32)
    tile_pos = np.arange(n_lat * tiles_per_step) % tiles_per_step
    pv = jnp.asarray(np.concatenate([np.zeros(ctx_steps * tiles_per_step), tile_pos > 0]), jnp.int32)
    nv = jnp.asarray(np.concatenate([np.zeros(ctx_steps * tiles_per_step), tile_pos < tiles_per_step - 1]), jnp.int32)

    lbv = jnp.cumsum(jax.nn.softmax(hgrn_lb_logits.astype(F32), axis=0), axis=0)
    lbv = (lbv - lbv[:1]).reshape(depth, 2, N_HEADS, HEAD_DIM)
    lvl = _level_ids()

    xs = x_all
    ctx_states = []
    for l in range(depth):
        wl = w_in[l]
        hk = N_HEADS * HEAD_DIM
        parts = [wl[:, j * hk:(j + 1) * hk].reshape(d, N_HEADS, HEAD_DIM) for j in range(5)]
        w_hg = jnp.concatenate(parts, axis=-1).transpose(1, 0, 2).astype(BF16)
        w_u = wl[:, 5 * hk:5 * hk + 2 * CONV_DIM].astype(BF16)
        w_gz = wl[:, 5 * hk + 2 * CONV_DIM:].astype(BF16)
        lb = lbv[l]
        gp = jnp.stack([jnp.log(lb[0]), jnp.log1p(-lb[0]), 1.0 - lb[0],
                        jnp.log(lb[1]), jnp.log1p(-lb[1]), 1.0 - lb[1],
                        hgrn_norm_g[l].reshape(N_HEADS, HEAD_DIM).astype(F32),
                        jnp.zeros((N_HEADS, HEAD_DIM), F32)], axis=1)
        s0 = jnp.concatenate([jnp.zeros((ctx_steps, 2, N_HEADS, HEAD_DIM, HEAD_DIM), F32),
                              state_hgrn[:, l].astype(F32)], axis=0)
        g1 = norm1_g[l].reshape(1, d).astype(F32)

        o_all, states = _scan(nc, xs, mod_step[l], g1, w_hg, gp, s0, lvl)
        ctx_states.append(states[:ctx_steps].reshape(n_ctx, 2, N_HEADS, HEAD_DIM, HEAD_DIM))

        cw = jnp.pad(conv_dw_w[l].astype(F32), ((0, 1), (0, 0)))
        x1, h2, logits_t = _post(
            pv, nv, xs, mod_tile[l], o_all, w_u, w_gz, w_proj_hgrn[l].astype(BF16),
            w_proj_conv[l].astype(BF16), w_out[l].astype(BF16), w_router[l].T.astype(F32), g1, cw,
            conv_dw_b[l].reshape(1, -1).astype(F32), conv_norm_g[l].reshape(1, -1).astype(F32),
            conv_norm_b[l].reshape(1, -1).astype(F32), norm2_g[l].reshape(1, d).astype(F32))

        comb_t, chosen_t = _route(logits_t, router_bias[l].reshape(N_EXPERTS, 1).astype(F32))
        plan = _routing_plan(chosen_t)
        comb_bt = comb_t.reshape(N_EXPERTS, n_tok // MOE_BLOCK, MOE_BLOCK).transpose(1, 2, 0).astype(BF16)
        xp = _permute(plan, h2)
        y_sorted = _expert_ffn(plan, xp, w_expert_gate_up[l], w_expert_down[l])
        xs = _combine(plan, y_sorted, comb_bt, h2, x1, mod_tile[l], w_shared_gate_up[l].astype(BF16),
                      w_shared_down[l].astype(BF16), final_norm_g.reshape(1, d).astype(F32),
                      final=(l == depth - 1))

    n_ctx_tok = n_ctx * ctx_len
    y_prompt = xs[:n_ctx_tok].reshape(x_prompt.shape).astype(x_prompt.dtype)
    y_sample = xs[n_ctx_tok:].reshape(x_sample.shape).astype(x_sample.dtype)
    new_state = jnp.stack(ctx_states, axis=1).astype(x_prompt.dtype)
    return (y_prompt, y_sample, new_state)
```

```python
import functools

import numpy as np
import jax
import jax.numpy as jnp
from jax import lax
from jax.experimental import pallas as pl
from jax.experimental.pallas import tpu as pltpu

F32 = jnp.float32
BF16 = jnp.bfloat16
HIGHEST = lax.Precision.HIGHEST

D_MODEL = 1024
N_HEADS = 8
HEAD_DIM = 128
CONV_DIM = 512
CONV_WIDTH = 31
CONV_HALO = 16
N_EXPERTS = 64
N_GROUPS = 8
GROUP_SIZE = N_EXPERTS // N_GROUPS
TOPK_GROUPS = 4
TOP_K = 8
EXPERT_FF = 256
ROUTED_SCALE = 2.5
N_MOD = 6
EPS = 1e-6

CHUNK = 128
N_LEVELS = 7
STEP = 1024
POST_TILE = 256
MOE_BLOCK = 256
UNIT = 16
FFN_TILE = 256
UNITS_PER_TILE = FFN_TILE // UNIT
TILES_PER_STEP = 2
SLOT_MAX = MOE_BLOCK * TOP_K + N_EXPERTS * UNIT
UNITS_PER_BLOCK = SLOT_MAX // UNIT
GROUP_UNITS = 32
GROUP_SLOTS = GROUP_UNITS * UNIT
GROUPS_PER_BLOCK = UNITS_PER_BLOCK // GROUP_UNITS
N_TOKENS = 8192
N_TILES_MAX = (N_TOKENS * TOP_K // UNIT + (N_TOKENS // MOE_BLOCK) * N_EXPERTS) // UNITS_PER_TILE + N_EXPERTS
HEAD_COLS = 5 * HEAD_DIM
VMEM_LIMIT = 52 * 1024 * 1024


def _dot(a, b):
    return jnp.dot(a, b, preferred_element_type=F32)


def _dot_nt(a, b):
    return lax.dot_general(a, b, (((1,), (1,)), ((), ())), preferred_element_type=F32)


def _dot_tn(a, b):
    return lax.dot_general(a, b, (((0,), (0,)), ((), ())), preferred_element_type=F32)


def _sigmoid(x):
    return 1.0 / (1.0 + jnp.exp(-x))


def _silu(x):
    return x * _sigmoid(x)


def _rms_mod(x, g, scale, shift):
    ms = jnp.mean(x * x, axis=-1, keepdims=True)
    return x * lax.rsqrt(ms + EPS) * g * (1.0 + scale) + shift


def _ada_kernel(c_ref, w_ref, b_ref, o_ref):
    cc = c_ref[...]
    o_ref[0] = jnp.dot(_silu(cc), w_ref[0], preferred_element_type=F32, precision=HIGHEST) + b_ref[0]


def _ada(cc, w_ada, b_ada):
    depth, d, n = w_ada.shape
    tn = 1536
    return pl.pallas_call(
        _ada_kernel,
        grid=(depth, n // tn),
        in_specs=[
            pl.BlockSpec((8, d), lambda l, j: (0, 0)),
            pl.BlockSpec((1, d, tn), lambda l, j: (l, 0, j)),
            pl.BlockSpec((1, 1, tn), lambda l, j: (l, 0, j)),
        ],
        out_specs=pl.BlockSpec((1, 8, tn), lambda l, j: (l, 0, j)),
        out_shape=jax.ShapeDtypeStruct((depth, 8, n), F32),
        compiler_params=pltpu.CompilerParams(vmem_limit_bytes=VMEM_LIMIT),
        name="ada_mod",
    )(cc, w_ada, b_ada.reshape(depth, 1, n))


def _forget_gate(z, log_lb, log1m_lb, one_m_lb):
    e = jnp.exp(-jnp.abs(z))
    r = 1.0 / (1.0 + e)
    log_sig = jnp.minimum(z, 0.0) - jnp.log(1.0 + e)
    k = one_m_lb * jnp.where(z > 0, e * r, r)
    b = log1m_lb + log_sig
    log_f = jnp.maximum(log_lb, b) + jnp.log(1.0 + jnp.exp(-jnp.abs(log_lb - b)))
    return log_f, k


def _level_reference(cum_ref, blk, fwd):
    half = blk // 2
    pieces = []
    if blk >= 8:
        for i in range(CHUNK // blk):
            row = i * blk + (half - 1 if fwd else half)
            pieces.append(jnp.broadcast_to(cum_ref[row:row + 1, :], (blk, HEAD_DIM)))
    else:
        sub = lax.broadcasted_iota(jnp.int32, (8, HEAD_DIM), 0)
        for i in range(CHUNK // 8):
            lo_row = 8 * i + (1 if fwd else 2)
            hi_row = 8 * i + (5 if fwd else 6)
            lo = jnp.broadcast_to(cum_ref[lo_row:lo_row + 1, :], (8, HEAD_DIM))
            hi = jnp.broadcast_to(cum_ref[hi_row:hi_row + 1, :], (8, HEAD_DIM))
            pieces.append(jnp.where(sub >= 4, hi, lo))
    return pieces[0] if len(pieces) == 1 else jnp.concatenate(pieces, axis=0)


def _chunk_step(q, k, v, log_f, st, lvl, cum_ref, fwd):
    r_idx = lax.broadcasted_iota(jnp.int32, (CHUNK, CHUNK), 0)
    c_idx = lax.broadcasted_iota(jnp.int32, (CHUNK, CHUNK), 1)
    tri = jnp.where((r_idx >= c_idx) if fwd else (r_idx <= c_idx), 1.0, 0.0).astype(F32)
    cum = jnp.dot(tri, log_f, preferred_element_type=F32, precision=HIGHEST)
    cum_ref[...] = cum

    scores = jnp.where(lvl == 0, _dot_nt(q.astype(BF16), k.astype(BF16)), 0.0)
    for lev in range(1, N_LEVELS + 1):
        if lev == 1:
            qe = q * jnp.exp(log_f)
            ke = k
        else:
            e = jnp.exp(-jnp.abs(cum - _level_reference(cum_ref, 1 << lev, fwd)))
            qe = q * e
            ke = k * e
        s = _dot_nt(qe.astype(BF16), ke.astype(BF16))
        scores = jnp.where(lvl == lev, s, scores)

    total = cum_ref[CHUNK - 1:CHUNK, :] if fwd else cum_ref[0:1, :]
    st_b = st.astype(BF16)
    o = _dot(scores.astype(BF16), v.astype(BF16)) + _dot_nt((q * jnp.exp(cum)).astype(BF16), st_b)
    k_st = k * jnp.exp(total - cum)
    st_new = st * jnp.exp(total) + _dot_tn(v.astype(BF16), k_st.astype(BF16))
    return o, st_new


def _scan_kernel(nc_ref, x_ref, mod_ref, g1_ref, wq_ref, wv_ref, wf_ref, wb_ref, wg_ref, gp_ref, s0_ref,
                 lvlf_ref, lvlb_ref, o_out, st_out, h_ref, w_ref, z_ref, of_ref, ob_ref, cumf_ref, cumb_ref):
    step = pl.program_id(0)
    head = pl.program_id(1)
    n_seq_chunks = nc_ref[step]
    n_chunks = STEP // CHUNK

    @pl.when(head == 0)
    def _():
        g1 = g1_ref[...]
        shift = mod_ref[0, 0:1, :]
        scale = mod_ref[0, 1:2, :]

        def body(i, carry):
            rows = pl.ds(pl.multiple_of(i * CHUNK, CHUNK), CHUNK)
            h_ref[rows, :] = _rms_mod(x_ref[rows, :], g1, scale, shift).astype(BF16)
            return carry

        lax.fori_loop(0, n_chunks, body, 0)

    for j, wj_ref in enumerate((wq_ref, wv_ref, wf_ref, wb_ref, wg_ref)):
        w_ref[:, j * HEAD_DIM:(j + 1) * HEAD_DIM] = wj_ref[...].astype(BF16)
    z_ref[...] = _dot(h_ref[...], w_ref[...])
    st_out[...] = jnp.zeros(st_out.shape, F32)
    st0 = (s0_ref[0, 0, 0].T, s0_ref[0, 1, 0].T)

    def one_direction(c, st, fwd):
        d = 0 if fwd else 1
        zcol = (2 if fwd else 3) * HEAD_DIM
        pos = lax.rem(c, n_seq_chunks)
        first = (pos == 0) if fwd else (pos == n_seq_chunks - 1)
        rows = pl.ds(c * CHUNK, CHUNK)
        q = z_ref[rows, 0:HEAD_DIM]
        v = z_ref[rows, HEAD_DIM:2 * HEAD_DIM]
        log_f, k = _forget_gate(z_ref[rows, zcol:zcol + HEAD_DIM], gp_ref[0, 3 * d:3 * d + 1, :],
                                gp_ref[0, 3 * d + 1:3 * d + 2, :], gp_ref[0, 3 * d + 2:3 * d + 3, :])
        st = jnp.where(first, st0[d], st)
        o, st = _chunk_step(q, k, v, log_f, st, (lvlf_ref if fwd else lvlb_ref)[...],
                            cumf_ref if fwd else cumb_ref, fwd)

        (of_ref if fwd else ob_ref)[rows, :] = o
        st_out[0, lax.div(c, n_seq_chunks), d, 0] = st.T
        return st

    carry = st0
    for i in range(n_chunks):
        carry = (one_direction(i, carry[0], True), one_direction(n_chunks - 1 - i, carry[1], False))

    norm_g = gp_ref[0, 6:7, :]

    def finish(i, carry):
        rows = pl.ds(pl.multiple_of(i * CHUNK, CHUNK), CHUNK)
        o = of_ref[rows, :] + ob_ref[rows, :]
        o = o * lax.rsqrt(jnp.mean(o * o, axis=-1, keepdims=True) + EPS)
        og = z_ref[rows, 4 * HEAD_DIM:5 * HEAD_DIM]
        o_out[rows, :] = (o * norm_g * _silu(og)).astype(BF16)
        return carry

    lax.fori_loop(0, n_chunks, finish, 0)


def _scan(nc, x_all, mod_step, g1, w_in_l, gp, s0, lvl_f, lvl_b):
    n_tok = x_all.shape[0]
    n_steps = n_tok // STEP
    grid_spec = pltpu.PrefetchScalarGridSpec(
        num_scalar_prefetch=1,
        grid=(n_steps, N_HEADS),
        in_specs=[
            pl.BlockSpec((STEP, D_MODEL), lambda s, h, nc: (s, 0)),
            pl.BlockSpec((1, 8, D_MODEL), lambda s, h, nc: (s, 0, 0)),
            pl.BlockSpec((1, D_MODEL), lambda s, h, nc: (0, 0)),
        ] + [
            pl.BlockSpec((D_MODEL, HEAD_DIM), lambda s, h, nc, j=j: (0, j * N_HEADS + h)) for j in range(5)
        ] + [
            pl.BlockSpec((1, 8, HEAD_DIM), lambda s, h, nc: (h, 0, 0)),
            pl.BlockSpec((1, 2, 1, HEAD_DIM, HEAD_DIM), lambda s, h, nc: (s, 0, h, 0, 0)),
            pl.BlockSpec((CHUNK, CHUNK), lambda s, h, nc: (0, 0)),
            pl.BlockSpec((CHUNK, CHUNK), lambda s, h, nc: (0, 0)),
        ],
        out_specs=[
            pl.BlockSpec((STEP, HEAD_DIM), lambda s, h, nc: (s, h)),
            pl.BlockSpec((1, 4, 2, 1, HEAD_DIM, HEAD_DIM), lambda s, h, nc: (s, 0, 0, h, 0, 0)),
        ],
        scratch_shapes=[
            pltpu.VMEM((STEP, D_MODEL), BF16),
            pltpu.VMEM((D_MODEL, HEAD_COLS), BF16),
            pltpu.VMEM((STEP, HEAD_COLS), F32),
            pltpu.VMEM((STEP, HEAD_DIM), F32),
            pltpu.VMEM((STEP, HEAD_DIM), F32),
            pltpu.VMEM((CHUNK, HEAD_DIM), F32),
            pltpu.VMEM((CHUNK, HEAD_DIM), F32),
        ],
    )
    return pl.pallas_call(
        _scan_kernel,
        grid_spec=grid_spec,
        out_shape=[
            jax.ShapeDtypeStruct((n_tok, D_MODEL), BF16),
            jax.ShapeDtypeStruct((n_steps, 4, 2, N_HEADS, HEAD_DIM, HEAD_DIM), F32),
        ],
        compiler_params=pltpu.CompilerParams(
            dimension_semantics=("arbitrary", "arbitrary"), vmem_limit_bytes=VMEM_LIMIT),
        name="hgrn_scan",
    )(nc, x_all, mod_step, g1, *([w_in_l] * 5), gp, s0, lvl_f, lvl_b)


def _post_kernel(pv_ref, nv_ref, x_ref, xp_ref, xn_ref, mod_ref, o_ref, wu_ref, wgz_ref, wph_ref,
                 wpc_ref, wo_ref, wr_ref, g1_ref, cw_ref, cb_ref, lg_ref, lb_ref, g2_ref,
                 x1_out, h2_out, lg_out, cu_ref):
    i = pl.program_id(0)
    g1 = g1_ref[...]
    shift1 = mod_ref[0, 0:1, :]
    scale1 = mod_ref[0, 1:2, :]
    gate1 = mod_ref[0, 2:3, :]
    shift2 = mod_ref[0, 3:4, :]
    scale2 = mod_ref[0, 4:5, :]

    def glu(xv):
        h = _rms_mod(xv, g1, scale1, shift1).astype(BF16)
        u = _dot(h, wu_ref[...])
        return h, u[:, :CONV_DIM] * _sigmoid(u[:, CONV_DIM:])

    x = x_ref[...]
    h, glu_mid = glu(x)
    _, glu_prev = glu(xp_ref[...])
    _, glu_next = glu(xn_ref[...])
    cu_ref[0:CONV_HALO, :] = glu_prev * pv_ref[i].astype(F32)
    cu_ref[CONV_HALO:CONV_HALO + POST_TILE, :] = glu_mid
    cu_ref[CONV_HALO + POST_TILE:, :] = glu_next * nv_ref[i].astype(F32)

    off = CONV_HALO - CONV_WIDTH // 2
    acc = jnp.zeros((POST_TILE, CONV_DIM), F32) + cb_ref[...]
    for j in range(CONV_WIDTH):
        acc = acc + cu_ref[off + j:off + j + POST_TILE, :] * cw_ref[j:j + 1, :]
    mu = jnp.mean(acc, axis=-1, keepdims=True)
    cen = acc - mu
    var = jnp.mean(cen * cen, axis=-1, keepdims=True)
    cv = cen * lax.rsqrt(var + EPS) * lg_ref[...] + lb_ref[...]
    y_b = _dot(_silu(cv).astype(BF16), wpc_ref[...])

    y_a = _dot(o_ref[...], wph_ref[...])
    gz = _sigmoid(_dot(h, wgz_ref[...]))
    merged = gz[:, :D_MODEL] * y_a + gz[:, D_MODEL:] * y_b
    x1 = x + gate1 * _dot(merged.astype(BF16), wo_ref[...])
    x1_out[...] = x1
    h2 = _rms_mod(x1, g2_ref[...], scale2, shift2)
    h2_out[...] = h2.astype(BF16)
    lg_out[...] = lax.dot_general(wr_ref[...], h2, (((1,), (1,)), ((), ())),
                                  preferred_element_type=F32, precision=HIGHEST)


def _post(pv, nv, x_all, mod_tile, o_all, w_u, w_gz, w_ph, w_pc, w_o, w_rt, g1, cw, cb, lg, lb, g2):
    n_tok = x_all.shape[0]
    n_tiles = n_tok // POST_TILE
    halo_per_tile = POST_TILE // CONV_HALO
    n_halo_blocks = n_tok // CONV_HALO

    def full(a):
        return pl.BlockSpec(a.shape, lambda i, pv, nv: (0,) * a.ndim)

    grid_spec = pltpu.PrefetchScalarGridSpec(
        num_scalar_prefetch=2,
        grid=(n_tiles,),
        in_specs=[
            pl.BlockSpec((POST_TILE, D_MODEL), lambda i, pv, nv: (i, 0)),
            pl.BlockSpec((CONV_HALO, D_MODEL), lambda i, pv, nv: (jnp.maximum(i * halo_per_tile - 1, 0), 0)),
            pl.BlockSpec((CONV_HALO, D_MODEL),
                         lambda i, pv, nv: (jnp.minimum((i + 1) * halo_per_tile, n_halo_blocks - 1), 0)),
            pl.BlockSpec((1, 8, D_MODEL), lambda i, pv, nv: (i, 0, 0)),
            pl.BlockSpec((POST_TILE, D_MODEL), lambda i, pv, nv: (i, 0)),
            full(w_u), full(w_gz), full(w_ph), full(w_pc), full(w_o), full(w_rt),
            full(g1), full(cw), full(cb), full(lg), full(lb), full(g2),
        ],
        out_specs=[
            pl.BlockSpec((POST_TILE, D_MODEL), lambda i, pv, nv: (i, 0)),
            pl.BlockSpec((POST_TILE, D_MODEL), lambda i, pv, nv: (i, 0)),
            pl.BlockSpec((N_EXPERTS, POST_TILE), lambda i, pv, nv: (0, i)),
        ],
        scratch_shapes=[pltpu.VMEM((POST_TILE + 2 * CONV_HALO, CONV_DIM), F32)],
    )
    return pl.pallas_call(
        _post_kernel,
        grid_spec=grid_spec,
        out_shape=[
            jax.ShapeDtypeStruct((n_tok, D_MODEL), F32),
            jax.ShapeDtypeStruct((n_tok, D_MODEL), BF16),
            jax.ShapeDtypeStruct((N_EXPERTS, n_tok), F32),
        ],
        compiler_params=pltpu.CompilerParams(
            dimension_semantics=("arbitrary",), vmem_limit_bytes=VMEM_LIMIT),
        name="post_mixer",
    )(pv, nv, x_all, x_all, x_all, mod_tile, o_all, w_u, w_gz, w_ph, w_pc, w_o, w_rt, g1, cw, cb, lg, lb, g2)


def _route_kernel(lg_ref, bias_ref, comb_out, chosen_out):
    n = lg_ref.shape[1]
    scores = _sigmoid(lg_ref[...])
    sel = scores + bias_ref[...]
    neg = jnp.float32(-jnp.inf)

    sel3 = sel.reshape(N_GROUPS, GROUP_SIZE, n)
    m1 = jnp.max(sel3, axis=1, keepdims=True)
    is_m1 = sel3 == m1
    n_m1 = jnp.sum(is_m1.astype(F32), axis=1, keepdims=True)
    m2 = jnp.max(jnp.where(is_m1, neg, sel3), axis=1, keepdims=True)
    grp = (m1 + jnp.where(n_m1 > 1.5, m1, m2)).reshape(N_GROUPS, n)

    gidx = lax.broadcasted_iota(jnp.int32, (N_GROUPS, n), 0)
    rank = jnp.zeros((N_GROUPS, n), F32)
    for g in range(N_GROUPS):
        other = grp[g:g + 1, :]
        ahead = (other > grp) | ((other == grp) & (g < gidx))
        rank = rank + ahead.astype(F32)
    keep_g = rank < TOPK_GROUPS - 0.5
    keep = jnp.broadcast_to(keep_g.reshape(N_GROUPS, 1, n), (N_GROUPS, GROUP_SIZE, n)).reshape(N_EXPERTS, n)
    cand = jnp.where(keep, sel, neg)

    eidx = lax.broadcasted_iota(jnp.int32, (N_EXPERTS, n), 0)
    chosen = jnp.zeros((N_EXPERTS, n), F32)
    for _ in range(TOP_K):
        best = jnp.max(cand, axis=0, keepdims=True)
        first = jnp.min(jnp.where(cand == best, eidx, N_EXPERTS), axis=0, keepdims=True)
        hit = eidx == first
        chosen = jnp.where(hit, 1.0, chosen)
        cand = jnp.where(hit, neg, cand)
    w = scores * chosen
    comb_out[...] = w / jnp.sum(w, axis=0, keepdims=True) * ROUTED_SCALE
    chosen_out[...] = chosen


def _route(logits_t, bias):
    n_tok = logits_t.shape[1]
    tile = 512
    return pl.pallas_call(
        _route_kernel,
        grid=(n_tok // tile,),
        in_specs=[
            pl.BlockSpec((N_EXPERTS, tile), lambda i: (0, i)),
            pl.BlockSpec((N_EXPERTS, 1), lambda i: (0, 0)),
        ],
        out_specs=[pl.BlockSpec((N_EXPERTS, tile), lambda i: (0, i)),
                   pl.BlockSpec((N_EXPERTS, tile), lambda i: (0, i))],
        out_shape=[jax.ShapeDtypeStruct((N_EXPERTS, n_tok), F32),
                   jax.ShapeDtypeStruct((N_EXPERTS, n_tok), F32)],
        name="router",
    )(logits_t, bias)


def _count_le(bounds, idx):
    return jnp.sum(jnp.where(bounds <= idx, 1.0, 0.0), axis=0, keepdims=True)


def _pick_row(onehot, col):
    return jnp.sum(jnp.where(onehot, col, 0.0), axis=0, keepdims=True)


def _plan_kernel(ch_ref, rankt_out, es_out, rs_out, src_out, et_out, ysrc_out, misc_out):
    nb = N_TOKENS // MOE_BLOCK
    n_src = N_TILES_MAX * UNITS_PER_TILE
    e_col = lax.broadcasted_iota(jnp.int32, (N_EXPERTS, 1), 0).astype(F32)
    blk_lane = lax.broadcasted_iota(jnp.int32, (N_EXPERTS, 128), 1)
    t_r = lax.broadcasted_iota(jnp.int32, (MOE_BLOCK, MOE_BLOCK), 0)
    t_c = lax.broadcasted_iota(jnp.int32, (MOE_BLOCK, MOE_BLOCK), 1)
    earlier = jnp.where(t_c < t_r, 1.0, 0.0).astype(BF16)
    e_r = lax.broadcasted_iota(jnp.int32, (N_EXPERTS, N_EXPERTS), 0)
    e_c = lax.broadcasted_iota(jnp.int32, (N_EXPERTS, N_EXPERTS), 1)
    upto = jnp.where(e_c <= e_r, 1.0, 0.0).astype(F32)
    slot = lax.broadcasted_iota(jnp.int32, (1, SLOT_MAX), 1).astype(F32)

    def col_cumsum(col):
        wide = jnp.broadcast_to(col, (N_EXPERTS, 128))
        return jnp.dot(upto, wide, preferred_element_type=F32, precision=HIGHEST)[:, 0:1]

    def block_stats(b, carry):
        units_mat, start_mat = carry
        cb = ch_ref[:, pl.ds(pl.multiple_of(b * MOE_BLOCK, MOE_BLOCK), MOE_BLOCK)]
        rank_t = _dot_nt(earlier, cb.astype(BF16))
        cb_t = cb.T
        rankt_out[b] = jnp.where(cb_t > 0, rank_t, -1.0).astype(BF16)
        cnt = jnp.sum(cb, axis=1, keepdims=True)
        units = jnp.floor((cnt + (UNIT - 1)) * (1.0 / UNIT))
        incl = col_cumsum(units)
        start = incl - units
        e_slot = _count_le(incl * UNIT, slot)
        onehot = e_col == e_slot
        r = slot - _pick_row(onehot, start * UNIT)
        valid = (e_slot < N_EXPERTS - 0.5) & (r < _pick_row(onehot, cnt))
        es_out[b] = jnp.where(valid, e_slot, -1.0).astype(jnp.int32)
        rs_out[b] = jnp.where(valid, r, -2.0)
        units_mat = jnp.where(blk_lane == b, units, units_mat)
        start_mat = jnp.where(blk_lane == b, start, start_mat)
        return units_mat, start_mat

    zeros = jnp.zeros((N_EXPERTS, 128), F32)
    units_mat, start_mat = lax.fori_loop(0, nb, block_stats, (zeros, zeros))

    tot = jnp.sum(units_mat, axis=1, keepdims=True)
    tiles_e = jnp.floor((tot + (UNITS_PER_TILE - 1)) * (1.0 / UNITS_PER_TILE))
    incl_t = col_cumsum(tiles_e)
    start_t = incl_t - tiles_e
    n_used = incl_t[N_EXPERTS - 1:N_EXPERTS, :]
    b_r = lax.broadcasted_iota(jnp.int32, (128, 128), 0)
    b_c = lax.broadcasted_iota(jnp.int32, (128, 128), 1)
    before = jnp.where(b_r < b_c, 1.0, 0.0).astype(F32)
    cum_b = jnp.dot(units_mat, before, preferred_element_type=F32, precision=HIGHEST)
    run_pos = start_t * UNITS_PER_TILE + cum_b

    tile_idx = lax.broadcasted_iota(jnp.int32, (1, 512), 1).astype(F32)
    et_out[...] = jnp.minimum(_count_le(incl_t, tile_idx), N_EXPERTS - 1.0).astype(jnp.int32)

    incl_ub_t = (cum_b + units_mat).T
    cum_b_t = cum_b.T
    start_mat_t = start_mat.T
    b_col = lax.broadcasted_iota(jnp.int32, (128, 1), 0)
    chunk = 1024

    def src_chunk(c, carry):
        off = pl.multiple_of(c * chunk, chunk)
        p = (lax.broadcasted_iota(jnp.int32, (1, chunk), 1) + off).astype(F32)
        e_p = jnp.minimum(_count_le(incl_t * UNITS_PER_TILE, p), N_EXPERTS - 1.0)
        onehot_e = e_col == e_p
        q = p - _pick_row(onehot_e, start_t * UNITS_PER_TILE)
        valid = (p < n_used * UNITS_PER_TILE) & (q < _pick_row(onehot_e, tot))
        sel = jnp.where(onehot_e, 1.0, 0.0).astype(F32)
        incl_sel = jnp.dot(incl_ub_t, sel, preferred_element_type=F32, precision=HIGHEST)
        b_p = jnp.sum(jnp.where((incl_sel <= q) & (b_col < nb), 1.0, 0.0), axis=0, keepdims=True)
        b_p = jnp.minimum(b_p, nb - 1.0)
        onehot_b = b_col.astype(F32) == b_p
        cum_sel = jnp.dot(cum_b_t, sel, preferred_element_type=F32, precision=HIGHEST)
        start_sel = jnp.dot(start_mat_t, sel, preferred_element_type=F32, precision=HIGHEST)
        unit = b_p * UNITS_PER_BLOCK + _pick_row(onehot_b, start_sel) + q - _pick_row(onehot_b, cum_sel)
        src_out[:, pl.ds(off, chunk)] = jnp.where(valid, unit, 0.0).astype(jnp.int32)
        return carry

    lax.fori_loop(0, n_src // chunk, src_chunk, 0)

    unit_idx = lax.broadcasted_iota(jnp.int32, (1, 256), 1).astype(F32)
    misc_lane = lax.broadcasted_iota(jnp.int32, (1, 128), 1)

    def block_units(b, used):
        here = blk_lane == b
        units = jnp.sum(jnp.where(here, units_mat, 0.0), axis=1, keepdims=True)
        start = jnp.sum(jnp.where(here, start_mat, 0.0), axis=1, keepdims=True)
        pos = jnp.sum(jnp.where(here, run_pos, 0.0), axis=1, keepdims=True)
        incl = start + units
        e_unit = _count_le(incl, unit_idx)
        onehot = e_col == e_unit
        val = _pick_row(onehot, pos - start) + unit_idx
        ysrc_out[b] = jnp.where(e_unit < N_EXPERTS - 0.5, val, 0.0).astype(jnp.int32)
        return jnp.where(misc_lane == b, incl[N_EXPERTS - 1:N_EXPERTS, :], used)

    used = lax.fori_loop(0, nb, block_units, jnp.zeros((1, 128), F32))
    misc_out[0:1, :] = used.astype(jnp.int32)
    misc_out[1:2, :] = jnp.broadcast_to(n_used, (1, 128)).astype(jnp.int32)
    misc_out[2:8, :] = jnp.zeros((6, 128), jnp.int32)


def _routing_plan(chosen):
    nb = N_TOKENS // MOE_BLOCK
    n_src = N_TILES_MAX * UNITS_PER_TILE
    assert n_src % 1024 == 0 and N_TILES_MAX <= 512 and nb <= 128 and UNITS_PER_BLOCK <= 256
    rank_bt, e_slot, r_slot, src, e_tile, ysrc, misc = pl.pallas_call(
        _plan_kernel,
        out_shape=[
            jax.ShapeDtypeStruct((nb, MOE_BLOCK, N_EXPERTS), BF16),
            jax.ShapeDtypeStruct((nb, 1, SLOT_MAX), jnp.int32),
            jax.ShapeDtypeStruct((nb, 1, SLOT_MAX), F32),
            jax.ShapeDtypeStruct((1, n_src), jnp.int32),
            jax.ShapeDtypeStruct((1, 512), jnp.int32),
            jax.ShapeDtypeStruct((nb, 1, 256), jnp.int32),
            jax.ShapeDtypeStruct((8, 128), jnp.int32),
        ],
        compiler_params=pltpu.CompilerParams(vmem_limit_bytes=VMEM_LIMIT),
        name="moe_plan",
    )(chosen)
    used_units_b = misc[0, :nb]
    return dict(rank_bt=rank_bt, e_slot=e_slot, r_slot=r_slot, src=src.reshape(-1),
                e_tile=e_tile[0, :N_TILES_MAX], n_used=misc[1, :1],
                ysrc=ysrc[:, 0, :UNITS_PER_BLOCK].reshape(-1),
                used_tiles_b=(used_units_b * UNIT + FFN_TILE - 1) // FFN_TILE,
                used_groups_b=(used_units_b + GROUP_UNITS - 1) // GROUP_UNITS)


def _permute_kernel(ut_ref, h_ref, rank_ref, es_ref, rs_ref, xp_out):
    n_used = ut_ref[pl.program_id(0)]
    sub_e = lax.broadcasted_iota(jnp.int32, (N_EXPERTS, FFN_TILE), 0)

    def body(i, carry):
        rows = pl.ds(pl.multiple_of(i * FFN_TILE, FFN_TILE), FFN_TILE)

        @pl.when(i < n_used)
        def _():
            onehot_e = jnp.where(sub_e == es_ref[0, :, rows], 1.0, 0.0).astype(BF16)
            slot_rank = _dot(rank_ref[0], onehot_e)
            pick = jnp.where(slot_rank == rs_ref[0, :, rows], 1.0, 0.0).astype(BF16)
            xp_out[0, rows, :] = _dot_tn(pick, h_ref[...]).astype(BF16)

        @pl.when(i >= n_used)
        def _():
            xp_out[0, rows, :] = jnp.zeros((FFN_TILE, D_MODEL), BF16)

        return carry

    lax.fori_loop(0, SLOT_MAX // FFN_TILE, body, 0)


def _permute(plan, h2):
    nb = h2.shape[0] // MOE_BLOCK
    grid_spec = pltpu.PrefetchScalarGridSpec(
        num_scalar_prefetch=1,
        grid=(nb,),
        in_specs=[
            pl.BlockSpec((MOE_BLOCK, D_MODEL), lambda b, ut: (b, 0)),
            pl.BlockSpec((1, MOE_BLOCK, N_EXPERTS), lambda b, ut: (b, 0, 0)),
            pl.BlockSpec((1, 1, SLOT_MAX), lambda b, ut: (b, 0, 0)),
            pl.BlockSpec((1, 1, SLOT_MAX), lambda b, ut: (b, 0, 0)),
        ],
        out_specs=pl.BlockSpec((1, SLOT_MAX, D_MODEL), lambda b, ut: (b, 0, 0)),
    )
    return pl.pallas_call(
        _permute_kernel,
        grid_spec=grid_spec,
        out_shape=jax.ShapeDtypeStruct((nb, SLOT_MAX, D_MODEL), BF16),
        compiler_params=pltpu.CompilerParams(
            dimension_semantics=("arbitrary",), vmem_limit_bytes=VMEM_LIMIT),
        name="moe_permute",
    )(plan["used_tiles_b"], h2, plan["rank_bt"], plan["e_slot"], plan["r_slot"])


def _ffn_kernel(src_ref, et_ref, nu_ref, *refs):
    n_units = TILES_PER_STEP * UNITS_PER_TILE
    x_refs = refs[:n_units]
    w_refs = refs[n_units:n_units + 2 * TILES_PER_STEP]
    y_out = refs[n_units + 2 * TILES_PER_STEP]
    scratch = refs[n_units + 2 * TILES_PER_STEP + 1:]
    wb_refs, xt_ref = scratch[:2 * TILES_PER_STEP], scratch[2 * TILES_PER_STEP]
    step = pl.program_id(0)

    @pl.when(step * TILES_PER_STEP < nu_ref[0])
    def _():
        for t in range(TILES_PER_STEP):
            j = step * TILES_PER_STEP + t
            @pl.when((step == 0) | (et_ref[j] != et_ref[jnp.maximum(j - TILES_PER_STEP, 0)]))
            def _():
                wb_refs[2 * t][...] = w_refs[2 * t][0].astype(BF16)
                wb_refs[2 * t + 1][...] = w_refs[2 * t + 1][0].astype(BF16)

        for u in range(n_units):
            xt_ref[u * UNIT:(u + 1) * UNIT, :] = x_refs[u][0]
        for t in range(TILES_PER_STEP):
            rows = slice(t * FFN_TILE, (t + 1) * FFN_TILE)
            gu = _dot(xt_ref[rows, :], wb_refs[2 * t][...])
            act = _silu(gu[:, :EXPERT_FF]) * gu[:, EXPERT_FF:]
            y_out[rows, :] = _dot(act.astype(BF16), wb_refs[2 * t + 1][...]).astype(BF16)

    @pl.when(step * TILES_PER_STEP >= nu_ref[0])
    def _():
        y_out[...] = jnp.zeros(y_out.shape, BF16)


def _expert_ffn(plan, xp, w_gu, w_d):
    xp_units = xp.reshape(-1, UNIT, D_MODEL)
    n_units = TILES_PER_STEP * UNITS_PER_TILE
    assert N_TILES_MAX % TILES_PER_STEP == 0

    def unit_spec(u):
        return pl.BlockSpec((1, UNIT, D_MODEL),
                            lambda s, src, et, nu, u=u: (src[s * n_units + u], 0, 0))

    def weight_specs(t):
        return [pl.BlockSpec((1, D_MODEL, 2 * EXPERT_FF),
                             lambda s, src, et, nu, t=t: (et[s * TILES_PER_STEP + t], 0, 0)),
                pl.BlockSpec((1, EXPERT_FF, D_MODEL),
                             lambda s, src, et, nu, t=t: (et[s * TILES_PER_STEP + t], 0, 0))]

    grid_spec = pltpu.PrefetchScalarGridSpec(
        num_scalar_prefetch=3,
        grid=(N_TILES_MAX // TILES_PER_STEP,),
        in_specs=[unit_spec(u) for u in range(n_units)]
        + [spec for t in range(TILES_PER_STEP) for spec in weight_specs(t)],
        out_specs=pl.BlockSpec((TILES_PER_STEP * FFN_TILE, D_MODEL), lambda s, src, et, nu: (s, 0)),
        scratch_shapes=[
            pltpu.VMEM((D_MODEL, 2 * EXPERT_FF), BF16) if i % 2 == 0 else pltpu.VMEM((EXPERT_FF, D_MODEL), BF16)
            for i in range(2 * TILES_PER_STEP)
        ] + [pltpu.VMEM((TILES_PER_STEP * FFN_TILE, D_MODEL), BF16)],
    )
    return pl.pallas_call(
        _ffn_kernel,
        grid_spec=grid_spec,
        out_shape=jax.ShapeDtypeStruct((N_TILES_MAX * FFN_TILE, D_MODEL), BF16),
        compiler_params=pltpu.CompilerParams(
            dimension_semantics=("arbitrary",), vmem_limit_bytes=VMEM_LIMIT),
        name="moe_expert_ffn",
    )(plan["src"], plan["e_tile"], plan["n_used"], *([xp_units] * n_units),
      *([w_gu, w_d] * TILES_PER_STEP))


def _combine_kernel(ysrc_ref, ug_ref, *refs, final):
    y_refs = refs[:GROUP_UNITS]
    (rank_ref, comb_ref, er_ref, rr_ref, h_ref, x1_ref, mod_ref, wsgu_ref, wsd_ref, fg_ref,
     o_ref, acc_ref, yt_ref) = refs[GROUP_UNITS:]
    b = pl.program_id(0)
    g = pl.program_id(1)

    @pl.when(g == 0)
    def _():
        gu = _dot(h_ref[...], wsgu_ref[...])
        act = _silu(gu[:, :EXPERT_FF]) * gu[:, EXPERT_FF:]
        acc_ref[...] = _dot(act.astype(BF16), wsd_ref[...])

    @pl.when(g < ug_ref[b])
    def _():
        sub_e = lax.broadcasted_iota(jnp.int32, (N_EXPERTS, GROUP_SLOTS), 0)
        onehot_e = jnp.where(sub_e == er_ref[0], 1.0, 0.0).astype(BF16)
        slot_rank = _dot(rank_ref[0], onehot_e)
        slot_w = _dot(comb_ref[0], onehot_e)
        wc = jnp.where(slot_rank == rr_ref[0], slot_w, 0.0).astype(BF16)
        for u in range(GROUP_UNITS):
            yt_ref[u * UNIT:(u + 1) * UNIT, :] = y_refs[u][0]
        acc_ref[...] += _dot(wc, yt_ref[...])

    @pl.when(g == GROUPS_PER_BLOCK - 1)
    def _():
        x2 = x1_ref[...] + mod_ref[0, 5:6, :] * acc_ref[...]
        if final:
            ms = jnp.mean(x2 * x2, axis=-1, keepdims=True)
            x2 = x2 * lax.rsqrt(ms + EPS) * fg_ref[...]
        o_ref[...] = x2


def _combine(plan, y_sorted, comb_bt, h2, x1, mod_tile, ws_gu, ws_d, final_g, final):
    n_tok = h2.shape[0]
    nb = n_tok // MOE_BLOCK
    y_units = y_sorted.reshape(-1, UNIT, D_MODEL)

    def unit_spec(u):
        return pl.BlockSpec(
            (1, UNIT, D_MODEL),
            lambda b, g, ysrc, ug, u=u: (ysrc[(b * GROUPS_PER_BLOCK + g) * GROUP_UNITS + u], 0, 0))

    def full(a):
        return pl.BlockSpec(a.shape, lambda b, g, ysrc, ug: (0,) * a.ndim)

    grid_spec = pltpu.PrefetchScalarGridSpec(
        num_scalar_prefetch=2,
        grid=(nb, GROUPS_PER_BLOCK),
        in_specs=[unit_spec(u) for u in range(GROUP_UNITS)] + [
            pl.BlockSpec((1, MOE_BLOCK, N_EXPERTS), lambda b, g, ysrc, ug: (b, 0, 0)),
            pl.BlockSpec((1, MOE_BLOCK, N_EXPERTS), lambda b, g, ysrc, ug: (b, 0, 0)),
            pl.BlockSpec((1, 1, GROUP_SLOTS), lambda b, g, ysrc, ug: (b, 0, g)),
            pl.BlockSpec((1, 1, GROUP_SLOTS), lambda b, g, ysrc, ug: (b, 0, g)),
            pl.BlockSpec((MOE_BLOCK, D_MODEL), lambda b, g, ysrc, ug: (b, 0)),
            pl.BlockSpec((MOE_BLOCK, D_MODEL), lambda b, g, ysrc, ug: (b, 0)),
            pl.BlockSpec((1, 8, D_MODEL), lambda b, g, ysrc, ug: (b, 0, 0)),
            full(ws_gu), full(ws_d), full(final_g),
        ],
        out_specs=pl.BlockSpec((MOE_BLOCK, D_MODEL), lambda b, g, ysrc, ug: (b, 0)),
        scratch_shapes=[
            pltpu.VMEM((MOE_BLOCK, D_MODEL), F32),
            pltpu.VMEM((GROUP_SLOTS, D_MODEL), BF16),
        ],
    )
    return pl.pallas_call(
        functools.partial(_combine_kernel, final=final),
        grid_spec=grid_spec,
        out_shape=jax.ShapeDtypeStruct((n_tok, D_MODEL), F32),
        compiler_params=pltpu.CompilerParams(
            dimension_semantics=("arbitrary", "arbitrary"), vmem_limit_bytes=VMEM_LIMIT),
        name="moe_combine",
    )(plan["ysrc"], plan["used_groups_b"], *([y_units] * GROUP_UNITS), plan["rank_bt"], comb_bt,
      plan["e_slot"], plan["r_slot"], h2, x1, mod_tile, ws_gu, ws_d, final_g)


def _level_ids():
    t = np.arange(CHUNK)
    x = t[:, None] ^ t[None, :]
    lvl = np.zeros((CHUNK, CHUNK), np.int32)
    nz = x > 0
    lvl[nz] = np.floor(np.log2(x[nz])).astype(np.int32) + 1
    past_f = t[:, None] >= t[None, :]
    return jnp.asarray(np.where(past_f, lvl, -1)), jnp.asarray(np.where(past_f.T, lvl, -1))


def kernel(x_prompt, x_sample, state_hgrn, c, c_ctx, w_ada, b_ada, norm1_g, w_in, hgrn_lb_logits, hgrn_norm_g, w_proj_hgrn, conv_dw_w, conv_dw_b, conv_norm_g, conv_norm_b, w_proj_conv, w_out, norm2_g, w_router, router_bias, w_expert_gate_up, w_expert_down, w_shared_gate_up, w_shared_down, final_norm_g):
    n_ctx, ctx_len, d = x_prompt.shape
    n_lat, lat_len, _ = x_sample.shape
    depth = w_ada.shape[0]
    assert d == D_MODEL and ctx_len * 4 == STEP and lat_len == STEP and n_ctx % 4 == 0
    ctx_steps = n_ctx * ctx_len // STEP
    n_steps = ctx_steps + n_lat
    n_tok = n_steps * STEP
    assert n_tok == N_TOKENS and MOE_BLOCK == POST_TILE
    tiles_per_step = STEP // POST_TILE

    x_all = jnp.concatenate([x_prompt.reshape(-1, d), x_sample.reshape(-1, d)], axis=0)

    cc = jnp.zeros((8, d), F32).at[:n_lat].set(c.astype(F32)).at[n_lat].set(c_ctx.astype(F32))
    mod = _ada(cc, w_ada, b_ada).reshape(depth, 8, N_MOD, d)
    step_src = np.array([n_lat] * ctx_steps + list(range(n_lat)))
    mod_step = jnp.pad(mod[:, step_src], ((0, 0), (0, 0), (0, 8 - N_MOD), (0, 0)))
    mod_tile = jnp.repeat(mod_step, tiles_per_step, axis=1)

    nc = jnp.asarray([ctx_len // CHUNK] * ctx_steps + [lat_len // CHUNK] * n_lat, jnp.int32)
    tile_pos = np.arange(n_lat * tiles_per_step) % tiles_per_step
    pv = jnp.asarray(np.concatenate([np.zeros(ctx_steps * tiles_per_step), tile_pos > 0]), jnp.int32)
    nv = jnp.asarray(np.concatenate([np.zeros(ctx_steps * tiles_per_step), tile_pos < tiles_per_step - 1]), jnp.int32)

    lbv = jnp.cumsum(jax.nn.softmax(hgrn_lb_logits.astype(F32), axis=0), axis=0)
    lbv = (lbv - lbv[:1]).reshape(depth, 2, N_HEADS, HEAD_DIM)
    lvl_f, lvl_b = _level_ids()

    xs = x_all
    ctx_states = []
    for l in range(depth):
        wl = w_in[l]
        hk = N_HEADS * HEAD_DIM
        w_u = wl[:, 5 * hk:5 * hk + 2 * CONV_DIM].astype(BF16)
        w_gz = wl[:, 5 * hk + 2 * CONV_DIM:].astype(BF16)
        lb = lbv[l]
        gp = jnp.stack([jnp.log(lb[0]), jnp.log1p(-lb[0]), 1.0 - lb[0],
                        jnp.log(lb[1]), jnp.log1p(-lb[1]), 1.0 - lb[1],
                        hgrn_norm_g[l].reshape(N_HEADS, HEAD_DIM).astype(F32),
                        jnp.zeros((N_HEADS, HEAD_DIM), F32)], axis=1)
        s0 = jnp.concatenate([jnp.zeros((ctx_steps, 2, N_HEADS, HEAD_DIM, HEAD_DIM), F32),
                              state_hgrn[:, l].astype(F32)], axis=0)
        g1 = norm1_g[l].reshape(1, d).astype(F32)

        o_all, states = _scan(nc, xs, mod_step[l], g1, wl, gp, s0, lvl_f, lvl_b)
        ctx_states.append(states[:ctx_steps].reshape(n_ctx, 2, N_HEADS, HEAD_DIM, HEAD_DIM))

        cw = jnp.pad(conv_dw_w[l].astype(F32), ((0, 1), (0, 0)))
        x1, h2, logits_t = _post(
            pv, nv, xs, mod_tile[l], o_all, w_u, w_gz, w_proj_hgrn[l].astype(BF16),
            w_proj_conv[l].astype(BF16), w_out[l].astype(BF16), w_router[l].T.astype(F32), g1, cw,
            conv_dw_b[l].reshape(1, -1).astype(F32), conv_norm_g[l].reshape(1, -1).astype(F32),
            conv_norm_b[l].reshape(1, -1).astype(F32), norm2_g[l].reshape(1, d).astype(F32))

        comb_t, chosen_t = _route(logits_t, router_bias[l].reshape(N_EXPERTS, 1).astype(F32))
        plan = _routing_plan(chosen_t)
        comb_bt = comb_t.reshape(N_EXPERTS, n_tok // MOE_BLOCK, MOE_BLOCK).transpose(1, 2, 0).astype(BF16)
        xp = _permute(plan, h2)
        y_sorted = _expert_ffn(plan, xp, w_expert_gate_up[l], w_expert_down[l])
        xs = _combine(plan, y_sorted, comb_bt, h2, x1, mod_tile[l], w_shared_gate_up[l].astype(BF16),
                      w_shared_down[l].astype(BF16), final_norm_g.reshape(1, d).astype(F32),
                      final=(l == depth - 1))

    n_ctx_tok = n_ctx * ctx_len
    y_prompt = xs[:n_ctx_tok].reshape(x_prompt.shape).astype(x_prompt.dtype)
    y_sample = xs[n_ctx_tok:].reshape(x_sample.shape).astype(x_sample.dtype)
    new_state = jnp.stack(ctx_states, axis=1).astype(x_prompt.dtype)
    return (y_prompt, y_sample, new_state)
```

```python
import functools

import numpy as np
import jax
import jax.numpy as jnp
from jax import lax
from jax.experimental import pallas as pl
from jax.experimental.pallas import tpu as pltpu

F32 = jnp.float32
BF16 = jnp.bfloat16
HIGHEST = lax.Precision.HIGHEST

D_MODEL = 1024
N_HEADS = 8
HEAD_DIM = 128
CONV_DIM = 512
CONV_WIDTH = 31
CONV_HALO = 16
N_EXPERTS = 64
N_GROUPS = 8
GROUP_SIZE = N_EXPERTS // N_GROUPS
TOPK_GROUPS = 4
TOP_K = 8
EXPERT_FF = 256
ROUTED_SCALE = 2.5
N_MOD = 6
EPS = 1e-6

CHUNK = 128
N_LEVELS = 7
STEP = 1024
POST_TILE = 256
MOE_BLOCK = 256
UNIT = 16
FFN_TILE = 256
UNITS_PER_TILE = FFN_TILE // UNIT
TILES_PER_STEP = 2
SLOT_MAX = MOE_BLOCK * TOP_K + N_EXPERTS * UNIT
UNITS_PER_BLOCK = SLOT_MAX // UNIT
GROUP_UNITS = 32
GROUP_SLOTS = GROUP_UNITS * UNIT
GROUPS_PER_BLOCK = UNITS_PER_BLOCK // GROUP_UNITS
N_TOKENS = 8192
N_TILES_MAX = (N_TOKENS * TOP_K // UNIT + (N_TOKENS // MOE_BLOCK) * N_EXPERTS) // UNITS_PER_TILE + N_EXPERTS
HEAD_COLS = 5 * HEAD_DIM
VMEM_LIMIT = 52 * 1024 * 1024


def _dot(a, b):
    return jnp.dot(a, b, preferred_element_type=F32)


def _dot_nt(a, b):
    return lax.dot_general(a, b, (((1,), (1,)), ((), ())), preferred_element_type=F32)


def _dot_tn(a, b):
    return lax.dot_general(a, b, (((0,), (0,)), ((), ())), preferred_element_type=F32)


def _sigmoid(x):
    return 1.0 / (1.0 + jnp.exp(-x))


def _silu(x):
    return x * _sigmoid(x)


def _rms_mod(x, g, scale, shift):
    ms = jnp.mean(x * x, axis=-1, keepdims=True)
    return x * lax.rsqrt(ms + EPS) * g * (1.0 + scale) + shift


def _ada_kernel(c_ref, w_ref, b_ref, o_ref):
    cc = c_ref[...]
    o_ref[0] = jnp.dot(_silu(cc), w_ref[0], preferred_element_type=F32, precision=HIGHEST) + b_ref[0]


def _ada(cc, w_ada, b_ada):
    depth, d, n = w_ada.shape
    tn = 1536
    return pl.pallas_call(
        _ada_kernel,
        grid=(depth, n // tn),
        in_specs=[
            pl.BlockSpec((8, d), lambda l, j: (0, 0)),
            pl.BlockSpec((1, d, tn), lambda l, j: (l, 0, j)),
            pl.BlockSpec((1, 1, tn), lambda l, j: (l, 0, j)),
        ],
        out_specs=pl.BlockSpec((1, 8, tn), lambda l, j: (l, 0, j)),
        out_shape=jax.ShapeDtypeStruct((depth, 8, n), F32),
        compiler_params=pltpu.CompilerParams(vmem_limit_bytes=VMEM_LIMIT),
        name="ada_mod",
    )(cc, w_ada, b_ada.reshape(depth, 1, n))


def _forget_gate(z, log_lb, log1m_lb, one_m_lb):
    e = jnp.exp(-jnp.abs(z))
    r = 1.0 / (1.0 + e)
    log_sig = jnp.minimum(z, 0.0) - jnp.log(1.0 + e)
    k = one_m_lb * jnp.where(z > 0, e * r, r)
    b = log1m_lb + log_sig
    log_f = jnp.maximum(log_lb, b) + jnp.log(1.0 + jnp.exp(-jnp.abs(log_lb - b)))
    return log_f, k


def _level_reference(cum_ref, blk, fwd):
    half = blk // 2
    pieces = []
    if blk >= 8:
        for i in range(CHUNK // blk):
            row = i * blk + (half - 1 if fwd else half)
            pieces.append(jnp.broadcast_to(cum_ref[row:row + 1, :], (blk, HEAD_DIM)))
    else:
        sub = lax.broadcasted_iota(jnp.int32, (8, HEAD_DIM), 0)
        for i in range(CHUNK // 8):
            lo_row = 8 * i + (1 if fwd else 2)
            hi_row = 8 * i + (5 if fwd else 6)
            lo = jnp.broadcast_to(cum_ref[lo_row:lo_row + 1, :], (8, HEAD_DIM))
            hi = jnp.broadcast_to(cum_ref[hi_row:hi_row + 1, :], (8, HEAD_DIM))
            pieces.append(jnp.where(sub >= 4, hi, lo))
    return pieces[0] if len(pieces) == 1 else jnp.concatenate(pieces, axis=0)


def _chunk_step(q, k, v, log_f, st, lvl, cum_ref, fwd):
    r_idx = lax.broadcasted_iota(jnp.int32, (CHUNK, CHUNK), 0)
    c_idx = lax.broadcasted_iota(jnp.int32, (CHUNK, CHUNK), 1)
    tri = jnp.where((r_idx >= c_idx) if fwd else (r_idx <= c_idx), 1.0, 0.0).astype(F32)
    cum = jnp.dot(tri, log_f, preferred_element_type=F32, precision=HIGHEST)
    cum_ref[...] = cum

    scores = jnp.where(lvl == 0, _dot_nt(q.astype(BF16), k.astype(BF16)), 0.0)
    for lev in range(1, N_LEVELS + 1):
        if lev == 1:
            qe = q * jnp.exp(log_f)
            ke = k
        else:
            e = jnp.exp(-jnp.abs(cum - _level_reference(cum_ref, 1 << lev, fwd)))
            qe = q * e
            ke = k * e
        s = _dot_nt(qe.astype(BF16), ke.astype(BF16))
        scores = jnp.where(lvl == lev, s, scores)

    total = cum_ref[CHUNK - 1:CHUNK, :] if fwd else cum_ref[0:1, :]
    st_b = st.astype(BF16)
    o = _dot(scores.astype(BF16), v.astype(BF16)) + _dot_nt((q * jnp.exp(cum)).astype(BF16), st_b)
    k_st = k * jnp.exp(total - cum)
    st_new = st * jnp.exp(total) + _dot_tn(v.astype(BF16), k_st.astype(BF16))
    return o, st_new


def _scan_kernel(nc_ref, x_ref, mod_ref, g1_ref, wq_ref, wv_ref, wf_ref, wb_ref, wg_ref, gp_ref, s0_ref,
                 lvlf_ref, lvlb_ref, o_out, st_out, h_ref, w_ref, z_ref, of_ref, ob_ref, cumf_ref, cumb_ref):
    step = pl.program_id(0)
    head = pl.program_id(1)
    n_seq_chunks = nc_ref[step]
    n_chunks = STEP // CHUNK

    @pl.when(head == 0)
    def _():
        g1 = g1_ref[...]
        shift = mod_ref[0, 0:1, :]
        scale = mod_ref[0, 1:2, :]

        def body(i, carry):
            rows = pl.ds(pl.multiple_of(i * CHUNK, CHUNK), CHUNK)
            h_ref[rows, :] = _rms_mod(x_ref[rows, :], g1, scale, shift).astype(BF16)
            return carry

        lax.fori_loop(0, n_chunks, body, 0)

    for j, wj_ref in enumerate((wq_ref, wv_ref, wf_ref, wb_ref, wg_ref)):
        w_ref[:, j * HEAD_DIM:(j + 1) * HEAD_DIM] = wj_ref[...].astype(BF16)
    z_ref[...] = _dot(h_ref[...], w_ref[...])
    st_out[...] = jnp.zeros(st_out.shape, F32)
    st0 = (s0_ref[0, 0, 0].T, s0_ref[0, 1, 0].T)

    def one_direction(c, st, fwd):
        d = 0 if fwd else 1
        zcol = (2 if fwd else 3) * HEAD_DIM
        pos = lax.rem(c, n_seq_chunks)
        first = (pos == 0) if fwd else (pos == n_seq_chunks - 1)
        rows = pl.ds(c * CHUNK, CHUNK)
        q = z_ref[rows, 0:HEAD_DIM]
        v = z_ref[rows, HEAD_DIM:2 * HEAD_DIM]
        log_f, k = _forget_gate(z_ref[rows, zcol:zcol + HEAD_DIM], gp_ref[0, 3 * d:3 * d + 1, :],
                                gp_ref[0, 3 * d + 1:3 * d + 2, :], gp_ref[0, 3 * d + 2:3 * d + 3, :])
        st = jnp.where(first, st0[d], st)
        o, st = _chunk_step(q, k, v, log_f, st, (lvlf_ref if fwd else lvlb_ref)[...],
                            cumf_ref if fwd else cumb_ref, fwd)

        (of_ref if fwd else ob_ref)[rows, :] = o
        st_out[0, lax.div(c, n_seq_chunks), d, 0] = st.T
        return st

    carry = st0
    for i in range(n_chunks):
        carry = (one_direction(i, carry[0], True), one_direction(n_chunks - 1 - i, carry[1], False))

    norm_g = gp_ref[0, 6:7, :]

    def finish(i, carry):
        rows = pl.ds(pl.multiple_of(i * CHUNK, CHUNK), CHUNK)
        o = of_ref[rows, :] + ob_ref[rows, :]
        o = o * lax.rsqrt(jnp.mean(o * o, axis=-1, keepdims=True) + EPS)
        og = z_ref[rows, 4 * HEAD_DIM:5 * HEAD_DIM]
        o_out[rows, :] = (o * norm_g * _silu(og)).astype(BF16)
        return carry

    lax.fori_loop(0, n_chunks, finish, 0)


def _scan(nc, x_all, mod_step, g1, w_in_l, gp, s0, lvl_f, lvl_b):
    n_tok = x_all.shape[0]
    n_steps = n_tok // STEP
    grid_spec = pltpu.PrefetchScalarGridSpec(
        num_scalar_prefetch=1,
        grid=(n_steps, N_HEADS),
        in_specs=[
            pl.BlockSpec((STEP, D_MODEL), lambda s, h, nc: (s, 0)),
            pl.BlockSpec((1, 8, D_MODEL), lambda s, h, nc: (s, 0, 0)),
            pl.BlockSpec((1, D_MODEL), lambda s, h, nc: (0, 0)),
        ] + [
            pl.BlockSpec((D_MODEL, HEAD_DIM), lambda s, h, nc, j=j: (0, j * N_HEADS + h)) for j in range(5)
        ] + [
            pl.BlockSpec((1, 8, HEAD_DIM), lambda s, h, nc: (h, 0, 0)),
            pl.BlockSpec((1, 2, 1, HEAD_DIM, HEAD_DIM), lambda s, h, nc: (s, 0, h, 0, 0)),
            pl.BlockSpec((CHUNK, CHUNK), lambda s, h, nc: (0, 0)),
            pl.BlockSpec((CHUNK, CHUNK), lambda s, h, nc: (0, 0)),
        ],
        out_specs=[
            pl.BlockSpec((STEP, HEAD_DIM), lambda s, h, nc: (s, h)),
            pl.BlockSpec((1, 4, 2, 1, HEAD_DIM, HEAD_DIM), lambda s, h, nc: (s, 0, 0, h, 0, 0)),
        ],
        scratch_shapes=[
            pltpu.VMEM((STEP, D_MODEL), BF16),
            pltpu.VMEM((D_MODEL, HEAD_COLS), BF16),
            pltpu.VMEM((STEP, HEAD_COLS), F32),
            pltpu.VMEM((STEP, HEAD_DIM), F32),
            pltpu.VMEM((STEP, HEAD_DIM), F32),
            pltpu.VMEM((CHUNK, HEAD_DIM), F32),
            pltpu.VMEM((CHUNK, HEAD_DIM), F32),
        ],
    )
    return pl.pallas_call(
        _scan_kernel,
        grid_spec=grid_spec,
        out_shape=[
            jax.ShapeDtypeStruct((n_tok, D_MODEL), BF16),
            jax.ShapeDtypeStruct((n_steps, 4, 2, N_HEADS, HEAD_DIM, HEAD_DIM), F32),
        ],
        compiler_params=pltpu.CompilerParams(
            dimension_semantics=("arbitrary", "arbitrary"), vmem_limit_bytes=VMEM_LIMIT),
        name="hgrn_scan",
    )(nc, x_all, mod_step, g1, *([w_in_l] * 5), gp, s0, lvl_f, lvl_b)


def _post_kernel(pv_ref, nv_ref, x_ref, xp_ref, xn_ref, mod_ref, o_ref, wu_f32, wgz_f32, wph_f32,
                 wpc_f32, wo_f32, wr_ref, g1_ref, cw_ref, cb_ref, lg_ref, lb_ref, g2_ref,
                 x1_out, h2_out, lg_out, cu_ref, wu_ref, wgz_ref, wph_ref, wpc_ref, wo_ref):
    i = pl.program_id(0)

    @pl.when(i == 0)
    def _():
        for dst, src in ((wu_ref, wu_f32), (wgz_ref, wgz_f32), (wph_ref, wph_f32),
                         (wpc_ref, wpc_f32), (wo_ref, wo_f32)):
            dst[...] = src[...].astype(BF16)

    g1 = g1_ref[...]
    shift1 = mod_ref[0, 0:1, :]
    scale1 = mod_ref[0, 1:2, :]
    gate1 = mod_ref[0, 2:3, :]
    shift2 = mod_ref[0, 3:4, :]
    scale2 = mod_ref[0, 4:5, :]

    def glu(xv):
        h = _rms_mod(xv, g1, scale1, shift1).astype(BF16)
        u = _dot(h, wu_ref[...])
        return h, u[:, :CONV_DIM] * _sigmoid(u[:, CONV_DIM:])

    x = x_ref[...]
    h, glu_mid = glu(x)
    _, glu_prev = glu(xp_ref[...])
    _, glu_next = glu(xn_ref[...])
    cu_ref[0:CONV_HALO, :] = glu_prev * pv_ref[i].astype(F32)
    cu_ref[CONV_HALO:CONV_HALO + POST_TILE, :] = glu_mid
    cu_ref[CONV_HALO + POST_TILE:, :] = glu_next * nv_ref[i].astype(F32)

    off = CONV_HALO - CONV_WIDTH // 2
    acc = jnp.zeros((POST_TILE, CONV_DIM), F32) + cb_ref[...]
    for j in range(CONV_WIDTH):
        acc = acc + cu_ref[off + j:off + j + POST_TILE, :] * cw_ref[j:j + 1, :]
    mu = jnp.mean(acc, axis=-1, keepdims=True)
    cen = acc - mu
    var = jnp.mean(cen * cen, axis=-1, keepdims=True)
    cv = cen * lax.rsqrt(var + EPS) * lg_ref[...] + lb_ref[...]
    y_b = _dot(_silu(cv).astype(BF16), wpc_ref[...])

    y_a = _dot(o_ref[...], wph_ref[...])
    gz = _sigmoid(_dot(h, wgz_ref[...]))
    merged = gz[:, :D_MODEL] * y_a + gz[:, D_MODEL:] * y_b
    x1 = x + gate1 * _dot(merged.astype(BF16), wo_ref[...])
    x1_out[...] = x1
    h2 = _rms_mod(x1, g2_ref[...], scale2, shift2)
    h2_out[...] = h2.astype(BF16)
    lg_out[...] = lax.dot_general(wr_ref[...], h2, (((1,), (1,)), ((), ())),
                                  preferred_element_type=F32, precision=HIGHEST)


def _post(pv, nv, x_all, mod_tile, o_all, w_in_l, w_ph, w_pc, w_o, w_rt, g1, cw, cb, lg, lb, g2):
    n_tok = x_all.shape[0]
    n_tiles = n_tok // POST_TILE
    halo_per_tile = POST_TILE // CONV_HALO
    n_halo_blocks = n_tok // CONV_HALO
    glu_cols, gate_cols = 2 * CONV_DIM, 2 * D_MODEL
    glu_start = 5 * N_HEADS * HEAD_DIM
    assert glu_start % glu_cols == 0 and (glu_start + glu_cols) % gate_cols == 0

    def full(a):
        return pl.BlockSpec(a.shape, lambda i, pv, nv: (0,) * a.ndim)

    def resident(shape, index):
        return pl.BlockSpec(shape, lambda i, pv, nv: index, pipeline_mode=pl.Buffered(1))

    grid_spec = pltpu.PrefetchScalarGridSpec(
        num_scalar_prefetch=2,
        grid=(n_tiles,),
        in_specs=[
            pl.BlockSpec((POST_TILE, D_MODEL), lambda i, pv, nv: (i, 0)),
            pl.BlockSpec((CONV_HALO, D_MODEL), lambda i, pv, nv: (jnp.maximum(i * halo_per_tile - 1, 0), 0)),
            pl.BlockSpec((CONV_HALO, D_MODEL),
                         lambda i, pv, nv: (jnp.minimum((i + 1) * halo_per_tile, n_halo_blocks - 1), 0)),
            pl.BlockSpec((1, 8, D_MODEL), lambda i, pv, nv: (i, 0, 0)),
            pl.BlockSpec((POST_TILE, D_MODEL), lambda i, pv, nv: (i, 0)),
            resident((D_MODEL, glu_cols), (0, glu_start // glu_cols)),
            resident((D_MODEL, gate_cols), (0, (glu_start + glu_cols) // gate_cols)),
            resident(w_ph.shape, (0, 0)), resident(w_pc.shape, (0, 0)), resident(w_o.shape, (0, 0)),
            full(w_rt), full(g1), full(cw), full(cb), full(lg), full(lb), full(g2),
        ],
        out_specs=[
            pl.BlockSpec((POST_TILE, D_MODEL), lambda i, pv, nv: (i, 0)),
            pl.BlockSpec((POST_TILE, D_MODEL), lambda i, pv, nv: (i, 0)),
            pl.BlockSpec((N_EXPERTS, POST_TILE), lambda i, pv, nv: (0, i)),
        ],
        scratch_shapes=[
            pltpu.VMEM((POST_TILE + 2 * CONV_HALO, CONV_DIM), F32),
            pltpu.VMEM((D_MODEL, glu_cols), BF16), pltpu.VMEM((D_MODEL, gate_cols), BF16),
            pltpu.VMEM(w_ph.shape, BF16), pltpu.VMEM(w_pc.shape, BF16), pltpu.VMEM(w_o.shape, BF16),
        ],
    )
    return pl.pallas_call(
        _post_kernel,
        grid_spec=grid_spec,
        out_shape=[
            jax.ShapeDtypeStruct((n_tok, D_MODEL), F32),
            jax.ShapeDtypeStruct((n_tok, D_MODEL), BF16),
            jax.ShapeDtypeStruct((N_EXPERTS, n_tok), F32),
        ],
        compiler_params=pltpu.CompilerParams(
            dimension_semantics=("arbitrary",), vmem_limit_bytes=VMEM_LIMIT),
        name="post_mixer",
    )(pv, nv, x_all, x_all, x_all, mod_tile, o_all, w_in_l, w_in_l, w_ph, w_pc, w_o, w_rt, g1, cw, cb, lg, lb, g2)


def _route_kernel(lg_ref, bias_ref, comb_out, chosen_out):
    n = lg_ref.shape[1]
    scores = _sigmoid(lg_ref[...])
    sel = scores + bias_ref[...]
    neg = jnp.float32(-jnp.inf)

    sel3 = sel.reshape(N_GROUPS, GROUP_SIZE, n)
    m1 = jnp.max(sel3, axis=1, keepdims=True)
    is_m1 = sel3 == m1
    n_m1 = jnp.sum(is_m1.astype(F32), axis=1, keepdims=True)
    m2 = jnp.max(jnp.where(is_m1, neg, sel3), axis=1, keepdims=True)
    grp = (m1 + jnp.where(n_m1 > 1.5, m1, m2)).reshape(N_GROUPS, n)

    gidx = lax.broadcasted_iota(jnp.int32, (N_GROUPS, n), 0)
    rank = jnp.zeros((N_GROUPS, n), F32)
    for g in range(N_GROUPS):
        other = grp[g:g + 1, :]
        ahead = (other > grp) | ((other == grp) & (g < gidx))
        rank = rank + ahead.astype(F32)
    keep_g = rank < TOPK_GROUPS - 0.5
    keep = jnp.broadcast_to(keep_g.reshape(N_GROUPS, 1, n), (N_GROUPS, GROUP_SIZE, n)).reshape(N_EXPERTS, n)
    cand = jnp.where(keep, sel, neg)

    eidx = lax.broadcasted_iota(jnp.int32, (N_EXPERTS, n), 0)
    chosen = jnp.zeros((N_EXPERTS, n), F32)
    for _ in range(TOP_K):
        best = jnp.max(cand, axis=0, keepdims=True)
        first = jnp.min(jnp.where(cand == best, eidx, N_EXPERTS), axis=0, keepdims=True)
        hit = eidx == first
        chosen = jnp.where(hit, 1.0, chosen)
        cand = jnp.where(hit, neg, cand)
    w = scores * chosen
    comb_out[...] = w / jnp.sum(w, axis=0, keepdims=True) * ROUTED_SCALE
    chosen_out[...] = chosen


def _route(logits_t, bias):
    n_tok = logits_t.shape[1]
    tile = 512
    return pl.pallas_call(
        _route_kernel,
        grid=(n_tok // tile,),
        in_specs=[
            pl.BlockSpec((N_EXPERTS, tile), lambda i: (0, i)),
            pl.BlockSpec((N_EXPERTS, 1), lambda i: (0, 0)),
        ],
        out_specs=[pl.BlockSpec((N_EXPERTS, tile), lambda i: (0, i)),
                   pl.BlockSpec((N_EXPERTS, tile), lambda i: (0, i))],
        out_shape=[jax.ShapeDtypeStruct((N_EXPERTS, n_tok), F32),
                   jax.ShapeDtypeStruct((N_EXPERTS, n_tok), F32)],
        name="router",
    )(logits_t, bias)


def _count_le(bounds, idx):
    return jnp.sum(jnp.where(bounds <= idx, 1.0, 0.0), axis=0, keepdims=True)


def _pick_row(onehot, col):
    return jnp.sum(jnp.where(onehot, col, 0.0), axis=0, keepdims=True)


def _plan_kernel(ch_ref, rankt_out, es_out, rs_out, src_out, et_out, ysrc_out, misc_out):
    nb = N_TOKENS // MOE_BLOCK
    n_src = N_TILES_MAX * UNITS_PER_TILE
    e_col = lax.broadcasted_iota(jnp.int32, (N_EXPERTS, 1), 0).astype(F32)
    blk_lane = lax.broadcasted_iota(jnp.int32, (N_EXPERTS, 128), 1)
    t_r = lax.broadcasted_iota(jnp.int32, (MOE_BLOCK, MOE_BLOCK), 0)
    t_c = lax.broadcasted_iota(jnp.int32, (MOE_BLOCK, MOE_BLOCK), 1)
    earlier = jnp.where(t_c < t_r, 1.0, 0.0).astype(BF16)
    e_r = lax.broadcasted_iota(jnp.int32, (N_EXPERTS, N_EXPERTS), 0)
    e_c = lax.broadcasted_iota(jnp.int32, (N_EXPERTS, N_EXPERTS), 1)
    upto = jnp.where(e_c <= e_r, 1.0, 0.0).astype(F32)
    slot = lax.broadcasted_iota(jnp.int32, (1, SLOT_MAX), 1).astype(F32)

    def col_cumsum(col):
        wide = jnp.broadcast_to(col, (N_EXPERTS, 128))
        return jnp.dot(upto, wide, preferred_element_type=F32, precision=HIGHEST)[:, 0:1]

    def block_stats(b, carry):
        units_mat, start_mat = carry
        cb = ch_ref[:, pl.ds(pl.multiple_of(b * MOE_BLOCK, MOE_BLOCK), MOE_BLOCK)]
        rank_t = _dot_nt(earlier, cb.astype(BF16))
        cb_t = cb.T
        rankt_out[b] = jnp.where(cb_t > 0, rank_t, -1.0).astype(BF16)
        cnt = jnp.sum(cb, axis=1, keepdims=True)
        units = jnp.floor((cnt + (UNIT - 1)) * (1.0 / UNIT))
        incl = col_cumsum(units)
        start = incl - units
        e_slot = _count_le(incl * UNIT, slot)
        onehot = e_col == e_slot
        r = slot - _pick_row(onehot, start * UNIT)
        valid = (e_slot < N_EXPERTS - 0.5) & (r < _pick_row(onehot, cnt))
        es_out[b] = jnp.where(valid, e_slot, -1.0).astype(jnp.int32)
        rs_out[b] = jnp.where(valid, r, -2.0)
        units_mat = jnp.where(blk_lane == b, units, units_mat)
        start_mat = jnp.where(blk_lane == b, start, start_mat)
        return units_mat, start_mat

    zeros = jnp.zeros((N_EXPERTS, 128), F32)
    units_mat, start_mat = lax.fori_loop(0, nb, block_stats, (zeros, zeros))

    tot = jnp.sum(units_mat, axis=1, keepdims=True)
    tiles_e = jnp.floor((tot + (UNITS_PER_TILE - 1)) * (1.0 / UNITS_PER_TILE))
    incl_t = col_cumsum(tiles_e)
    start_t = incl_t - tiles_e
    n_used = incl_t[N_EXPERTS - 1:N_EXPERTS, :]
    b_r = lax.broadcasted_iota(jnp.int32, (128, 128), 0)
    b_c = lax.broadcasted_iota(jnp.int32, (128, 128), 1)
    before = jnp.where(b_r < b_c, 1.0, 0.0).astype(F32)
    cum_b = jnp.dot(units_mat, before, preferred_element_type=F32, precision=HIGHEST)
    run_pos = start_t * UNITS_PER_TILE + cum_b

    tile_idx = lax.broadcasted_iota(jnp.int32, (1, 512), 1).astype(F32)
    et_out[...] = jnp.minimum(_count_le(incl_t, tile_idx), N_EXPERTS - 1.0).astype(jnp.int32)

    incl_ub_t = (cum_b + units_mat).T
    cum_b_t = cum_b.T
    start_mat_t = start_mat.T
    b_col = lax.broadcasted_iota(jnp.int32, (128, 1), 0)
    chunk = 1024

    def src_chunk(c, carry):
        off = pl.multiple_of(c * chunk, chunk)
        p = (lax.broadcasted_iota(jnp.int32, (1, chunk), 1) + off).astype(F32)
        e_p = jnp.minimum(_count_le(incl_t * UNITS_PER_TILE, p), N_EXPERTS - 1.0)
        onehot_e = e_col == e_p
        q = p - _pick_row(onehot_e, start_t * UNITS_PER_TILE)
        valid = (p < n_used * UNITS_PER_TILE) & (q < _pick_row(onehot_e, tot))
        sel = jnp.where(onehot_e, 1.0, 0.0).astype(F32)
        incl_sel = jnp.dot(incl_ub_t, sel, preferred_element_type=F32, precision=HIGHEST)
        b_p = jnp.sum(jnp.where((incl_sel <= q) & (b_col < nb), 1.0, 0.0), axis=0, keepdims=True)
        b_p = jnp.minimum(b_p, nb - 1.0)
        onehot_b = b_col.astype(F32) == b_p
        cum_sel = jnp.dot(cum_b_t, sel, preferred_element_type=F32, precision=HIGHEST)
        start_sel = jnp.dot(start_mat_t, sel, preferred_element_type=F32, precision=HIGHEST)
        unit = b_p * UNITS_PER_BLOCK + _pick_row(onehot_b, start_sel) + q - _pick_row(onehot_b, cum_sel)
        src_out[:, pl.ds(off, chunk)] = jnp.where(valid, unit, 0.0).astype(jnp.int32)
        return carry

    lax.fori_loop(0, n_src // chunk, src_chunk, 0)

    unit_idx = lax.broadcasted_iota(jnp.int32, (1, 256), 1).astype(F32)
    misc_lane = lax.broadcasted_iota(jnp.int32, (1, 128), 1)

    def block_units(b, used):
        here = blk_lane == b
        units = jnp.sum(jnp.where(here, units_mat, 0.0), axis=1, keepdims=True)
        start = jnp.sum(jnp.where(here, start_mat, 0.0), axis=1, keepdims=True)
        pos = jnp.sum(jnp.where(here, run_pos, 0.0), axis=1, keepdims=True)
        incl = start + units
        e_unit = _count_le(incl, unit_idx)
        onehot = e_col == e_unit
        val = _pick_row(onehot, pos - start) + unit_idx
        ysrc_out[b] = jnp.where(e_unit < N_EXPERTS - 0.5, val, 0.0).astype(jnp.int32)
        return jnp.where(misc_lane == b, incl[N_EXPERTS - 1:N_EXPERTS, :], used)

    used = lax.fori_loop(0, nb, block_units, jnp.zeros((1, 128), F32))
    misc_out[0:1, :] = used.astype(jnp.int32)
    misc_out[1:2, :] = jnp.broadcast_to(n_used, (1, 128)).astype(jnp.int32)
    misc_out[2:8, :] = jnp.zeros((6, 128), jnp.int32)


def _routing_plan(chosen):
    nb = N_TOKENS // MOE_BLOCK
    n_src = N_TILES_MAX * UNITS_PER_TILE
    assert n_src % 1024 == 0 and N_TILES_MAX <= 512 and nb <= 128 and UNITS_PER_BLOCK <= 256
    rank_bt, e_slot, r_slot, src, e_tile, ysrc, misc = pl.pallas_call(
        _plan_kernel,
        out_shape=[
            jax.ShapeDtypeStruct((nb, MOE_BLOCK, N_EXPERTS), BF16),
            jax.ShapeDtypeStruct((nb, 1, SLOT_MAX), jnp.int32),
            jax.ShapeDtypeStruct((nb, 1, SLOT_MAX), F32),
            jax.ShapeDtypeStruct((1, n_src), jnp.int32),
            jax.ShapeDtypeStruct((1, 512), jnp.int32),
            jax.ShapeDtypeStruct((nb, 1, 256), jnp.int32),
            jax.ShapeDtypeStruct((8, 128), jnp.int32),
        ],
        compiler_params=pltpu.CompilerParams(vmem_limit_bytes=VMEM_LIMIT),
        name="moe_plan",
    )(chosen)
    used_units_b = misc[0, :nb]
    return dict(rank_bt=rank_bt, e_slot=e_slot, r_slot=r_slot, src=src.reshape(-1),
                e_tile=e_tile[0, :N_TILES_MAX], n_used=misc[1, :1],
                ysrc=ysrc[:, 0, :UNITS_PER_BLOCK].reshape(-1),
                used_tiles_b=(used_units_b * UNIT + FFN_TILE - 1) // FFN_TILE,
                used_groups_b=(used_units_b + GROUP_UNITS - 1) // GROUP_UNITS)


def _permute_kernel(ut_ref, h_ref, rank_ref, es_ref, rs_ref, xp_out):
    n_used = ut_ref[pl.program_id(0)]
    sub_e = lax.broadcasted_iota(jnp.int32, (N_EXPERTS, FFN_TILE), 0)

    def body(i, carry):
        rows = pl.ds(pl.multiple_of(i * FFN_TILE, FFN_TILE), FFN_TILE)

        @pl.when(i < n_used)
        def _():
            onehot_e = jnp.where(sub_e == es_ref[0, :, rows], 1.0, 0.0).astype(BF16)
            slot_rank = _dot(rank_ref[0], onehot_e)
            pick = jnp.where(slot_rank == rs_ref[0, :, rows], 1.0, 0.0).astype(BF16)
            xp_out[0, rows, :] = _dot_tn(pick, h_ref[...]).astype(BF16)

        @pl.when(i >= n_used)
        def _():
            xp_out[0, rows, :] = jnp.zeros((FFN_TILE, D_MODEL), BF16)

        return carry

    lax.fori_loop(0, SLOT_MAX // FFN_TILE, body, 0)


def _permute(plan, h2):
    nb = h2.shape[0] // MOE_BLOCK
    grid_spec = pltpu.PrefetchScalarGridSpec(
        num_scalar_prefetch=1,
        grid=(nb,),
        in_specs=[
            pl.BlockSpec((MOE_BLOCK, D_MODEL), lambda b, ut: (b, 0)),
            pl.BlockSpec((1, MOE_BLOCK, N_EXPERTS), lambda b, ut: (b, 0, 0)),
            pl.BlockSpec((1, 1, SLOT_MAX), lambda b, ut: (b, 0, 0)),
            pl.BlockSpec((1, 1, SLOT_MAX), lambda b, ut: (b, 0, 0)),
        ],
        out_specs=pl.BlockSpec((1, SLOT_MAX, D_MODEL), lambda b, ut: (b, 0, 0)),
    )
    return pl.pallas_call(
        _permute_kernel,
        grid_spec=grid_spec,
        out_shape=jax.ShapeDtypeStruct((nb, SLOT_MAX, D_MODEL), BF16),
        compiler_params=pltpu.CompilerParams(
            dimension_semantics=("arbitrary",), vmem_limit_bytes=VMEM_LIMIT),
        name="moe_permute",
    )(plan["used_tiles_b"], h2, plan["rank_bt"], plan["e_slot"], plan["r_slot"])


def _unit_copies(table_ref, base, src_hbm, dst_ref, slot, sem, n_units, start):
    for u in range(n_units):
        copy = pltpu.make_async_copy(src_hbm.at[table_ref[base + u]],
                                     dst_ref.at[slot, pl.ds(u * UNIT, UNIT), :], sem.at[slot])
        if start:
            copy.start()
        else:
            copy.wait()


def _ffn_kernel(src_ref, et_ref, nu_ref, xp_hbm, *refs):
    w_refs = refs[:2 * TILES_PER_STEP]
    y_out = refs[2 * TILES_PER_STEP]
    scratch = refs[2 * TILES_PER_STEP + 1:]
    wb_refs, xt_ref, sem = scratch[:2 * TILES_PER_STEP], scratch[2 * TILES_PER_STEP], scratch[2 * TILES_PER_STEP + 1]
    n_units = TILES_PER_STEP * UNITS_PER_TILE
    step = pl.program_id(0)
    slot = lax.rem(step, 2)
    n_used = nu_ref[0]

    def copies(s, sl, start):
        _unit_copies(src_ref, s * n_units, xp_hbm, xt_ref, sl, sem, n_units, start)

    @pl.when((step == 0) & (n_used > 0))
    def _():
        copies(0, 0, True)

    @pl.when((step + 1 < pl.num_programs(0)) & ((step + 1) * TILES_PER_STEP < n_used))
    def _():
        copies(step + 1, 1 - slot, True)

    @pl.when(step * TILES_PER_STEP < n_used)
    def _():
        for t in range(TILES_PER_STEP):
            j = step * TILES_PER_STEP + t

            @pl.when((step == 0) | (et_ref[j] != et_ref[jnp.maximum(j - TILES_PER_STEP, 0)]))
            def _():
                wb_refs[2 * t][...] = w_refs[2 * t][0].astype(BF16)
                wb_refs[2 * t + 1][...] = w_refs[2 * t + 1][0].astype(BF16)

        copies(step, slot, False)
        for t in range(TILES_PER_STEP):
            rows = slice(t * FFN_TILE, (t + 1) * FFN_TILE)
            gu = _dot(xt_ref[slot, rows, :], wb_refs[2 * t][...])
            act = _silu(gu[:, :EXPERT_FF]) * gu[:, EXPERT_FF:]
            y_out[rows, :] = _dot(act.astype(BF16), wb_refs[2 * t + 1][...]).astype(BF16)

    @pl.when(step * TILES_PER_STEP >= n_used)
    def _():
        y_out[...] = jnp.zeros(y_out.shape, BF16)


def _expert_ffn(plan, xp, w_gu, w_d):
    xp_units = xp.reshape(-1, UNIT, D_MODEL)
    assert N_TILES_MAX % TILES_PER_STEP == 0

    def weight_specs(t):
        return [pl.BlockSpec((1, D_MODEL, 2 * EXPERT_FF),
                             lambda s, src, et, nu, t=t: (et[s * TILES_PER_STEP + t], 0, 0)),
                pl.BlockSpec((1, EXPERT_FF, D_MODEL),
                             lambda s, src, et, nu, t=t: (et[s * TILES_PER_STEP + t], 0, 0))]

    grid_spec = pltpu.PrefetchScalarGridSpec(
        num_scalar_prefetch=3,
        grid=(N_TILES_MAX // TILES_PER_STEP,),
        in_specs=[pl.BlockSpec(memory_space=pl.ANY)]
        + [spec for t in range(TILES_PER_STEP) for spec in weight_specs(t)],
        out_specs=pl.BlockSpec((TILES_PER_STEP * FFN_TILE, D_MODEL), lambda s, src, et, nu: (s, 0)),
        scratch_shapes=[
            pltpu.VMEM((D_MODEL, 2 * EXPERT_FF), BF16) if i % 2 == 0 else pltpu.VMEM((EXPERT_FF, D_MODEL), BF16)
            for i in range(2 * TILES_PER_STEP)
        ] + [pltpu.VMEM((2, TILES_PER_STEP * FFN_TILE, D_MODEL), BF16), pltpu.SemaphoreType.DMA((2,))],
    )
    return pl.pallas_call(
        _ffn_kernel,
        grid_spec=grid_spec,
        out_shape=jax.ShapeDtypeStruct((N_TILES_MAX * FFN_TILE, D_MODEL), BF16),
        compiler_params=pltpu.CompilerParams(
            dimension_semantics=("arbitrary",), vmem_limit_bytes=VMEM_LIMIT),
        name="moe_expert_ffn",
    )(plan["src"], plan["e_tile"], plan["n_used"], xp_units, *([w_gu, w_d] * TILES_PER_STEP))


def _combine_kernel(ysrc_ref, ug_ref, y_hbm, rank_ref, comb_ref, er_ref, rr_ref, h_ref, x1_ref, mod_ref,
                    wsgu_ref, wsd_ref, fg_ref, o_ref, acc_ref, yt_ref, sem, *, final):
    t = pl.program_id(0)
    b = lax.div(t, GROUPS_PER_BLOCK)
    g = t - b * GROUPS_PER_BLOCK
    slot = lax.rem(t, 2)

    def active(tt):
        bb = lax.div(tt, GROUPS_PER_BLOCK)
        return tt - bb * GROUPS_PER_BLOCK < ug_ref[bb]

    def copies(tt, sl, start):
        _unit_copies(ysrc_ref, tt * GROUP_UNITS, y_hbm, yt_ref, sl, sem, GROUP_UNITS, start)

    @pl.when((t == 0) & active(0))
    def _():
        copies(0, 0, True)

    last_t = pl.num_programs(0) - 1

    @pl.when((t < last_t) & active(jnp.minimum(t + 1, last_t)))
    def _():
        copies(t + 1, 1 - slot, True)

    @pl.when(g == 0)
    def _():
        gu = _dot(h_ref[...], wsgu_ref[...])
        act = _silu(gu[:, :EXPERT_FF]) * gu[:, EXPERT_FF:]
        acc_ref[...] = _dot(act.astype(BF16), wsd_ref[...])

    @pl.when(active(t))
    def _():
        sub_e = lax.broadcasted_iota(jnp.int32, (N_EXPERTS, GROUP_SLOTS), 0)
        onehot_e = jnp.where(sub_e == er_ref[0], 1.0, 0.0).astype(BF16)
        slot_rank = _dot(rank_ref[0], onehot_e)
        slot_w = _dot(comb_ref[0], onehot_e)
        wc = jnp.where(slot_rank == rr_ref[0], slot_w, 0.0).astype(BF16)
        copies(t, slot, False)
        acc_ref[...] += _dot(wc, yt_ref[slot])

    @pl.when(g == GROUPS_PER_BLOCK - 1)
    def _():
        x2 = x1_ref[...] + mod_ref[0, 5:6, :] * acc_ref[...]
        if final:
            ms = jnp.mean(x2 * x2, axis=-1, keepdims=True)
            x2 = x2 * lax.rsqrt(ms + EPS) * fg_ref[...]
        o_ref[...] = x2


def _combine(plan, y_sorted, comb_bt, h2, x1, mod_tile, ws_gu, ws_d, final_g, final):
    n_tok = h2.shape[0]
    nb = n_tok // MOE_BLOCK
    y_units = y_sorted.reshape(-1, UNIT, D_MODEL)
    G = GROUPS_PER_BLOCK

    def full(a):
        return pl.BlockSpec(a.shape, lambda t, ysrc, ug: (0,) * a.ndim)

    grid_spec = pltpu.PrefetchScalarGridSpec(
        num_scalar_prefetch=2,
        grid=(nb * G,),
        in_specs=[
            pl.BlockSpec(memory_space=pl.ANY),
            pl.BlockSpec((1, MOE_BLOCK, N_EXPERTS), lambda t, ysrc, ug: (t // G, 0, 0)),
            pl.BlockSpec((1, MOE_BLOCK, N_EXPERTS), lambda t, ysrc, ug: (t // G, 0, 0)),
            pl.BlockSpec((1, 1, GROUP_SLOTS), lambda t, ysrc, ug: (t // G, 0, t % G)),
            pl.BlockSpec((1, 1, GROUP_SLOTS), lambda t, ysrc, ug: (t // G, 0, t % G)),
            pl.BlockSpec((MOE_BLOCK, D_MODEL), lambda t, ysrc, ug: (t // G, 0)),
            pl.BlockSpec((MOE_BLOCK, D_MODEL), lambda t, ysrc, ug: (t // G, 0)),
            pl.BlockSpec((1, 8, D_MODEL), lambda t, ysrc, ug: (t // G, 0, 0)),
            full(ws_gu), full(ws_d), full(final_g),
        ],
        out_specs=pl.BlockSpec((MOE_BLOCK, D_MODEL), lambda t, ysrc, ug: (t // G, 0)),
        scratch_shapes=[
            pltpu.VMEM((MOE_BLOCK, D_MODEL), F32),
            pltpu.VMEM((2, GROUP_SLOTS, D_MODEL), BF16),
            pltpu.SemaphoreType.DMA((2,)),
        ],
    )
    return pl.pallas_call(
        functools.partial(_combine_kernel, final=final),
        grid_spec=grid_spec,
        out_shape=jax.ShapeDtypeStruct((n_tok, D_MODEL), F32),
        compiler_params=pltpu.CompilerParams(
            dimension_semantics=("arbitrary",), vmem_limit_bytes=VMEM_LIMIT),
        name="moe_combine",
    )(plan["ysrc"], plan["used_groups_b"], y_units, plan["rank_bt"], comb_bt,
      plan["e_slot"], plan["r_slot"], h2, x1, mod_tile, ws_gu, ws_d, final_g)


def _level_ids():
    t = np.arange(CHUNK)
    x = t[:, None] ^ t[None, :]
    lvl = np.zeros((CHUNK, CHUNK), np.int32)
    nz = x > 0
    lvl[nz] = np.floor(np.log2(x[nz])).astype(np.int32) + 1
    past_f = t[:, None] >= t[None, :]
    return jnp.asarray(np.where(past_f, lvl, -1)), jnp.asarray(np.where(past_f.T, lvl, -1))


def kernel(x_prompt, x_sample, state_hgrn, c, c_ctx, w_ada, b_ada, norm1_g, w_in, hgrn_lb_logits, hgrn_norm_g, w_proj_hgrn, conv_dw_w, conv_dw_b, conv_norm_g, conv_norm_b, w_proj_conv, w_out, norm2_g, w_router, router_bias, w_expert_gate_up, w_expert_down, w_shared_gate_up, w_shared_down, final_norm_g):
    n_ctx, ctx_len, d = x_prompt.shape
    n_lat, lat_len, _ = x_sample.shape
    depth = w_ada.shape[0]
    assert d == D_MODEL and ctx_len * 4 == STEP and lat_len == STEP and n_ctx % 4 == 0
    ctx_steps = n_ctx * ctx_len // STEP
    n_steps = ctx_steps + n_lat
    n_tok = n_steps * STEP
    assert n_tok == N_TOKENS and MOE_BLOCK == POST_TILE
    tiles_per_step = STEP // POST_TILE

    x_all = jnp.concatenate([x_prompt.reshape(-1, d), x_sample.reshape(-1, d)], axis=0)

    cc = jnp.zeros((8, d), F32).at[:n_lat].set(c.astype(F32)).at[n_lat].set(c_ctx.astype(F32))
    mod = _ada(cc, w_ada, b_ada).reshape(depth, 8, N_MOD, d)
    step_src = np.array([n_lat] * ctx_steps + list(range(n_lat)))
    mod_step = jnp.pad(mod[:, step_src], ((0, 0), (0, 0), (0, 8 - N_MOD), (0, 0)))
    mod_tile = jnp.repeat(mod_step, tiles_per_step, axis=1)

    nc = jnp.asarray([ctx_len // CHUNK] * ctx_steps + [lat_len // CHUNK] * n_lat, jnp.int32)
    tile_pos = np.arange(n_lat * tiles_per_step) % tiles_per_step
    pv = jnp.asarray(np.concatenate([np.zeros(ctx_steps * tiles_per_step), tile_pos > 0]), jnp.int32)
    nv = jnp.asarray(np.concatenate([np.zeros(ctx_steps * tiles_per_step), tile_pos < tiles_per_step - 1]), jnp.int32)

    lbv = jnp.cumsum(jax.nn.softmax(hgrn_lb_logits.astype(F32), axis=0), axis=0)
    lbv = (lbv - lbv[:1]).reshape(depth, 2, N_HEADS, HEAD_DIM)
    lvl_f, lvl_b = _level_ids()

    xs = x_all
    ctx_states = []
    for l in range(depth):
        wl = w_in[l]
        lb = lbv[l]
        gp = jnp.stack([jnp.log(lb[0]), jnp.log1p(-lb[0]), 1.0 - lb[0],
                        jnp.log(lb[1]), jnp.log1p(-lb[1]), 1.0 - lb[1],
                        hgrn_norm_g[l].reshape(N_HEADS, HEAD_DIM).astype(F32),
                        jnp.zeros((N_HEADS, HEAD_DIM), F32)], axis=1)
        s0 = jnp.concatenate([jnp.zeros((ctx_steps, 2, N_HEADS, HEAD_DIM, HEAD_DIM), F32),
                              state_hgrn[:, l].astype(F32)], axis=0)
        g1 = norm1_g[l].reshape(1, d).astype(F32)

        o_all, states = _scan(nc, xs, mod_step[l], g1, wl, gp, s0, lvl_f, lvl_b)
        ctx_states.append(states[:ctx_steps].reshape(n_ctx, 2, N_HEADS, HEAD_DIM, HEAD_DIM))

        cw = jnp.pad(conv_dw_w[l].astype(F32), ((0, 1), (0, 0)))
        x1, h2, logits_t = _post(
            pv, nv, xs, mod_tile[l], o_all, wl, w_proj_hgrn[l], w_proj_conv[l], w_out[l],
            w_router[l].T.astype(F32), g1, cw,
            conv_dw_b[l].reshape(1, -1).astype(F32), conv_norm_g[l].reshape(1, -1).astype(F32),
            conv_norm_b[l].reshape(1, -1).astype(F32), norm2_g[l].reshape(1, d).astype(F32))

        comb_t, chosen_t = _route(logits_t, router_bias[l].reshape(N_EXPERTS, 1).astype(F32))
        plan = _routing_plan(chosen_t)
        comb_bt = comb_t.reshape(N_EXPERTS, n_tok // MOE_BLOCK, MOE_BLOCK).transpose(1, 2, 0).astype(BF16)
        xp = _permute(plan, h2)
        y_sorted = _expert_ffn(plan, xp, w_expert_gate_up[l], w_expert_down[l])
        xs = _combine(plan, y_sorted, comb_bt, h2, x1, mod_tile[l], w_shared_gate_up[l].astype(BF16),
                      w_shared_down[l].astype(BF16), final_norm_g.reshape(1, d).astype(F32),
                      final=(l == depth - 1))

    n_ctx_tok = n_ctx * ctx_len
    y_prompt = xs[:n_ctx_tok].reshape(x_prompt.shape).astype(x_prompt.dtype)
    y_sample = xs[n_ctx_tok:].reshape(x_sample.shape).astype(x_sample.dtype)
    new_state = jnp.stack(ctx_states, axis=1).astype(x_prompt.dtype)
    return (y_prompt, y_sample, new_state)
```

```python
import functools

import numpy as np
import jax
import jax.numpy as jnp
from jax import lax
from jax.experimental import pallas as pl
from jax.experimental.pallas import tpu as pltpu

F32 = jnp.float32
BF16 = jnp.bfloat16
HIGHEST = lax.Precision.HIGHEST

D_MODEL = 1024
N_HEADS = 8
HEAD_DIM = 128
CONV_DIM = 512
CONV_WIDTH = 31
CONV_HALO = 16
N_EXPERTS = 64
N_GROUPS = 8
GROUP_SIZE = N_EXPERTS // N_GROUPS
TOPK_GROUPS = 4
TOP_K = 8
EXPERT_FF = 256
ROUTED_SCALE = 2.5
N_MOD = 6
EPS = 1e-6

CHUNK = 128
N_LEVELS = 7
STEP = 1024
POST_TILE = 256
MOE_BLOCK = 256
UNIT = 16
FFN_TILE = 256
UNITS_PER_TILE = FFN_TILE // UNIT
TILES_PER_STEP = 2
SLOT_MAX = MOE_BLOCK * TOP_K + N_EXPERTS * UNIT
UNITS_PER_BLOCK = SLOT_MAX // UNIT
GROUP_UNITS = 32
GROUP_SLOTS = GROUP_UNITS * UNIT
GROUPS_PER_BLOCK = UNITS_PER_BLOCK // GROUP_UNITS
N_TOKENS = 8192
N_TILES_MAX = (N_TOKENS * TOP_K // UNIT + (N_TOKENS // MOE_BLOCK) * N_EXPERTS) // UNITS_PER_TILE + N_EXPERTS
HEAD_COLS = 5 * HEAD_DIM
VMEM_LIMIT = 52 * 1024 * 1024


def _dot(a, b):
    return jnp.dot(a, b, preferred_element_type=F32)


def _dot_nt(a, b):
    return lax.dot_general(a, b, (((1,), (1,)), ((), ())), preferred_element_type=F32)


def _dot_tn(a, b):
    return lax.dot_general(a, b, (((0,), (0,)), ((), ())), preferred_element_type=F32)


def _sigmoid(x):
    return 1.0 / (1.0 + jnp.exp(-x))


def _silu(x):
    return x * _sigmoid(x)


def _rms_mod(x, g, scale, shift):
    ms = jnp.mean(x * x, axis=-1, keepdims=True)
    return x * lax.rsqrt(ms + EPS) * g * (1.0 + scale) + shift


def _ada_kernel(c_ref, w_ref, b_ref, o_ref):
    cc = c_ref[...]
    o_ref[0] = jnp.dot(_silu(cc), w_ref[0], preferred_element_type=F32, precision=HIGHEST) + b_ref[0]


def _ada(cc, w_ada, b_ada):
    depth, d, n = w_ada.shape
    tn = 1536
    return pl.pallas_call(
        _ada_kernel,
        grid=(depth, n // tn),
        in_specs=[
            pl.BlockSpec((8, d), lambda l, j: (0, 0)),
            pl.BlockSpec((1, d, tn), lambda l, j: (l, 0, j)),
            pl.BlockSpec((1, 1, tn), lambda l, j: (l, 0, j)),
        ],
        out_specs=pl.BlockSpec((1, 8, tn), lambda l, j: (l, 0, j)),
        out_shape=jax.ShapeDtypeStruct((depth, 8, n), F32),
        compiler_params=pltpu.CompilerParams(vmem_limit_bytes=VMEM_LIMIT),
        name="ada_mod",
    )(cc, w_ada, b_ada.reshape(depth, 1, n))


def _forget_gate(z, log_lb, log1m_lb, one_m_lb):
    e = jnp.exp(-jnp.abs(z))
    r = 1.0 / (1.0 + e)
    log_sig = jnp.minimum(z, 0.0) - jnp.log(1.0 + e)
    k = one_m_lb * jnp.where(z > 0, e * r, r)
    b = log1m_lb + log_sig
    log_f = jnp.maximum(log_lb, b) + jnp.log(1.0 + jnp.exp(-jnp.abs(log_lb - b)))
    return log_f, k


def _level_reference(cum_ref, blk, fwd):
    half = blk // 2
    pieces = []
    if blk >= 8:
        for i in range(CHUNK // blk):
            row = i * blk + (half - 1 if fwd else half)
            pieces.append(jnp.broadcast_to(cum_ref[row:row + 1, :], (blk, HEAD_DIM)))
    else:
        sub = lax.broadcasted_iota(jnp.int32, (8, HEAD_DIM), 0)
        for i in range(CHUNK // 8):
            lo_row = 8 * i + (1 if fwd else 2)
            hi_row = 8 * i + (5 if fwd else 6)
            lo = jnp.broadcast_to(cum_ref[lo_row:lo_row + 1, :], (8, HEAD_DIM))
            hi = jnp.broadcast_to(cum_ref[hi_row:hi_row + 1, :], (8, HEAD_DIM))
            pieces.append(jnp.where(sub >= 4, hi, lo))
    return pieces[0] if len(pieces) == 1 else jnp.concatenate(pieces, axis=0)


def _chunk_step(q, k, v, log_f, st, lvl, cum_ref, fwd):
    r_idx = lax.broadcasted_iota(jnp.int32, (CHUNK, CHUNK), 0)
    c_idx = lax.broadcasted_iota(jnp.int32, (CHUNK, CHUNK), 1)
    tri = jnp.where((r_idx >= c_idx) if fwd else (r_idx <= c_idx), 1.0, 0.0).astype(F32)
    cum = jnp.dot(tri, log_f, preferred_element_type=F32, precision=HIGHEST)
    cum_ref[...] = cum

    scores = jnp.where(lvl == 0, _dot_nt(q.astype(BF16), k.astype(BF16)), 0.0)
    for lev in range(1, N_LEVELS + 1):
        if lev == 1:
            qe = q * jnp.exp(log_f)
            ke = k
        else:
            e = jnp.exp(-jnp.abs(cum - _level_reference(cum_ref, 1 << lev, fwd)))
            qe = q * e
            ke = k * e
        s = _dot_nt(qe.astype(BF16), ke.astype(BF16))
        scores = jnp.where(lvl == lev, s, scores)

    total = cum_ref[CHUNK - 1:CHUNK, :] if fwd else cum_ref[0:1, :]
    st_b = st.astype(BF16)
    o = _dot(scores.astype(BF16), v.astype(BF16)) + _dot_nt((q * jnp.exp(cum)).astype(BF16), st_b)
    k_st = k * jnp.exp(total - cum)
    st_new = st * jnp.exp(total) + _dot_tn(v.astype(BF16), k_st.astype(BF16))
    return o, st_new


def _scan_kernel(nc_ref, x_ref, mod_ref, g1_ref, wq_ref, wv_ref, wf_ref, wb_ref, wg_ref, gp_ref, s0_ref,
                 lvlf_ref, lvlb_ref, o_out, st_out, h_ref, w_ref, z_ref, of_ref, ob_ref, cumf_ref, cumb_ref):
    step = pl.program_id(0)
    head = pl.program_id(1)
    n_seq_chunks = nc_ref[step]
    n_chunks = STEP // CHUNK

    @pl.when(head == 0)
    def _():
        g1 = g1_ref[...]
        shift = mod_ref[0, 0:1, :]
        scale = mod_ref[0, 1:2, :]

        def body(i, carry):
            rows = pl.ds(pl.multiple_of(i * CHUNK, CHUNK), CHUNK)
            h_ref[rows, :] = _rms_mod(x_ref[rows, :], g1, scale, shift).astype(BF16)
            return carry

        lax.fori_loop(0, n_chunks, body, 0)

    for j, wj_ref in enumerate((wq_ref, wv_ref, wf_ref, wb_ref, wg_ref)):
        w_ref[:, j * HEAD_DIM:(j + 1) * HEAD_DIM] = wj_ref[0].astype(BF16)
    z_ref[...] = _dot(h_ref[...], w_ref[...])
    st_out[...] = jnp.zeros(st_out.shape, F32)
    st0 = (s0_ref[0, 0, 0].T, s0_ref[0, 1, 0].T)

    def one_direction(c, st, fwd):
        d = 0 if fwd else 1
        zcol = (2 if fwd else 3) * HEAD_DIM
        pos = lax.rem(c, n_seq_chunks)
        first = (pos == 0) if fwd else (pos == n_seq_chunks - 1)
        rows = pl.ds(c * CHUNK, CHUNK)
        q = z_ref[rows, 0:HEAD_DIM]
        v = z_ref[rows, HEAD_DIM:2 * HEAD_DIM]
        log_f, k = _forget_gate(z_ref[rows, zcol:zcol + HEAD_DIM], gp_ref[0, 3 * d:3 * d + 1, :],
                                gp_ref[0, 3 * d + 1:3 * d + 2, :], gp_ref[0, 3 * d + 2:3 * d + 3, :])
        st = jnp.where(first, st0[d], st)
        o, st = _chunk_step(q, k, v, log_f, st, (lvlf_ref if fwd else lvlb_ref)[...],
                            cumf_ref if fwd else cumb_ref, fwd)

        (of_ref if fwd else ob_ref)[rows, :] = o
        st_out[0, lax.div(c, n_seq_chunks), d, 0] = st.T
        return st

    carry = st0
    for i in range(n_chunks):
        carry = (one_direction(i, carry[0], True), one_direction(n_chunks - 1 - i, carry[1], False))

    norm_g = gp_ref[0, 6:7, :]

    def finish(i, carry):
        rows = pl.ds(pl.multiple_of(i * CHUNK, CHUNK), CHUNK)
        o = of_ref[rows, :] + ob_ref[rows, :]
        o = o * lax.rsqrt(jnp.mean(o * o, axis=-1, keepdims=True) + EPS)
        og = z_ref[rows, 4 * HEAD_DIM:5 * HEAD_DIM]
        o_out[rows, :] = (o * norm_g * _silu(og)).astype(BF16)
        return carry

    lax.fori_loop(0, n_chunks, finish, 0)


def _scan(nc, x_all, mod_step, g1, w_in, layer, gp, s0, lvl_f, lvl_b):
    n_tok = x_all.shape[0]
    n_steps = n_tok // STEP
    grid_spec = pltpu.PrefetchScalarGridSpec(
        num_scalar_prefetch=1,
        grid=(n_steps, N_HEADS),
        in_specs=[
            pl.BlockSpec((STEP, D_MODEL), lambda s, h, nc: (s, 0)),
            pl.BlockSpec((1, 8, D_MODEL), lambda s, h, nc: (s, 0, 0)),
            pl.BlockSpec((1, D_MODEL), lambda s, h, nc: (0, 0)),
        ] + [
            pl.BlockSpec((1, D_MODEL, HEAD_DIM), lambda s, h, nc, j=j: (layer, 0, j * N_HEADS + h))
            for j in range(5)
        ] + [
            pl.BlockSpec((1, 8, HEAD_DIM), lambda s, h, nc: (h, 0, 0)),
            pl.BlockSpec((1, 2, 1, HEAD_DIM, HEAD_DIM), lambda s, h, nc: (s, 0, h, 0, 0)),
            pl.BlockSpec((CHUNK, CHUNK), lambda s, h, nc: (0, 0)),
            pl.BlockSpec((CHUNK, CHUNK), lambda s, h, nc: (0, 0)),
        ],
        out_specs=[
            pl.BlockSpec((STEP, HEAD_DIM), lambda s, h, nc: (s, h)),
            pl.BlockSpec((1, 4, 2, 1, HEAD_DIM, HEAD_DIM), lambda s, h, nc: (s, 0, 0, h, 0, 0)),
        ],
        scratch_shapes=[
            pltpu.VMEM((STEP, D_MODEL), BF16),
            pltpu.VMEM((D_MODEL, HEAD_COLS), BF16),
            pltpu.VMEM((STEP, HEAD_COLS), F32),
            pltpu.VMEM((STEP, HEAD_DIM), F32),
            pltpu.VMEM((STEP, HEAD_DIM), F32),
            pltpu.VMEM((CHUNK, HEAD_DIM), F32),
            pltpu.VMEM((CHUNK, HEAD_DIM), F32),
        ],
    )
    return pl.pallas_call(
        _scan_kernel,
        grid_spec=grid_spec,
        out_shape=[
            jax.ShapeDtypeStruct((n_tok, D_MODEL), BF16),
            jax.ShapeDtypeStruct((n_steps, 4, 2, N_HEADS, HEAD_DIM, HEAD_DIM), F32),
        ],
        compiler_params=pltpu.CompilerParams(
            dimension_semantics=("arbitrary", "arbitrary"), vmem_limit_bytes=VMEM_LIMIT),
        name="hgrn_scan",
    )(nc, x_all, mod_step, g1, *([w_in] * 5), gp, s0, lvl_f, lvl_b)


def _post_kernel(pv_ref, nv_ref, x_ref, xp_ref, xn_ref, mod_ref, o_ref, wu_f32, wgz_f32, wph_f32,
                 wpc_f32, wo_f32, wr_ref, g1_ref, cw_ref, cb_ref, lg_ref, lb_ref, g2_ref,
                 x1_out, h2_out, lg_out, cu_ref, wu_ref, wgz_ref, wph_ref, wpc_ref, wo_ref):
    i = pl.program_id(0)

    @pl.when(i == 0)
    def _():
        for dst, src in ((wu_ref, wu_f32), (wgz_ref, wgz_f32), (wph_ref, wph_f32),
                         (wpc_ref, wpc_f32), (wo_ref, wo_f32)):
            dst[...] = (src[0] if len(src.shape) == 3 else src[...]).astype(BF16)

    g1 = g1_ref[...]
    shift1 = mod_ref[0, 0:1, :]
    scale1 = mod_ref[0, 1:2, :]
    gate1 = mod_ref[0, 2:3, :]
    shift2 = mod_ref[0, 3:4, :]
    scale2 = mod_ref[0, 4:5, :]

    def glu(xv):
        h = _rms_mod(xv, g1, scale1, shift1).astype(BF16)
        u = _dot(h, wu_ref[...])
        return h, u[:, :CONV_DIM] * _sigmoid(u[:, CONV_DIM:])

    x = x_ref[...]
    h, glu_mid = glu(x)
    _, glu_prev = glu(xp_ref[...])
    _, glu_next = glu(xn_ref[...])
    cu_ref[0:CONV_HALO, :] = glu_prev * pv_ref[i].astype(F32)
    cu_ref[CONV_HALO:CONV_HALO + POST_TILE, :] = glu_mid
    cu_ref[CONV_HALO + POST_TILE:, :] = glu_next * nv_ref[i].astype(F32)

    off = CONV_HALO - CONV_WIDTH // 2
    acc = jnp.zeros((POST_TILE, CONV_DIM), F32) + cb_ref[...]
    for j in range(CONV_WIDTH):
        acc = acc + cu_ref[off + j:off + j + POST_TILE, :] * cw_ref[j:j + 1, :]
    mu = jnp.mean(acc, axis=-1, keepdims=True)
    cen = acc - mu
    var = jnp.mean(cen * cen, axis=-1, keepdims=True)
    cv = cen * lax.rsqrt(var + EPS) * lg_ref[...] + lb_ref[...]
    y_b = _dot(_silu(cv).astype(BF16), wpc_ref[...])

    y_a = _dot(o_ref[...], wph_ref[...])
    gz = _sigmoid(_dot(h, wgz_ref[...]))
    merged = gz[:, :D_MODEL] * y_a + gz[:, D_MODEL:] * y_b
    x1 = x + gate1 * _dot(merged.astype(BF16), wo_ref[...])
    x1_out[...] = x1
    h2 = _rms_mod(x1, g2_ref[...], scale2, shift2)
    h2_out[...] = h2.astype(BF16)
    lg_out[...] = lax.dot_general(wr_ref[...], h2, (((1,), (1,)), ((), ())),
                                  preferred_element_type=F32, precision=HIGHEST)


def _post(pv, nv, x_all, mod_tile, o_all, w_in, layer, w_ph, w_pc, w_o, w_rt, g1, cw, cb, lg, lb, g2):
    n_tok = x_all.shape[0]
    n_tiles = n_tok // POST_TILE
    halo_per_tile = POST_TILE // CONV_HALO
    n_halo_blocks = n_tok // CONV_HALO
    glu_cols, gate_cols = 2 * CONV_DIM, 2 * D_MODEL
    glu_start = 5 * N_HEADS * HEAD_DIM
    assert glu_start % glu_cols == 0 and (glu_start + glu_cols) % gate_cols == 0

    def full(a):
        return pl.BlockSpec(a.shape, lambda i, pv, nv: (0,) * a.ndim)

    def resident(shape, index):
        return pl.BlockSpec(shape, lambda i, pv, nv: index, pipeline_mode=pl.Buffered(1))

    grid_spec = pltpu.PrefetchScalarGridSpec(
        num_scalar_prefetch=2,
        grid=(n_tiles,),
        in_specs=[
            pl.BlockSpec((POST_TILE, D_MODEL), lambda i, pv, nv: (i, 0)),
            pl.BlockSpec((CONV_HALO, D_MODEL), lambda i, pv, nv: (jnp.maximum(i * halo_per_tile - 1, 0), 0)),
            pl.BlockSpec((CONV_HALO, D_MODEL),
                         lambda i, pv, nv: (jnp.minimum((i + 1) * halo_per_tile, n_halo_blocks - 1), 0)),
            pl.BlockSpec((1, 8, D_MODEL), lambda i, pv, nv: (i, 0, 0)),
            pl.BlockSpec((POST_TILE, D_MODEL), lambda i, pv, nv: (i, 0)),
            resident((1, D_MODEL, glu_cols), (layer, 0, glu_start // glu_cols)),
            resident((1, D_MODEL, gate_cols), (layer, 0, (glu_start + glu_cols) // gate_cols)),
            resident(w_ph.shape, (0, 0)), resident(w_pc.shape, (0, 0)), resident(w_o.shape, (0, 0)),
            full(w_rt), full(g1), full(cw), full(cb), full(lg), full(lb), full(g2),
        ],
        out_specs=[
            pl.BlockSpec((POST_TILE, D_MODEL), lambda i, pv, nv: (i, 0)),
            pl.BlockSpec((POST_TILE, D_MODEL), lambda i, pv, nv: (i, 0)),
            pl.BlockSpec((N_EXPERTS, POST_TILE), lambda i, pv, nv: (0, i)),
        ],
        scratch_shapes=[
            pltpu.VMEM((POST_TILE + 2 * CONV_HALO, CONV_DIM), F32),
            pltpu.VMEM((D_MODEL, glu_cols), BF16), pltpu.VMEM((D_MODEL, gate_cols), BF16),
            pltpu.VMEM(w_ph.shape, BF16), pltpu.VMEM(w_pc.shape, BF16), pltpu.VMEM(w_o.shape, BF16),
        ],
    )
    return pl.pallas_call(
        _post_kernel,
        grid_spec=grid_spec,
        out_shape=[
            jax.ShapeDtypeStruct((n_tok, D_MODEL), F32),
            jax.ShapeDtypeStruct((n_tok, D_MODEL), BF16),
            jax.ShapeDtypeStruct((N_EXPERTS, n_tok), F32),
        ],
        compiler_params=pltpu.CompilerParams(
            dimension_semantics=("arbitrary",), vmem_limit_bytes=VMEM_LIMIT),
        name="post_mixer",
    )(pv, nv, x_all, x_all, x_all, mod_tile, o_all, w_in, w_in, w_ph, w_pc, w_o, w_rt, g1, cw, cb, lg, lb, g2)


def _route_kernel(lg_ref, bias_ref, comb_out, chosen_out):
    n = lg_ref.shape[1]
    scores = _sigmoid(lg_ref[...])
    sel = scores + bias_ref[...]
    neg = jnp.float32(-jnp.inf)

    sel3 = sel.reshape(N_GROUPS, GROUP_SIZE, n)
    m1 = jnp.max(sel3, axis=1, keepdims=True)
    is_m1 = sel3 == m1
    n_m1 = jnp.sum(is_m1.astype(F32), axis=1, keepdims=True)
    m2 = jnp.max(jnp.where(is_m1, neg, sel3), axis=1, keepdims=True)
    grp = (m1 + jnp.where(n_m1 > 1.5, m1, m2)).reshape(N_GROUPS, n)

    gidx = lax.broadcasted_iota(jnp.int32, (N_GROUPS, n), 0)
    rank = jnp.zeros((N_GROUPS, n), F32)
    for g in range(N_GROUPS):
        other = grp[g:g + 1, :]
        ahead = (other > grp) | ((other == grp) & (g < gidx))
        rank = rank + ahead.astype(F32)
    keep_g = rank < TOPK_GROUPS - 0.5
    keep = jnp.broadcast_to(keep_g.reshape(N_GROUPS, 1, n), (N_GROUPS, GROUP_SIZE, n)).reshape(N_EXPERTS, n)
    cand = jnp.where(keep, sel, neg)

    eidx = lax.broadcasted_iota(jnp.int32, (N_EXPERTS, n), 0)
    chosen = jnp.zeros((N_EXPERTS, n), F32)
    for _ in range(TOP_K):
        best = jnp.max(cand, axis=0, keepdims=True)
        first = jnp.min(jnp.where(cand == best, eidx, N_EXPERTS), axis=0, keepdims=True)
        hit = eidx == first
        chosen = jnp.where(hit, 1.0, chosen)
        cand = jnp.where(hit, neg, cand)
    w = scores * chosen
    comb_out[...] = w / jnp.sum(w, axis=0, keepdims=True) * ROUTED_SCALE
    chosen_out[...] = chosen


def _route(logits_t, bias):
    n_tok = logits_t.shape[1]
    tile = 512
    return pl.pallas_call(
        _route_kernel,
        grid=(n_tok // tile,),
        in_specs=[
            pl.BlockSpec((N_EXPERTS, tile), lambda i: (0, i)),
            pl.BlockSpec((N_EXPERTS, 1), lambda i: (0, 0)),
        ],
        out_specs=[pl.BlockSpec((N_EXPERTS, tile), lambda i: (0, i)),
                   pl.BlockSpec((N_EXPERTS, tile), lambda i: (0, i))],
        out_shape=[jax.ShapeDtypeStruct((N_EXPERTS, n_tok), F32),
                   jax.ShapeDtypeStruct((N_EXPERTS, n_tok), F32)],
        name="router",
    )(logits_t, bias)


def _count_le(bounds, idx):
    return jnp.sum(jnp.where(bounds <= idx, 1.0, 0.0), axis=0, keepdims=True)


def _pick_row(onehot, col):
    return jnp.sum(jnp.where(onehot, col, 0.0), axis=0, keepdims=True)


def _plan_kernel(ch_ref, rankt_out, es_out, rs_out, src_out, et_out, ysrc_out, misc_out):
    nb = N_TOKENS // MOE_BLOCK
    n_src = N_TILES_MAX * UNITS_PER_TILE
    e_col = lax.broadcasted_iota(jnp.int32, (N_EXPERTS, 1), 0).astype(F32)
    blk_lane = lax.broadcasted_iota(jnp.int32, (N_EXPERTS, 128), 1)
    t_r = lax.broadcasted_iota(jnp.int32, (MOE_BLOCK, MOE_BLOCK), 0)
    t_c = lax.broadcasted_iota(jnp.int32, (MOE_BLOCK, MOE_BLOCK), 1)
    earlier = jnp.where(t_c < t_r, 1.0, 0.0).astype(BF16)
    e_r = lax.broadcasted_iota(jnp.int32, (N_EXPERTS, N_EXPERTS), 0)
    e_c = lax.broadcasted_iota(jnp.int32, (N_EXPERTS, N_EXPERTS), 1)
    upto = jnp.where(e_c <= e_r, 1.0, 0.0).astype(F32)
    slot = lax.broadcasted_iota(jnp.int32, (1, SLOT_MAX), 1).astype(F32)

    def col_cumsum(col):
        wide = jnp.broadcast_to(col, (N_EXPERTS, 128))
        return jnp.dot(upto, wide, preferred_element_type=F32, precision=HIGHEST)[:, 0:1]

    def block_stats(b, carry):
        units_mat, start_mat = carry
        cb = ch_ref[:, pl.ds(pl.multiple_of(b * MOE_BLOCK, MOE_BLOCK), MOE_BLOCK)]
        rank_t = _dot_nt(earlier, cb.astype(BF16))
        cb_t = cb.T
        rankt_out[b] = jnp.where(cb_t > 0, rank_t, -1.0).astype(BF16)
        cnt = jnp.sum(cb, axis=1, keepdims=True)
        units = jnp.floor((cnt + (UNIT - 1)) * (1.0 / UNIT))
        incl = col_cumsum(units)
        start = incl - units
        e_slot = _count_le(incl * UNIT, slot)
        onehot = e_col == e_slot
        r = slot - _pick_row(onehot, start * UNIT)
        valid = (e_slot < N_EXPERTS - 0.5) & (r < _pick_row(onehot, cnt))
        es_out[b] = jnp.where(valid, e_slot, -1.0).astype(jnp.int32)
        rs_out[b] = jnp.where(valid, r, -2.0)
        units_mat = jnp.where(blk_lane == b, units, units_mat)
        start_mat = jnp.where(blk_lane == b, start, start_mat)
        return units_mat, start_mat

    zeros = jnp.zeros((N_EXPERTS, 128), F32)
    units_mat, start_mat = lax.fori_loop(0, nb, block_stats, (zeros, zeros))

    tot = jnp.sum(units_mat, axis=1, keepdims=True)
    tiles_e = jnp.floor((tot + (UNITS_PER_TILE - 1)) * (1.0 / UNITS_PER_TILE))
    incl_t = col_cumsum(tiles_e)
    start_t = incl_t - tiles_e
    n_used = incl_t[N_EXPERTS - 1:N_EXPERTS, :]
    b_r = lax.broadcasted_iota(jnp.int32, (128, 128), 0)
    b_c = lax.broadcasted_iota(jnp.int32, (128, 128), 1)
    before = jnp.where(b_r < b_c, 1.0, 0.0).astype(F32)
    cum_b = jnp.dot(units_mat, before, preferred_element_type=F32, precision=HIGHEST)
    run_pos = start_t * UNITS_PER_TILE + cum_b

    tile_idx = lax.broadcasted_iota(jnp.int32, (1, 512), 1).astype(F32)
    et_out[...] = jnp.minimum(_count_le(incl_t, tile_idx), N_EXPERTS - 1.0).astype(jnp.int32)

    incl_ub_t = (cum_b + units_mat).T
    cum_b_t = cum_b.T
    start_mat_t = start_mat.T
    b_col = lax.broadcasted_iota(jnp.int32, (128, 1), 0)
    chunk = 1024

    def src_chunk(c, carry):
        off = pl.multiple_of(c * chunk, chunk)
        p = (lax.broadcasted_iota(jnp.int32, (1, chunk), 1) + off).astype(F32)
        e_p = jnp.minimum(_count_le(incl_t * UNITS_PER_TILE, p), N_EXPERTS - 1.0)
        onehot_e = e_col == e_p
        q = p - _pick_row(onehot_e, start_t * UNITS_PER_TILE)
        valid = (p < n_used * UNITS_PER_TILE) & (q < _pick_row(onehot_e, tot))
        sel = jnp.where(onehot_e, 1.0, 0.0).astype(F32)
        incl_sel = jnp.dot(incl_ub_t, sel, preferred_element_type=F32, precision=HIGHEST)
        b_p = jnp.sum(jnp.where((incl_sel <= q) & (b_col < nb), 1.0, 0.0), axis=0, keepdims=True)
        b_p = jnp.minimum(b_p, nb - 1.0)
        onehot_b = b_col.astype(F32) == b_p
        cum_sel = jnp.dot(cum_b_t, sel, preferred_element_type=F32, precision=HIGHEST)
        start_sel = jnp.dot(start_mat_t, sel, preferred_element_type=F32, precision=HIGHEST)
        unit = b_p * UNITS_PER_BLOCK + _pick_row(onehot_b, start_sel) + q - _pick_row(onehot_b, cum_sel)
        src_out[:, pl.ds(off, chunk)] = jnp.where(valid, unit, 0.0).astype(jnp.int32)
        return carry

    lax.fori_loop(0, n_src // chunk, src_chunk, 0)

    unit_idx = lax.broadcasted_iota(jnp.int32, (1, 256), 1).astype(F32)
    misc_lane = lax.broadcasted_iota(jnp.int32, (1, 128), 1)

    def block_units(b, used):
        here = blk_lane == b
        units = jnp.sum(jnp.where(here, units_mat, 0.0), axis=1, keepdims=True)
        start = jnp.sum(jnp.where(here, start_mat, 0.0), axis=1, keepdims=True)
        pos = jnp.sum(jnp.where(here, run_pos, 0.0), axis=1, keepdims=True)
        incl = start + units
        e_unit = _count_le(incl, unit_idx)
        onehot = e_col == e_unit
        val = _pick_row(onehot, pos - start) + unit_idx
        ysrc_out[b] = jnp.where(e_unit < N_EXPERTS - 0.5, val, 0.0).astype(jnp.int32)
        return jnp.where(misc_lane == b, incl[N_EXPERTS - 1:N_EXPERTS, :], used)

    used = lax.fori_loop(0, nb, block_units, jnp.zeros((1, 128), F32))
    misc_out[0:1, :] = used.astype(jnp.int32)
    misc_out[1:2, :] = jnp.broadcast_to(n_used, (1, 128)).astype(jnp.int32)
    misc_out[2:8, :] = jnp.zeros((6, 128), jnp.int32)


def _routing_plan(chosen):
    nb = N_TOKENS // MOE_BLOCK
    n_src = N_TILES_MAX * UNITS_PER_TILE
    assert n_src % 1024 == 0 and N_TILES_MAX <= 512 and nb <= 128 and UNITS_PER_BLOCK <= 256
    rank_bt, e_slot, r_slot, src, e_tile, ysrc, misc = pl.pallas_call(
        _plan_kernel,
        out_shape=[
            jax.ShapeDtypeStruct((nb, MOE_BLOCK, N_EXPERTS), BF16),
            jax.ShapeDtypeStruct((nb, 1, SLOT_MAX), jnp.int32),
            jax.ShapeDtypeStruct((nb, 1, SLOT_MAX), F32),
            jax.ShapeDtypeStruct((1, n_src), jnp.int32),
            jax.ShapeDtypeStruct((1, 512), jnp.int32),
            jax.ShapeDtypeStruct((nb, 1, 256), jnp.int32),
            jax.ShapeDtypeStruct((8, 128), jnp.int32),
        ],
        compiler_params=pltpu.CompilerParams(vmem_limit_bytes=VMEM_LIMIT),
        name="moe_plan",
    )(chosen)
    used_units_b = misc[0, :nb]
    return dict(rank_bt=rank_bt, e_slot=e_slot, r_slot=r_slot, src=src.reshape(-1),
                e_tile=e_tile[0, :N_TILES_MAX], n_used=misc[1, :1],
                ysrc=ysrc[:, 0, :UNITS_PER_BLOCK].reshape(-1),
                used_tiles_b=(used_units_b * UNIT + FFN_TILE - 1) // FFN_TILE,
                used_groups_b=(used_units_b + GROUP_UNITS - 1) // GROUP_UNITS)


def _permute_kernel(ut_ref, h_ref, rank_ref, es_ref, rs_ref, xp_out):
    n_used = ut_ref[pl.program_id(0)]
    sub_e = lax.broadcasted_iota(jnp.int32, (N_EXPERTS, FFN_TILE), 0)

    def body(i, carry):
        rows = pl.ds(pl.multiple_of(i * FFN_TILE, FFN_TILE), FFN_TILE)

        @pl.when(i < n_used)
        def _():
            onehot_e = jnp.where(sub_e == es_ref[0, :, rows], 1.0, 0.0).astype(BF16)
            slot_rank = _dot(rank_ref[0], onehot_e)
            pick = jnp.where(slot_rank == rs_ref[0, :, rows], 1.0, 0.0).astype(BF16)
            xp_out[0, rows, :] = _dot_tn(pick, h_ref[...]).astype(BF16)

        @pl.when(i >= n_used)
        def _():
            xp_out[0, rows, :] = jnp.zeros((FFN_TILE, D_MODEL), BF16)

        return carry

    lax.fori_loop(0, SLOT_MAX // FFN_TILE, body, 0)


def _permute(plan, h2):
    nb = h2.shape[0] // MOE_BLOCK
    grid_spec = pltpu.PrefetchScalarGridSpec(
        num_scalar_prefetch=1,
        grid=(nb,),
        in_specs=[
            pl.BlockSpec((MOE_BLOCK, D_MODEL), lambda b, ut: (b, 0)),
            pl.BlockSpec((1, MOE_BLOCK, N_EXPERTS), lambda b, ut: (b, 0, 0)),
            pl.BlockSpec((1, 1, SLOT_MAX), lambda b, ut: (b, 0, 0)),
            pl.BlockSpec((1, 1, SLOT_MAX), lambda b, ut: (b, 0, 0)),
        ],
        out_specs=pl.BlockSpec((1, SLOT_MAX, D_MODEL), lambda b, ut: (b, 0, 0)),
    )
    return pl.pallas_call(
        _permute_kernel,
        grid_spec=grid_spec,
        out_shape=jax.ShapeDtypeStruct((nb, SLOT_MAX, D_MODEL), BF16),
        compiler_params=pltpu.CompilerParams(
            dimension_semantics=("arbitrary",), vmem_limit_bytes=VMEM_LIMIT),
        name="moe_permute",
    )(plan["used_tiles_b"], h2, plan["rank_bt"], plan["e_slot"], plan["r_slot"])


def _unit_copies(table_ref, base, src_hbm, dst_ref, slot, sem, n_units, start):
    for u in range(n_units):
        copy = pltpu.make_async_copy(src_hbm.at[table_ref[base + u]],
                                     dst_ref.at[slot, pl.ds(u * UNIT, UNIT), :], sem.at[slot])
        if start:
            copy.start()
        else:
            copy.wait()


def _ffn_kernel(src_ref, et_ref, nu_ref, xp_hbm, *refs):
    w_refs = refs[:2 * TILES_PER_STEP]
    y_out = refs[2 * TILES_PER_STEP]
    scratch = refs[2 * TILES_PER_STEP + 1:]
    wb_refs, xt_ref, sem = scratch[:2 * TILES_PER_STEP], scratch[2 * TILES_PER_STEP], scratch[2 * TILES_PER_STEP + 1]
    n_units = TILES_PER_STEP * UNITS_PER_TILE
    step = pl.program_id(0)
    slot = lax.rem(step, 2)
    n_used = nu_ref[0]

    def copies(s, sl, start):
        _unit_copies(src_ref, s * n_units, xp_hbm, xt_ref, sl, sem, n_units, start)

    @pl.when((step == 0) & (n_used > 0))
    def _():
        copies(0, 0, True)

    @pl.when((step + 1 < pl.num_programs(0)) & ((step + 1) * TILES_PER_STEP < n_used))
    def _():
        copies(step + 1, 1 - slot, True)

    @pl.when(step * TILES_PER_STEP < n_used)
    def _():
        for t in range(TILES_PER_STEP):
            j = step * TILES_PER_STEP + t

            @pl.when((step == 0) | (et_ref[j] != et_ref[jnp.maximum(j - TILES_PER_STEP, 0)]))
            def _():
                wb_refs[2 * t][...] = w_refs[2 * t][0, 0].astype(BF16)
                wb_refs[2 * t + 1][...] = w_refs[2 * t + 1][0, 0].astype(BF16)

        copies(step, slot, False)
        for t in range(TILES_PER_STEP):
            rows = slice(t * FFN_TILE, (t + 1) * FFN_TILE)
            gu = _dot(xt_ref[slot, rows, :], wb_refs[2 * t][...])
            act = _silu(gu[:, :EXPERT_FF]) * gu[:, EXPERT_FF:]
            y_out[rows, :] = _dot(act.astype(BF16), wb_refs[2 * t + 1][...]).astype(BF16)

    @pl.when(step * TILES_PER_STEP >= n_used)
    def _():
        y_out[...] = jnp.zeros(y_out.shape, BF16)


def _expert_ffn(plan, xp, w_gu, w_d, layer):
    xp_units = xp.reshape(-1, UNIT, D_MODEL)
    assert N_TILES_MAX % TILES_PER_STEP == 0

    def weight_specs(t):
        return [pl.BlockSpec((1, 1, D_MODEL, 2 * EXPERT_FF),
                             lambda s, src, et, nu, t=t: (layer, et[s * TILES_PER_STEP + t], 0, 0)),
                pl.BlockSpec((1, 1, EXPERT_FF, D_MODEL),
                             lambda s, src, et, nu, t=t: (layer, et[s * TILES_PER_STEP + t], 0, 0))]

    grid_spec = pltpu.PrefetchScalarGridSpec(
        num_scalar_prefetch=3,
        grid=(N_TILES_MAX // TILES_PER_STEP,),
        in_specs=[pl.BlockSpec(memory_space=pl.ANY)]
        + [spec for t in range(TILES_PER_STEP) for spec in weight_specs(t)],
        out_specs=pl.BlockSpec((TILES_PER_STEP * FFN_TILE, D_MODEL), lambda s, src, et, nu: (s, 0)),
        scratch_shapes=[
            pltpu.VMEM((D_MODEL, 2 * EXPERT_FF), BF16) if i % 2 == 0 else pltpu.VMEM((EXPERT_FF, D_MODEL), BF16)
            for i in range(2 * TILES_PER_STEP)
        ] + [pltpu.VMEM((2, TILES_PER_STEP * FFN_TILE, D_MODEL), BF16), pltpu.SemaphoreType.DMA((2,))],
    )
    return pl.pallas_call(
        _ffn_kernel,
        grid_spec=grid_spec,
        out_shape=jax.ShapeDtypeStruct((N_TILES_MAX * FFN_TILE, D_MODEL), BF16),
        compiler_params=pltpu.CompilerParams(
            dimension_semantics=("arbitrary",), vmem_limit_bytes=VMEM_LIMIT),
        name="moe_expert_ffn",
    )(plan["src"], plan["e_tile"], plan["n_used"], xp_units, *([w_gu, w_d] * TILES_PER_STEP))


def _combine_kernel(ysrc_ref, ug_ref, y_hbm, rank_ref, comb_ref, er_ref, rr_ref, h_ref, x1_ref, mod_ref,
                    wsgu_ref, wsd_ref, fg_ref, o_ref, acc_ref, yt_ref, sem, *, final):
    t = pl.program_id(0)
    b = lax.div(t, GROUPS_PER_BLOCK)
    g = t - b * GROUPS_PER_BLOCK
    slot = lax.rem(t, 2)

    def active(tt):
        bb = lax.div(tt, GROUPS_PER_BLOCK)
        return tt - bb * GROUPS_PER_BLOCK < ug_ref[bb]

    def copies(tt, sl, start):
        _unit_copies(ysrc_ref, tt * GROUP_UNITS, y_hbm, yt_ref, sl, sem, GROUP_UNITS, start)

    @pl.when((t == 0) & active(0))
    def _():
        copies(0, 0, True)

    last_t = pl.num_programs(0) - 1

    @pl.when((t < last_t) & active(jnp.minimum(t + 1, last_t)))
    def _():
        copies(t + 1, 1 - slot, True)

    @pl.when(g == 0)
    def _():
        gu = _dot(h_ref[...], wsgu_ref[...])
        act = _silu(gu[:, :EXPERT_FF]) * gu[:, EXPERT_FF:]
        acc_ref[...] = _dot(act.astype(BF16), wsd_ref[...])

    @pl.when(active(t))
    def _():
        sub_e = lax.broadcasted_iota(jnp.int32, (N_EXPERTS, GROUP_SLOTS), 0)
        onehot_e = jnp.where(sub_e == er_ref[0], 1.0, 0.0).astype(BF16)
        slot_rank = _dot(rank_ref[0], onehot_e)
        slot_w = _dot(comb_ref[0], onehot_e)
        wc = jnp.where(slot_rank == rr_ref[0], slot_w, 0.0).astype(BF16)
        copies(t, slot, False)
        acc_ref[...] += _dot(wc, yt_ref[slot])

    @pl.when(g == GROUPS_PER_BLOCK - 1)
    def _():
        x2 = x1_ref[...] + mod_ref[0, 5:6, :] * acc_ref[...]
        if final:
            ms = jnp.mean(x2 * x2, axis=-1, keepdims=True)
            x2 = x2 * lax.rsqrt(ms + EPS) * fg_ref[...]
        o_ref[...] = x2


def _combine(plan, y_sorted, comb_bt, h2, x1, mod_tile, ws_gu, ws_d, final_g, final):
    n_tok = h2.shape[0]
    nb = n_tok // MOE_BLOCK
    y_units = y_sorted.reshape(-1, UNIT, D_MODEL)
    G = GROUPS_PER_BLOCK

    def full(a):
        return pl.BlockSpec(a.shape, lambda t, ysrc, ug: (0,) * a.ndim)

    grid_spec = pltpu.PrefetchScalarGridSpec(
        num_scalar_prefetch=2,
        grid=(nb * G,),
        in_specs=[
            pl.BlockSpec(memory_space=pl.ANY),
            pl.BlockSpec((1, MOE_BLOCK, N_EXPERTS), lambda t, ysrc, ug: (t // G, 0, 0)),
            pl.BlockSpec((1, MOE_BLOCK, N_EXPERTS), lambda t, ysrc, ug: (t // G, 0, 0)),
            pl.BlockSpec((1, 1, GROUP_SLOTS), lambda t, ysrc, ug: (t // G, 0, t % G)),
            pl.BlockSpec((1, 1, GROUP_SLOTS), lambda t, ysrc, ug: (t // G, 0, t % G)),
            pl.BlockSpec((MOE_BLOCK, D_MODEL), lambda t, ysrc, ug: (t // G, 0)),
            pl.BlockSpec((MOE_BLOCK, D_MODEL), lambda t, ysrc, ug: (t // G, 0)),
            pl.BlockSpec((1, 8, D_MODEL), lambda t, ysrc, ug: (t // G, 0, 0)),
            full(ws_gu), full(ws_d), full(final_g),
        ],
        out_specs=pl.BlockSpec((MOE_BLOCK, D_MODEL), lambda t, ysrc, ug: (t // G, 0)),
        scratch_shapes=[
            pltpu.VMEM((MOE_BLOCK, D_MODEL), F32),
            pltpu.VMEM((2, GROUP_SLOTS, D_MODEL), BF16),
            pltpu.SemaphoreType.DMA((2,)),
        ],
    )
    return pl.pallas_call(
        functools.partial(_combine_kernel, final=final),
        grid_spec=grid_spec,
        out_shape=jax.ShapeDtypeStruct((n_tok, D_MODEL), F32),
        compiler_params=pltpu.CompilerParams(
            dimension_semantics=("arbitrary",), vmem_limit_bytes=VMEM_LIMIT),
        name="moe_combine",
    )(plan["ysrc"], plan["used_groups_b"], y_units, plan["rank_bt"], comb_bt,
      plan["e_slot"], plan["r_slot"], h2, x1, mod_tile, ws_gu, ws_d, final_g)


def _level_ids():
    t = np.arange(CHUNK)
    x = t[:, None] ^ t[None, :]
    lvl = np.zeros((CHUNK, CHUNK), np.int32)
    nz = x > 0
    lvl[nz] = np.floor(np.log2(x[nz])).astype(np.int32) + 1
    past_f = t[:, None] >= t[None, :]
    return jnp.asarray(np.where(past_f, lvl, -1)), jnp.asarray(np.where(past_f.T, lvl, -1))


def kernel(x_prompt, x_sample, state_hgrn, c, c_ctx, w_ada, b_ada, norm1_g, w_in, hgrn_lb_logits, hgrn_norm_g, w_proj_hgrn, conv_dw_w, conv_dw_b, conv_norm_g, conv_norm_b, w_proj_conv, w_out, norm2_g, w_router, router_bias, w_expert_gate_up, w_expert_down, w_shared_gate_up, w_shared_down, final_norm_g):
    n_ctx, ctx_len, d = x_prompt.shape
    n_lat, lat_len, _ = x_sample.shape
    depth = w_ada.shape[0]
    assert d == D_MODEL and ctx_len * 4 == STEP and lat_len == STEP and n_ctx % 4 == 0
    ctx_steps = n_ctx * ctx_len // STEP
    n_steps = ctx_steps + n_lat
    n_tok = n_steps * STEP
    assert n_tok == N_TOKENS and MOE_BLOCK == POST_TILE
    tiles_per_step = STEP // POST_TILE

    x_all = jnp.concatenate([x_prompt.reshape(-1, d), x_sample.reshape(-1, d)], axis=0)

    cc = jnp.zeros((8, d), F32).at[:n_lat].set(c.astype(F32)).at[n_lat].set(c_ctx.astype(F32))
    mod = _ada(cc, w_ada, b_ada).reshape(depth, 8, N_MOD, d)
    step_src = np.array([n_lat] * ctx_steps + list(range(n_lat)))
    mod_step = jnp.pad(mod[:, step_src], ((0, 0), (0, 0), (0, 8 - N_MOD), (0, 0)))
    mod_tile = jnp.repeat(mod_step, tiles_per_step, axis=1)

    nc = jnp.asarray([ctx_len // CHUNK] * ctx_steps + [lat_len // CHUNK] * n_lat, jnp.int32)
    tile_pos = np.arange(n_lat * tiles_per_step) % tiles_per_step
    pv = jnp.asarray(np.concatenate([np.zeros(ctx_steps * tiles_per_step), tile_pos > 0]), jnp.int32)
    nv = jnp.asarray(np.concatenate([np.zeros(ctx_steps * tiles_per_step), tile_pos < tiles_per_step - 1]), jnp.int32)

    lbv = jnp.cumsum(jax.nn.softmax(hgrn_lb_logits.astype(F32), axis=0), axis=0)
    lbv = (lbv - lbv[:1]).reshape(depth, 2, N_HEADS, HEAD_DIM)
    lvl_f, lvl_b = _level_ids()

    xs = x_all
    ctx_states = []
    for l in range(depth):
        lb = lbv[l]
        gp = jnp.stack([jnp.log(lb[0]), jnp.log1p(-lb[0]), 1.0 - lb[0],
                        jnp.log(lb[1]), jnp.log1p(-lb[1]), 1.0 - lb[1],
                        hgrn_norm_g[l].reshape(N_HEADS, HEAD_DIM).astype(F32),
                        jnp.zeros((N_HEADS, HEAD_DIM), F32)], axis=1)
        s0 = jnp.concatenate([jnp.zeros((ctx_steps, 2, N_HEADS, HEAD_DIM, HEAD_DIM), F32),
                              state_hgrn[:, l].astype(F32)], axis=0)
        g1 = norm1_g[l].reshape(1, d).astype(F32)

        o_all, states = _scan(nc, xs, mod_step[l], g1, w_in, l, gp, s0, lvl_f, lvl_b)
        ctx_states.append(states[:ctx_steps].reshape(n_ctx, 2, N_HEADS, HEAD_DIM, HEAD_DIM))

        cw = jnp.pad(conv_dw_w[l].astype(F32), ((0, 1), (0, 0)))
        x1, h2, logits_t = _post(
            pv, nv, xs, mod_tile[l], o_all, w_in, l, w_proj_hgrn[l], w_proj_conv[l], w_out[l],
            w_router[l].T.astype(F32), g1, cw,
            conv_dw_b[l].reshape(1, -1).astype(F32), conv_norm_g[l].reshape(1, -1).astype(F32),
            conv_norm_b[l].reshape(1, -1).astype(F32), norm2_g[l].reshape(1, d).astype(F32))

        comb_t, chosen_t = _route(logits_t, router_bias[l].reshape(N_EXPERTS, 1).astype(F32))
        plan = _routing_plan(chosen_t)
        comb_bt = comb_t.reshape(N_EXPERTS, n_tok // MOE_BLOCK, MOE_BLOCK).transpose(1, 2, 0).astype(BF16)
        xp = _permute(plan, h2)
        y_sorted = _expert_ffn(plan, xp, w_expert_gate_up, w_expert_down, l)
        xs = _combine(plan, y_sorted, comb_bt, h2, x1, mod_tile[l], w_shared_gate_up[l].astype(BF16),
                      w_shared_down[l].astype(BF16), final_norm_g.reshape(1, d).astype(F32),
                      final=(l == depth - 1))

    n_ctx_tok = n_ctx * ctx_len
    y_prompt = xs[:n_ctx_tok].reshape(x_prompt.shape).astype(x_prompt.dtype)
    y_sample = xs[n_ctx_tok:].reshape(x_sample.shape).astype(x_sample.dtype)
    new_state = jnp.stack(ctx_states, axis=1).astype(x_prompt.dtype)
    return (y_prompt, y_sample, new_state)
```

```python
import functools

import numpy as np
import jax
import jax.numpy as jnp
from jax import lax
from jax.experimental import pallas as pl
from jax.experimental.pallas import tpu as pltpu

F32 = jnp.float32
BF16 = jnp.bfloat16
HIGHEST = lax.Precision.HIGHEST

D_MODEL = 1024
N_HEADS = 8
HEAD_DIM = 128
CONV_DIM = 512
CONV_WIDTH = 31
CONV_HALO = 16
N_EXPERTS = 64
N_GROUPS = 8
GROUP_SIZE = N_EXPERTS // N_GROUPS
TOPK_GROUPS = 4
TOP_K = 8
EXPERT_FF = 256
ROUTED_SCALE = 2.5
N_MOD = 6
EPS = 1e-6

CHUNK = 128
N_LEVELS = 7
STEP = 1024
POST_TILE = 256
MOE_BLOCK = 256
UNIT = 16
FFN_TILE = 256
UNITS_PER_TILE = FFN_TILE // UNIT
TILES_PER_STEP = 2
SLOT_MAX = MOE_BLOCK * TOP_K + N_EXPERTS * UNIT
UNITS_PER_BLOCK = SLOT_MAX // UNIT
GROUP_UNITS = 32
GROUP_SLOTS = GROUP_UNITS * UNIT
GROUPS_PER_BLOCK = UNITS_PER_BLOCK // GROUP_UNITS
N_TOKENS = 8192
N_TILES_MAX = (N_TOKENS * TOP_K // UNIT + (N_TOKENS // MOE_BLOCK) * N_EXPERTS) // UNITS_PER_TILE + N_EXPERTS
HEAD_COLS = 5 * HEAD_DIM
VMEM_LIMIT = 52 * 1024 * 1024


def _dot(a, b):
    return jnp.dot(a, b, preferred_element_type=F32)


def _dot_nt(a, b):
    return lax.dot_general(a, b, (((1,), (1,)), ((), ())), preferred_element_type=F32)


def _dot_tn(a, b):
    return lax.dot_general(a, b, (((0,), (0,)), ((), ())), preferred_element_type=F32)


def _sigmoid(x):
    return 1.0 / (1.0 + jnp.exp(-x))


def _silu(x):
    return x * _sigmoid(x)


def _rms_mod(x, g, scale, shift):
    ms = jnp.mean(x * x, axis=-1, keepdims=True)
    return x * lax.rsqrt(ms + EPS) * g * (1.0 + scale) + shift


def _ada_kernel(c_ref, w_ref, b_ref, o_ref):
    cc = c_ref[...]
    o_ref[0] = jnp.dot(_silu(cc), w_ref[0], preferred_element_type=F32, precision=HIGHEST) + b_ref[0]


def _ada(cc, w_ada, b_ada):
    depth, d, n = w_ada.shape
    tn = 1536
    return pl.pallas_call(
        _ada_kernel,
        grid=(depth, n // tn),
        in_specs=[
            pl.BlockSpec((8, d), lambda l, j: (0, 0)),
            pl.BlockSpec((1, d, tn), lambda l, j: (l, 0, j)),
            pl.BlockSpec((1, 1, tn), lambda l, j: (l, 0, j)),
        ],
        out_specs=pl.BlockSpec((1, 8, tn), lambda l, j: (l, 0, j)),
        out_shape=jax.ShapeDtypeStruct((depth, 8, n), F32),
        compiler_params=pltpu.CompilerParams(vmem_limit_bytes=VMEM_LIMIT),
        name="ada_mod",
    )(cc, w_ada, b_ada.reshape(depth, 1, n))


def _forget_gate(z, log_lb, log1m_lb, one_m_lb):
    e = jnp.exp(-jnp.abs(z))
    r = 1.0 / (1.0 + e)
    log_sig = jnp.minimum(z, 0.0) - jnp.log(1.0 + e)
    k = one_m_lb * jnp.where(z > 0, e * r, r)
    b = log1m_lb + log_sig
    log_f = jnp.maximum(log_lb, b) + jnp.log(1.0 + jnp.exp(-jnp.abs(log_lb - b)))
    return log_f, k


def _level_reference(cum_ref, blk, fwd):
    half = blk // 2
    pieces = []
    if blk >= 8:
        for i in range(CHUNK // blk):
            row = i * blk + (half - 1 if fwd else half)
            pieces.append(jnp.broadcast_to(cum_ref[row:row + 1, :], (blk, HEAD_DIM)))
    else:
        sub = lax.broadcasted_iota(jnp.int32, (8, HEAD_DIM), 0)
        for i in range(CHUNK // 8):
            lo_row = 8 * i + (1 if fwd else 2)
            hi_row = 8 * i + (5 if fwd else 6)
            lo = jnp.broadcast_to(cum_ref[lo_row:lo_row + 1, :], (8, HEAD_DIM))
            hi = jnp.broadcast_to(cum_ref[hi_row:hi_row + 1, :], (8, HEAD_DIM))
            pieces.append(jnp.where(sub >= 4, hi, lo))
    return pieces[0] if len(pieces) == 1 else jnp.concatenate(pieces, axis=0)


def _chunk_step(q, k, v, log_f, st, lvl, cum_ref, fwd):
    r_idx = lax.broadcasted_iota(jnp.int32, (CHUNK, CHUNK), 0)
    c_idx = lax.broadcasted_iota(jnp.int32, (CHUNK, CHUNK), 1)
    tri = jnp.where((r_idx >= c_idx) if fwd else (r_idx <= c_idx), 1.0, 0.0).astype(F32)
    cum = jnp.dot(tri, log_f, preferred_element_type=F32, precision=HIGHEST)
    cum_ref[...] = cum

    scores = jnp.where(lvl == 0, _dot_nt(q.astype(BF16), k.astype(BF16)), 0.0)
    for lev in range(1, N_LEVELS + 1):
        if lev == 1:
            qe = q * jnp.exp(log_f)
            ke = k
        else:
            e = jnp.exp(-jnp.abs(cum - _level_reference(cum_ref, 1 << lev, fwd)))
            qe = q * e
            ke = k * e
        s = _dot_nt(qe.astype(BF16), ke.astype(BF16))
        scores = jnp.where(lvl == lev, s, scores)

    total = cum_ref[CHUNK - 1:CHUNK, :] if fwd else cum_ref[0:1, :]
    st_b = st.astype(BF16)
    o = _dot(scores.astype(BF16), v.astype(BF16)) + _dot_nt((q * jnp.exp(cum)).astype(BF16), st_b)
    k_st = k * jnp.exp(total - cum)
    st_new = st * jnp.exp(total) + _dot_tn(v.astype(BF16), k_st.astype(BF16))
    return o, st_new


def _scan_kernel(nc_ref, x_ref, mod_ref, g1_ref, wq_ref, wv_ref, wf_ref, wb_ref, wg_ref, gp_ref, s0_ref,
                 lvlf_ref, lvlb_ref, o_out, st_out, h_ref, w_ref, z_ref, of_ref, ob_ref, cumf_ref, cumb_ref):
    step = pl.program_id(0)
    head = pl.program_id(1)
    n_seq_chunks = nc_ref[step]
    n_chunks = STEP // CHUNK

    @pl.when(head == 0)
    def _():
        g1 = g1_ref[...]
        shift = mod_ref[0, 0:1, :]
        scale = mod_ref[0, 1:2, :]

        def body(i, carry):
            rows = pl.ds(pl.multiple_of(i * CHUNK, CHUNK), CHUNK)
            h_ref[rows, :] = _rms_mod(x_ref[rows, :], g1, scale, shift).astype(BF16)
            return carry

        lax.fori_loop(0, n_chunks, body, 0)

    for j, wj_ref in enumerate((wq_ref, wv_ref, wf_ref, wb_ref, wg_ref)):
        w_ref[:, j * HEAD_DIM:(j + 1) * HEAD_DIM] = wj_ref[0].astype(BF16)
    z_ref[...] = _dot(h_ref[...], w_ref[...])
    st_out[...] = jnp.zeros(st_out.shape, F32)
    st0 = (s0_ref[0, 0, 0].T, s0_ref[0, 1, 0].T)

    def one_direction(c, st, fwd):
        d = 0 if fwd else 1
        zcol = (2 if fwd else 3) * HEAD_DIM
        pos = lax.rem(c, n_seq_chunks)
        first = (pos == 0) if fwd else (pos == n_seq_chunks - 1)
        rows = pl.ds(c * CHUNK, CHUNK)
        q = z_ref[rows, 0:HEAD_DIM]
        v = z_ref[rows, HEAD_DIM:2 * HEAD_DIM]
        log_f, k = _forget_gate(z_ref[rows, zcol:zcol + HEAD_DIM], gp_ref[0, 3 * d:3 * d + 1, :],
                                gp_ref[0, 3 * d + 1:3 * d + 2, :], gp_ref[0, 3 * d + 2:3 * d + 3, :])
        st = jnp.where(first, st0[d], st)
        o, st = _chunk_step(q, k, v, log_f, st, (lvlf_ref if fwd else lvlb_ref)[...],
                            cumf_ref if fwd else cumb_ref, fwd)

        (of_ref if fwd else ob_ref)[rows, :] = o
        st_out[0, lax.div(c, n_seq_chunks), d, 0] = st.T
        return st

    carry = st0
    for i in range(n_chunks):
        carry = (one_direction(i, carry[0], True), one_direction(n_chunks - 1 - i, carry[1], False))

    norm_g = gp_ref[0, 6:7, :]

    def finish(i, carry):
        rows = pl.ds(pl.multiple_of(i * CHUNK, CHUNK), CHUNK)
        o = of_ref[rows, :] + ob_ref[rows, :]
        o = o * lax.rsqrt(jnp.mean(o * o, axis=-1, keepdims=True) + EPS)
        og = z_ref[rows, 4 * HEAD_DIM:5 * HEAD_DIM]
        o_out[rows, :] = (o * norm_g * _silu(og)).astype(BF16)
        return carry

    lax.fori_loop(0, n_chunks, finish, 0)


def _scan(nc, x_all, mod_step, g1, w_in, layer, gp, s0, lvl_f, lvl_b):
    n_tok = x_all.shape[0]
    n_steps = n_tok // STEP
    grid_spec = pltpu.PrefetchScalarGridSpec(
        num_scalar_prefetch=1,
        grid=(n_steps, N_HEADS),
        in_specs=[
            pl.BlockSpec((STEP, D_MODEL), lambda s, h, nc: (s, 0)),
            pl.BlockSpec((1, 8, D_MODEL), lambda s, h, nc: (s, 0, 0)),
            pl.BlockSpec((1, D_MODEL), lambda s, h, nc: (0, 0)),
        ] + [
            pl.BlockSpec((1, D_MODEL, HEAD_DIM), lambda s, h, nc, j=j: (layer, 0, j * N_HEADS + h))
            for j in range(5)
        ] + [
            pl.BlockSpec((1, 8, HEAD_DIM), lambda s, h, nc: (h, 0, 0)),
            pl.BlockSpec((1, 2, 1, HEAD_DIM, HEAD_DIM), lambda s, h, nc: (s, 0, h, 0, 0)),
            pl.BlockSpec((CHUNK, CHUNK), lambda s, h, nc: (0, 0)),
            pl.BlockSpec((CHUNK, CHUNK), lambda s, h, nc: (0, 0)),
        ],
        out_specs=[
            pl.BlockSpec((STEP, HEAD_DIM), lambda s, h, nc: (s, h)),
            pl.BlockSpec((1, 4, 2, 1, HEAD_DIM, HEAD_DIM), lambda s, h, nc: (s, 0, 0, h, 0, 0)),
        ],
        scratch_shapes=[
            pltpu.VMEM((STEP, D_MODEL), BF16),
            pltpu.VMEM((D_MODEL, HEAD_COLS), BF16),
            pltpu.VMEM((STEP, HEAD_COLS), F32),
            pltpu.VMEM((STEP, HEAD_DIM), F32),
            pltpu.VMEM((STEP, HEAD_DIM), F32),
            pltpu.VMEM((CHUNK, HEAD_DIM), F32),
            pltpu.VMEM((CHUNK, HEAD_DIM), F32),
        ],
    )
    return pl.pallas_call(
        _scan_kernel,
        grid_spec=grid_spec,
        out_shape=[
            jax.ShapeDtypeStruct((n_tok, D_MODEL), BF16),
            jax.ShapeDtypeStruct((n_steps, 4, 2, N_HEADS, HEAD_DIM, HEAD_DIM), F32),
        ],
        compiler_params=pltpu.CompilerParams(
            dimension_semantics=("arbitrary", "arbitrary"), vmem_limit_bytes=VMEM_LIMIT),
        name="hgrn_scan",
    )(nc, x_all, mod_step, g1, *([w_in] * 5), gp, s0, lvl_f, lvl_b)


def _post_kernel(pv_ref, nv_ref, x_ref, xp_ref, xn_ref, mod_ref, o_ref, wu_f32, wgz_f32, wph_f32,
                 wpc_f32, wo_f32, wr_ref, g1_ref, cw_ref, cb_ref, lg_ref, lb_ref, g2_ref,
                 x1_out, h2_out, lg_out, cu_ref, wu_ref, wgz_ref, wph_ref, wpc_ref, wo_ref):
    i = pl.program_id(0)

    @pl.when(i == 0)
    def _():
        for dst, src in ((wu_ref, wu_f32), (wgz_ref, wgz_f32), (wph_ref, wph_f32),
                         (wpc_ref, wpc_f32), (wo_ref, wo_f32)):
            dst[...] = (src[0] if len(src.shape) == 3 else src[...]).astype(BF16)

    g1 = g1_ref[...]
    shift1 = mod_ref[0, 0:1, :]
    scale1 = mod_ref[0, 1:2, :]
    gate1 = mod_ref[0, 2:3, :]
    shift2 = mod_ref[0, 3:4, :]
    scale2 = mod_ref[0, 4:5, :]

    def glu(xv):
        h = _rms_mod(xv, g1, scale1, shift1).astype(BF16)
        u = _dot(h, wu_ref[...])
        return h, u[:, :CONV_DIM] * _sigmoid(u[:, CONV_DIM:])

    x = x_ref[...]
    h, glu_mid = glu(x)
    _, glu_prev = glu(xp_ref[...])
    _, glu_next = glu(xn_ref[...])
    cu_ref[0, 0:CONV_HALO, :] = glu_prev * pv_ref[i].astype(F32)
    cu_ref[0, CONV_HALO:CONV_HALO + POST_TILE, :] = glu_mid
    cu_ref[0, CONV_HALO + POST_TILE:, :] = glu_next * nv_ref[i].astype(F32)
    n_keep = POST_TILE + 2 * CONV_HALO - 8
    for k in range(1, 8):
        cu_ref[k, 0:n_keep, :] = cu_ref[0, k:k + n_keep, :]

    off = CONV_HALO - CONV_WIDTH // 2
    acc = jnp.zeros((POST_TILE, CONV_DIM), F32) + cb_ref[...]
    for j in range(CONV_WIDTH):
        base, k = divmod(off + j, 8)
        acc = acc + cu_ref[k, 8 * base:8 * base + POST_TILE, :] * cw_ref[j:j + 1, :]
    mu = jnp.mean(acc, axis=-1, keepdims=True)
    cen = acc - mu
    var = jnp.mean(cen * cen, axis=-1, keepdims=True)
    cv = cen * lax.rsqrt(var + EPS) * lg_ref[...] + lb_ref[...]
    y_b = _dot(_silu(cv).astype(BF16), wpc_ref[...])

    y_a = _dot(o_ref[...], wph_ref[...])
    gz = _sigmoid(_dot(h, wgz_ref[...]))
    merged = gz[:, :D_MODEL] * y_a + gz[:, D_MODEL:] * y_b
    x1 = x + gate1 * _dot(merged.astype(BF16), wo_ref[...])
    x1_out[...] = x1
    h2 = _rms_mod(x1, g2_ref[...], scale2, shift2)
    h2_out[...] = h2.astype(BF16)
    lg_out[...] = lax.dot_general(wr_ref[...], h2, (((1,), (1,)), ((), ())),
                                  preferred_element_type=F32, precision=HIGHEST)


def _post(pv, nv, x_all, mod_tile, o_all, w_in, layer, w_ph, w_pc, w_o, w_rt, g1, cw, cb, lg, lb, g2):
    n_tok = x_all.shape[0]
    n_tiles = n_tok // POST_TILE
    halo_per_tile = POST_TILE // CONV_HALO
    n_halo_blocks = n_tok // CONV_HALO
    glu_cols, gate_cols = 2 * CONV_DIM, 2 * D_MODEL
    glu_start = 5 * N_HEADS * HEAD_DIM
    assert glu_start % glu_cols == 0 and (glu_start + glu_cols) % gate_cols == 0

    def full(a):
        return pl.BlockSpec(a.shape, lambda i, pv, nv: (0,) * a.ndim)

    def resident(shape, index):
        return pl.BlockSpec(shape, lambda i, pv, nv: index, pipeline_mode=pl.Buffered(1))

    grid_spec = pltpu.PrefetchScalarGridSpec(
        num_scalar_prefetch=2,
        grid=(n_tiles,),
        in_specs=[
            pl.BlockSpec((POST_TILE, D_MODEL), lambda i, pv, nv: (i, 0)),
            pl.BlockSpec((CONV_HALO, D_MODEL), lambda i, pv, nv: (jnp.maximum(i * halo_per_tile - 1, 0), 0)),
            pl.BlockSpec((CONV_HALO, D_MODEL),
                         lambda i, pv, nv: (jnp.minimum((i + 1) * halo_per_tile, n_halo_blocks - 1), 0)),
            pl.BlockSpec((1, 8, D_MODEL), lambda i, pv, nv: (i, 0, 0)),
            pl.BlockSpec((POST_TILE, D_MODEL), lambda i, pv, nv: (i, 0)),
            resident((1, D_MODEL, glu_cols), (layer, 0, glu_start // glu_cols)),
            resident((1, D_MODEL, gate_cols), (layer, 0, (glu_start + glu_cols) // gate_cols)),
            resident(w_ph.shape, (0, 0)), resident(w_pc.shape, (0, 0)), resident(w_o.shape, (0, 0)),
            full(w_rt), full(g1), full(cw), full(cb), full(lg), full(lb), full(g2),
        ],
        out_specs=[
            pl.BlockSpec((POST_TILE, D_MODEL), lambda i, pv, nv: (i, 0)),
            pl.BlockSpec((POST_TILE, D_MODEL), lambda i, pv, nv: (i, 0)),
            pl.BlockSpec((N_EXPERTS, POST_TILE), lambda i, pv, nv: (0, i)),
        ],
        scratch_shapes=[
            pltpu.VMEM((8, POST_TILE + 2 * CONV_HALO, CONV_DIM), F32),
            pltpu.VMEM((D_MODEL, glu_cols), BF16), pltpu.VMEM((D_MODEL, gate_cols), BF16),
            pltpu.VMEM(w_ph.shape, BF16), pltpu.VMEM(w_pc.shape, BF16), pltpu.VMEM(w_o.shape, BF16),
        ],
    )
    return pl.pallas_call(
        _post_kernel,
        grid_spec=grid_spec,
        out_shape=[
            jax.ShapeDtypeStruct((n_tok, D_MODEL), F32),
            jax.ShapeDtypeStruct((n_tok, D_MODEL), BF16),
            jax.ShapeDtypeStruct((N_EXPERTS, n_tok), F32),
        ],
        compiler_params=pltpu.CompilerParams(
            dimension_semantics=("arbitrary",), vmem_limit_bytes=VMEM_LIMIT),
        name="post_mixer",
    )(pv, nv, x_all, x_all, x_all, mod_tile, o_all, w_in, w_in, w_ph, w_pc, w_o, w_rt, g1, cw, cb, lg, lb, g2)


def _route_kernel(lg_ref, bias_ref, comb_out, chosen_out):
    n = lg_ref.shape[1]
    scores = _sigmoid(lg_ref[...])
    sel = scores + bias_ref[...]
    neg = jnp.float32(-jnp.inf)

    sel3 = sel.reshape(N_GROUPS, GROUP_SIZE, n)
    m1 = jnp.max(sel3, axis=1, keepdims=True)
    is_m1 = sel3 == m1
    n_m1 = jnp.sum(is_m1.astype(F32), axis=1, keepdims=True)
    m2 = jnp.max(jnp.where(is_m1, neg, sel3), axis=1, keepdims=True)
    grp = (m1 + jnp.where(n_m1 > 1.5, m1, m2)).reshape(N_GROUPS, n)

    gidx = lax.broadcasted_iota(jnp.int32, (N_GROUPS, n), 0)
    rank = jnp.zeros((N_GROUPS, n), F32)
    for g in range(N_GROUPS):
        other = grp[g:g + 1, :]
        ahead = (other > grp) | ((other == grp) & (g < gidx))
        rank = rank + ahead.astype(F32)
    keep_g = rank < TOPK_GROUPS - 0.5
    keep = jnp.broadcast_to(keep_g.reshape(N_GROUPS, 1, n), (N_GROUPS, GROUP_SIZE, n)).reshape(N_EXPERTS, n)
    cand = jnp.where(keep, sel, neg)

    eidx = lax.broadcasted_iota(jnp.int32, (N_EXPERTS, n), 0)
    chosen = jnp.zeros((N_EXPERTS, n), F32)
    for _ in range(TOP_K):
        best = jnp.max(cand, axis=0, keepdims=True)
        first = jnp.min(jnp.where(cand == best, eidx, N_EXPERTS), axis=0, keepdims=True)
        hit = eidx == first
        chosen = jnp.where(hit, 1.0, chosen)
        cand = jnp.where(hit, neg, cand)
    w = scores * chosen
    comb_out[...] = w / jnp.sum(w, axis=0, keepdims=True) * ROUTED_SCALE
    chosen_out[...] = chosen


def _route(logits_t, bias):
    n_tok = logits_t.shape[1]
    tile = 512
    return pl.pallas_call(
        _route_kernel,
        grid=(n_tok // tile,),
        in_specs=[
            pl.BlockSpec((N_EXPERTS, tile), lambda i: (0, i)),
            pl.BlockSpec((N_EXPERTS, 1), lambda i: (0, 0)),
        ],
        out_specs=[pl.BlockSpec((N_EXPERTS, tile), lambda i: (0, i)),
                   pl.BlockSpec((N_EXPERTS, tile), lambda i: (0, i))],
        out_shape=[jax.ShapeDtypeStruct((N_EXPERTS, n_tok), F32),
                   jax.ShapeDtypeStruct((N_EXPERTS, n_tok), F32)],
        name="router",
    )(logits_t, bias)


def _count_le(bounds, idx):
    return jnp.sum(jnp.where(bounds <= idx, 1.0, 0.0), axis=0, keepdims=True)


def _pick_row(onehot, col):
    return jnp.sum(jnp.where(onehot, col, 0.0), axis=0, keepdims=True)


def _plan_kernel(ch_ref, rankt_out, es_out, rs_out, src_out, et_out, ysrc_out, misc_out):
    nb = N_TOKENS // MOE_BLOCK
    n_src = N_TILES_MAX * UNITS_PER_TILE
    e_col = lax.broadcasted_iota(jnp.int32, (N_EXPERTS, 1), 0).astype(F32)
    blk_lane = lax.broadcasted_iota(jnp.int32, (N_EXPERTS, 128), 1)
    t_r = lax.broadcasted_iota(jnp.int32, (MOE_BLOCK, MOE_BLOCK), 0)
    t_c = lax.broadcasted_iota(jnp.int32, (MOE_BLOCK, MOE_BLOCK), 1)
    earlier = jnp.where(t_c < t_r, 1.0, 0.0).astype(BF16)
    e_r = lax.broadcasted_iota(jnp.int32, (N_EXPERTS, N_EXPERTS), 0)
    e_c = lax.broadcasted_iota(jnp.int32, (N_EXPERTS, N_EXPERTS), 1)
    upto = jnp.where(e_c <= e_r, 1.0, 0.0).astype(F32)
    slot = lax.broadcasted_iota(jnp.int32, (1, SLOT_MAX), 1).astype(F32)

    def col_cumsum(col):
        wide = jnp.broadcast_to(col, (N_EXPERTS, 128))
        return jnp.dot(upto, wide, preferred_element_type=F32, precision=HIGHEST)[:, 0:1]

    def block_stats(b, carry):
        units_mat, start_mat = carry
        cb = ch_ref[:, pl.ds(pl.multiple_of(b * MOE_BLOCK, MOE_BLOCK), MOE_BLOCK)]
        rank_t = _dot_nt(earlier, cb.astype(BF16))
        cb_t = cb.T
        rankt_out[b] = jnp.where(cb_t > 0, rank_t, -1.0).astype(BF16)
        cnt = jnp.sum(cb, axis=1, keepdims=True)
        units = jnp.floor((cnt + (UNIT - 1)) * (1.0 / UNIT))
        incl = col_cumsum(units)
        start = incl - units
        e_slot = _count_le(incl * UNIT, slot)
        onehot = e_col == e_slot
        r = slot - _pick_row(onehot, start * UNIT)
        valid = (e_slot < N_EXPERTS - 0.5) & (r < _pick_row(onehot, cnt))
        es_out[b] = jnp.where(valid, e_slot, -1.0).astype(jnp.int32)
        rs_out[b] = jnp.where(valid, r, -2.0)
        units_mat = jnp.where(blk_lane == b, units, units_mat)
        start_mat = jnp.where(blk_lane == b, start, start_mat)
        return units_mat, start_mat

    zeros = jnp.zeros((N_EXPERTS, 128), F32)
    units_mat, start_mat = lax.fori_loop(0, nb, block_stats, (zeros, zeros))

    tot = jnp.sum(units_mat, axis=1, keepdims=True)
    tiles_e = jnp.floor((tot + (UNITS_PER_TILE - 1)) * (1.0 / UNITS_PER_TILE))
    incl_t = col_cumsum(tiles_e)
    start_t = incl_t - tiles_e
    n_used = incl_t[N_EXPERTS - 1:N_EXPERTS, :]
    b_r = lax.broadcasted_iota(jnp.int32, (128, 128), 0)
    b_c = lax.broadcasted_iota(jnp.int32, (128, 128), 1)
    before = jnp.where(b_r < b_c, 1.0, 0.0).astype(F32)
    cum_b = jnp.dot(units_mat, before, preferred_element_type=F32, precision=HIGHEST)
    run_pos = start_t * UNITS_PER_TILE + cum_b

    tile_idx = lax.broadcasted_iota(jnp.int32, (1, 512), 1).astype(F32)
    et_out[...] = jnp.minimum(_count_le(incl_t, tile_idx), N_EXPERTS - 1.0).astype(jnp.int32)

    incl_ub_t = (cum_b + units_mat).T
    cum_b_t = cum_b.T
    start_mat_t = start_mat.T
    b_col = lax.broadcasted_iota(jnp.int32, (128, 1), 0)
    chunk = 1024

    def src_chunk(c, carry):
        off = pl.multiple_of(c * chunk, chunk)
        p = (lax.broadcasted_iota(jnp.int32, (1, chunk), 1) + off).astype(F32)
        e_p = jnp.minimum(_count_le(incl_t * UNITS_PER_TILE, p), N_EXPERTS - 1.0)
        onehot_e = e_col == e_p
        q = p - _pick_row(onehot_e, start_t * UNITS_PER_TILE)
        valid = (p < n_used * UNITS_PER_TILE) & (q < _pick_row(onehot_e, tot))
        sel = jnp.where(onehot_e, 1.0, 0.0).astype(F32)
        incl_sel = jnp.dot(incl_ub_t, sel, preferred_element_type=F32, precision=HIGHEST)
        b_p = jnp.sum(jnp.where((incl_sel <= q) & (b_col < nb), 1.0, 0.0), axis=0, keepdims=True)
        b_p = jnp.minimum(b_p, nb - 1.0)
        onehot_b = b_col.astype(F32) == b_p
        cum_sel = jnp.dot(cum_b_t, sel, preferred_element_type=F32, precision=HIGHEST)
        start_sel = jnp.dot(start_mat_t, sel, preferred_element_type=F32, precision=HIGHEST)
        unit = b_p * UNITS_PER_BLOCK + _pick_row(onehot_b, start_sel) + q - _pick_row(onehot_b, cum_sel)
        src_out[:, pl.ds(off, chunk)] = jnp.where(valid, unit, 0.0).astype(jnp.int32)
        return carry

    lax.fori_loop(0, n_src // chunk, src_chunk, 0)

    unit_idx = lax.broadcasted_iota(jnp.int32, (1, 256), 1).astype(F32)
    misc_lane = lax.broadcasted_iota(jnp.int32, (1, 128), 1)

    def block_units(b, used):
        here = blk_lane == b
        units = jnp.sum(jnp.where(here, units_mat, 0.0), axis=1, keepdims=True)
        start = jnp.sum(jnp.where(here, start_mat, 0.0), axis=1, keepdims=True)
        pos = jnp.sum(jnp.where(here, run_pos, 0.0), axis=1, keepdims=True)
        incl = start + units
        e_unit = _count_le(incl, unit_idx)
        onehot = e_col == e_unit
        val = _pick_row(onehot, pos - start) + unit_idx
        ysrc_out[b] = jnp.where(e_unit < N_EXPERTS - 0.5, val, 0.0).astype(jnp.int32)
        return jnp.where(misc_lane == b, incl[N_EXPERTS - 1:N_EXPERTS, :], used)

    used = lax.fori_loop(0, nb, block_units, jnp.zeros((1, 128), F32))
    misc_out[0:1, :] = used.astype(jnp.int32)
    misc_out[1:2, :] = jnp.broadcast_to(n_used, (1, 128)).astype(jnp.int32)
    misc_out[2:8, :] = jnp.zeros((6, 128), jnp.int32)


def _routing_plan(chosen):
    nb = N_TOKENS // MOE_BLOCK
    n_src = N_TILES_MAX * UNITS_PER_TILE
    assert n_src % 1024 == 0 and N_TILES_MAX <= 512 and nb <= 128 and UNITS_PER_BLOCK <= 256
    rank_bt, e_slot, r_slot, src, e_tile, ysrc, misc = pl.pallas_call(
        _plan_kernel,
        out_shape=[
            jax.ShapeDtypeStruct((nb, MOE_BLOCK, N_EXPERTS), BF16),
            jax.ShapeDtypeStruct((nb, 1, SLOT_MAX), jnp.int32),
            jax.ShapeDtypeStruct((nb, 1, SLOT_MAX), F32),
            jax.ShapeDtypeStruct((1, n_src), jnp.int32),
            jax.ShapeDtypeStruct((1, 512), jnp.int32),
            jax.ShapeDtypeStruct((nb, 1, 256), jnp.int32),
            jax.ShapeDtypeStruct((8, 128), jnp.int32),
        ],
        compiler_params=pltpu.CompilerParams(vmem_limit_bytes=VMEM_LIMIT),
        name="moe_plan",
    )(chosen)
    return dict(rank_bt=rank_bt, e_slot=e_slot, r_slot=r_slot, src=src.reshape(-1),
                e_tile=e_tile[0, :N_TILES_MAX], n_used=misc[1, :1],
                ysrc=ysrc[:, 0, :UNITS_PER_BLOCK].reshape(-1))


def _permute_kernel(h_ref, rank_ref, es_ref, rs_ref, xp_out):
    sub_e = lax.broadcasted_iota(jnp.int32, (N_EXPERTS, FFN_TILE), 0)
    for i in range(SLOT_MAX // FFN_TILE):
        rows = slice(i * FFN_TILE, (i + 1) * FFN_TILE)
        onehot_e = jnp.where(sub_e == es_ref[0, :, rows], 1.0, 0.0).astype(BF16)
        slot_rank = _dot(rank_ref[0], onehot_e)
        pick = jnp.where(slot_rank == rs_ref[0, :, rows], 1.0, 0.0).astype(BF16)
        xp_out[0, rows, :] = _dot_tn(pick, h_ref[...]).astype(BF16)


def _permute(plan, h2):
    nb = h2.shape[0] // MOE_BLOCK
    return pl.pallas_call(
        _permute_kernel,
        grid=(nb,),
        in_specs=[
            pl.BlockSpec((MOE_BLOCK, D_MODEL), lambda b: (b, 0)),
            pl.BlockSpec((1, MOE_BLOCK, N_EXPERTS), lambda b: (b, 0, 0)),
            pl.BlockSpec((1, 1, SLOT_MAX), lambda b: (b, 0, 0)),
            pl.BlockSpec((1, 1, SLOT_MAX), lambda b: (b, 0, 0)),
        ],
        out_specs=pl.BlockSpec((1, SLOT_MAX, D_MODEL), lambda b: (b, 0, 0)),
        out_shape=jax.ShapeDtypeStruct((nb, SLOT_MAX, D_MODEL), BF16),
        compiler_params=pltpu.CompilerParams(
            dimension_semantics=("arbitrary",), vmem_limit_bytes=VMEM_LIMIT),
        name="moe_permute",
    )(h2, plan["rank_bt"], plan["e_slot"], plan["r_slot"])


def _unit_copies(table_ref, base, src_hbm, dst_ref, slot, sem, n_units, start):
    for u in range(n_units):
        copy = pltpu.make_async_copy(src_hbm.at[table_ref[base + u]],
                                     dst_ref.at[slot, pl.ds(u * UNIT, UNIT), :], sem.at[slot])
        if start:
            copy.start()
        else:
            copy.wait()


def _ffn_kernel(src_ref, et_ref, nu_ref, xp_hbm, *refs):
    w_refs = refs[:2 * TILES_PER_STEP]
    y_out = refs[2 * TILES_PER_STEP]
    scratch = refs[2 * TILES_PER_STEP + 1:]
    wb_refs, xt_ref, sem = scratch[:2 * TILES_PER_STEP], scratch[2 * TILES_PER_STEP], scratch[2 * TILES_PER_STEP + 1]
    n_units = TILES_PER_STEP * UNITS_PER_TILE
    step = pl.program_id(0)
    slot = lax.rem(step, 2)
    n_used = nu_ref[0]

    def copies(s, sl, start):
        _unit_copies(src_ref, s * n_units, xp_hbm, xt_ref, sl, sem, n_units, start)

    @pl.when((step == 0) & (n_used > 0))
    def _():
        copies(0, 0, True)

    @pl.when((step + 1 < pl.num_programs(0)) & ((step + 1) * TILES_PER_STEP < n_used))
    def _():
        copies(step + 1, 1 - slot, True)

    @pl.when(step * TILES_PER_STEP < n_used)
    def _():
        for t in range(TILES_PER_STEP):
            j = step * TILES_PER_STEP + t

            @pl.when((step == 0) | (et_ref[j] != et_ref[jnp.maximum(j - TILES_PER_STEP, 0)]))
            def _():
                wb_refs[2 * t][...] = w_refs[2 * t][0, 0].astype(BF16)
                wb_refs[2 * t + 1][...] = w_refs[2 * t + 1][0, 0].astype(BF16)

        copies(step, slot, False)
        for t in range(TILES_PER_STEP):
            rows = slice(t * FFN_TILE, (t + 1) * FFN_TILE)
            gu = _dot(xt_ref[slot, rows, :], wb_refs[2 * t][...])
            act = _silu(gu[:, :EXPERT_FF]) * gu[:, EXPERT_FF:]
            y_out[rows, :] = _dot(act.astype(BF16), wb_refs[2 * t + 1][...]).astype(BF16)

    @pl.when(step * TILES_PER_STEP >= n_used)
    def _():
        y_out[...] = jnp.zeros(y_out.shape, BF16)


def _expert_ffn(plan, xp, w_gu, w_d, layer):
    xp_units = xp.reshape(-1, UNIT, D_MODEL)
    assert N_TILES_MAX % TILES_PER_STEP == 0

    def weight_specs(t):
        return [pl.BlockSpec((1, 1, D_MODEL, 2 * EXPERT_FF),
                             lambda s, src, et, nu, t=t: (layer, et[s * TILES_PER_STEP + t], 0, 0)),
                pl.BlockSpec((1, 1, EXPERT_FF, D_MODEL),
                             lambda s, src, et, nu, t=t: (layer, et[s * TILES_PER_STEP + t], 0, 0))]

    grid_spec = pltpu.PrefetchScalarGridSpec(
        num_scalar_prefetch=3,
        grid=(N_TILES_MAX // TILES_PER_STEP,),
        in_specs=[pl.BlockSpec(memory_space=pl.ANY)]
        + [spec for t in range(TILES_PER_STEP) for spec in weight_specs(t)],
        out_specs=pl.BlockSpec((TILES_PER_STEP * FFN_TILE, D_MODEL), lambda s, src, et, nu: (s, 0)),
        scratch_shapes=[
            pltpu.VMEM((D_MODEL, 2 * EXPERT_FF), BF16) if i % 2 == 0 else pltpu.VMEM((EXPERT_FF, D_MODEL), BF16)
            for i in range(2 * TILES_PER_STEP)
        ] + [pltpu.VMEM((2, TILES_PER_STEP * FFN_TILE, D_MODEL), BF16), pltpu.SemaphoreType.DMA((2,))],
    )
    return pl.pallas_call(
        _ffn_kernel,
        grid_spec=grid_spec,
        out_shape=jax.ShapeDtypeStruct((N_TILES_MAX * FFN_TILE, D_MODEL), BF16),
        compiler_params=pltpu.CompilerParams(
            dimension_semantics=("arbitrary",), vmem_limit_bytes=VMEM_LIMIT),
        name="moe_expert_ffn",
    )(plan["src"], plan["e_tile"], plan["n_used"], xp_units, *([w_gu, w_d] * TILES_PER_STEP))


def _combine_kernel(ysrc_ref, y_hbm, rank_ref, comb_ref, er_ref, rr_ref, h_ref, x1_ref, mod_ref,
                    wsgu_ref, wsd_ref, fg_ref, o_ref, yt_ref, sem, *, final):
    b = pl.program_id(0)
    slot = lax.rem(b, 2)

    def copies(bb, sl, start):
        _unit_copies(ysrc_ref, bb * UNITS_PER_BLOCK, y_hbm, yt_ref, sl, sem, UNITS_PER_BLOCK, start)

    @pl.when(b == 0)
    def _():
        copies(0, 0, True)

    @pl.when(b + 1 < pl.num_programs(0))
    def _():
        copies(b + 1, 1 - slot, True)

    gu = _dot(h_ref[...], wsgu_ref[...])
    shared = _dot((_silu(gu[:, :EXPERT_FF]) * gu[:, EXPERT_FF:]).astype(BF16), wsd_ref[...])

    sub_e = lax.broadcasted_iota(jnp.int32, (N_EXPERTS, GROUP_SLOTS), 0)
    pieces = []
    for g in range(GROUPS_PER_BLOCK):
        cols = slice(g * GROUP_SLOTS, (g + 1) * GROUP_SLOTS)
        onehot_e = jnp.where(sub_e == er_ref[0, :, cols], 1.0, 0.0).astype(BF16)
        slot_rank = _dot(rank_ref[0], onehot_e)
        slot_w = _dot(comb_ref[0], onehot_e)
        pieces.append(jnp.where(slot_rank == rr_ref[0, :, cols], slot_w, 0.0).astype(BF16))
    weights = jnp.concatenate(pieces, axis=1)

    copies(b, slot, False)
    x2 = x1_ref[...] + mod_ref[0, 5:6, :] * (shared + _dot(weights, yt_ref[slot]))
    if final:
        ms = jnp.mean(x2 * x2, axis=-1, keepdims=True)
        x2 = x2 * lax.rsqrt(ms + EPS) * fg_ref[...]
    o_ref[...] = x2


def _combine(plan, y_sorted, comb_bt, h2, x1, mod_tile, ws_gu, ws_d, final_g, final):
    n_tok = h2.shape[0]
    nb = n_tok // MOE_BLOCK
    y_units = y_sorted.reshape(-1, UNIT, D_MODEL)

    def full(a):
        return pl.BlockSpec(a.shape, lambda b, ysrc: (0,) * a.ndim)

    grid_spec = pltpu.PrefetchScalarGridSpec(
        num_scalar_prefetch=1,
        grid=(nb,),
        in_specs=[
            pl.BlockSpec(memory_space=pl.ANY),
            pl.BlockSpec((1, MOE_BLOCK, N_EXPERTS), lambda b, ysrc: (b, 0, 0)),
            pl.BlockSpec((1, MOE_BLOCK, N_EXPERTS), lambda b, ysrc: (b, 0, 0)),
            pl.BlockSpec((1, 1, SLOT_MAX), lambda b, ysrc: (b, 0, 0)),
            pl.BlockSpec((1, 1, SLOT_MAX), lambda b, ysrc: (b, 0, 0)),
            pl.BlockSpec((MOE_BLOCK, D_MODEL), lambda b, ysrc: (b, 0)),
            pl.BlockSpec((MOE_BLOCK, D_MODEL), lambda b, ysrc: (b, 0)),
            pl.BlockSpec((1, 8, D_MODEL), lambda b, ysrc: (b, 0, 0)),
            full(ws_gu), full(ws_d), full(final_g),
        ],
        out_specs=pl.BlockSpec((MOE_BLOCK, D_MODEL), lambda b, ysrc: (b, 0)),
        scratch_shapes=[
            pltpu.VMEM((2, SLOT_MAX, D_MODEL), BF16),
            pltpu.SemaphoreType.DMA((2,)),
        ],
    )
    return pl.pallas_call(
        functools.partial(_combine_kernel, final=final),
        grid_spec=grid_spec,
        out_shape=jax.ShapeDtypeStruct((n_tok, D_MODEL), F32),
        compiler_params=pltpu.CompilerParams(
            dimension_semantics=("arbitrary",), vmem_limit_bytes=VMEM_LIMIT),
        name="moe_combine",
    )(plan["ysrc"], y_units, plan["rank_bt"], comb_bt,
      plan["e_slot"], plan["r_slot"], h2, x1, mod_tile, ws_gu, ws_d, final_g)


def _level_ids():
    t = np.arange(CHUNK)
    x = t[:, None] ^ t[None, :]
    lvl = np.zeros((CHUNK, CHUNK), np.int32)
    nz = x > 0
    lvl[nz] = np.floor(np.log2(x[nz])).astype(np.int32) + 1
    past_f = t[:, None] >= t[None, :]
    return jnp.asarray(np.where(past_f, lvl, -1)), jnp.asarray(np.where(past_f.T, lvl, -1))


def kernel(x_prompt, x_sample, state_hgrn, c, c_ctx, w_ada, b_ada, norm1_g, w_in, hgrn_lb_logits, hgrn_norm_g, w_proj_hgrn, conv_dw_w, conv_dw_b, conv_norm_g, conv_norm_b, w_proj_conv, w_out, norm2_g, w_router, router_bias, w_expert_gate_up, w_expert_down, w_shared_gate_up, w_shared_down, final_norm_g):
    n_ctx, ctx_len, d = x_prompt.shape
    n_lat, lat_len, _ = x_sample.shape
    depth = w_ada.shape[0]
    assert d == D_MODEL and ctx_len * 4 == STEP and lat_len == STEP and n_ctx % 4 == 0
    ctx_steps = n_ctx * ctx_len // STEP
    n_steps = ctx_steps + n_lat
    n_tok = n_steps * STEP
    assert n_tok == N_TOKENS and MOE_BLOCK == POST_TILE
    tiles_per_step = STEP // POST_TILE

    x_all = jnp.concatenate([x_prompt.reshape(-1, d), x_sample.reshape(-1, d)], axis=0)

    cc = jnp.zeros((8, d), F32).at[:n_lat].set(c.astype(F32)).at[n_lat].set(c_ctx.astype(F32))
    mod = _ada(cc, w_ada, b_ada).reshape(depth, 8, N_MOD, d)
    step_src = np.array([n_lat] * ctx_steps + list(range(n_lat)))
    mod_step = jnp.pad(mod[:, step_src], ((0, 0), (0, 0), (0, 8 - N_MOD), (0, 0)))
    mod_tile = jnp.repeat(mod_step, tiles_per_step, axis=1)

    nc = jnp.asarray([ctx_len // CHUNK] * ctx_steps + [lat_len // CHUNK] * n_lat, jnp.int32)
    tile_pos = np.arange(n_lat * tiles_per_step) % tiles_per_step
    pv = jnp.asarray(np.concatenate([np.zeros(ctx_steps * tiles_per_step), tile_pos > 0]), jnp.int32)
    nv = jnp.asarray(np.concatenate([np.zeros(ctx_steps * tiles_per_step), tile_pos < tiles_per_step - 1]), jnp.int32)

    lbv = jnp.cumsum(jax.nn.softmax(hgrn_lb_logits.astype(F32), axis=0), axis=0)
    lbv = (lbv - lbv[:1]).reshape(depth, 2, N_HEADS, HEAD_DIM)
    lvl_f, lvl_b = _level_ids()

    xs = x_all
    ctx_states = []
    for l in range(depth):
        lb = lbv[l]
        gp = jnp.stack([jnp.log(lb[0]), jnp.log1p(-lb[0]), 1.0 - lb[0],
                        jnp.log(lb[1]), jnp.log1p(-lb[1]), 1.0 - lb[1],
                        hgrn_norm_g[l].reshape(N_HEADS, HEAD_DIM).astype(F32),
                        jnp.zeros((N_HEADS, HEAD_DIM), F32)], axis=1)
        s0 = jnp.concatenate([jnp.zeros((ctx_steps, 2, N_HEADS, HEAD_DIM, HEAD_DIM), F32),
                              state_hgrn[:, l].astype(F32)], axis=0)
        g1 = norm1_g[l].reshape(1, d).astype(F32)

        o_all, states = _scan(nc, xs, mod_step[l], g1, w_in, l, gp, s0, lvl_f, lvl_b)
        ctx_states.append(states[:ctx_steps].reshape(n_ctx, 2, N_HEADS, HEAD_DIM, HEAD_DIM))

        cw = jnp.pad(conv_dw_w[l].astype(F32), ((0, 1), (0, 0)))
        x1, h2, logits_t = _post(
            pv, nv, xs, mod_tile[l], o_all, w_in, l, w_proj_hgrn[l], w_proj_conv[l], w_out[l],
            w_router[l].T.astype(F32), g1, cw,
            conv_dw_b[l].reshape(1, -1).astype(F32), conv_norm_g[l].reshape(1, -1).astype(F32),
            conv_norm_b[l].reshape(1, -1).astype(F32), norm2_g[l].reshape(1, d).astype(F32))

        comb_t, chosen_t = _route(logits_t, router_bias[l].reshape(N_EXPERTS, 1).astype(F32))
        plan = _routing_plan(chosen_t)
        comb_bt = comb_t.reshape(N_EXPERTS, n_tok // MOE_BLOCK, MOE_BLOCK).transpose(1, 2, 0).astype(BF16)
        xp = _permute(plan, h2)
        y_sorted = _expert_ffn(plan, xp, w_expert_gate_up, w_expert_down, l)
        xs = _combine(plan, y_sorted, comb_bt, h2, x1, mod_tile[l], w_shared_gate_up[l].astype(BF16),
                      w_shared_down[l].astype(BF16), final_norm_g.reshape(1, d).astype(F32),
                      final=(l == depth - 1))

    n_ctx_tok = n_ctx * ctx_len
    y_prompt = xs[:n_ctx_tok].reshape(x_prompt.shape).astype(x_prompt.dtype)
    y_sample = xs[n_ctx_tok:].reshape(x_sample.shape).astype(x_sample.dtype)
    new_state = jnp.stack(ctx_states, axis=1).astype(x_prompt.dtype)
    return (y_prompt, y_sample, new_state)
```

```python
import functools

import numpy as np
import jax
import jax.numpy as jnp
from jax import lax
from jax.experimental import pallas as pl
from jax.experimental.pallas import tpu as pltpu

F32 = jnp.float32
BF16 = jnp.bfloat16
HIGHEST = lax.Precision.HIGHEST

D_MODEL = 1024
N_HEADS = 8
HEAD_DIM = 128
CONV_DIM = 512
CONV_WIDTH = 31
CONV_HALO = 16
N_EXPERTS = 64
N_GROUPS = 8
GROUP_SIZE = N_EXPERTS // N_GROUPS
TOPK_GROUPS = 4
TOP_K = 8
EXPERT_FF = 256
ROUTED_SCALE = 2.5
N_MOD = 6
EPS = 1e-6

CHUNK = 128
N_LEVELS = 7
LOCKSTEP = 2
STEP = 1024
POST_TILE = 256
MOE_BLOCK = 256
UNIT = 16
FFN_TILE = 256
UNITS_PER_TILE = FFN_TILE // UNIT
TILES_PER_STEP = 2
SLOT_MAX = MOE_BLOCK * TOP_K + N_EXPERTS * UNIT
UNITS_PER_BLOCK = SLOT_MAX // UNIT
GROUP_UNITS = 32
GROUP_SLOTS = GROUP_UNITS * UNIT
GROUPS_PER_BLOCK = UNITS_PER_BLOCK // GROUP_UNITS
N_TOKENS = 8192
N_TILES_MAX = (N_TOKENS * TOP_K // UNIT + (N_TOKENS // MOE_BLOCK) * N_EXPERTS) // UNITS_PER_TILE + N_EXPERTS
HEAD_COLS = 5 * HEAD_DIM
VMEM_LIMIT = 52 * 1024 * 1024


def _dot(a, b):
    return jnp.dot(a, b, preferred_element_type=F32)


def _dot_nt(a, b):
    return lax.dot_general(a, b, (((1,), (1,)), ((), ())), preferred_element_type=F32)


def _dot_tn(a, b):
    return lax.dot_general(a, b, (((0,), (0,)), ((), ())), preferred_element_type=F32)


def _sigmoid(x):
    return 1.0 / (1.0 + jnp.exp(-x))


def _silu(x):
    return x * _sigmoid(x)


def _rms_mod(x, g, scale, shift):
    ms = jnp.mean(x * x, axis=-1, keepdims=True)
    return x * lax.rsqrt(ms + EPS) * g * (1.0 + scale) + shift


def _ada_kernel(c_ref, w_ref, b_ref, o_ref):
    cc = c_ref[...]
    o_ref[0] = jnp.dot(_silu(cc), w_ref[0], preferred_element_type=F32, precision=HIGHEST) + b_ref[0]


def _ada(cc, w_ada, b_ada):
    depth, d, n = w_ada.shape
    tn = 1536
    return pl.pallas_call(
        _ada_kernel,
        grid=(depth, n // tn),
        in_specs=[
            pl.BlockSpec((8, d), lambda l, j: (0, 0)),
            pl.BlockSpec((1, d, tn), lambda l, j: (l, 0, j)),
            pl.BlockSpec((1, 1, tn), lambda l, j: (l, 0, j)),
        ],
        out_specs=pl.BlockSpec((1, 8, tn), lambda l, j: (l, 0, j)),
        out_shape=jax.ShapeDtypeStruct((depth, 8, n), F32),
        compiler_params=pltpu.CompilerParams(vmem_limit_bytes=VMEM_LIMIT),
        name="ada_mod",
    )(cc, w_ada, b_ada.reshape(depth, 1, n))


def _forget_gate(z, log_lb, log1m_lb, one_m_lb):
    e = jnp.exp(-jnp.abs(z))
    r = 1.0 / (1.0 + e)
    log_sig = jnp.minimum(z, 0.0) - jnp.log(1.0 + e)
    k = one_m_lb * jnp.where(z > 0, e * r, r)
    b = log1m_lb + log_sig
    log_f = jnp.maximum(log_lb, b) + jnp.log(1.0 + jnp.exp(-jnp.abs(log_lb - b)))
    return log_f, k


def _level_reference(cum_ref, blk, fwd):
    half = blk // 2
    pieces = []
    if blk >= 8:
        for i in range(CHUNK // blk):
            row = i * blk + (half - 1 if fwd else half)
            pieces.append(jnp.broadcast_to(cum_ref[row:row + 1, :], (blk, HEAD_DIM)))
    else:
        sub = lax.broadcasted_iota(jnp.int32, (8, HEAD_DIM), 0)
        for i in range(CHUNK // 8):
            lo_row = 8 * i + (1 if fwd else 2)
            hi_row = 8 * i + (5 if fwd else 6)
            lo = jnp.broadcast_to(cum_ref[lo_row:lo_row + 1, :], (8, HEAD_DIM))
            hi = jnp.broadcast_to(cum_ref[hi_row:hi_row + 1, :], (8, HEAD_DIM))
            pieces.append(jnp.where(sub >= 4, hi, lo))
    return pieces[0] if len(pieces) == 1 else jnp.concatenate(pieces, axis=0)


def _chunk_steps(chains):
    r_idx = lax.broadcasted_iota(jnp.int32, (CHUNK, CHUNK), 0)
    c_idx = lax.broadcasted_iota(jnp.int32, (CHUNK, CHUNK), 1)
    work = []
    for q, k, v, log_f, lvl, cum_ref, fwd in chains:
        tri = jnp.where((r_idx >= c_idx) if fwd else (r_idx <= c_idx), 1.0, 0.0).astype(F32)
        cum = jnp.dot(tri, log_f, preferred_element_type=F32, precision=HIGHEST)
        cum_ref[...] = cum
        q_b = q.astype(BF16)
        k_b = k.astype(BF16)
        work.append(dict(cum=cum, q_b=q_b, k_b=k_b, scores=jnp.where(lvl == 0, _dot_nt(q_b, k_b), 0.0)))

    for lev in range(1, N_LEVELS + 1):
        for (q, k, v, log_f, lvl, cum_ref, fwd), w in zip(chains, work):
            if lev == 1:
                qe = w["q_b"] * jnp.exp(log_f).astype(BF16)
                ke = w["k_b"]
            else:
                e = jnp.exp(-jnp.abs(w["cum"] - _level_reference(cum_ref, 1 << lev, fwd))).astype(BF16)
                qe = w["q_b"] * e
                ke = w["k_b"] * e
            w["scores"] = jnp.where(lvl == lev, _dot_nt(qe, ke), w["scores"])

    def finish(chain, w, st):
        q, k, v, log_f, lvl, cum_ref, fwd = chain
        total = cum_ref[CHUNK - 1:CHUNK, :] if fwd else cum_ref[0:1, :]
        o = (_dot(w["scores"].astype(BF16), v.astype(BF16))
             + _dot_nt((q * jnp.exp(w["cum"])).astype(BF16), st.astype(BF16)))
        k_st = k * jnp.exp(total - w["cum"])
        return o, st * jnp.exp(total) + _dot_tn(v.astype(BF16), k_st.astype(BF16))

    return [functools.partial(finish, chain, w) for chain, w in zip(chains, work)]


def _scan_kernel(nc_ref, x_ref, mod_ref, g1_ref, wq_ref, wv_ref, wf_ref, wb_ref, wg_ref, gp_ref, s0_ref,
                 lvlf_ref, lvlb_ref, o_out, st_out, h_ref, w_ref, z_ref, of_ref, ob_ref, cum_ref):
    step = pl.program_id(0)
    head = pl.program_id(1)
    n_seq_chunks = nc_ref[step]
    n_chunks = STEP // CHUNK

    @pl.when(head == 0)
    def _():
        g1 = g1_ref[...]
        shift = mod_ref[0, 0:1, :]
        scale = mod_ref[0, 1:2, :]

        def body(i, carry):
            rows = pl.ds(pl.multiple_of(i * CHUNK, CHUNK), CHUNK)
            h_ref[rows, :] = _rms_mod(x_ref[rows, :], g1, scale, shift).astype(BF16)
            return carry

        lax.fori_loop(0, n_chunks, body, 0)

    for j, wj_ref in enumerate((wq_ref, wv_ref, wf_ref, wb_ref, wg_ref)):
        w_ref[:, j * HEAD_DIM:(j + 1) * HEAD_DIM] = wj_ref[0].astype(BF16)
    z_ref[...] = _dot(h_ref[...], w_ref[...])
    st_out[...] = jnp.zeros(st_out.shape, F32)
    st0 = (s0_ref[0, 0, 0].T, s0_ref[0, 1, 0].T)

    def chain(c, fwd, cum_ref):
        d = 0 if fwd else 1
        zcol = (2 if fwd else 3) * HEAD_DIM
        rows = pl.ds(c * CHUNK, CHUNK)
        log_f, k = _forget_gate(z_ref[rows, zcol:zcol + HEAD_DIM], gp_ref[0, 3 * d:3 * d + 1, :],
                                gp_ref[0, 3 * d + 1:3 * d + 2, :], gp_ref[0, 3 * d + 2:3 * d + 3, :])
        return (z_ref[rows, 0:HEAD_DIM], k, z_ref[rows, HEAD_DIM:2 * HEAD_DIM], log_f,
                (lvlf_ref if fwd else lvlb_ref)[...], cum_ref, fwd)

    carry = list(st0)
    for i0 in range(0, n_chunks, LOCKSTEP):
        todo = [(i if fwd else n_chunks - 1 - i, fwd) for i in range(i0, i0 + LOCKSTEP) for fwd in (True, False)]
        finishers = _chunk_steps([chain(c, fwd, cum_ref.at[n]) for n, (c, fwd) in enumerate(todo)])
        for (c, fwd), finish_chain in zip(todo, finishers):
            d = 0 if fwd else 1
            pos = lax.rem(c, n_seq_chunks)
            first = (pos == 0) if fwd else (pos == n_seq_chunks - 1)
            o, carry[d] = finish_chain(jnp.where(first, st0[d], carry[d]))
            (of_ref if fwd else ob_ref)[pl.ds(c * CHUNK, CHUNK), :] = o
            st_out[0, lax.div(c, n_seq_chunks), d, 0] = carry[d].T

    norm_g = gp_ref[0, 6:7, :]

    def finish(i, carry):
        rows = pl.ds(pl.multiple_of(i * CHUNK, CHUNK), CHUNK)
        o = of_ref[rows, :] + ob_ref[rows, :]
        o = o * lax.rsqrt(jnp.mean(o * o, axis=-1, keepdims=True) + EPS)
        og = z_ref[rows, 4 * HEAD_DIM:5 * HEAD_DIM]
        o_out[rows, :] = (o * norm_g * _silu(og)).astype(BF16)
        return carry

    lax.fori_loop(0, n_chunks, finish, 0)


def _scan(nc, x_all, mod_step, g1, w_in, layer, gp, s0, lvl_f, lvl_b):
    n_tok = x_all.shape[0]
    n_steps = n_tok // STEP
    grid_spec = pltpu.PrefetchScalarGridSpec(
        num_scalar_prefetch=1,
        grid=(n_steps, N_HEADS),
        in_specs=[
            pl.BlockSpec((STEP, D_MODEL), lambda s, h, nc: (s, 0)),
            pl.BlockSpec((1, 8, D_MODEL), lambda s, h, nc: (s, 0, 0)),
            pl.BlockSpec((1, D_MODEL), lambda s, h, nc: (0, 0)),
        ] + [
            pl.BlockSpec((1, D_MODEL, HEAD_DIM), lambda s, h, nc, j=j: (layer, 0, j * N_HEADS + h))
            for j in range(5)
        ] + [
            pl.BlockSpec((1, 8, HEAD_DIM), lambda s, h, nc: (h, 0, 0)),
            pl.BlockSpec((1, 2, 1, HEAD_DIM, HEAD_DIM), lambda s, h, nc: (s, 0, h, 0, 0)),
            pl.BlockSpec((CHUNK, CHUNK), lambda s, h, nc: (0, 0)),
            pl.BlockSpec((CHUNK, CHUNK), lambda s, h, nc: (0, 0)),
        ],
        out_specs=[
            pl.BlockSpec((STEP, HEAD_DIM), lambda s, h, nc: (s, h)),
            pl.BlockSpec((1, 4, 2, 1, HEAD_DIM, HEAD_DIM), lambda s, h, nc: (s, 0, 0, h, 0, 0)),
        ],
        scratch_shapes=[
            pltpu.VMEM((STEP, D_MODEL), BF16),
            pltpu.VMEM((D_MODEL, HEAD_COLS), BF16),
            pltpu.VMEM((STEP, HEAD_COLS), F32),
            pltpu.VMEM((STEP, HEAD_DIM), F32),
            pltpu.VMEM((STEP, HEAD_DIM), F32),
            pltpu.VMEM((2 * LOCKSTEP, CHUNK, HEAD_DIM), F32),
        ],
    )
    return pl.pallas_call(
        _scan_kernel,
        grid_spec=grid_spec,
        out_shape=[
            jax.ShapeDtypeStruct((n_tok, D_MODEL), BF16),
            jax.ShapeDtypeStruct((n_steps, 4, 2, N_HEADS, HEAD_DIM, HEAD_DIM), F32),
        ],
        compiler_params=pltpu.CompilerParams(
            dimension_semantics=("arbitrary", "arbitrary"), vmem_limit_bytes=VMEM_LIMIT),
        name="hgrn_scan",
    )(nc, x_all, mod_step, g1, *([w_in] * 5), gp, s0, lvl_f, lvl_b)


def _post_kernel(pv_ref, nv_ref, x_ref, xp_ref, xn_ref, mod_ref, o_ref, wu_f32, wgz_f32, wph_f32,
                 wpc_f32, wo_f32, wr_ref, g1_ref, cw_ref, cb_ref, lg_ref, lb_ref, g2_ref,
                 x1_out, h2_out, lg_out, cu_ref, wu_ref, wgz_ref, wph_ref, wpc_ref, wo_ref):
    i = pl.program_id(0)

    @pl.when(i == 0)
    def _():
        for dst, src in ((wu_ref, wu_f32), (wgz_ref, wgz_f32), (wph_ref, wph_f32),
                         (wpc_ref, wpc_f32), (wo_ref, wo_f32)):
            dst[...] = (src[0] if len(src.shape) == 3 else src[...]).astype(BF16)

    g1 = g1_ref[...]
    shift1 = mod_ref[0, 0:1, :]
    scale1 = mod_ref[0, 1:2, :]
    gate1 = mod_ref[0, 2:3, :]
    shift2 = mod_ref[0, 3:4, :]
    scale2 = mod_ref[0, 4:5, :]

    def glu(xv):
        h = _rms_mod(xv, g1, scale1, shift1).astype(BF16)
        u = _dot(h, wu_ref[...])
        return h, u[:, :CONV_DIM] * _sigmoid(u[:, CONV_DIM:])

    x = x_ref[...]
    h, glu_mid = glu(x)
    _, glu_prev = glu(xp_ref[...])
    _, glu_next = glu(xn_ref[...])
    cu_ref[0, 0:CONV_HALO, :] = glu_prev * pv_ref[i].astype(F32)
    cu_ref[0, CONV_HALO:CONV_HALO + POST_TILE, :] = glu_mid
    cu_ref[0, CONV_HALO + POST_TILE:, :] = glu_next * nv_ref[i].astype(F32)
    n_keep = POST_TILE + 2 * CONV_HALO - 8
    for k in range(1, 8):
        cu_ref[k, 0:n_keep, :] = cu_ref[0, k:k + n_keep, :]

    off = CONV_HALO - CONV_WIDTH // 2
    acc = jnp.zeros((POST_TILE, CONV_DIM), F32) + cb_ref[...]
    for j in range(CONV_WIDTH):
        base, k = divmod(off + j, 8)
        acc = acc + cu_ref[k, 8 * base:8 * base + POST_TILE, :] * cw_ref[j:j + 1, :]
    mu = jnp.mean(acc, axis=-1, keepdims=True)
    cen = acc - mu
    var = jnp.mean(cen * cen, axis=-1, keepdims=True)
    cv = cen * lax.rsqrt(var + EPS) * lg_ref[...] + lb_ref[...]
    y_b = _dot(_silu(cv).astype(BF16), wpc_ref[...])

    y_a = _dot(o_ref[...], wph_ref[...])
    gz = _sigmoid(_dot(h, wgz_ref[...]))
    merged = gz[:, :D_MODEL] * y_a + gz[:, D_MODEL:] * y_b
    x1 = x + gate1 * _dot(merged.astype(BF16), wo_ref[...])
    x1_out[...] = x1
    h2 = _rms_mod(x1, g2_ref[...], scale2, shift2)
    h2_out[...] = h2.astype(BF16)
    lg_out[...] = lax.dot_general(wr_ref[...], h2, (((1,), (1,)), ((), ())),
                                  preferred_element_type=F32, precision=HIGHEST)


def _post(pv, nv, x_all, mod_tile, o_all, w_in, layer, w_ph, w_pc, w_o, w_rt, g1, cw, cb, lg, lb, g2):
    n_tok = x_all.shape[0]
    n_tiles = n_tok // POST_TILE
    halo_per_tile = POST_TILE // CONV_HALO
    n_halo_blocks = n_tok // CONV_HALO
    glu_cols, gate_cols = 2 * CONV_DIM, 2 * D_MODEL
    glu_start = 5 * N_HEADS * HEAD_DIM
    assert glu_start % glu_cols == 0 and (glu_start + glu_cols) % gate_cols == 0

    def full(a):
        return pl.BlockSpec(a.shape, lambda i, pv, nv: (0,) * a.ndim)

    def resident(shape, index):
        return pl.BlockSpec(shape, lambda i, pv, nv: index, pipeline_mode=pl.Buffered(1))

    grid_spec = pltpu.PrefetchScalarGridSpec(
        num_scalar_prefetch=2,
        grid=(n_tiles,),
        in_specs=[
            pl.BlockSpec((POST_TILE, D_MODEL), lambda i, pv, nv: (i, 0)),
            pl.BlockSpec((CONV_HALO, D_MODEL), lambda i, pv, nv: (jnp.maximum(i * halo_per_tile - 1, 0), 0)),
            pl.BlockSpec((CONV_HALO, D_MODEL),
                         lambda i, pv, nv: (jnp.minimum((i + 1) * halo_per_tile, n_halo_blocks - 1), 0)),
            pl.BlockSpec((1, 8, D_MODEL), lambda i, pv, nv: (i, 0, 0)),
            pl.BlockSpec((POST_TILE, D_MODEL), lambda i, pv, nv: (i, 0)),
            resident((1, D_MODEL, glu_cols), (layer, 0, glu_start // glu_cols)),
            resident((1, D_MODEL, gate_cols), (layer, 0, (glu_start + glu_cols) // gate_cols)),
            resident(w_ph.shape, (0, 0)), resident(w_pc.shape, (0, 0)), resident(w_o.shape, (0, 0)),
            full(w_rt), full(g1), full(cw), full(cb), full(lg), full(lb), full(g2),
        ],
        out_specs=[
            pl.BlockSpec((POST_TILE, D_MODEL), lambda i, pv, nv: (i, 0)),
            pl.BlockSpec((POST_TILE, D_MODEL), lambda i, pv, nv: (i, 0)),
            pl.BlockSpec((N_EXPERTS, POST_TILE), lambda i, pv, nv: (0, i)),
        ],
        scratch_shapes=[
            pltpu.VMEM((8, POST_TILE + 2 * CONV_HALO, CONV_DIM), F32),
            pltpu.VMEM((D_MODEL, glu_cols), BF16), pltpu.VMEM((D_MODEL, gate_cols), BF16),
            pltpu.VMEM(w_ph.shape, BF16), pltpu.VMEM(w_pc.shape, BF16), pltpu.VMEM(w_o.shape, BF16),
        ],
    )
    return pl.pallas_call(
        _post_kernel,
        grid_spec=grid_spec,
        out_shape=[
            jax.ShapeDtypeStruct((n_tok, D_MODEL), F32),
            jax.ShapeDtypeStruct((n_tok, D_MODEL), BF16),
            jax.ShapeDtypeStruct((N_EXPERTS, n_tok), F32),
        ],
        compiler_params=pltpu.CompilerParams(
            dimension_semantics=("arbitrary",), vmem_limit_bytes=VMEM_LIMIT),
        name="post_mixer",
    )(pv, nv, x_all, x_all, x_all, mod_tile, o_all, w_in, w_in, w_ph, w_pc, w_o, w_rt, g1, cw, cb, lg, lb, g2)


def _route_kernel(lg_ref, bias_ref, comb_out, chosen_out):
    n = lg_ref.shape[1]
    scores = _sigmoid(lg_ref[...])
    sel = scores + bias_ref[...]
    neg = jnp.float32(-jnp.inf)

    sel3 = sel.reshape(N_GROUPS, GROUP_SIZE, n)
    m1 = jnp.max(sel3, axis=1, keepdims=True)
    is_m1 = sel3 == m1
    n_m1 = jnp.sum(is_m1.astype(F32), axis=1, keepdims=True)
    m2 = jnp.max(jnp.where(is_m1, neg, sel3), axis=1, keepdims=True)
    grp = (m1 + jnp.where(n_m1 > 1.5, m1, m2)).reshape(N_GROUPS, n)

    gidx = lax.broadcasted_iota(jnp.int32, (N_GROUPS, n), 0)
    rank = jnp.zeros((N_GROUPS, n), F32)
    for g in range(N_GROUPS):
        other = grp[g:g + 1, :]
        ahead = (other > grp) | ((other == grp) & (g < gidx))
        rank = rank + ahead.astype(F32)
    keep_g = rank < TOPK_GROUPS - 0.5
    keep = jnp.broadcast_to(keep_g.reshape(N_GROUPS, 1, n), (N_GROUPS, GROUP_SIZE, n)).reshape(N_EXPERTS, n)
    cand = jnp.where(keep, sel, neg)

    eidx = lax.broadcasted_iota(jnp.int32, (N_EXPERTS, n), 0)
    chosen = jnp.zeros((N_EXPERTS, n), F32)
    for _ in range(TOP_K):
        best = jnp.max(cand, axis=0, keepdims=True)
        first = jnp.min(jnp.where(cand == best, eidx, N_EXPERTS), axis=0, keepdims=True)
        hit = eidx == first
        chosen = jnp.where(hit, 1.0, chosen)
        cand = jnp.where(hit, neg, cand)
    w = scores * chosen
    comb_out[...] = w / jnp.sum(w, axis=0, keepdims=True) * ROUTED_SCALE
    chosen_out[...] = chosen


def _route(logits_t, bias):
    n_tok = logits_t.shape[1]
    tile = 512
    return pl.pallas_call(
        _route_kernel,
        grid=(n_tok // tile,),
        in_specs=[
            pl.BlockSpec((N_EXPERTS, tile), lambda i: (0, i)),
            pl.BlockSpec((N_EXPERTS, 1), lambda i: (0, 0)),
        ],
        out_specs=[pl.BlockSpec((N_EXPERTS, tile), lambda i: (0, i)),
                   pl.BlockSpec((N_EXPERTS, tile), lambda i: (0, i))],
        out_shape=[jax.ShapeDtypeStruct((N_EXPERTS, n_tok), F32),
                   jax.ShapeDtypeStruct((N_EXPERTS, n_tok), F32)],
        name="router",
    )(logits_t, bias)


def _count_le(bounds, idx):
    return jnp.sum(jnp.where(bounds <= idx, 1.0, 0.0), axis=0, keepdims=True)


def _pick_row(onehot, col):
    return jnp.sum(jnp.where(onehot, col, 0.0), axis=0, keepdims=True)


def _plan_kernel(ch_ref, rankt_out, es_out, rs_out, src_out, et_out, ysrc_out, misc_out):
    nb = N_TOKENS // MOE_BLOCK
    n_src = N_TILES_MAX * UNITS_PER_TILE
    e_col = lax.broadcasted_iota(jnp.int32, (N_EXPERTS, 1), 0).astype(F32)
    blk_lane = lax.broadcasted_iota(jnp.int32, (N_EXPERTS, 128), 1)
    t_r = lax.broadcasted_iota(jnp.int32, (MOE_BLOCK, MOE_BLOCK), 0)
    t_c = lax.broadcasted_iota(jnp.int32, (MOE_BLOCK, MOE_BLOCK), 1)
    earlier = jnp.where(t_c < t_r, 1.0, 0.0).astype(BF16)
    e_r = lax.broadcasted_iota(jnp.int32, (N_EXPERTS, N_EXPERTS), 0)
    e_c = lax.broadcasted_iota(jnp.int32, (N_EXPERTS, N_EXPERTS), 1)
    upto = jnp.where(e_c <= e_r, 1.0, 0.0).astype(F32)
    slot = lax.broadcasted_iota(jnp.int32, (1, SLOT_MAX), 1).astype(F32)

    def col_cumsum(col):
        wide = jnp.broadcast_to(col, (N_EXPERTS, 128))
        return jnp.dot(upto, wide, preferred_element_type=F32, precision=HIGHEST)[:, 0:1]

    def block_stats(b, carry):
        units_mat, start_mat = carry
        cb = ch_ref[:, pl.ds(pl.multiple_of(b * MOE_BLOCK, MOE_BLOCK), MOE_BLOCK)]
        rank_t = _dot_nt(earlier, cb.astype(BF16))
        cb_t = cb.T
        rankt_out[b] = jnp.where(cb_t > 0, rank_t, -1.0).astype(BF16)
        cnt = jnp.sum(cb, axis=1, keepdims=True)
        units = jnp.floor((cnt + (UNIT - 1)) * (1.0 / UNIT))
        incl = col_cumsum(units)
        start = incl - units
        e_slot = _count_le(incl * UNIT, slot)
        onehot = e_col == e_slot
        r = slot - _pick_row(onehot, start * UNIT)
        valid = (e_slot < N_EXPERTS - 0.5) & (r < _pick_row(onehot, cnt))
        es_out[b] = jnp.where(valid, e_slot, -1.0).astype(jnp.int32)
        rs_out[b] = jnp.where(valid, r, -2.0)
        units_mat = jnp.where(blk_lane == b, units, units_mat)
        start_mat = jnp.where(blk_lane == b, start, start_mat)
        return units_mat, start_mat

    zeros = jnp.zeros((N_EXPERTS, 128), F32)
    units_mat, start_mat = lax.fori_loop(0, nb, block_stats, (zeros, zeros))

    tot = jnp.sum(units_mat, axis=1, keepdims=True)
    tiles_e = jnp.floor((tot + (UNITS_PER_TILE - 1)) * (1.0 / UNITS_PER_TILE))
    incl_t = col_cumsum(tiles_e)
    start_t = incl_t - tiles_e
    n_used = incl_t[N_EXPERTS - 1:N_EXPERTS, :]
    b_r = lax.broadcasted_iota(jnp.int32, (128, 128), 0)
    b_c = lax.broadcasted_iota(jnp.int32, (128, 128), 1)
    before = jnp.where(b_r < b_c, 1.0, 0.0).astype(F32)
    cum_b = jnp.dot(units_mat, before, preferred_element_type=F32, precision=HIGHEST)
    run_pos = start_t * UNITS_PER_TILE + cum_b

    tile_idx = lax.broadcasted_iota(jnp.int32, (1, 512), 1).astype(F32)
    et_out[...] = jnp.minimum(_count_le(incl_t, tile_idx), N_EXPERTS - 1.0).astype(jnp.int32)

    incl_ub_t = (cum_b + units_mat).T
    cum_b_t = cum_b.T
    start_mat_t = start_mat.T
    b_col = lax.broadcasted_iota(jnp.int32, (128, 1), 0)
    chunk = 1024

    def src_chunk(c, carry):
        off = pl.multiple_of(c * chunk, chunk)
        p = (lax.broadcasted_iota(jnp.int32, (1, chunk), 1) + off).astype(F32)
        e_p = jnp.minimum(_count_le(incl_t * UNITS_PER_TILE, p), N_EXPERTS - 1.0)
        onehot_e = e_col == e_p
        q = p - _pick_row(onehot_e, start_t * UNITS_PER_TILE)
        valid = (p < n_used * UNITS_PER_TILE) & (q < _pick_row(onehot_e, tot))
        sel = jnp.where(onehot_e, 1.0, 0.0).astype(F32)
        incl_sel = jnp.dot(incl_ub_t, sel, preferred_element_type=F32, precision=HIGHEST)
        b_p = jnp.sum(jnp.where((incl_sel <= q) & (b_col < nb), 1.0, 0.0), axis=0, keepdims=True)
        b_p = jnp.minimum(b_p, nb - 1.0)
        onehot_b = b_col.astype(F32) == b_p
        cum_sel = jnp.dot(cum_b_t, sel, preferred_element_type=F32, precision=HIGHEST)
        start_sel = jnp.dot(start_mat_t, sel, preferred_element_type=F32, precision=HIGHEST)
        unit = b_p * UNITS_PER_BLOCK + _pick_row(onehot_b, start_sel) + q - _pick_row(onehot_b, cum_sel)
        src_out[:, pl.ds(off, chunk)] = jnp.where(valid, unit, 0.0).astype(jnp.int32)
        return carry

    lax.fori_loop(0, n_src // chunk, src_chunk, 0)

    unit_idx = lax.broadcasted_iota(jnp.int32, (1, 256), 1).astype(F32)
    misc_lane = lax.broadcasted_iota(jnp.int32, (1, 128), 1)

    def block_units(b, used):
        here = blk_lane == b
        units = jnp.sum(jnp.where(here, units_mat, 0.0), axis=1, keepdims=True)
        start = jnp.sum(jnp.where(here, start_mat, 0.0), axis=1, keepdims=True)
        pos = jnp.sum(jnp.where(here, run_pos, 0.0), axis=1, keepdims=True)
        incl = start + units
        e_unit = _count_le(incl, unit_idx)
        onehot = e_col == e_unit
        val = _pick_row(onehot, pos - start) + unit_idx
        ysrc_out[b] = jnp.where(e_unit < N_EXPERTS - 0.5, val, 0.0).astype(jnp.int32)
        return jnp.where(misc_lane == b, incl[N_EXPERTS - 1:N_EXPERTS, :], used)

    used = lax.fori_loop(0, nb, block_units, jnp.zeros((1, 128), F32))
    misc_out[0:1, :] = used.astype(jnp.int32)
    misc_out[1:2, :] = jnp.broadcast_to(n_used, (1, 128)).astype(jnp.int32)
    misc_out[2:8, :] = jnp.zeros((6, 128), jnp.int32)


def _routing_plan(chosen):
    nb = N_TOKENS // MOE_BLOCK
    n_src = N_TILES_MAX * UNITS_PER_TILE
    assert n_src % 1024 == 0 and N_TILES_MAX <= 512 and nb <= 128 and UNITS_PER_BLOCK <= 256
    rank_bt, e_slot, r_slot, src, e_tile, ysrc, misc = pl.pallas_call(
        _plan_kernel,
        out_shape=[
            jax.ShapeDtypeStruct((nb, MOE_BLOCK, N_EXPERTS), BF16),
            jax.ShapeDtypeStruct((nb, 1, SLOT_MAX), jnp.int32),
            jax.ShapeDtypeStruct((nb, 1, SLOT_MAX), F32),
            jax.ShapeDtypeStruct((1, n_src), jnp.int32),
            jax.ShapeDtypeStruct((1, 512), jnp.int32),
            jax.ShapeDtypeStruct((nb, 1, 256), jnp.int32),
            jax.ShapeDtypeStruct((8, 128), jnp.int32),
        ],
        compiler_params=pltpu.CompilerParams(vmem_limit_bytes=VMEM_LIMIT),
        name="moe_plan",
    )(chosen)
    return dict(rank_bt=rank_bt, e_slot=e_slot, r_slot=r_slot, src=src.reshape(-1),
                e_tile=e_tile[0, :N_TILES_MAX], n_used=misc[1, :1],
                ysrc=ysrc[:, 0, :UNITS_PER_BLOCK].reshape(-1))


def _permute_kernel(h_ref, rank_ref, es_ref, rs_ref, xp_out):
    sub_e = lax.broadcasted_iota(jnp.int32, (N_EXPERTS, FFN_TILE), 0)
    tiles = [slice(i * FFN_TILE, (i + 1) * FFN_TILE) for i in range(SLOT_MAX // FFN_TILE)]
    picks = []
    for rows in tiles:
        onehot_e = jnp.where(sub_e == es_ref[0, :, rows], 1.0, 0.0).astype(BF16)
        slot_rank = _dot(rank_ref[0], onehot_e)
        picks.append(jnp.where(slot_rank == rs_ref[0, :, rows], 1.0, 0.0).astype(BF16))
    for rows, pick in zip(tiles, picks):
        xp_out[0, rows, :] = _dot_tn(pick, h_ref[...]).astype(BF16)


def _permute(plan, h2):
    nb = h2.shape[0] // MOE_BLOCK
    return pl.pallas_call(
        _permute_kernel,
        grid=(nb,),
        in_specs=[
            pl.BlockSpec((MOE_BLOCK, D_MODEL), lambda b: (b, 0)),
            pl.BlockSpec((1, MOE_BLOCK, N_EXPERTS), lambda b: (b, 0, 0)),
            pl.BlockSpec((1, 1, SLOT_MAX), lambda b: (b, 0, 0)),
            pl.BlockSpec((1, 1, SLOT_MAX), lambda b: (b, 0, 0)),
        ],
        out_specs=pl.BlockSpec((1, SLOT_MAX, D_MODEL), lambda b: (b, 0, 0)),
        out_shape=jax.ShapeDtypeStruct((nb, SLOT_MAX, D_MODEL), BF16),
        compiler_params=pltpu.CompilerParams(
            dimension_semantics=("arbitrary",), vmem_limit_bytes=VMEM_LIMIT),
        name="moe_permute",
    )(h2, plan["rank_bt"], plan["e_slot"], plan["r_slot"])


def _unit_copies(table_ref, base, src_hbm, dst_ref, slot, sem, n_units, start):
    for u in range(n_units):
        copy = pltpu.make_async_copy(src_hbm.at[table_ref[base + u]],
                                     dst_ref.at[slot, pl.ds(u * UNIT, UNIT), :], sem.at[slot])
        if start:
            copy.start()
        else:
            copy.wait()


def _ffn_kernel(src_ref, et_ref, nu_ref, xp_hbm, *refs):
    w_refs = refs[:2 * TILES_PER_STEP]
    y_out = refs[2 * TILES_PER_STEP]
    scratch = refs[2 * TILES_PER_STEP + 1:]
    wb_refs, xt_ref, sem = scratch[:2 * TILES_PER_STEP], scratch[2 * TILES_PER_STEP], scratch[2 * TILES_PER_STEP + 1]
    n_units = TILES_PER_STEP * UNITS_PER_TILE
    step = pl.program_id(0)
    slot = lax.rem(step, 2)
    n_used = nu_ref[0]

    def copies(s, sl, start):
        _unit_copies(src_ref, s * n_units, xp_hbm, xt_ref, sl, sem, n_units, start)

    @pl.when((step == 0) & (n_used > 0))
    def _():
        copies(0, 0, True)

    @pl.when((step + 1 < pl.num_programs(0)) & ((step + 1) * TILES_PER_STEP < n_used))
    def _():
        copies(step + 1, 1 - slot, True)

    @pl.when(step * TILES_PER_STEP < n_used)
    def _():
        for t in range(TILES_PER_STEP):
            j = step * TILES_PER_STEP + t

            @pl.when((step == 0) | (et_ref[j] != et_ref[jnp.maximum(j - TILES_PER_STEP, 0)]))
            def _():
                wb_refs[2 * t][...] = w_refs[2 * t][0, 0].astype(BF16)
                wb_refs[2 * t + 1][...] = w_refs[2 * t + 1][0, 0].astype(BF16)

        copies(step, slot, False)
        tiles = [slice(t * FFN_TILE, (t + 1) * FFN_TILE) for t in range(TILES_PER_STEP)]
        gus = [_dot(xt_ref[slot, rows, :], wb_refs[2 * t][...]) for t, rows in enumerate(tiles)]
        acts = [(_silu(gu[:, :EXPERT_FF]) * gu[:, EXPERT_FF:]).astype(BF16) for gu in gus]
        for t, (rows, act) in enumerate(zip(tiles, acts)):
            y_out[rows, :] = _dot(act, wb_refs[2 * t + 1][...]).astype(BF16)

    @pl.when(step * TILES_PER_STEP >= n_used)
    def _():
        y_out[...] = jnp.zeros(y_out.shape, BF16)


def _expert_ffn(plan, xp, w_gu, w_d, layer):
    xp_units = xp.reshape(-1, UNIT, D_MODEL)
    assert N_TILES_MAX % TILES_PER_STEP == 0

    def weight_specs(t):
        return [pl.BlockSpec((1, 1, D_MODEL, 2 * EXPERT_FF),
                             lambda s, src, et, nu, t=t: (layer, et[s * TILES_PER_STEP + t], 0, 0)),
                pl.BlockSpec((1, 1, EXPERT_FF, D_MODEL),
                             lambda s, src, et, nu, t=t: (layer, et[s * TILES_PER_STEP + t], 0, 0))]

    grid_spec = pltpu.PrefetchScalarGridSpec(
        num_scalar_prefetch=3,
        grid=(N_TILES_MAX // TILES_PER_STEP,),
        in_specs=[pl.BlockSpec(memory_space=pl.ANY)]
        + [spec for t in range(TILES_PER_STEP) for spec in weight_specs(t)],
        out_specs=pl.BlockSpec((TILES_PER_STEP * FFN_TILE, D_MODEL), lambda s, src, et, nu: (s, 0)),
        scratch_shapes=[
            pltpu.VMEM((D_MODEL, 2 * EXPERT_FF), BF16) if i % 2 == 0 else pltpu.VMEM((EXPERT_FF, D_MODEL), BF16)
            for i in range(2 * TILES_PER_STEP)
        ] + [pltpu.VMEM((2, TILES_PER_STEP * FFN_TILE, D_MODEL), BF16), pltpu.SemaphoreType.DMA((2,))],
    )
    return pl.pallas_call(
        _ffn_kernel,
        grid_spec=grid_spec,
        out_shape=jax.ShapeDtypeStruct((N_TILES_MAX * FFN_TILE, D_MODEL), BF16),
        compiler_params=pltpu.CompilerParams(
            dimension_semantics=("arbitrary",), vmem_limit_bytes=VMEM_LIMIT),
        name="moe_expert_ffn",
    )(plan["src"], plan["e_tile"], plan["n_used"], xp_units, *([w_gu, w_d] * TILES_PER_STEP))


def _combine_kernel(ysrc_ref, y_hbm, rank_ref, comb_ref, er_ref, rr_ref, h_ref, x1_ref, mod_ref,
                    wsgu_ref, wsd_ref, fg_ref, o_ref, yt_ref, sem, *, final):
    b = pl.program_id(0)
    slot = lax.rem(b, 2)

    def copies(bb, sl, start):
        _unit_copies(ysrc_ref, bb * UNITS_PER_BLOCK, y_hbm, yt_ref, sl, sem, UNITS_PER_BLOCK, start)

    @pl.when(b == 0)
    def _():
        copies(0, 0, True)

    @pl.when(b + 1 < pl.num_programs(0))
    def _():
        copies(b + 1, 1 - slot, True)

    gu = _dot(h_ref[...], wsgu_ref[...])
    shared = _dot((_silu(gu[:, :EXPERT_FF]) * gu[:, EXPERT_FF:]).astype(BF16), wsd_ref[...])

    sub_e = lax.broadcasted_iota(jnp.int32, (N_EXPERTS, GROUP_SLOTS), 0)
    groups = [slice(g * GROUP_SLOTS, (g + 1) * GROUP_SLOTS) for g in range(GROUPS_PER_BLOCK)]
    onehots = [jnp.where(sub_e == er_ref[0, :, cols], 1.0, 0.0).astype(BF16) for cols in groups]
    slot_ranks = [_dot(rank_ref[0], onehot_e) for onehot_e in onehots]
    slot_ws = [_dot(comb_ref[0], onehot_e) for onehot_e in onehots]
    weights = jnp.concatenate(
        [jnp.where(slot_rank == rr_ref[0, :, cols], slot_w, 0.0).astype(BF16)
         for cols, slot_rank, slot_w in zip(groups, slot_ranks, slot_ws)], axis=1)

    copies(b, slot, False)
    x2 = x1_ref[...] + mod_ref[0, 5:6, :] * (shared + _dot(weights, yt_ref[slot]))
    if final:
        ms = jnp.mean(x2 * x2, axis=-1, keepdims=True)
        x2 = x2 * lax.rsqrt(ms + EPS) * fg_ref[...]
    o_ref[...] = x2


def _combine(plan, y_sorted, comb_bt, h2, x1, mod_tile, ws_gu, ws_d, final_g, final):
    n_tok = h2.shape[0]
    nb = n_tok // MOE_BLOCK
    y_units = y_sorted.reshape(-1, UNIT, D_MODEL)

    def full(a):
        return pl.BlockSpec(a.shape, lambda b, ysrc: (0,) * a.ndim)

    grid_spec = pltpu.PrefetchScalarGridSpec(
        num_scalar_prefetch=1,
        grid=(nb,),
        in_specs=[
            pl.BlockSpec(memory_space=pl.ANY),
            pl.BlockSpec((1, MOE_BLOCK, N_EXPERTS), lambda b, ysrc: (b, 0, 0)),
            pl.BlockSpec((1, MOE_BLOCK, N_EXPERTS), lambda b, ysrc: (b, 0, 0)),
            pl.BlockSpec((1, 1, SLOT_MAX), lambda b, ysrc: (b, 0, 0)),
            pl.BlockSpec((1, 1, SLOT_MAX), lambda b, ysrc: (b, 0, 0)),
            pl.BlockSpec((MOE_BLOCK, D_MODEL), lambda b, ysrc: (b, 0)),
            pl.BlockSpec((MOE_BLOCK, D_MODEL), lambda b, ysrc: (b, 0)),
            pl.BlockSpec((1, 8, D_MODEL), lambda b, ysrc: (b, 0, 0)),
            full(ws_gu), full(ws_d), full(final_g),
        ],
        out_specs=pl.BlockSpec((MOE_BLOCK, D_MODEL), lambda b, ysrc: (b, 0)),
        scratch_shapes=[
            pltpu.VMEM((2, SLOT_MAX, D_MODEL), BF16),
            pltpu.SemaphoreType.DMA((2,)),
        ],
    )
    return pl.pallas_call(
        functools.partial(_combine_kernel, final=final),
        grid_spec=grid_spec,
        out_shape=jax.ShapeDtypeStruct((n_tok, D_MODEL), F32),
        compiler_params=pltpu.CompilerParams(
            dimension_semantics=("arbitrary",), vmem_limit_bytes=VMEM_LIMIT),
        name="moe_combine",
    )(plan["ysrc"], y_units, plan["rank_bt"], comb_bt,
      plan["e_slot"], plan["r_slot"], h2, x1, mod_tile, ws_gu, ws_d, final_g)


def _level_ids():
    t = np.arange(CHUNK)
    x = t[:, None] ^ t[None, :]
    lvl = np.zeros((CHUNK, CHUNK), np.int32)
    nz = x > 0
    lvl[nz] = np.floor(np.log2(x[nz])).astype(np.int32) + 1
    past_f = t[:, None] >= t[None, :]
    return jnp.asarray(np.where(past_f, lvl, -1)), jnp.asarray(np.where(past_f.T, lvl, -1))


def kernel(x_prompt, x_sample, state_hgrn, c, c_ctx, w_ada, b_ada, norm1_g, w_in, hgrn_lb_logits, hgrn_norm_g, w_proj_hgrn, conv_dw_w, conv_dw_b, conv_norm_g, conv_norm_b, w_proj_conv, w_out, norm2_g, w_router, router_bias, w_expert_gate_up, w_expert_down, w_shared_gate_up, w_shared_down, final_norm_g):
    n_ctx, ctx_len, d = x_prompt.shape
    n_lat, lat_len, _ = x_sample.shape
    depth = w_ada.shape[0]
    assert d == D_MODEL and ctx_len * 4 == STEP and lat_len == STEP and n_ctx % 4 == 0
    ctx_steps = n_ctx * ctx_len // STEP
    n_steps = ctx_steps + n_lat
    n_tok = n_steps * STEP
    assert n_tok == N_TOKENS and MOE_BLOCK == POST_TILE
    tiles_per_step = STEP // POST_TILE

    x_all = jnp.concatenate([x_prompt.reshape(-1, d), x_sample.reshape(-1, d)], axis=0)

    cc = jnp.zeros((8, d), F32).at[:n_lat].set(c.astype(F32)).at[n_lat].set(c_ctx.astype(F32))
    mod = _ada(cc, w_ada, b_ada).reshape(depth, 8, N_MOD, d)
    step_src = np.array([n_lat] * ctx_steps + list(range(n_lat)))
    mod_step = jnp.pad(mod[:, step_src], ((0, 0), (0, 0), (0, 8 - N_MOD), (0, 0)))
    mod_tile = jnp.repeat(mod_step, tiles_per_step, axis=1)

    nc = jnp.asarray([ctx_len // CHUNK] * ctx_steps + [lat_len // CHUNK] * n_lat, jnp.int32)
    tile_pos = np.arange(n_lat * tiles_per_step) % tiles_per_step
    pv = jnp.asarray(np.concatenate([np.zeros(ctx_steps * tiles_per_step), tile_pos > 0]), jnp.int32)
    nv = jnp.asarray(np.concatenate([np.zeros(ctx_steps * tiles_per_step), tile_pos < tiles_per_step - 1]), jnp.int32)

    lbv = jnp.cumsum(jax.nn.softmax(hgrn_lb_logits.astype(F32), axis=0), axis=0)
    lbv = (lbv - lbv[:1]).reshape(depth, 2, N_HEADS, HEAD_DIM)
    lvl_f, lvl_b = _level_ids()

    xs = x_all
    ctx_states = []
    for l in range(depth):
        lb = lbv[l]
        gp = jnp.stack([jnp.log(lb[0]), jnp.log1p(-lb[0]), 1.0 - lb[0],
                        jnp.log(lb[1]), jnp.log1p(-lb[1]), 1.0 - lb[1],
                        hgrn_norm_g[l].reshape(N_HEADS, HEAD_DIM).astype(F32),
                        jnp.zeros((N_HEADS, HEAD_DIM), F32)], axis=1)
        s0 = jnp.concatenate([jnp.zeros((ctx_steps, 2, N_HEADS, HEAD_DIM, HEAD_DIM), F32),
                              state_hgrn[:, l].astype(F32)], axis=0)
        g1 = norm1_g[l].reshape(1, d).astype(F32)

        o_all, states = _scan(nc, xs, mod_step[l], g1, w_in, l, gp, s0, lvl_f, lvl_b)
        ctx_states.append(states[:ctx_steps].reshape(n_ctx, 2, N_HEADS, HEAD_DIM, HEAD_DIM))

        cw = jnp.pad(conv_dw_w[l].astype(F32), ((0, 1), (0, 0)))
        x1, h2, logits_t = _post(
            pv, nv, xs, mod_tile[l], o_all, w_in, l, w_proj_hgrn[l], w_proj_conv[l], w_out[l],
            w_router[l].T.astype(F32), g1, cw,
            conv_dw_b[l].reshape(1, -1).astype(F32), conv_norm_g[l].reshape(1, -1).astype(F32),
            conv_norm_b[l].reshape(1, -1).astype(F32), norm2_g[l].reshape(1, d).astype(F32))

        comb_t, chosen_t = _route(logits_t, router_bias[l].reshape(N_EXPERTS, 1).astype(F32))
        plan = _routing_plan(chosen_t)
        comb_bt = comb_t.reshape(N_EXPERTS, n_tok // MOE_BLOCK, MOE_BLOCK).transpose(1, 2, 0).astype(BF16)
        xp = _permute(plan, h2)
        y_sorted = _expert_ffn(plan, xp, w_expert_gate_up, w_expert_down, l)
        xs = _combine(plan, y_sorted, comb_bt, h2, x1, mod_tile[l], w_shared_gate_up[l].astype(BF16),
                      w_shared_down[l].astype(BF16), final_norm_g.reshape(1, d).astype(F32),
                      final=(l == depth - 1))

    n_ctx_tok = n_ctx * ctx_len
    y_prompt = xs[:n_ctx_tok].reshape(x_prompt.shape).astype(x_prompt.dtype)
    y_sample = xs[n_ctx_tok:].reshape(x_sample.shape).astype(x_sample.dtype)
    new_state = jnp.stack(ctx_states, axis=1).astype(x_prompt.dtype)
    return (y_prompt, y_sample, new_state)
```

```python
import functools

import numpy as np
import jax
import jax.numpy as jnp
from jax import lax
from jax.experimental import pallas as pl
from jax.experimental.pallas import tpu as pltpu

F32 = jnp.float32
BF16 = jnp.bfloat16
HIGHEST = lax.Precision.HIGHEST

D_MODEL = 1024
N_HEADS = 8
HEAD_DIM = 128
CONV_DIM = 512
CONV_WIDTH = 31
CONV_HALO = 16
N_EXPERTS = 64
N_GROUPS = 8
GROUP_SIZE = N_EXPERTS // N_GROUPS
TOPK_GROUPS = 4
TOP_K = 8
EXPERT_FF = 256
ROUTED_SCALE = 2.5
N_MOD = 6
EPS = 1e-6

CHUNK = 128
N_LEVELS = 7
LOCKSTEP = 2
STEP = 1024
POST_TILE = 256
MOE_BLOCK = 256
UNIT = 16
FFN_TILE = 256
UNITS_PER_TILE = FFN_TILE // UNIT
TILES_PER_STEP = 2
SLOT_MAX = MOE_BLOCK * TOP_K + N_EXPERTS * UNIT
UNITS_PER_BLOCK = SLOT_MAX // UNIT
GROUP_UNITS = 32
GROUP_SLOTS = GROUP_UNITS * UNIT
GROUPS_PER_BLOCK = UNITS_PER_BLOCK // GROUP_UNITS
N_TOKENS = 8192
N_TILES_MAX = (N_TOKENS * TOP_K // UNIT + (N_TOKENS // MOE_BLOCK) * N_EXPERTS) // UNITS_PER_TILE + N_EXPERTS
HEAD_COLS = 5 * HEAD_DIM
VMEM_LIMIT = 52 * 1024 * 1024


def _dot(a, b):
    return jnp.dot(a, b, preferred_element_type=F32)


def _dot_nt(a, b):
    return lax.dot_general(a, b, (((1,), (1,)), ((), ())), preferred_element_type=F32)


def _dot_tn(a, b):
    return lax.dot_general(a, b, (((0,), (0,)), ((), ())), preferred_element_type=F32)


def _sigmoid(x):
    return 1.0 / (1.0 + jnp.exp(-x))


def _silu(x):
    return x * _sigmoid(x)


def _rms_mod(x, g, scale, shift):
    ms = jnp.mean(x * x, axis=-1, keepdims=True)
    return x * lax.rsqrt(ms + EPS) * g * (1.0 + scale) + shift


def _ada_kernel(c_ref, w_ref, b_ref, o_ref):
    cc = c_ref[...]
    o_ref[0] = jnp.dot(_silu(cc), w_ref[0], preferred_element_type=F32, precision=HIGHEST) + b_ref[0]


def _ada(cc, w_ada, b_ada):
    depth, d, n = w_ada.shape
    tn = 1536
    return pl.pallas_call(
        _ada_kernel,
        grid=(depth, n // tn),
        in_specs=[
            pl.BlockSpec((8, d), lambda l, j: (0, 0)),
            pl.BlockSpec((1, d, tn), lambda l, j: (l, 0, j)),
            pl.BlockSpec((1, 1, tn), lambda l, j: (l, 0, j)),
        ],
        out_specs=pl.BlockSpec((1, 8, tn), lambda l, j: (l, 0, j)),
        out_shape=jax.ShapeDtypeStruct((depth, 8, n), F32),
        compiler_params=pltpu.CompilerParams(vmem_limit_bytes=VMEM_LIMIT),
        name="ada_mod",
    )(cc, w_ada, b_ada.reshape(depth, 1, n))


def _forget_gate(z, log_lb, log1m_lb, one_m_lb):
    e = jnp.exp(-jnp.abs(z))
    r = 1.0 / (1.0 + e)
    log_sig = jnp.minimum(z, 0.0) - jnp.log(1.0 + e)
    k = one_m_lb * jnp.where(z > 0, e * r, r)
    b = log1m_lb + log_sig
    log_f = jnp.maximum(log_lb, b) + jnp.log(1.0 + jnp.exp(-jnp.abs(log_lb - b)))
    return log_f, k


def _level_reference(cum_ref, blk, fwd):
    half = blk // 2
    pieces = []
    if blk >= 8:
        for i in range(CHUNK // blk):
            row = i * blk + (half - 1 if fwd else half)
            pieces.append(jnp.broadcast_to(cum_ref[row:row + 1, :], (blk, HEAD_DIM)))
    else:
        sub = lax.broadcasted_iota(jnp.int32, (8, HEAD_DIM), 0)
        for i in range(CHUNK // 8):
            lo_row = 8 * i + (1 if fwd else 2)
            hi_row = 8 * i + (5 if fwd else 6)
            lo = jnp.broadcast_to(cum_ref[lo_row:lo_row + 1, :], (8, HEAD_DIM))
            hi = jnp.broadcast_to(cum_ref[hi_row:hi_row + 1, :], (8, HEAD_DIM))
            pieces.append(jnp.where(sub >= 4, hi, lo))
    return pieces[0] if len(pieces) == 1 else jnp.concatenate(pieces, axis=0)


def _chunk_steps(chains):
    r_idx = lax.broadcasted_iota(jnp.int32, (CHUNK, CHUNK), 0)
    c_idx = lax.broadcasted_iota(jnp.int32, (CHUNK, CHUNK), 1)
    work = []
    for q, k, v, log_f, lvl, cum_ref, fwd in chains:
        tri = jnp.where((r_idx >= c_idx) if fwd else (r_idx <= c_idx), 1.0, 0.0).astype(F32)
        cum = jnp.dot(tri, log_f, preferred_element_type=F32, precision=HIGHEST)
        cum_ref[...] = cum
        q_b = q.astype(BF16)
        k_b = k.astype(BF16)
        work.append(dict(cum=cum, q_b=q_b, k_b=k_b, scores=jnp.where(lvl == 0, _dot_nt(q_b, k_b), 0.0)))

    for lev in range(1, N_LEVELS + 1):
        for (q, k, v, log_f, lvl, cum_ref, fwd), w in zip(chains, work):
            if lev == 1:
                qe = w["q_b"] * jnp.exp(log_f).astype(BF16)
                ke = w["k_b"]
            else:
                e = jnp.exp(-jnp.abs(w["cum"] - _level_reference(cum_ref, 1 << lev, fwd))).astype(BF16)
                qe = w["q_b"] * e
                ke = w["k_b"] * e
            w["scores"] = jnp.where(lvl == lev, _dot_nt(qe, ke), w["scores"])

    def finish(chain, w, st):
        q, k, v, log_f, lvl, cum_ref, fwd = chain
        total = cum_ref[CHUNK - 1:CHUNK, :] if fwd else cum_ref[0:1, :]
        o = (_dot(w["scores"].astype(BF16), v.astype(BF16))
             + _dot_nt((q * jnp.exp(w["cum"])).astype(BF16), st.astype(BF16)))
        k_st = k * jnp.exp(total - w["cum"])
        return o, st * jnp.exp(total) + _dot_tn(v.astype(BF16), k_st.astype(BF16))

    return [functools.partial(finish, chain, w) for chain, w in zip(chains, work)]


def _scan_kernel(nc_ref, x_ref, mod_ref, g1_ref, wq_ref, wv_ref, wf_ref, wb_ref, wg_ref, gp_ref, s0_ref,
                 lvlf_ref, lvlb_ref, o_out, st_out, h_ref, w_ref, z_ref, of_ref, ob_ref, cum_ref):
    head = pl.program_id(0)
    step = pl.program_id(1)
    n_seq_chunks = nc_ref[step]
    n_chunks = STEP // CHUNK

    @pl.when(head == 0)
    def _():
        g1 = g1_ref[...]
        shift = mod_ref[0, 0:1, :]
        scale = mod_ref[0, 1:2, :]

        for i in range(n_chunks):
            rows = pl.ds(i * CHUNK, CHUNK)
            h_ref[step, rows, :] = _rms_mod(x_ref[rows, :], g1, scale, shift).astype(BF16)

    @pl.when(step == 0)
    def _():
        for j, wj_ref in enumerate((wq_ref, wv_ref, wf_ref, wb_ref, wg_ref)):
            w_ref[:, j * HEAD_DIM:(j + 1) * HEAD_DIM] = wj_ref[0].astype(BF16)

    z_ref[...] = _dot(h_ref[step], w_ref[...])
    st_out[...] = jnp.zeros(st_out.shape, F32)
    st0 = (s0_ref[0, 0, 0].T, s0_ref[0, 1, 0].T)

    def chain(c, fwd, cum_ref):
        d = 0 if fwd else 1
        zcol = (2 if fwd else 3) * HEAD_DIM
        rows = pl.ds(c * CHUNK, CHUNK)
        log_f, k = _forget_gate(z_ref[rows, zcol:zcol + HEAD_DIM], gp_ref[0, 3 * d:3 * d + 1, :],
                                gp_ref[0, 3 * d + 1:3 * d + 2, :], gp_ref[0, 3 * d + 2:3 * d + 3, :])
        return (z_ref[rows, 0:HEAD_DIM], k, z_ref[rows, HEAD_DIM:2 * HEAD_DIM], log_f,
                (lvlf_ref if fwd else lvlb_ref)[...], cum_ref, fwd)

    carry = list(st0)
    for i0 in range(0, n_chunks, LOCKSTEP):
        todo = [(i if fwd else n_chunks - 1 - i, fwd) for i in range(i0, i0 + LOCKSTEP) for fwd in (True, False)]
        finishers = _chunk_steps([chain(c, fwd, cum_ref.at[n]) for n, (c, fwd) in enumerate(todo)])
        for (c, fwd), finish_chain in zip(todo, finishers):
            d = 0 if fwd else 1
            pos = lax.rem(c, n_seq_chunks)
            first = (pos == 0) if fwd else (pos == n_seq_chunks - 1)
            o, carry[d] = finish_chain(jnp.where(first, st0[d], carry[d]))
            (of_ref if fwd else ob_ref)[pl.ds(c * CHUNK, CHUNK), :] = o
            st_out[0, lax.div(c, n_seq_chunks), d, 0] = carry[d].T

    norm_g = gp_ref[0, 6:7, :]

    chunk_rows = [pl.ds(i * CHUNK, CHUNK) for i in range(n_chunks)]
    os_ = [of_ref[rows, :] + ob_ref[rows, :] for rows in chunk_rows]
    invs = [lax.rsqrt(jnp.mean(o * o, axis=-1, keepdims=True) + EPS) for o in os_]
    for rows, o, inv in zip(chunk_rows, os_, invs):
        og = z_ref[rows, 4 * HEAD_DIM:5 * HEAD_DIM]
        o_out[rows, :] = (o * inv * norm_g * _silu(og)).astype(BF16)


def _scan(nc, x_all, mod_step, g1, w_in, layer, gp, s0, lvl_f, lvl_b):
    n_tok = x_all.shape[0]
    n_steps = n_tok // STEP
    grid_spec = pltpu.PrefetchScalarGridSpec(
        num_scalar_prefetch=1,
        grid=(N_HEADS, n_steps),
        in_specs=[
            pl.BlockSpec((STEP, D_MODEL), lambda h, s, nc: (jnp.where(h == 0, s, 0), 0)),
            pl.BlockSpec((1, 8, D_MODEL), lambda h, s, nc: (s, 0, 0)),
            pl.BlockSpec((1, D_MODEL), lambda h, s, nc: (0, 0)),
        ] + [
            pl.BlockSpec((1, D_MODEL, HEAD_DIM), lambda h, s, nc, j=j: (layer, 0, j * N_HEADS + h))
            for j in range(5)
        ] + [
            pl.BlockSpec((1, 8, HEAD_DIM), lambda h, s, nc: (h, 0, 0)),
            pl.BlockSpec((1, 2, 1, HEAD_DIM, HEAD_DIM), lambda h, s, nc: (s, 0, h, 0, 0)),
            pl.BlockSpec((CHUNK, CHUNK), lambda h, s, nc: (0, 0)),
            pl.BlockSpec((CHUNK, CHUNK), lambda h, s, nc: (0, 0)),
        ],
        out_specs=[
            pl.BlockSpec((STEP, HEAD_DIM), lambda h, s, nc: (s, h)),
            pl.BlockSpec((1, 4, 2, 1, HEAD_DIM, HEAD_DIM), lambda h, s, nc: (s, 0, 0, h, 0, 0)),
        ],
        scratch_shapes=[
            pltpu.VMEM((n_steps, STEP, D_MODEL), BF16),
            pltpu.VMEM((D_MODEL, HEAD_COLS), BF16),
            pltpu.VMEM((STEP, HEAD_COLS), F32),
            pltpu.VMEM((STEP, HEAD_DIM), F32),
            pltpu.VMEM((STEP, HEAD_DIM), F32),
            pltpu.VMEM((2 * LOCKSTEP, CHUNK, HEAD_DIM), F32),
        ],
    )
    return pl.pallas_call(
        _scan_kernel,
        grid_spec=grid_spec,
        out_shape=[
            jax.ShapeDtypeStruct((n_tok, D_MODEL), BF16),
            jax.ShapeDtypeStruct((n_steps, 4, 2, N_HEADS, HEAD_DIM, HEAD_DIM), F32),
        ],
        compiler_params=pltpu.CompilerParams(
            dimension_semantics=("arbitrary", "arbitrary"), vmem_limit_bytes=VMEM_LIMIT),
        name="hgrn_scan",
    )(nc, x_all, mod_step, g1, *([w_in] * 5), gp, s0, lvl_f, lvl_b)


def _post_kernel(pv_ref, nv_ref, x_ref, xp_ref, xn_ref, mod_ref, o_ref, wu_f32, wgz_f32, wph_f32,
                 wpc_f32, wo_f32, wr_ref, g1_ref, cw_ref, cb_ref, lg_ref, lb_ref, g2_ref,
                 x1_out, h2_out, lg_out, cu_ref, wu_ref, wgz_ref, wph_ref, wpc_ref, wo_ref):
    i = pl.program_id(0)

    @pl.when(i == 0)
    def _():
        for dst, src in ((wu_ref, wu_f32), (wgz_ref, wgz_f32), (wph_ref, wph_f32),
                         (wpc_ref, wpc_f32), (wo_ref, wo_f32)):
            dst[...] = (src[0] if len(src.shape) == 3 else src[...]).astype(BF16)

    g1 = g1_ref[...]
    shift1 = mod_ref[0, 0:1, :]
    scale1 = mod_ref[0, 1:2, :]
    gate1 = mod_ref[0, 2:3, :]
    shift2 = mod_ref[0, 3:4, :]
    scale2 = mod_ref[0, 4:5, :]

    def glu(xv):
        h = _rms_mod(xv, g1, scale1, shift1).astype(BF16)
        u = _dot(h, wu_ref[...])
        return h, u[:, :CONV_DIM] * _sigmoid(u[:, CONV_DIM:])

    x = x_ref[...]
    h, glu_mid = glu(x)
    _, glu_prev = glu(xp_ref[...])
    _, glu_next = glu(xn_ref[...])
    cu_ref[0, 0:CONV_HALO, :] = glu_prev * pv_ref[i].astype(F32)
    cu_ref[0, CONV_HALO:CONV_HALO + POST_TILE, :] = glu_mid
    cu_ref[0, CONV_HALO + POST_TILE:, :] = glu_next * nv_ref[i].astype(F32)
    n_keep = POST_TILE + 2 * CONV_HALO - 8
    for k in range(1, 8):
        cu_ref[k, 0:n_keep, :] = cu_ref[0, k:k + n_keep, :]

    off = CONV_HALO - CONV_WIDTH // 2
    acc = jnp.zeros((POST_TILE, CONV_DIM), F32) + cb_ref[...]
    for j in range(CONV_WIDTH):
        base, k = divmod(off + j, 8)
        acc = acc + cu_ref[k, 8 * base:8 * base + POST_TILE, :] * cw_ref[j:j + 1, :]
    mu = jnp.mean(acc, axis=-1, keepdims=True)
    cen = acc - mu
    var = jnp.mean(cen * cen, axis=-1, keepdims=True)
    cv = cen * lax.rsqrt(var + EPS) * lg_ref[...] + lb_ref[...]
    y_b = _dot(_silu(cv).astype(BF16), wpc_ref[...])

    y_a = _dot(o_ref[...], wph_ref[...])
    gz = _sigmoid(_dot(h, wgz_ref[...]))
    merged = gz[:, :D_MODEL] * y_a + gz[:, D_MODEL:] * y_b
    x1 = x + gate1 * _dot(merged.astype(BF16), wo_ref[...])
    x1_out[...] = x1
    h2 = _rms_mod(x1, g2_ref[...], scale2, shift2)
    h2_out[...] = h2.astype(BF16)
    lg_out[...] = lax.dot_general(wr_ref[...], h2, (((1,), (1,)), ((), ())),
                                  preferred_element_type=F32, precision=HIGHEST)


def _post(pv, nv, x_all, mod_tile, o_all, w_in, layer, w_ph, w_pc, w_o, w_rt, g1, cw, cb, lg, lb, g2):
    n_tok = x_all.shape[0]
    n_tiles = n_tok // POST_TILE
    halo_per_tile = POST_TILE // CONV_HALO
    n_halo_blocks = n_tok // CONV_HALO
    glu_cols, gate_cols = 2 * CONV_DIM, 2 * D_MODEL
    glu_start = 5 * N_HEADS * HEAD_DIM
    assert glu_start % glu_cols == 0 and (glu_start + glu_cols) % gate_cols == 0

    def full(a):
        return pl.BlockSpec(a.shape, lambda i, pv, nv: (0,) * a.ndim)

    def resident(shape, index):
        return pl.BlockSpec(shape, lambda i, pv, nv: index, pipeline_mode=pl.Buffered(1))

    grid_spec = pltpu.PrefetchScalarGridSpec(
        num_scalar_prefetch=2,
        grid=(n_tiles,),
        in_specs=[
            pl.BlockSpec((POST_TILE, D_MODEL), lambda i, pv, nv: (i, 0)),
            pl.BlockSpec((CONV_HALO, D_MODEL), lambda i, pv, nv: (jnp.maximum(i * halo_per_tile - 1, 0), 0)),
            pl.BlockSpec((CONV_HALO, D_MODEL),
                         lambda i, pv, nv: (jnp.minimum((i + 1) * halo_per_tile, n_halo_blocks - 1), 0)),
            pl.BlockSpec((1, 8, D_MODEL), lambda i, pv, nv: (i, 0, 0)),
            pl.BlockSpec((POST_TILE, D_MODEL), lambda i, pv, nv: (i, 0)),
            resident((1, D_MODEL, glu_cols), (layer, 0, glu_start // glu_cols)),
            resident((1, D_MODEL, gate_cols), (layer, 0, (glu_start + glu_cols) // gate_cols)),
            resident(w_ph.shape, (0, 0)), resident(w_pc.shape, (0, 0)), resident(w_o.shape, (0, 0)),
            full(w_rt), full(g1), full(cw), full(cb), full(lg), full(lb), full(g2),
        ],
        out_specs=[
            pl.BlockSpec((POST_TILE, D_MODEL), lambda i, pv, nv: (i, 0)),
            pl.BlockSpec((POST_TILE, D_MODEL), lambda i, pv, nv: (i, 0)),
            pl.BlockSpec((N_EXPERTS, POST_TILE), lambda i, pv, nv: (0, i)),
        ],
        scratch_shapes=[
            pltpu.VMEM((8, POST_TILE + 2 * CONV_HALO, CONV_DIM), F32),
            pltpu.VMEM((D_MODEL, glu_cols), BF16), pltpu.VMEM((D_MODEL, gate_cols), BF16),
            pltpu.VMEM(w_ph.shape, BF16), pltpu.VMEM(w_pc.shape, BF16), pltpu.VMEM(w_o.shape, BF16),
        ],
    )
    return pl.pallas_call(
        _post_kernel,
        grid_spec=grid_spec,
        out_shape=[
            jax.ShapeDtypeStruct((n_tok, D_MODEL), F32),
            jax.ShapeDtypeStruct((n_tok, D_MODEL), BF16),
            jax.ShapeDtypeStruct((N_EXPERTS, n_tok), F32),
        ],
        compiler_params=pltpu.CompilerParams(
            dimension_semantics=("arbitrary",), vmem_limit_bytes=VMEM_LIMIT),
        name="post_mixer",
    )(pv, nv, x_all, x_all, x_all, mod_tile, o_all, w_in, w_in, w_ph, w_pc, w_o, w_rt, g1, cw, cb, lg, lb, g2)


def _route_kernel(lg_ref, bias_ref, comb_out, chosen_out):
    n = lg_ref.shape[1]
    scores = _sigmoid(lg_ref[...])
    sel = scores + bias_ref[...]
    neg = jnp.float32(-jnp.inf)

    sel3 = sel.reshape(N_GROUPS, GROUP_SIZE, n)
    m1 = jnp.max(sel3, axis=1, keepdims=True)
    is_m1 = sel3 == m1
    n_m1 = jnp.sum(is_m1.astype(F32), axis=1, keepdims=True)
    m2 = jnp.max(jnp.where(is_m1, neg, sel3), axis=1, keepdims=True)
    grp = (m1 + jnp.where(n_m1 > 1.5, m1, m2)).reshape(N_GROUPS, n)

    gidx = lax.broadcasted_iota(jnp.int32, (N_GROUPS, n), 0)
    rank = jnp.zeros((N_GROUPS, n), F32)
    for g in range(N_GROUPS):
        other = grp[g:g + 1, :]
        ahead = (other > grp) | ((other == grp) & (g < gidx))
        rank = rank + ahead.astype(F32)
    keep_g = rank < TOPK_GROUPS - 0.5
    keep = jnp.broadcast_to(keep_g.reshape(N_GROUPS, 1, n), (N_GROUPS, GROUP_SIZE, n)).reshape(N_EXPERTS, n)
    cand = jnp.where(keep, sel, neg)

    eidx = lax.broadcasted_iota(jnp.int32, (N_EXPERTS, n), 0)
    chosen = jnp.zeros((N_EXPERTS, n), F32)
    for _ in range(TOP_K):
        best = jnp.max(cand, axis=0, keepdims=True)
        first = jnp.min(jnp.where(cand == best, eidx, N_EXPERTS), axis=0, keepdims=True)
        hit = eidx == first
        chosen = jnp.where(hit, 1.0, chosen)
        cand = jnp.where(hit, neg, cand)
    w = scores * chosen
    comb_out[...] = w / jnp.sum(w, axis=0, keepdims=True) * ROUTED_SCALE
    chosen_out[...] = chosen


def _route(logits_t, bias):
    n_tok = logits_t.shape[1]
    tile = 512
    return pl.pallas_call(
        _route_kernel,
        grid=(n_tok // tile,),
        in_specs=[
            pl.BlockSpec((N_EXPERTS, tile), lambda i: (0, i)),
            pl.BlockSpec((N_EXPERTS, 1), lambda i: (0, 0)),
        ],
        out_specs=[pl.BlockSpec((N_EXPERTS, tile), lambda i: (0, i)),
                   pl.BlockSpec((N_EXPERTS, tile), lambda i: (0, i))],
        out_shape=[jax.ShapeDtypeStruct((N_EXPERTS, n_tok), F32),
                   jax.ShapeDtypeStruct((N_EXPERTS, n_tok), F32)],
        name="router",
    )(logits_t, bias)


def _count_le(bounds, idx):
    return jnp.sum(jnp.where(bounds <= idx, 1.0, 0.0), axis=0, keepdims=True)


def _pick_row(onehot, col):
    return jnp.sum(jnp.where(onehot, col, 0.0), axis=0, keepdims=True)


def _plan_kernel(ch_ref, rankt_out, es_out, rs_out, src_out, et_out, ysrc_out, misc_out):
    nb = N_TOKENS // MOE_BLOCK
    n_src = N_TILES_MAX * UNITS_PER_TILE
    e_col = lax.broadcasted_iota(jnp.int32, (N_EXPERTS, 1), 0).astype(F32)
    blk_lane = lax.broadcasted_iota(jnp.int32, (N_EXPERTS, 128), 1)
    t_r = lax.broadcasted_iota(jnp.int32, (MOE_BLOCK, MOE_BLOCK), 0)
    t_c = lax.broadcasted_iota(jnp.int32, (MOE_BLOCK, MOE_BLOCK), 1)
    earlier = jnp.where(t_c < t_r, 1.0, 0.0).astype(BF16)
    e_r = lax.broadcasted_iota(jnp.int32, (N_EXPERTS, N_EXPERTS), 0)
    e_c = lax.broadcasted_iota(jnp.int32, (N_EXPERTS, N_EXPERTS), 1)
    upto = jnp.where(e_c <= e_r, 1.0, 0.0).astype(F32)
    slot = lax.broadcasted_iota(jnp.int32, (1, SLOT_MAX), 1).astype(F32)

    def col_cumsum(col):
        wide = jnp.broadcast_to(col, (N_EXPERTS, 128))
        return jnp.dot(upto, wide, preferred_element_type=F32, precision=HIGHEST)[:, 0:1]

    def block_stats(b, carry):
        units_mat, start_mat = carry
        cb = ch_ref[:, pl.ds(pl.multiple_of(b * MOE_BLOCK, MOE_BLOCK), MOE_BLOCK)]
        rank_t = _dot_nt(earlier, cb.astype(BF16))
        cb_t = cb.T
        rankt_out[b] = jnp.where(cb_t > 0, rank_t, -1.0).astype(BF16)
        cnt = jnp.sum(cb, axis=1, keepdims=True)
        units = jnp.floor((cnt + (UNIT - 1)) * (1.0 / UNIT))
        incl = col_cumsum(units)
        start = incl - units
        e_slot = _count_le(incl * UNIT, slot)
        onehot = e_col == e_slot
        r = slot - _pick_row(onehot, start * UNIT)
        valid = (e_slot < N_EXPERTS - 0.5) & (r < _pick_row(onehot, cnt))
        es_out[b] = jnp.where(valid, e_slot, -1.0).astype(jnp.int32)
        rs_out[b] = jnp.where(valid, r, -2.0)
        units_mat = jnp.where(blk_lane == b, units, units_mat)
        start_mat = jnp.where(blk_lane == b, start, start_mat)
        return units_mat, start_mat

    zeros = jnp.zeros((N_EXPERTS, 128), F32)
    units_mat, start_mat = lax.fori_loop(0, nb, block_stats, (zeros, zeros))

    tot = jnp.sum(units_mat, axis=1, keepdims=True)
    tiles_e = jnp.floor((tot + (UNITS_PER_TILE - 1)) * (1.0 / UNITS_PER_TILE))
    incl_t = col_cumsum(tiles_e)
    start_t = incl_t - tiles_e
    n_used = incl_t[N_EXPERTS - 1:N_EXPERTS, :]
    b_r = lax.broadcasted_iota(jnp.int32, (128, 128), 0)
    b_c = lax.broadcasted_iota(jnp.int32, (128, 128), 1)
    before = jnp.where(b_r < b_c, 1.0, 0.0).astype(F32)
    cum_b = jnp.dot(units_mat, before, preferred_element_type=F32, precision=HIGHEST)
    run_pos = start_t * UNITS_PER_TILE + cum_b

    tile_idx = lax.broadcasted_iota(jnp.int32, (1, 512), 1).astype(F32)
    et_out[...] = jnp.minimum(_count_le(incl_t, tile_idx), N_EXPERTS - 1.0).astype(jnp.int32)

    incl_ub_t = (cum_b + units_mat).T
    cum_b_t = cum_b.T
    start_mat_t = start_mat.T
    b_col = lax.broadcasted_iota(jnp.int32, (128, 1), 0)
    chunk = 1024

    def src_chunk(c, carry):
        off = pl.multiple_of(c * chunk, chunk)
        p = (lax.broadcasted_iota(jnp.int32, (1, chunk), 1) + off).astype(F32)
        e_p = jnp.minimum(_count_le(incl_t * UNITS_PER_TILE, p), N_EXPERTS - 1.0)
        onehot_e = e_col == e_p
        q = p - _pick_row(onehot_e, start_t * UNITS_PER_TILE)
        valid = (p < n_used * UNITS_PER_TILE) & (q < _pick_row(onehot_e, tot))
        sel = jnp.where(onehot_e, 1.0, 0.0).astype(F32)
        incl_sel = jnp.dot(incl_ub_t, sel, preferred_element_type=F32, precision=HIGHEST)
        b_p = jnp.sum(jnp.where((incl_sel <= q) & (b_col < nb), 1.0, 0.0), axis=0, keepdims=True)
        b_p = jnp.minimum(b_p, nb - 1.0)
        onehot_b = b_col.astype(F32) == b_p
        cum_sel = jnp.dot(cum_b_t, sel, preferred_element_type=F32, precision=HIGHEST)
        start_sel = jnp.dot(start_mat_t, sel, preferred_element_type=F32, precision=HIGHEST)
        unit = b_p * UNITS_PER_BLOCK + _pick_row(onehot_b, start_sel) + q - _pick_row(onehot_b, cum_sel)
        src_out[:, pl.ds(off, chunk)] = jnp.where(valid, unit, 0.0).astype(jnp.int32)
        return carry

    lax.fori_loop(0, n_src // chunk, src_chunk, 0)

    unit_idx = lax.broadcasted_iota(jnp.int32, (1, 256), 1).astype(F32)
    misc_lane = lax.broadcasted_iota(jnp.int32, (1, 128), 1)

    def block_units(b, used):
        here = blk_lane == b
        units = jnp.sum(jnp.where(here, units_mat, 0.0), axis=1, keepdims=True)
        start = jnp.sum(jnp.where(here, start_mat, 0.0), axis=1, keepdims=True)
        pos = jnp.sum(jnp.where(here, run_pos, 0.0), axis=1, keepdims=True)
        incl = start + units
        e_unit = _count_le(incl, unit_idx)
        onehot = e_col == e_unit
        val = _pick_row(onehot, pos - start) + unit_idx
        ysrc_out[b] = jnp.where(e_unit < N_EXPERTS - 0.5, val, 0.0).astype(jnp.int32)
        return jnp.where(misc_lane == b, incl[N_EXPERTS - 1:N_EXPERTS, :], used)

    used = lax.fori_loop(0, nb, block_units, jnp.zeros((1, 128), F32))
    misc_out[0:1, :] = used.astype(jnp.int32)
    misc_out[1:2, :] = jnp.broadcast_to(n_used, (1, 128)).astype(jnp.int32)
    misc_out[2:8, :] = jnp.zeros((6, 128), jnp.int32)


def _routing_plan(chosen):
    nb = N_TOKENS // MOE_BLOCK
    n_src = N_TILES_MAX * UNITS_PER_TILE
    assert n_src % 1024 == 0 and N_TILES_MAX <= 512 and nb <= 128 and UNITS_PER_BLOCK <= 256
    rank_bt, e_slot, r_slot, src, e_tile, ysrc, misc = pl.pallas_call(
        _plan_kernel,
        out_shape=[
            jax.ShapeDtypeStruct((nb, MOE_BLOCK, N_EXPERTS), BF16),
            jax.ShapeDtypeStruct((nb, 1, SLOT_MAX), jnp.int32),
            jax.ShapeDtypeStruct((nb, 1, SLOT_MAX), F32),
            jax.ShapeDtypeStruct((1, n_src), jnp.int32),
            jax.ShapeDtypeStruct((1, 512), jnp.int32),
            jax.ShapeDtypeStruct((nb, 1, 256), jnp.int32),
            jax.ShapeDtypeStruct((8, 128), jnp.int32),
        ],
        compiler_params=pltpu.CompilerParams(vmem_limit_bytes=VMEM_LIMIT),
        name="moe_plan",
    )(chosen)
    return dict(rank_bt=rank_bt, e_slot=e_slot, r_slot=r_slot, src=src.reshape(-1),
                e_tile=e_tile[0, :N_TILES_MAX], n_used=misc[1, :1],
                ysrc=ysrc[:, 0, :UNITS_PER_BLOCK].reshape(-1))


def _permute_kernel(h_ref, rank_ref, es_ref, rs_ref, xp_out):
    sub_e = lax.broadcasted_iota(jnp.int32, (N_EXPERTS, FFN_TILE), 0)
    tiles = [slice(i * FFN_TILE, (i + 1) * FFN_TILE) for i in range(SLOT_MAX // FFN_TILE)]
    picks = []
    for rows in tiles:
        onehot_e = jnp.where(sub_e == es_ref[0, :, rows], 1.0, 0.0).astype(BF16)
        slot_rank = _dot(rank_ref[0], onehot_e)
        picks.append(jnp.where(slot_rank == rs_ref[0, :, rows], 1.0, 0.0).astype(BF16))
    for rows, pick in zip(tiles, picks):
        xp_out[0, rows, :] = _dot_tn(pick, h_ref[...]).astype(BF16)


def _permute(plan, h2):
    nb = h2.shape[0] // MOE_BLOCK
    return pl.pallas_call(
        _permute_kernel,
        grid=(nb,),
        in_specs=[
            pl.BlockSpec((MOE_BLOCK, D_MODEL), lambda b: (b, 0)),
            pl.BlockSpec((1, MOE_BLOCK, N_EXPERTS), lambda b: (b, 0, 0)),
            pl.BlockSpec((1, 1, SLOT_MAX), lambda b: (b, 0, 0)),
            pl.BlockSpec((1, 1, SLOT_MAX), lambda b: (b, 0, 0)),
        ],
        out_specs=pl.BlockSpec((1, SLOT_MAX, D_MODEL), lambda b: (b, 0, 0)),
        out_shape=jax.ShapeDtypeStruct((nb, SLOT_MAX, D_MODEL), BF16),
        compiler_params=pltpu.CompilerParams(
            dimension_semantics=("arbitrary",), vmem_limit_bytes=VMEM_LIMIT),
        name="moe_permute",
    )(h2, plan["rank_bt"], plan["e_slot"], plan["r_slot"])


def _unit_copies(table_ref, base, src_hbm, dst_ref, slot, sem, n_units, start):
    for u in range(n_units):
        copy = pltpu.make_async_copy(src_hbm.at[table_ref[base + u]],
                                     dst_ref.at[slot, pl.ds(u * UNIT, UNIT), :], sem.at[slot])
        if start:
            copy.start()
        else:
            copy.wait()


def _ffn_kernel(src_ref, et_ref, nu_ref, xp_hbm, *refs):
    w_refs = refs[:2 * TILES_PER_STEP]
    y_out = refs[2 * TILES_PER_STEP]
    scratch = refs[2 * TILES_PER_STEP + 1:]
    wb_refs, xt_ref, sem = scratch[:2 * TILES_PER_STEP], scratch[2 * TILES_PER_STEP], scratch[2 * TILES_PER_STEP + 1]
    n_units = TILES_PER_STEP * UNITS_PER_TILE
    step = pl.program_id(0)
    slot = lax.rem(step, 2)
    n_used = nu_ref[0]

    def copies(s, sl, start):
        _unit_copies(src_ref, s * n_units, xp_hbm, xt_ref, sl, sem, n_units, start)

    @pl.when((step == 0) & (n_used > 0))
    def _():
        copies(0, 0, True)

    @pl.when((step + 1 < pl.num_programs(0)) & ((step + 1) * TILES_PER_STEP < n_used))
    def _():
        copies(step + 1, 1 - slot, True)

    @pl.when(step * TILES_PER_STEP < n_used)
    def _():
        for t in range(TILES_PER_STEP):
            j = step * TILES_PER_STEP + t

            @pl.when((step == 0) | (et_ref[j] != et_ref[jnp.maximum(j - TILES_PER_STEP, 0)]))
            def _():
                wb_refs[2 * t][...] = w_refs[2 * t][0, 0].astype(BF16)
                wb_refs[2 * t + 1][...] = w_refs[2 * t + 1][0, 0].astype(BF16)

        copies(step, slot, False)
        tiles = [slice(t * FFN_TILE, (t + 1) * FFN_TILE) for t in range(TILES_PER_STEP)]
        gus = [_dot(xt_ref[slot, rows, :], wb_refs[2 * t][...]) for t, rows in enumerate(tiles)]
        acts = [(_silu(gu[:, :EXPERT_FF]) * gu[:, EXPERT_FF:]).astype(BF16) for gu in gus]
        for t, (rows, act) in enumerate(zip(tiles, acts)):
            y_out[rows, :] = _dot(act, wb_refs[2 * t + 1][...]).astype(BF16)

    @pl.when(step * TILES_PER_STEP >= n_used)
    def _():
        y_out[...] = jnp.zeros(y_out.shape, BF16)


def _expert_ffn(plan, xp, w_gu, w_d, layer):
    xp_units = xp.reshape(-1, UNIT, D_MODEL)
    assert N_TILES_MAX % TILES_PER_STEP == 0

    def weight_specs(t):
        return [pl.BlockSpec((1, 1, D_MODEL, 2 * EXPERT_FF),
                             lambda s, src, et, nu, t=t: (layer, et[s * TILES_PER_STEP + t], 0, 0)),
                pl.BlockSpec((1, 1, EXPERT_FF, D_MODEL),
                             lambda s, src, et, nu, t=t: (layer, et[s * TILES_PER_STEP + t], 0, 0))]

    grid_spec = pltpu.PrefetchScalarGridSpec(
        num_scalar_prefetch=3,
        grid=(N_TILES_MAX // TILES_PER_STEP,),
        in_specs=[pl.BlockSpec(memory_space=pl.ANY)]
        + [spec for t in range(TILES_PER_STEP) for spec in weight_specs(t)],
        out_specs=pl.BlockSpec((TILES_PER_STEP * FFN_TILE, D_MODEL), lambda s, src, et, nu: (s, 0)),
        scratch_shapes=[
            pltpu.VMEM((D_MODEL, 2 * EXPERT_FF), BF16) if i % 2 == 0 else pltpu.VMEM((EXPERT_FF, D_MODEL), BF16)
            for i in range(2 * TILES_PER_STEP)
        ] + [pltpu.VMEM((2, TILES_PER_STEP * FFN_TILE, D_MODEL), BF16), pltpu.SemaphoreType.DMA((2,))],
    )
    return pl.pallas_call(
        _ffn_kernel,
        grid_spec=grid_spec,
        out_shape=jax.ShapeDtypeStruct((N_TILES_MAX * FFN_TILE, D_MODEL), BF16),
        compiler_params=pltpu.CompilerParams(
            dimension_semantics=("arbitrary",), vmem_limit_bytes=VMEM_LIMIT),
        name="moe_expert_ffn",
    )(plan["src"], plan["e_tile"], plan["n_used"], xp_units, *([w_gu, w_d] * TILES_PER_STEP))


def _combine_kernel(ysrc_ref, y_hbm, rank_ref, comb_ref, er_ref, rr_ref, h_ref, x1_ref, mod_ref,
                    wsgu_ref, wsd_ref, fg_ref, o_ref, yt_ref, sem, *, final):
    b = pl.program_id(0)
    slot = lax.rem(b, 2)

    def copies(bb, sl, start):
        _unit_copies(ysrc_ref, bb * UNITS_PER_BLOCK, y_hbm, yt_ref, sl, sem, UNITS_PER_BLOCK, start)

    @pl.when(b == 0)
    def _():
        copies(0, 0, True)

    @pl.when(b + 1 < pl.num_programs(0))
    def _():
        copies(b + 1, 1 - slot, True)

    gu = _dot(h_ref[...], wsgu_ref[...])
    shared = _dot((_silu(gu[:, :EXPERT_FF]) * gu[:, EXPERT_FF:]).astype(BF16), wsd_ref[...])

    sub_e = lax.broadcasted_iota(jnp.int32, (N_EXPERTS, GROUP_SLOTS), 0)
    groups = [slice(g * GROUP_SLOTS, (g + 1) * GROUP_SLOTS) for g in range(GROUPS_PER_BLOCK)]
    onehots = [jnp.where(sub_e == er_ref[0, :, cols], 1.0, 0.0).astype(BF16) for cols in groups]
    slot_ranks = [_dot(rank_ref[0], onehot_e) for onehot_e in onehots]
    slot_ws = [_dot(comb_ref[0], onehot_e) for onehot_e in onehots]
    weights = jnp.concatenate(
        [jnp.where(slot_rank == rr_ref[0, :, cols], slot_w, 0.0).astype(BF16)
         for cols, slot_rank, slot_w in zip(groups, slot_ranks, slot_ws)], axis=1)

    copies(b, slot, False)
    x2 = x1_ref[...] + mod_ref[0, 5:6, :] * (shared + _dot(weights, yt_ref[slot]))
    if final:
        ms = jnp.mean(x2 * x2, axis=-1, keepdims=True)
        x2 = x2 * lax.rsqrt(ms + EPS) * fg_ref[...]
    o_ref[...] = x2


def _combine(plan, y_sorted, comb_bt, h2, x1, mod_tile, ws_gu, ws_d, final_g, final):
    n_tok = h2.shape[0]
    nb = n_tok // MOE_BLOCK
    y_units = y_sorted.reshape(-1, UNIT, D_MODEL)

    def full(a):
        return pl.BlockSpec(a.shape, lambda b, ysrc: (0,) * a.ndim)

    grid_spec = pltpu.PrefetchScalarGridSpec(
        num_scalar_prefetch=1,
        grid=(nb,),
        in_specs=[
            pl.BlockSpec(memory_space=pl.ANY),
            pl.BlockSpec((1, MOE_BLOCK, N_EXPERTS), lambda b, ysrc: (b, 0, 0)),
            pl.BlockSpec((1, MOE_BLOCK, N_EXPERTS), lambda b, ysrc: (b, 0, 0)),
            pl.BlockSpec((1, 1, SLOT_MAX), lambda b, ysrc: (b, 0, 0)),
            pl.BlockSpec((1, 1, SLOT_MAX), lambda b, ysrc: (b, 0, 0)),
            pl.BlockSpec((MOE_BLOCK, D_MODEL), lambda b, ysrc: (b, 0)),
            pl.BlockSpec((MOE_BLOCK, D_MODEL), lambda b, ysrc: (b, 0)),
            pl.BlockSpec((1, 8, D_MODEL), lambda b, ysrc: (b, 0, 0)),
            full(ws_gu), full(ws_d), full(final_g),
        ],
        out_specs=pl.BlockSpec((MOE_BLOCK, D_MODEL), lambda b, ysrc: (b, 0)),
        scratch_shapes=[
            pltpu.VMEM((2, SLOT_MAX, D_MODEL), BF16),
            pltpu.SemaphoreType.DMA((2,)),
        ],
    )
    return pl.pallas_call(
        functools.partial(_combine_kernel, final=final),
        grid_spec=grid_spec,
        out_shape=jax.ShapeDtypeStruct((n_tok, D_MODEL), F32),
        compiler_params=pltpu.CompilerParams(
            dimension_semantics=("arbitrary",), vmem_limit_bytes=VMEM_LIMIT),
        name="moe_combine",
    )(plan["ysrc"], y_units, plan["rank_bt"], comb_bt,
      plan["e_slot"], plan["r_slot"], h2, x1, mod_tile, ws_gu, ws_d, final_g)


def _level_ids():
    t = np.arange(CHUNK)
    x = t[:, None] ^ t[None, :]
    lvl = np.zeros((CHUNK, CHUNK), np.int32)
    nz = x > 0
    lvl[nz] = np.floor(np.log2(x[nz])).astype(np.int32) + 1
    past_f = t[:, None] >= t[None, :]
    return jnp.asarray(np.where(past_f, lvl, -1)), jnp.asarray(np.where(past_f.T, lvl, -1))


def kernel(x_prompt, x_sample, state_hgrn, c, c_ctx, w_ada, b_ada, norm1_g, w_in, hgrn_lb_logits, hgrn_norm_g, w_proj_hgrn, conv_dw_w, conv_dw_b, conv_norm_g, conv_norm_b, w_proj_conv, w_out, norm2_g, w_router, router_bias, w_expert_gate_up, w_expert_down, w_shared_gate_up, w_shared_down, final_norm_g):
    n_ctx, ctx_len, d = x_prompt.shape
    n_lat, lat_len, _ = x_sample.shape
    depth = w_ada.shape[0]
    assert d == D_MODEL and ctx_len * 4 == STEP and lat_len == STEP and n_ctx % 4 == 0
    ctx_steps = n_ctx * ctx_len // STEP
    n_steps = ctx_steps + n_lat
    n_tok = n_steps * STEP
    assert n_tok == N_TOKENS and MOE_BLOCK == POST_TILE
    tiles_per_step = STEP // POST_TILE

    x_all = jnp.concatenate([x_prompt.reshape(-1, d), x_sample.reshape(-1, d)], axis=0)

    cc = jnp.zeros((8, d), F32).at[:n_lat].set(c.astype(F32)).at[n_lat].set(c_ctx.astype(F32))
    mod = _ada(cc, w_ada, b_ada).reshape(depth, 8, N_MOD, d)
    step_src = np.array([n_lat] * ctx_steps + list(range(n_lat)))
    mod_step = jnp.pad(mod[:, step_src], ((0, 0), (0, 0), (0, 8 - N_MOD), (0, 0)))
    mod_tile = jnp.repeat(mod_step, tiles_per_step, axis=1)

    nc = jnp.asarray([ctx_len // CHUNK] * ctx_steps + [lat_len // CHUNK] * n_lat, jnp.int32)
    tile_pos = np.arange(n_lat * tiles_per_step) % tiles_per_step
    pv = jnp.asarray(np.concatenate([np.zeros(ctx_steps * tiles_per_step), tile_pos > 0]), jnp.int32)
    nv = jnp.asarray(np.concatenate([np.zeros(ctx_steps * tiles_per_step), tile_pos < tiles_per_step - 1]), jnp.int32)

    lbv = jnp.cumsum(jax.nn.softmax(hgrn_lb_logits.astype(F32), axis=0), axis=0)
    lbv = (lbv - lbv[:1]).reshape(depth, 2, N_HEADS, HEAD_DIM)
    lvl_f, lvl_b = _level_ids()

    xs = x_all
    ctx_states = []
    for l in range(depth):
        lb = lbv[l]
        gp = jnp.stack([jnp.log(lb[0]), jnp.log1p(-lb[0]), 1.0 - lb[0],
                        jnp.log(lb[1]), jnp.log1p(-lb[1]), 1.0 - lb[1],
                        hgrn_norm_g[l].reshape(N_HEADS, HEAD_DIM).astype(F32),
                        jnp.zeros((N_HEADS, HEAD_DIM), F32)], axis=1)
        s0 = jnp.concatenate([jnp.zeros((ctx_steps, 2, N_HEADS, HEAD_DIM, HEAD_DIM), F32),
                              state_hgrn[:, l].astype(F32)], axis=0)
        g1 = norm1_g[l].reshape(1, d).astype(F32)

        o_all, states = _scan(nc, xs, mod_step[l], g1, w_in, l, gp, s0, lvl_f, lvl_b)
        ctx_states.append(states[:ctx_steps].reshape(n_ctx, 2, N_HEADS, HEAD_DIM, HEAD_DIM))

        cw = jnp.pad(conv_dw_w[l].astype(F32), ((0, 1), (0, 0)))
        x1, h2, logits_t = _post(
            pv, nv, xs, mod_tile[l], o_all, w_in, l, w_proj_hgrn[l], w_proj_conv[l], w_out[l],
            w_router[l].T.astype(F32), g1, cw,
            conv_dw_b[l].reshape(1, -1).astype(F32), conv_norm_g[l].reshape(1, -1).astype(F32),
            conv_norm_b[l].reshape(1, -1).astype(F32), norm2_g[l].reshape(1, d).astype(F32))

        comb_t, chosen_t = _route(logits_t, router_bias[l].reshape(N_EXPERTS, 1).astype(F32))
        plan = _routing_plan(chosen_t)
        comb_bt = comb_t.reshape(N_EXPERTS, n_tok // MOE_BLOCK, MOE_BLOCK).transpose(1, 2, 0).astype(BF16)
        xp = _permute(plan, h2)
        y_sorted = _expert_ffn(plan, xp, w_expert_gate_up, w_expert_down, l)
        xs = _combine(plan, y_sorted, comb_bt, h2, x1, mod_tile[l], w_shared_gate_up[l].astype(BF16),
                      w_shared_down[l].astype(BF16), final_norm_g.reshape(1, d).astype(F32),
                      final=(l == depth - 1))

    n_ctx_tok = n_ctx * ctx_len
    y_prompt = xs[:n_ctx_tok].reshape(x_prompt.shape).astype(x_prompt.dtype)
    y_sample = xs[n_ctx_tok:].reshape(x_sample.shape).astype(x_sample.dtype)
    new_state = jnp.stack(ctx_states, axis=1).astype(x_prompt.dtype)
    return (y_prompt, y_sample, new_state)
```

```python
import functools

import numpy as np
import jax
import jax.numpy as jnp
from jax import lax
from jax.experimental import pallas as pl
from jax.experimental.pallas import tpu as pltpu

F32 = jnp.float32
BF16 = jnp.bfloat16
HIGHEST = lax.Precision.HIGHEST

D_MODEL = 1024
N_HEADS = 8
HEAD_DIM = 128
CONV_DIM = 512
CONV_WIDTH = 31
CONV_HALO = 16
N_EXPERTS = 64
N_GROUPS = 8
GROUP_SIZE = N_EXPERTS // N_GROUPS
TOPK_GROUPS = 4
TOP_K = 8
EXPERT_FF = 256
ROUTED_SCALE = 2.5
N_MOD = 6
EPS = 1e-6

CHUNK = 128
N_LEVELS = 7
LOCKSTEP = 2
STEP = 1024
POST_TILE = 256
MOE_BLOCK = 256
UNIT = 16
FFN_TILE = 256
UNITS_PER_TILE = FFN_TILE // UNIT
TILES_PER_STEP = 2
SLOT_MAX = MOE_BLOCK * TOP_K + N_EXPERTS * UNIT
UNITS_PER_BLOCK = SLOT_MAX // UNIT
GROUP_UNITS = 32
GROUP_SLOTS = GROUP_UNITS * UNIT
GROUPS_PER_BLOCK = UNITS_PER_BLOCK // GROUP_UNITS
N_TOKENS = 8192
N_TILES_MAX = (N_TOKENS * TOP_K // UNIT + (N_TOKENS // MOE_BLOCK) * N_EXPERTS) // UNITS_PER_TILE + N_EXPERTS
HEAD_COLS = 5 * HEAD_DIM
VMEM_LIMIT = 52 * 1024 * 1024


def _dot(a, b):
    return jnp.dot(a, b, preferred_element_type=F32)


def _dot_nt(a, b):
    return lax.dot_general(a, b, (((1,), (1,)), ((), ())), preferred_element_type=F32)


def _dot_tn(a, b):
    return lax.dot_general(a, b, (((0,), (0,)), ((), ())), preferred_element_type=F32)


def _sigmoid(x):
    return 1.0 / (1.0 + jnp.exp(-x))


def _silu(x):
    return x * _sigmoid(x)


def _rms_mod(x, g, scale, shift):
    ms = jnp.mean(x * x, axis=-1, keepdims=True)
    return x * lax.rsqrt(ms + EPS) * g * (1.0 + scale) + shift


def _ada_kernel(c_ref, w_ref, b_ref, o_ref):
    cc = c_ref[...]
    o_ref[0] = jnp.dot(_silu(cc), w_ref[0], preferred_element_type=F32, precision=HIGHEST) + b_ref[0]


def _ada(cc, w_ada, b_ada):
    depth, d, n = w_ada.shape
    tn = 1536
    return pl.pallas_call(
        _ada_kernel,
        grid=(depth, n // tn),
        in_specs=[
            pl.BlockSpec((8, d), lambda l, j: (0, 0)),
            pl.BlockSpec((1, d, tn), lambda l, j: (l, 0, j)),
            pl.BlockSpec((1, 1, tn), lambda l, j: (l, 0, j)),
        ],
        out_specs=pl.BlockSpec((1, 8, tn), lambda l, j: (l, 0, j)),
        out_shape=jax.ShapeDtypeStruct((depth, 8, n), F32),
        compiler_params=pltpu.CompilerParams(vmem_limit_bytes=VMEM_LIMIT),
        name="ada_mod",
    )(cc, w_ada, b_ada.reshape(depth, 1, n))


def _forget_gate(z, log_lb, log1m_lb, one_m_lb):
    e = jnp.exp(-jnp.abs(z))
    r = 1.0 / (1.0 + e)
    log_sig = jnp.minimum(z, 0.0) - jnp.log(1.0 + e)
    k = one_m_lb * jnp.where(z > 0, e * r, r)
    b = log1m_lb + log_sig
    log_f = jnp.maximum(log_lb, b) + jnp.log(1.0 + jnp.exp(-jnp.abs(log_lb - b)))
    return log_f, k


def _level_reference(cum_ref, blk, fwd):
    half = blk // 2
    pieces = []
    if blk >= 8:
        for i in range(CHUNK // blk):
            row = i * blk + (half - 1 if fwd else half)
            pieces.append(jnp.broadcast_to(cum_ref[row:row + 1, :], (blk, HEAD_DIM)))
    else:
        sub = lax.broadcasted_iota(jnp.int32, (8, HEAD_DIM), 0)
        for i in range(CHUNK // 8):
            lo_row = 8 * i + (1 if fwd else 2)
            hi_row = 8 * i + (5 if fwd else 6)
            lo = jnp.broadcast_to(cum_ref[lo_row:lo_row + 1, :], (8, HEAD_DIM))
            hi = jnp.broadcast_to(cum_ref[hi_row:hi_row + 1, :], (8, HEAD_DIM))
            pieces.append(jnp.where(sub >= 4, hi, lo))
    return pieces[0] if len(pieces) == 1 else jnp.concatenate(pieces, axis=0)


def _chunk_steps(chains):
    r_idx = lax.broadcasted_iota(jnp.int32, (CHUNK, CHUNK), 0)
    c_idx = lax.broadcasted_iota(jnp.int32, (CHUNK, CHUNK), 1)
    work = []
    for q, k, v, log_f, lvl, cum_ref, fwd in chains:
        tri = jnp.where((r_idx >= c_idx) if fwd else (r_idx <= c_idx), 1.0, 0.0).astype(BF16)
        hi = log_f.astype(BF16)
        lo = (log_f - hi.astype(F32)).astype(BF16)
        both = _dot(tri, jnp.concatenate([hi, lo], axis=1))
        cum = both[:, :HEAD_DIM] + both[:, HEAD_DIM:]
        cum_ref[...] = cum
        q_b = q.astype(BF16)
        k_b = k.astype(BF16)
        work.append(dict(cum=cum, q_b=q_b, k_b=k_b, scores=jnp.where(lvl == 0, _dot_nt(q_b, k_b), 0.0)))

    for lev in range(1, N_LEVELS + 1):
        for (q, k, v, log_f, lvl, cum_ref, fwd), w in zip(chains, work):
            if lev == 1:
                qe = w["q_b"] * jnp.exp(log_f).astype(BF16)
                ke = w["k_b"]
            else:
                e = jnp.exp(-jnp.abs(w["cum"] - _level_reference(cum_ref, 1 << lev, fwd))).astype(BF16)
                qe = w["q_b"] * e
                ke = w["k_b"] * e
            w["scores"] = jnp.where(lvl == lev, _dot_nt(qe, ke), w["scores"])

    def finish(chain, w, st):
        q, k, v, log_f, lvl, cum_ref, fwd = chain
        total = cum_ref[CHUNK - 1:CHUNK, :] if fwd else cum_ref[0:1, :]
        o = (_dot(w["scores"].astype(BF16), v.astype(BF16))
             + _dot_nt((q * jnp.exp(w["cum"])).astype(BF16), st.astype(BF16)))
        k_st = k * jnp.exp(total - w["cum"])
        return o, st * jnp.exp(total) + _dot_tn(v.astype(BF16), k_st.astype(BF16))

    return [functools.partial(finish, chain, w) for chain, w in zip(chains, work)]


def _scan_kernel(nc_ref, x_ref, mod_ref, g1_ref, wq_ref, wv_ref, wf_ref, wb_ref, wg_ref, gp_ref, s0_ref,
                 lvlf_ref, lvlb_ref, o_out, st_out, h_ref, w_ref, z_ref, of_ref, ob_ref, cum_ref):
    head = pl.program_id(0)
    step = pl.program_id(1)
    n_seq_chunks = nc_ref[step]
    n_chunks = STEP // CHUNK

    @pl.when(head == 0)
    def _():
        g1 = g1_ref[...]
        shift = mod_ref[0, 0:1, :]
        scale = mod_ref[0, 1:2, :]

        for i in range(n_chunks):
            rows = pl.ds(i * CHUNK, CHUNK)
            h_ref[step, rows, :] = _rms_mod(x_ref[rows, :], g1, scale, shift).astype(BF16)

    @pl.when(step == 0)
    def _():
        for j, wj_ref in enumerate((wq_ref, wv_ref, wf_ref, wb_ref, wg_ref)):
            w_ref[:, j * HEAD_DIM:(j + 1) * HEAD_DIM] = wj_ref[0].astype(BF16)

    z_ref[...] = _dot(h_ref[step], w_ref[...])
    st_out[...] = jnp.zeros(st_out.shape, F32)
    st0 = (s0_ref[0, 0, 0].T, s0_ref[0, 1, 0].T)

    def chain(c, fwd, cum_ref):
        d = 0 if fwd else 1
        zcol = (2 if fwd else 3) * HEAD_DIM
        rows = pl.ds(c * CHUNK, CHUNK)
        log_f, k = _forget_gate(z_ref[rows, zcol:zcol + HEAD_DIM], gp_ref[0, 3 * d:3 * d + 1, :],
                                gp_ref[0, 3 * d + 1:3 * d + 2, :], gp_ref[0, 3 * d + 2:3 * d + 3, :])
        return (z_ref[rows, 0:HEAD_DIM], k, z_ref[rows, HEAD_DIM:2 * HEAD_DIM], log_f,
                (lvlf_ref if fwd else lvlb_ref)[...], cum_ref, fwd)

    carry = list(st0)
    for i0 in range(0, n_chunks, LOCKSTEP):
        todo = [(i if fwd else n_chunks - 1 - i, fwd) for i in range(i0, i0 + LOCKSTEP) for fwd in (True, False)]
        finishers = _chunk_steps([chain(c, fwd, cum_ref.at[n]) for n, (c, fwd) in enumerate(todo)])
        for (c, fwd), finish_chain in zip(todo, finishers):
            d = 0 if fwd else 1
            pos = lax.rem(c, n_seq_chunks)
            first = (pos == 0) if fwd else (pos == n_seq_chunks - 1)
            o, carry[d] = finish_chain(jnp.where(first, st0[d], carry[d]))
            (of_ref if fwd else ob_ref)[pl.ds(c * CHUNK, CHUNK), :] = o
            st_out[0, lax.div(c, n_seq_chunks), d, 0] = carry[d].T

    norm_g = gp_ref[0, 6:7, :]

    chunk_rows = [pl.ds(i * CHUNK, CHUNK) for i in range(n_chunks)]
    os_ = [of_ref[rows, :] + ob_ref[rows, :] for rows in chunk_rows]
    invs = [lax.rsqrt(jnp.mean(o * o, axis=-1, keepdims=True) + EPS) for o in os_]
    for rows, o, inv in zip(chunk_rows, os_, invs):
        og = z_ref[rows, 4 * HEAD_DIM:5 * HEAD_DIM]
        o_out[rows, :] = (o * inv * norm_g * _silu(og)).astype(BF16)


def _scan(nc, x_all, mod_step, g1, w_in, layer, gp, s0, lvl_f, lvl_b):
    n_tok = x_all.shape[0]
    n_steps = n_tok // STEP
    grid_spec = pltpu.PrefetchScalarGridSpec(
        num_scalar_prefetch=1,
        grid=(N_HEADS, n_steps),
        in_specs=[
            pl.BlockSpec((STEP, D_MODEL), lambda h, s, nc: (jnp.where(h == 0, s, 0), 0)),
            pl.BlockSpec((1, 8, D_MODEL), lambda h, s, nc: (s, 0, 0)),
            pl.BlockSpec((1, D_MODEL), lambda h, s, nc: (0, 0)),
        ] + [
            pl.BlockSpec((1, D_MODEL, HEAD_DIM), lambda h, s, nc, j=j: (layer, 0, j * N_HEADS + h))
            for j in range(5)
        ] + [
            pl.BlockSpec((1, 8, HEAD_DIM), lambda h, s, nc: (h, 0, 0)),
            pl.BlockSpec((1, 2, 1, HEAD_DIM, HEAD_DIM), lambda h, s, nc: (s, 0, h, 0, 0)),
            pl.BlockSpec((CHUNK, CHUNK), lambda h, s, nc: (0, 0)),
            pl.BlockSpec((CHUNK, CHUNK), lambda h, s, nc: (0, 0)),
        ],
        out_specs=[
            pl.BlockSpec((STEP, HEAD_DIM), lambda h, s, nc: (s, h)),
            pl.BlockSpec((1, 4, 2, 1, HEAD_DIM, HEAD_DIM), lambda h, s, nc: (s, 0, 0, h, 0, 0)),
        ],
        scratch_shapes=[
            pltpu.VMEM((n_steps, STEP, D_MODEL), BF16),
            pltpu.VMEM((D_MODEL, HEAD_COLS), BF16),
            pltpu.VMEM((STEP, HEAD_COLS), F32),
            pltpu.VMEM((STEP, HEAD_DIM), F32),
            pltpu.VMEM((STEP, HEAD_DIM), F32),
            pltpu.VMEM((2 * LOCKSTEP, CHUNK, HEAD_DIM), F32),
        ],
    )
    return pl.pallas_call(
        _scan_kernel,
        grid_spec=grid_spec,
        out_shape=[
            jax.ShapeDtypeStruct((n_tok, D_MODEL), BF16),
            jax.ShapeDtypeStruct((n_steps, 4, 2, N_HEADS, HEAD_DIM, HEAD_DIM), F32),
        ],
        compiler_params=pltpu.CompilerParams(
            dimension_semantics=("arbitrary", "arbitrary"), vmem_limit_bytes=VMEM_LIMIT),
        name="hgrn_scan",
    )(nc, x_all, mod_step, g1, *([w_in] * 5), gp, s0, lvl_f, lvl_b)


def _post_kernel(pv_ref, nv_ref, x_ref, xp_ref, xn_ref, mod_ref, o_ref, wu_f32, wgz_f32, wph_f32,
                 wpc_f32, wo_f32, wr_ref, g1_ref, cw_ref, cb_ref, lg_ref, lb_ref, g2_ref,
                 x1_out, h2_out, lg_out, cu_ref, wu_ref, wgz_ref, wph_ref, wpc_ref, wo_ref):
    i = pl.program_id(0)

    @pl.when(i == 0)
    def _():
        for dst, src in ((wu_ref, wu_f32), (wgz_ref, wgz_f32), (wph_ref, wph_f32),
                         (wpc_ref, wpc_f32), (wo_ref, wo_f32)):
            dst[...] = (src[0] if len(src.shape) == 3 else src[...]).astype(BF16)

    g1 = g1_ref[...]
    shift1 = mod_ref[0, 0:1, :]
    scale1 = mod_ref[0, 1:2, :]
    gate1 = mod_ref[0, 2:3, :]
    shift2 = mod_ref[0, 3:4, :]
    scale2 = mod_ref[0, 4:5, :]

    def glu(xv):
        h = _rms_mod(xv, g1, scale1, shift1).astype(BF16)
        u = _dot(h, wu_ref[...])
        return h, u[:, :CONV_DIM] * _sigmoid(u[:, CONV_DIM:])

    x = x_ref[...]
    h, glu_mid = glu(x)
    _, glu_prev = glu(xp_ref[...])
    _, glu_next = glu(xn_ref[...])
    cu_ref[0, 0:CONV_HALO, :] = glu_prev * pv_ref[i].astype(F32)
    cu_ref[0, CONV_HALO:CONV_HALO + POST_TILE, :] = glu_mid
    cu_ref[0, CONV_HALO + POST_TILE:, :] = glu_next * nv_ref[i].astype(F32)
    n_keep = POST_TILE + 2 * CONV_HALO - 8
    for k in range(1, 8):
        cu_ref[k, 0:n_keep, :] = cu_ref[0, k:k + n_keep, :]

    off = CONV_HALO - CONV_WIDTH // 2
    acc = jnp.zeros((POST_TILE, CONV_DIM), F32) + cb_ref[...]
    for j in range(CONV_WIDTH):
        base, k = divmod(off + j, 8)
        acc = acc + cu_ref[k, 8 * base:8 * base + POST_TILE, :] * cw_ref[j:j + 1, :]
    mu = jnp.mean(acc, axis=-1, keepdims=True)
    cen = acc - mu
    var = jnp.mean(cen * cen, axis=-1, keepdims=True)
    cv = cen * lax.rsqrt(var + EPS) * lg_ref[...] + lb_ref[...]
    y_b = _dot(_silu(cv).astype(BF16), wpc_ref[...])

    y_a = _dot(o_ref[...], wph_ref[...])
    gz = _sigmoid(_dot(h, wgz_ref[...]))
    merged = gz[:, :D_MODEL] * y_a + gz[:, D_MODEL:] * y_b
    x1 = x + gate1 * _dot(merged.astype(BF16), wo_ref[...])
    x1_out[...] = x1
    h2 = _rms_mod(x1, g2_ref[...], scale2, shift2)
    h2_out[...] = h2.astype(BF16)
    lg_out[...] = lax.dot_general(wr_ref[...], h2, (((1,), (1,)), ((), ())),
                                  preferred_element_type=F32, precision=HIGHEST)


def _post(pv, nv, x_all, mod_tile, o_all, w_in, layer, w_ph, w_pc, w_o, w_rt, g1, cw, cb, lg, lb, g2):
    n_tok = x_all.shape[0]
    n_tiles = n_tok // POST_TILE
    halo_per_tile = POST_TILE // CONV_HALO
    n_halo_blocks = n_tok // CONV_HALO
    glu_cols, gate_cols = 2 * CONV_DIM, 2 * D_MODEL
    glu_start = 5 * N_HEADS * HEAD_DIM
    assert glu_start % glu_cols == 0 and (glu_start + glu_cols) % gate_cols == 0

    def full(a):
        return pl.BlockSpec(a.shape, lambda i, pv, nv: (0,) * a.ndim)

    def resident(shape, index):
        return pl.BlockSpec(shape, lambda i, pv, nv: index, pipeline_mode=pl.Buffered(1))

    grid_spec = pltpu.PrefetchScalarGridSpec(
        num_scalar_prefetch=2,
        grid=(n_tiles,),
        in_specs=[
            pl.BlockSpec((POST_TILE, D_MODEL), lambda i, pv, nv: (i, 0)),
            pl.BlockSpec((CONV_HALO, D_MODEL), lambda i, pv, nv: (jnp.maximum(i * halo_per_tile - 1, 0), 0)),
            pl.BlockSpec((CONV_HALO, D_MODEL),
                         lambda i, pv, nv: (jnp.minimum((i + 1) * halo_per_tile, n_halo_blocks - 1), 0)),
            pl.BlockSpec((1, 8, D_MODEL), lambda i, pv, nv: (i, 0, 0)),
            pl.BlockSpec((POST_TILE, D_MODEL), lambda i, pv, nv: (i, 0)),
            resident((1, D_MODEL, glu_cols), (layer, 0, glu_start // glu_cols)),
            resident((1, D_MODEL, gate_cols), (layer, 0, (glu_start + glu_cols) // gate_cols)),
            resident(w_ph.shape, (0, 0)), resident(w_pc.shape, (0, 0)), resident(w_o.shape, (0, 0)),
            full(w_rt), full(g1), full(cw), full(cb), full(lg), full(lb), full(g2),
        ],
        out_specs=[
            pl.BlockSpec((POST_TILE, D_MODEL), lambda i, pv, nv: (i, 0)),
            pl.BlockSpec((POST_TILE, D_MODEL), lambda i, pv, nv: (i, 0)),
            pl.BlockSpec((N_EXPERTS, POST_TILE), lambda i, pv, nv: (0, i)),
        ],
        scratch_shapes=[
            pltpu.VMEM((8, POST_TILE + 2 * CONV_HALO, CONV_DIM), F32),
            pltpu.VMEM((D_MODEL, glu_cols), BF16), pltpu.VMEM((D_MODEL, gate_cols), BF16),
            pltpu.VMEM(w_ph.shape, BF16), pltpu.VMEM(w_pc.shape, BF16), pltpu.VMEM(w_o.shape, BF16),
        ],
    )
    return pl.pallas_call(
        _post_kernel,
        grid_spec=grid_spec,
        out_shape=[
            jax.ShapeDtypeStruct((n_tok, D_MODEL), F32),
            jax.ShapeDtypeStruct((n_tok, D_MODEL), BF16),
            jax.ShapeDtypeStruct((N_EXPERTS, n_tok), F32),
        ],
        compiler_params=pltpu.CompilerParams(
            dimension_semantics=("arbitrary",), vmem_limit_bytes=VMEM_LIMIT),
        name="post_mixer",
    )(pv, nv, x_all, x_all, x_all, mod_tile, o_all, w_in, w_in, w_ph, w_pc, w_o, w_rt, g1, cw, cb, lg, lb, g2)


def _route_kernel(lg_ref, bias_ref, comb_out, chosen_out):
    n = lg_ref.shape[1]
    scores = _sigmoid(lg_ref[...])
    sel = scores + bias_ref[...]
    neg = jnp.float32(-jnp.inf)

    sel3 = sel.reshape(N_GROUPS, GROUP_SIZE, n)
    m1 = jnp.max(sel3, axis=1, keepdims=True)
    is_m1 = sel3 == m1
    n_m1 = jnp.sum(is_m1.astype(F32), axis=1, keepdims=True)
    m2 = jnp.max(jnp.where(is_m1, neg, sel3), axis=1, keepdims=True)
    grp = (m1 + jnp.where(n_m1 > 1.5, m1, m2)).reshape(N_GROUPS, n)

    gidx = lax.broadcasted_iota(jnp.int32, (N_GROUPS, n), 0)
    rank = jnp.zeros((N_GROUPS, n), F32)
    for g in range(N_GROUPS):
        other = grp[g:g + 1, :]
        ahead = (other > grp) | ((other == grp) & (g < gidx))
        rank = rank + ahead.astype(F32)
    keep_g = rank < TOPK_GROUPS - 0.5
    keep = jnp.broadcast_to(keep_g.reshape(N_GROUPS, 1, n), (N_GROUPS, GROUP_SIZE, n)).reshape(N_EXPERTS, n)
    cand = jnp.where(keep, sel, neg)

    eidx = lax.broadcasted_iota(jnp.int32, (N_EXPERTS, n), 0)
    chosen = jnp.zeros((N_EXPERTS, n), F32)
    for _ in range(TOP_K):
        best = jnp.max(cand, axis=0, keepdims=True)
        first = jnp.min(jnp.where(cand == best, eidx, N_EXPERTS), axis=0, keepdims=True)
        hit = eidx == first
        chosen = jnp.where(hit, 1.0, chosen)
        cand = jnp.where(hit, neg, cand)
    w = scores * chosen
    comb_out[...] = w / jnp.sum(w, axis=0, keepdims=True) * ROUTED_SCALE
    chosen_out[...] = chosen


def _route(logits_t, bias):
    n_tok = logits_t.shape[1]
    tile = 512
    return pl.pallas_call(
        _route_kernel,
        grid=(n_tok // tile,),
        in_specs=[
            pl.BlockSpec((N_EXPERTS, tile), lambda i: (0, i)),
            pl.BlockSpec((N_EXPERTS, 1), lambda i: (0, 0)),
        ],
        out_specs=[pl.BlockSpec((N_EXPERTS, tile), lambda i: (0, i)),
                   pl.BlockSpec((N_EXPERTS, tile), lambda i: (0, i))],
        out_shape=[jax.ShapeDtypeStruct((N_EXPERTS, n_tok), F32),
                   jax.ShapeDtypeStruct((N_EXPERTS, n_tok), F32)],
        name="router",
    )(logits_t, bias)


def _count_le(bounds, idx):
    return jnp.sum(jnp.where(bounds <= idx, 1.0, 0.0), axis=0, keepdims=True)


def _pick_row(onehot, col):
    return jnp.sum(jnp.where(onehot, col, 0.0), axis=0, keepdims=True)


def _plan_kernel(ch_ref, rankt_out, es_out, rs_out, src_out, et_out, ysrc_out, misc_out):
    nb = N_TOKENS // MOE_BLOCK
    n_src = N_TILES_MAX * UNITS_PER_TILE
    e_col = lax.broadcasted_iota(jnp.int32, (N_EXPERTS, 1), 0).astype(F32)
    blk_lane = lax.broadcasted_iota(jnp.int32, (N_EXPERTS, 128), 1)
    t_r = lax.broadcasted_iota(jnp.int32, (MOE_BLOCK, MOE_BLOCK), 0)
    t_c = lax.broadcasted_iota(jnp.int32, (MOE_BLOCK, MOE_BLOCK), 1)
    earlier = jnp.where(t_c < t_r, 1.0, 0.0).astype(BF16)
    e_r = lax.broadcasted_iota(jnp.int32, (N_EXPERTS, N_EXPERTS), 0)
    e_c = lax.broadcasted_iota(jnp.int32, (N_EXPERTS, N_EXPERTS), 1)
    upto = jnp.where(e_c <= e_r, 1.0, 0.0).astype(F32)
    slot = lax.broadcasted_iota(jnp.int32, (1, SLOT_MAX), 1).astype(F32)

    def col_cumsum(col):
        wide = jnp.broadcast_to(col, (N_EXPERTS, 128))
        return jnp.dot(upto, wide, preferred_element_type=F32, precision=HIGHEST)[:, 0:1]

    def block_stats(b, carry):
        units_mat, start_mat = carry
        cb = ch_ref[:, pl.ds(pl.multiple_of(b * MOE_BLOCK, MOE_BLOCK), MOE_BLOCK)]
        rank_t = _dot_nt(earlier, cb.astype(BF16))
        cb_t = cb.T
        rankt_out[b] = jnp.where(cb_t > 0, rank_t, -1.0).astype(BF16)
        cnt = jnp.sum(cb, axis=1, keepdims=True)
        units = jnp.floor((cnt + (UNIT - 1)) * (1.0 / UNIT))
        incl = col_cumsum(units)
        start = incl - units
        e_slot = _count_le(incl * UNIT, slot)
        onehot = e_col == e_slot
        r = slot - _pick_row(onehot, start * UNIT)
        valid = (e_slot < N_EXPERTS - 0.5) & (r < _pick_row(onehot, cnt))
        es_out[b] = jnp.where(valid, e_slot, -1.0).astype(jnp.int32)
        rs_out[b] = jnp.where(valid, r, -2.0)
        units_mat = jnp.where(blk_lane == b, units, units_mat)
        start_mat = jnp.where(blk_lane == b, start, start_mat)
        return units_mat, start_mat

    zeros = jnp.zeros((N_EXPERTS, 128), F32)
    units_mat, start_mat = lax.fori_loop(0, nb, block_stats, (zeros, zeros))

    tot = jnp.sum(units_mat, axis=1, keepdims=True)
    tiles_e = jnp.floor((tot + (UNITS_PER_TILE - 1)) * (1.0 / UNITS_PER_TILE))
    incl_t = col_cumsum(tiles_e)
    start_t = incl_t - tiles_e
    n_used = incl_t[N_EXPERTS - 1:N_EXPERTS, :]
    b_r = lax.broadcasted_iota(jnp.int32, (128, 128), 0)
    b_c = lax.broadcasted_iota(jnp.int32, (128, 128), 1)
    before = jnp.where(b_r < b_c, 1.0, 0.0).astype(F32)
    cum_b = jnp.dot(units_mat, before, preferred_element_type=F32, precision=HIGHEST)
    run_pos = start_t * UNITS_PER_TILE + cum_b

    tile_idx = lax.broadcasted_iota(jnp.int32, (1, 512), 1).astype(F32)
    et_out[...] = jnp.minimum(_count_le(incl_t, tile_idx), N_EXPERTS - 1.0).astype(jnp.int32)

    incl_ub_t = (cum_b + units_mat).T
    cum_b_t = cum_b.T
    start_mat_t = start_mat.T
    b_col = lax.broadcasted_iota(jnp.int32, (128, 1), 0)
    chunk = 1024

    def src_chunk(c, carry):
        off = pl.multiple_of(c * chunk, chunk)
        p = (lax.broadcasted_iota(jnp.int32, (1, chunk), 1) + off).astype(F32)
        e_p = jnp.minimum(_count_le(incl_t * UNITS_PER_TILE, p), N_EXPERTS - 1.0)
        onehot_e = e_col == e_p
        q = p - _pick_row(onehot_e, start_t * UNITS_PER_TILE)
        valid = (p < n_used * UNITS_PER_TILE) & (q < _pick_row(onehot_e, tot))
        sel = jnp.where(onehot_e, 1.0, 0.0).astype(F32)
        incl_sel = jnp.dot(incl_ub_t, sel, preferred_element_type=F32, precision=HIGHEST)
        b_p = jnp.sum(jnp.where((incl_sel <= q) & (b_col < nb), 1.0, 0.0), axis=0, keepdims=True)
        b_p = jnp.minimum(b_p, nb - 1.0)
        onehot_b = b_col.astype(F32) == b_p
        cum_sel = jnp.dot(cum_b_t, sel, preferred_element_type=F32, precision=HIGHEST)
        start_sel = jnp.dot(start_mat_t, sel, preferred_element_type=F32, precision=HIGHEST)
        unit = b_p * UNITS_PER_BLOCK + _pick_row(onehot_b, start_sel) + q - _pick_row(onehot_b, cum_sel)
        src_out[:, pl.ds(off, chunk)] = jnp.where(valid, unit, 0.0).astype(jnp.int32)
        return carry

    lax.fori_loop(0, n_src // chunk, src_chunk, 0)

    unit_idx = lax.broadcasted_iota(jnp.int32, (1, 256), 1).astype(F32)
    misc_lane = lax.broadcasted_iota(jnp.int32, (1, 128), 1)

    def block_units(b, used):
        here = blk_lane == b
        units = jnp.sum(jnp.where(here, units_mat, 0.0), axis=1, keepdims=True)
        start = jnp.sum(jnp.where(here, start_mat, 0.0), axis=1, keepdims=True)
        pos = jnp.sum(jnp.where(here, run_pos, 0.0), axis=1, keepdims=True)
        incl = start + units
        e_unit = _count_le(incl, unit_idx)
        onehot = e_col == e_unit
        val = _pick_row(onehot, pos - start) + unit_idx
        ysrc_out[b] = jnp.where(e_unit < N_EXPERTS - 0.5, val, 0.0).astype(jnp.int32)
        return jnp.where(misc_lane == b, incl[N_EXPERTS - 1:N_EXPERTS, :], used)

    used = lax.fori_loop(0, nb, block_units, jnp.zeros((1, 128), F32))
    misc_out[0:1, :] = used.astype(jnp.int32)
    misc_out[1:2, :] = jnp.broadcast_to(n_used, (1, 128)).astype(jnp.int32)
    misc_out[2:8, :] = jnp.zeros((6, 128), jnp.int32)


def _routing_plan(chosen):
    nb = N_TOKENS // MOE_BLOCK
    n_src = N_TILES_MAX * UNITS_PER_TILE
    assert n_src % 1024 == 0 and N_TILES_MAX <= 512 and nb <= 128 and UNITS_PER_BLOCK <= 256
    rank_bt, e_slot, r_slot, src, e_tile, ysrc, misc = pl.pallas_call(
        _plan_kernel,
        out_shape=[
            jax.ShapeDtypeStruct((nb, MOE_BLOCK, N_EXPERTS), BF16),
            jax.ShapeDtypeStruct((nb, 1, SLOT_MAX), jnp.int32),
            jax.ShapeDtypeStruct((nb, 1, SLOT_MAX), F32),
            jax.ShapeDtypeStruct((1, n_src), jnp.int32),
            jax.ShapeDtypeStruct((1, 512), jnp.int32),
            jax.ShapeDtypeStruct((nb, 1, 256), jnp.int32),
            jax.ShapeDtypeStruct((8, 128), jnp.int32),
        ],
        compiler_params=pltpu.CompilerParams(vmem_limit_bytes=VMEM_LIMIT),
        name="moe_plan",
    )(chosen)
    return dict(rank_bt=rank_bt, e_slot=e_slot, r_slot=r_slot, src=src.reshape(-1),
                e_tile=e_tile[0, :N_TILES_MAX], n_used=misc[1, :1],
                ysrc=ysrc[:, 0, :UNITS_PER_BLOCK].reshape(-1))


def _permute_kernel(h_ref, rank_ref, es_ref, rs_ref, xp_out):
    sub_e = lax.broadcasted_iota(jnp.int32, (N_EXPERTS, FFN_TILE), 0)
    tiles = [slice(i * FFN_TILE, (i + 1) * FFN_TILE) for i in range(SLOT_MAX // FFN_TILE)]
    picks = []
    for rows in tiles:
        onehot_e = jnp.where(sub_e == es_ref[0, :, rows], 1.0, 0.0).astype(BF16)
        slot_rank = _dot(rank_ref[0], onehot_e)
        picks.append(jnp.where(slot_rank == rs_ref[0, :, rows], 1.0, 0.0).astype(BF16))
    for rows, pick in zip(tiles, picks):
        xp_out[0, rows, :] = _dot_tn(pick, h_ref[...]).astype(BF16)


def _permute(plan, h2):
    nb = h2.shape[0] // MOE_BLOCK
    return pl.pallas_call(
        _permute_kernel,
        grid=(nb,),
        in_specs=[
            pl.BlockSpec((MOE_BLOCK, D_MODEL), lambda b: (b, 0)),
            pl.BlockSpec((1, MOE_BLOCK, N_EXPERTS), lambda b: (b, 0, 0)),
            pl.BlockSpec((1, 1, SLOT_MAX), lambda b: (b, 0, 0)),
            pl.BlockSpec((1, 1, SLOT_MAX), lambda b: (b, 0, 0)),
        ],
        out_specs=pl.BlockSpec((1, SLOT_MAX, D_MODEL), lambda b: (b, 0, 0)),
        out_shape=jax.ShapeDtypeStruct((nb, SLOT_MAX, D_MODEL), BF16),
        compiler_params=pltpu.CompilerParams(
            dimension_semantics=("arbitrary",), vmem_limit_bytes=VMEM_LIMIT),
        name="moe_permute",
    )(h2, plan["rank_bt"], plan["e_slot"], plan["r_slot"])


def _unit_copies(table_ref, base, src_hbm, dst_ref, slot, sem, n_units, start):
    for u in range(n_units):
        copy = pltpu.make_async_copy(src_hbm.at[table_ref[base + u]],
                                     dst_ref.at[slot, pl.ds(u * UNIT, UNIT), :], sem.at[slot])
        if start:
            copy.start()
        else:
            copy.wait()


def _ffn_kernel(src_ref, et_ref, nu_ref, xp_hbm, *refs):
    w_refs = refs[:2 * TILES_PER_STEP]
    y_out = refs[2 * TILES_PER_STEP]
    scratch = refs[2 * TILES_PER_STEP + 1:]
    wb_refs, xt_ref, sem = scratch[:2 * TILES_PER_STEP], scratch[2 * TILES_PER_STEP], scratch[2 * TILES_PER_STEP + 1]
    n_units = TILES_PER_STEP * UNITS_PER_TILE
    step = pl.program_id(0)
    slot = lax.rem(step, 2)
    n_used = nu_ref[0]

    def copies(s, sl, start):
        _unit_copies(src_ref, s * n_units, xp_hbm, xt_ref, sl, sem, n_units, start)

    @pl.when((step == 0) & (n_used > 0))
    def _():
        copies(0, 0, True)

    @pl.when((step + 1 < pl.num_programs(0)) & ((step + 1) * TILES_PER_STEP < n_used))
    def _():
        copies(step + 1, 1 - slot, True)

    @pl.when(step * TILES_PER_STEP < n_used)
    def _():
        for t in range(TILES_PER_STEP):
            j = step * TILES_PER_STEP + t

            @pl.when((step == 0) | (et_ref[j] != et_ref[jnp.maximum(j - TILES_PER_STEP, 0)]))
            def _():
                wb_refs[2 * t][...] = w_refs[2 * t][0, 0].astype(BF16)
                wb_refs[2 * t + 1][...] = w_refs[2 * t + 1][0, 0].astype(BF16)

        copies(step, slot, False)
        tiles = [slice(t * FFN_TILE, (t + 1) * FFN_TILE) for t in range(TILES_PER_STEP)]
        gus = [_dot(xt_ref[slot, rows, :], wb_refs[2 * t][...]) for t, rows in enumerate(tiles)]
        acts = [(_silu(gu[:, :EXPERT_FF]) * gu[:, EXPERT_FF:]).astype(BF16) for gu in gus]
        for t, (rows, act) in enumerate(zip(tiles, acts)):
            y_out[rows, :] = _dot(act, wb_refs[2 * t + 1][...]).astype(BF16)

    @pl.when(step * TILES_PER_STEP >= n_used)
    def _():
        y_out[...] = jnp.zeros(y_out.shape, BF16)


def _expert_ffn(plan, xp, w_gu, w_d, layer):
    xp_units = xp.reshape(-1, UNIT, D_MODEL)
    assert N_TILES_MAX % TILES_PER_STEP == 0

    def weight_specs(t):
        return [pl.BlockSpec((1, 1, D_MODEL, 2 * EXPERT_FF),
                             lambda s, src, et, nu, t=t: (layer, et[s * TILES_PER_STEP + t], 0, 0)),
                pl.BlockSpec((1, 1, EXPERT_FF, D_MODEL),
                             lambda s, src, et, nu, t=t: (layer, et[s * TILES_PER_STEP + t], 0, 0))]

    grid_spec = pltpu.PrefetchScalarGridSpec(
        num_scalar_prefetch=3,
        grid=(N_TILES_MAX // TILES_PER_STEP,),
        in_specs=[pl.BlockSpec(memory_space=pl.ANY)]
        + [spec for t in range(TILES_PER_STEP) for spec in weight_specs(t)],
        out_specs=pl.BlockSpec((TILES_PER_STEP * FFN_TILE, D_MODEL), lambda s, src, et, nu: (s, 0)),
        scratch_shapes=[
            pltpu.VMEM((D_MODEL, 2 * EXPERT_FF), BF16) if i % 2 == 0 else pltpu.VMEM((EXPERT_FF, D_MODEL), BF16)
            for i in range(2 * TILES_PER_STEP)
        ] + [pltpu.VMEM((2, TILES_PER_STEP * FFN_TILE, D_MODEL), BF16), pltpu.SemaphoreType.DMA((2,))],
    )
    return pl.pallas_call(
        _ffn_kernel,
        grid_spec=grid_spec,
        out_shape=jax.ShapeDtypeStruct((N_TILES_MAX * FFN_TILE, D_MODEL), BF16),
        compiler_params=pltpu.CompilerParams(
            dimension_semantics=("arbitrary",), vmem_limit_bytes=VMEM_LIMIT),
        name="moe_expert_ffn",
    )(plan["src"], plan["e_tile"], plan["n_used"], xp_units, *([w_gu, w_d] * TILES_PER_STEP))


def _combine_kernel(ysrc_ref, y_hbm, rank_ref, comb_ref, er_ref, rr_ref, h_ref, x1_ref, mod_ref,
                    wsgu_ref, wsd_ref, fg_ref, o_ref, yt_ref, sem, *, final):
    b = pl.program_id(0)
    slot = lax.rem(b, 2)

    def copies(bb, sl, start):
        _unit_copies(ysrc_ref, bb * UNITS_PER_BLOCK, y_hbm, yt_ref, sl, sem, UNITS_PER_BLOCK, start)

    @pl.when(b == 0)
    def _():
        copies(0, 0, True)

    @pl.when(b + 1 < pl.num_programs(0))
    def _():
        copies(b + 1, 1 - slot, True)

    gu = _dot(h_ref[...], wsgu_ref[...])
    shared = _dot((_silu(gu[:, :EXPERT_FF]) * gu[:, EXPERT_FF:]).astype(BF16), wsd_ref[...])

    sub_e = lax.broadcasted_iota(jnp.int32, (N_EXPERTS, GROUP_SLOTS), 0)
    groups = [slice(g * GROUP_SLOTS, (g + 1) * GROUP_SLOTS) for g in range(GROUPS_PER_BLOCK)]
    onehots = [jnp.where(sub_e == er_ref[0, :, cols], 1.0, 0.0).astype(BF16) for cols in groups]
    slot_ranks = [_dot(rank_ref[0], onehot_e) for onehot_e in onehots]
    slot_ws = [_dot(comb_ref[0], onehot_e) for onehot_e in onehots]
    weights = jnp.concatenate(
        [jnp.where(slot_rank == rr_ref[0, :, cols], slot_w, 0.0).astype(BF16)
         for cols, slot_rank, slot_w in zip(groups, slot_ranks, slot_ws)], axis=1)

    copies(b, slot, False)
    x2 = x1_ref[...] + mod_ref[0, 5:6, :] * (shared + _dot(weights, yt_ref[slot]))
    if final:
        ms = jnp.mean(x2 * x2, axis=-1, keepdims=True)
        x2 = x2 * lax.rsqrt(ms + EPS) * fg_ref[...]
    o_ref[...] = x2


def _combine(plan, y_sorted, comb_bt, h2, x1, mod_tile, ws_gu, ws_d, final_g, final):
    n_tok = h2.shape[0]
    nb = n_tok // MOE_BLOCK
    y_units = y_sorted.reshape(-1, UNIT, D_MODEL)

    def full(a):
        return pl.BlockSpec(a.shape, lambda b, ysrc: (0,) * a.ndim)

    grid_spec = pltpu.PrefetchScalarGridSpec(
        num_scalar_prefetch=1,
        grid=(nb,),
        in_specs=[
            pl.BlockSpec(memory_space=pl.ANY),
            pl.BlockSpec((1, MOE_BLOCK, N_EXPERTS), lambda b, ysrc: (b, 0, 0)),
            pl.BlockSpec((1, MOE_BLOCK, N_EXPERTS), lambda b, ysrc: (b, 0, 0)),
            pl.BlockSpec((1, 1, SLOT_MAX), lambda b, ysrc: (b, 0, 0)),
            pl.BlockSpec((1, 1, SLOT_MAX), lambda b, ysrc: (b, 0, 0)),
            pl.BlockSpec((MOE_BLOCK, D_MODEL), lambda b, ysrc: (b, 0)),
            pl.BlockSpec((MOE_BLOCK, D_MODEL), lambda b, ysrc: (b, 0)),
            pl.BlockSpec((1, 8, D_MODEL), lambda b, ysrc: (b, 0, 0)),
            full(ws_gu), full(ws_d), full(final_g),
        ],
        out_specs=pl.BlockSpec((MOE_BLOCK, D_MODEL), lambda b, ysrc: (b, 0)),
        scratch_shapes=[
            pltpu.VMEM((2, SLOT_MAX, D_MODEL), BF16),
            pltpu.SemaphoreType.DMA((2,)),
        ],
    )
    return pl.pallas_call(
        functools.partial(_combine_kernel, final=final),
        grid_spec=grid_spec,
        out_shape=jax.ShapeDtypeStruct((n_tok, D_MODEL), F32),
        compiler_params=pltpu.CompilerParams(
            dimension_semantics=("arbitrary",), vmem_limit_bytes=VMEM_LIMIT),
        name="moe_combine",
    )(plan["ysrc"], y_units, plan["rank_bt"], comb_bt,
      plan["e_slot"], plan["r_slot"], h2, x1, mod_tile, ws_gu, ws_d, final_g)


def _level_ids():
    t = np.arange(CHUNK)
    x = t[:, None] ^ t[None, :]
    lvl = np.zeros((CHUNK, CHUNK), np.int32)
    nz = x > 0
    lvl[nz] = np.floor(np.log2(x[nz])).astype(np.int32) + 1
    past_f = t[:, None] >= t[None, :]
    return jnp.asarray(np.where(past_f, lvl, -1)), jnp.asarray(np.where(past_f.T, lvl, -1))


def kernel(x_prompt, x_sample, state_hgrn, c, c_ctx, w_ada, b_ada, norm1_g, w_in, hgrn_lb_logits, hgrn_norm_g, w_proj_hgrn, conv_dw_w, conv_dw_b, conv_norm_g, conv_norm_b, w_proj_conv, w_out, norm2_g, w_router, router_bias, w_expert_gate_up, w_expert_down, w_shared_gate_up, w_shared_down, final_norm_g):
    n_ctx, ctx_len, d = x_prompt.shape
    n_lat, lat_len, _ = x_sample.shape
    depth = w_ada.shape[0]
    assert d == D_MODEL and ctx_len * 4 == STEP and lat_len == STEP and n_ctx % 4 == 0
    ctx_steps = n_ctx * ctx_len // STEP
    n_steps = ctx_steps + n_lat
    n_tok = n_steps * STEP
    assert n_tok == N_TOKENS and MOE_BLOCK == POST_TILE
    tiles_per_step = STEP // POST_TILE

    x_all = jnp.concatenate([x_prompt.reshape(-1, d), x_sample.reshape(-1, d)], axis=0)

    cc = jnp.zeros((8, d), F32).at[:n_lat].set(c.astype(F32)).at[n_lat].set(c_ctx.astype(F32))
    mod = _ada(cc, w_ada, b_ada).reshape(depth, 8, N_MOD, d)
    step_src = np.array([n_lat] * ctx_steps + list(range(n_lat)))
    mod_step = jnp.pad(mod[:, step_src], ((0, 0), (0, 0), (0, 8 - N_MOD), (0, 0)))
    mod_tile = jnp.repeat(mod_step, tiles_per_step, axis=1)

    nc = jnp.asarray([ctx_len // CHUNK] * ctx_steps + [lat_len // CHUNK] * n_lat, jnp.int32)
    tile_pos = np.arange(n_lat * tiles_per_step) % tiles_per_step
    pv = jnp.asarray(np.concatenate([np.zeros(ctx_steps * tiles_per_step), tile_pos > 0]), jnp.int32)
    nv = jnp.asarray(np.concatenate([np.zeros(ctx_steps * tiles_per_step), tile_pos < tiles_per_step - 1]), jnp.int32)

    lbv = jnp.cumsum(jax.nn.softmax(hgrn_lb_logits.astype(F32), axis=0), axis=0)
    lbv = (lbv - lbv[:1]).reshape(depth, 2, N_HEADS, HEAD_DIM)
    lvl_f, lvl_b = _level_ids()

    xs = x_all
    ctx_states = []
    for l in range(depth):
        lb = lbv[l]
        gp = jnp.stack([jnp.log(lb[0]), jnp.log1p(-lb[0]), 1.0 - lb[0],
                        jnp.log(lb[1]), jnp.log1p(-lb[1]), 1.0 - lb[1],
                        hgrn_norm_g[l].reshape(N_HEADS, HEAD_DIM).astype(F32),
                        jnp.zeros((N_HEADS, HEAD_DIM), F32)], axis=1)
        s0 = jnp.concatenate([jnp.zeros((ctx_steps, 2, N_HEADS, HEAD_DIM, HEAD_DIM), F32),
                              state_hgrn[:, l].astype(F32)], axis=0)
        g1 = norm1_g[l].reshape(1, d).astype(F32)

        o_all, states = _scan(nc, xs, mod_step[l], g1, w_in, l, gp, s0, lvl_f, lvl_b)
        ctx_states.append(states[:ctx_steps].reshape(n_ctx, 2, N_HEADS, HEAD_DIM, HEAD_DIM))

        cw = jnp.pad(conv_dw_w[l].astype(F32), ((0, 1), (0, 0)))
        x1, h2, logits_t = _post(
            pv, nv, xs, mod_tile[l], o_all, w_in, l, w_proj_hgrn[l], w_proj_conv[l], w_out[l],
            w_router[l].T.astype(F32), g1, cw,
            conv_dw_b[l].reshape(1, -1).astype(F32), conv_norm_g[l].reshape(1, -1).astype(F32),
            conv_norm_b[l].reshape(1, -1).astype(F32), norm2_g[l].reshape(1, d).astype(F32))

        comb_t, chosen_t = _route(logits_t, router_bias[l].reshape(N_EXPERTS, 1).astype(F32))
        plan = _routing_plan(chosen_t)
        comb_bt = comb_t.reshape(N_EXPERTS, n_tok // MOE_BLOCK, MOE_BLOCK).transpose(1, 2, 0).astype(BF16)
        xp = _permute(plan, h2)
        y_sorted = _expert_ffn(plan, xp, w_expert_gate_up, w_expert_down, l)
        xs = _combine(plan, y_sorted, comb_bt, h2, x1, mod_tile[l], w_shared_gate_up[l].astype(BF16),
                      w_shared_down[l].astype(BF16), final_norm_g.reshape(1, d).astype(F32),
                      final=(l == depth - 1))

    n_ctx_tok = n_ctx * ctx_len
    y_prompt = xs[:n_ctx_tok].reshape(x_prompt.shape).astype(x_prompt.dtype)
    y_sample = xs[n_ctx_tok:].reshape(x_sample.shape).astype(x_sample.dtype)
    new_state = jnp.stack(ctx_states, axis=1).astype(x_prompt.dtype)
    return (y_prompt, y_sample, new_state)
```

```python
import functools

import numpy as np
import jax
import jax.numpy as jnp
from jax import lax
from jax.experimental import pallas as pl
from jax.experimental.pallas import tpu as pltpu

F32 = jnp.float32
BF16 = jnp.bfloat16
HIGHEST = lax.Precision.HIGHEST

D_MODEL = 1024
N_HEADS = 8
HEAD_DIM = 128
CONV_DIM = 512
CONV_WIDTH = 31
CONV_HALO = 16
N_EXPERTS = 64
N_GROUPS = 8
GROUP_SIZE = N_EXPERTS // N_GROUPS
TOPK_GROUPS = 4
TOP_K = 8
EXPERT_FF = 256
ROUTED_SCALE = 2.5
N_MOD = 6
EPS = 1e-6

CHUNK = 128
N_LEVELS = 7
LOCKSTEP = 2
STEP = 1024
POST_TILE = 256
MOE_BLOCK = 256
UNIT = 16
FFN_TILE = 512
UNITS_PER_TILE = FFN_TILE // UNIT
TILES_PER_STEP = 1
SLOT_MAX = MOE_BLOCK * TOP_K + N_EXPERTS * UNIT
UNITS_PER_BLOCK = SLOT_MAX // UNIT
GROUP_UNITS = 32
GROUP_SLOTS = GROUP_UNITS * UNIT
GROUPS_PER_BLOCK = UNITS_PER_BLOCK // GROUP_UNITS
N_TOKENS = 8192
N_TILES_MAX = (N_TOKENS * TOP_K // UNIT + (N_TOKENS // MOE_BLOCK) * N_EXPERTS) // UNITS_PER_TILE + N_EXPERTS
HEAD_COLS = 5 * HEAD_DIM
VMEM_LIMIT = 52 * 1024 * 1024


def _dot(a, b):
    return jnp.dot(a, b, preferred_element_type=F32)


def _dot_nt(a, b):
    return lax.dot_general(a, b, (((1,), (1,)), ((), ())), preferred_element_type=F32)


def _dot_tn(a, b):
    return lax.dot_general(a, b, (((0,), (0,)), ((), ())), preferred_element_type=F32)


def _sigmoid(x):
    return 1.0 / (1.0 + jnp.exp(-x))


def _silu(x):
    return x * _sigmoid(x)


def _rms_mod(x, g, scale, shift):
    ms = jnp.mean(x * x, axis=-1, keepdims=True)
    return x * lax.rsqrt(ms + EPS) * g * (1.0 + scale) + shift


def _ada_kernel(c_ref, w_ref, b_ref, o_ref):
    cc = c_ref[...]
    o_ref[0] = jnp.dot(_silu(cc), w_ref[0], preferred_element_type=F32, precision=HIGHEST) + b_ref[0]


def _ada(cc, w_ada, b_ada):
    depth, d, n = w_ada.shape
    tn = 1536
    return pl.pallas_call(
        _ada_kernel,
        grid=(depth, n // tn),
        in_specs=[
            pl.BlockSpec((8, d), lambda l, j: (0, 0)),
            pl.BlockSpec((1, d, tn), lambda l, j: (l, 0, j)),
            pl.BlockSpec((1, 1, tn), lambda l, j: (l, 0, j)),
        ],
        out_specs=pl.BlockSpec((1, 8, tn), lambda l, j: (l, 0, j)),
        out_shape=jax.ShapeDtypeStruct((depth, 8, n), F32),
        compiler_params=pltpu.CompilerParams(vmem_limit_bytes=VMEM_LIMIT),
        name="ada_mod",
    )(cc, w_ada, b_ada.reshape(depth, 1, n))


def _forget_gate(z, log_lb, log1m_lb, one_m_lb):
    e = jnp.exp(-jnp.abs(z))
    r = 1.0 / (1.0 + e)
    log_sig = jnp.minimum(z, 0.0) - jnp.log(1.0 + e)
    k = one_m_lb * jnp.where(z > 0, e * r, r)
    b = log1m_lb + log_sig
    log_f = jnp.maximum(log_lb, b) + jnp.log(1.0 + jnp.exp(-jnp.abs(log_lb - b)))
    return log_f, k


def _level_reference(cum_ref, blk, fwd):
    half = blk // 2
    pieces = []
    if blk >= 8:
        for i in range(CHUNK // blk):
            row = i * blk + (half - 1 if fwd else half)
            pieces.append(jnp.broadcast_to(cum_ref[row:row + 1, :], (blk, HEAD_DIM)))
    else:
        sub = lax.broadcasted_iota(jnp.int32, (8, HEAD_DIM), 0)
        for i in range(CHUNK // 8):
            lo_row = 8 * i + (1 if fwd else 2)
            hi_row = 8 * i + (5 if fwd else 6)
            lo = jnp.broadcast_to(cum_ref[lo_row:lo_row + 1, :], (8, HEAD_DIM))
            hi = jnp.broadcast_to(cum_ref[hi_row:hi_row + 1, :], (8, HEAD_DIM))
            pieces.append(jnp.where(sub >= 4, hi, lo))
    return pieces[0] if len(pieces) == 1 else jnp.concatenate(pieces, axis=0)


def _chunk_steps(chains):
    r_idx = lax.broadcasted_iota(jnp.int32, (CHUNK, CHUNK), 0)
    c_idx = lax.broadcasted_iota(jnp.int32, (CHUNK, CHUNK), 1)
    work = []
    for q, k, v, log_f, lvl, cum_ref, fwd in chains:
        tri = jnp.where((r_idx >= c_idx) if fwd else (r_idx <= c_idx), 1.0, 0.0).astype(BF16)
        hi = log_f.astype(BF16)
        lo = (log_f - hi.astype(F32)).astype(BF16)
        both = _dot(tri, jnp.concatenate([hi, lo], axis=1))
        cum = both[:, :HEAD_DIM] + both[:, HEAD_DIM:]
        cum_ref[...] = cum
        q_b = q.astype(BF16)
        k_b = k.astype(BF16)
        work.append(dict(cum=cum, q_b=q_b, k_b=k_b, scores=jnp.where(lvl == 0, _dot_nt(q_b, k_b), 0.0)))

    for lev in range(1, N_LEVELS + 1):
        for (q, k, v, log_f, lvl, cum_ref, fwd), w in zip(chains, work):
            if lev == 1:
                qe = w["q_b"] * jnp.exp(log_f).astype(BF16)
                ke = w["k_b"]
            else:
                e = jnp.exp(-jnp.abs(w["cum"] - _level_reference(cum_ref, 1 << lev, fwd))).astype(BF16)
                qe = w["q_b"] * e
                ke = w["k_b"] * e
            w["scores"] = jnp.where(lvl == lev, _dot_nt(qe, ke), w["scores"])

    def finish(chain, w, st):
        q, k, v, log_f, lvl, cum_ref, fwd = chain
        total = cum_ref[CHUNK - 1:CHUNK, :] if fwd else cum_ref[0:1, :]
        o = (_dot(w["scores"].astype(BF16), v.astype(BF16))
             + _dot_nt((q * jnp.exp(w["cum"])).astype(BF16), st.astype(BF16)))
        k_st = k * jnp.exp(total - w["cum"])
        return o, st * jnp.exp(total) + _dot_tn(v.astype(BF16), k_st.astype(BF16))

    return [functools.partial(finish, chain, w) for chain, w in zip(chains, work)]


def _scan_kernel(nc_ref, x_ref, mod_ref, g1_ref, wq_ref, wv_ref, wf_ref, wb_ref, wg_ref, gp_ref, s0_ref,
                 lvlf_ref, lvlb_ref, o_out, st_out, h_ref, w_ref, z_ref, of_ref, ob_ref, cum_ref):
    head = pl.program_id(0)
    step = pl.program_id(1)
    n_seq_chunks = nc_ref[step]
    n_chunks = STEP // CHUNK

    @pl.when(head == 0)
    def _():
        g1 = g1_ref[...]
        shift = mod_ref[0, 0:1, :]
        scale = mod_ref[0, 1:2, :]

        for i in range(n_chunks):
            rows = pl.ds(i * CHUNK, CHUNK)
            h_ref[step, rows, :] = _rms_mod(x_ref[rows, :], g1, scale, shift).astype(BF16)

    @pl.when(step == 0)
    def _():
        for j, wj_ref in enumerate((wq_ref, wv_ref, wf_ref, wb_ref, wg_ref)):
            w_ref[:, j * HEAD_DIM:(j + 1) * HEAD_DIM] = wj_ref[0].astype(BF16)

    z_ref[...] = _dot(h_ref[step], w_ref[...])
    st_out[...] = jnp.zeros(st_out.shape, F32)
    st0 = (s0_ref[0, 0, 0].T, s0_ref[0, 1, 0].T)

    def chain(c, fwd, cum_ref):
        d = 0 if fwd else 1
        zcol = (2 if fwd else 3) * HEAD_DIM
        rows = pl.ds(c * CHUNK, CHUNK)
        log_f, k = _forget_gate(z_ref[rows, zcol:zcol + HEAD_DIM], gp_ref[0, 3 * d:3 * d + 1, :],
                                gp_ref[0, 3 * d + 1:3 * d + 2, :], gp_ref[0, 3 * d + 2:3 * d + 3, :])
        return (z_ref[rows, 0:HEAD_DIM], k, z_ref[rows, HEAD_DIM:2 * HEAD_DIM], log_f,
                (lvlf_ref if fwd else lvlb_ref)[...], cum_ref, fwd)

    carry = list(st0)
    for i0 in range(0, n_chunks, LOCKSTEP):
        todo = [(i if fwd else n_chunks - 1 - i, fwd) for i in range(i0, i0 + LOCKSTEP) for fwd in (True, False)]
        finishers = _chunk_steps([chain(c, fwd, cum_ref.at[n]) for n, (c, fwd) in enumerate(todo)])
        for (c, fwd), finish_chain in zip(todo, finishers):
            d = 0 if fwd else 1
            pos = lax.rem(c, n_seq_chunks)
            first = (pos == 0) if fwd else (pos == n_seq_chunks - 1)
            o, carry[d] = finish_chain(jnp.where(first, st0[d], carry[d]))
            (of_ref if fwd else ob_ref)[pl.ds(c * CHUNK, CHUNK), :] = o
            st_out[0, lax.div(c, n_seq_chunks), d, 0] = carry[d].T

    norm_g = gp_ref[0, 6:7, :]

    chunk_rows = [pl.ds(i * CHUNK, CHUNK) for i in range(n_chunks)]
    os_ = [of_ref[rows, :] + ob_ref[rows, :] for rows in chunk_rows]
    invs = [lax.rsqrt(jnp.mean(o * o, axis=-1, keepdims=True) + EPS) for o in os_]
    for rows, o, inv in zip(chunk_rows, os_, invs):
        og = z_ref[rows, 4 * HEAD_DIM:5 * HEAD_DIM]
        o_out[rows, :] = (o * inv * norm_g * _silu(og)).astype(BF16)


def _scan(nc, x_all, mod_step, g1, w_in, layer, gp, s0, lvl_f, lvl_b):
    n_tok = x_all.shape[0]
    n_steps = n_tok // STEP
    grid_spec = pltpu.PrefetchScalarGridSpec(
        num_scalar_prefetch=1,
        grid=(N_HEADS, n_steps),
        in_specs=[
            pl.BlockSpec((STEP, D_MODEL), lambda h, s, nc: (jnp.where(h == 0, s, 0), 0)),
            pl.BlockSpec((1, 8, D_MODEL), lambda h, s, nc: (s, 0, 0)),
            pl.BlockSpec((1, D_MODEL), lambda h, s, nc: (0, 0)),
        ] + [
            pl.BlockSpec((1, D_MODEL, HEAD_DIM), lambda h, s, nc, j=j: (layer, 0, j * N_HEADS + h))
            for j in range(5)
        ] + [
            pl.BlockSpec((1, 8, HEAD_DIM), lambda h, s, nc: (h, 0, 0)),
            pl.BlockSpec((1, 2, 1, HEAD_DIM, HEAD_DIM), lambda h, s, nc: (s, 0, h, 0, 0)),
            pl.BlockSpec((CHUNK, CHUNK), lambda h, s, nc: (0, 0)),
            pl.BlockSpec((CHUNK, CHUNK), lambda h, s, nc: (0, 0)),
        ],
        out_specs=[
            pl.BlockSpec((STEP, HEAD_DIM), lambda h, s, nc: (s, h)),
            pl.BlockSpec((1, 4, 2, 1, HEAD_DIM, HEAD_DIM), lambda h, s, nc: (s, 0, 0, h, 0, 0)),
        ],
        scratch_shapes=[
            pltpu.VMEM((n_steps, STEP, D_MODEL), BF16),
            pltpu.VMEM((D_MODEL, HEAD_COLS), BF16),
            pltpu.VMEM((STEP, HEAD_COLS), F32),
            pltpu.VMEM((STEP, HEAD_DIM), F32),
            pltpu.VMEM((STEP, HEAD_DIM), F32),
            pltpu.VMEM((2 * LOCKSTEP, CHUNK, HEAD_DIM), F32),
        ],
    )
    return pl.pallas_call(
        _scan_kernel,
        grid_spec=grid_spec,
        out_shape=[
            jax.ShapeDtypeStruct((n_tok, D_MODEL), BF16),
            jax.ShapeDtypeStruct((n_steps, 4, 2, N_HEADS, HEAD_DIM, HEAD_DIM), F32),
        ],
        compiler_params=pltpu.CompilerParams(
            dimension_semantics=("arbitrary", "arbitrary"), vmem_limit_bytes=VMEM_LIMIT),
        name="hgrn_scan",
    )(nc, x_all, mod_step, g1, *([w_in] * 5), gp, s0, lvl_f, lvl_b)


def _post_kernel(pv_ref, nv_ref, x_ref, xp_ref, xn_ref, mod_ref, o_ref, wu_f32, wgz_f32, wph_f32,
                 wpc_f32, wo_f32, wr_ref, g1_ref, cw_ref, cb_ref, lg_ref, lb_ref, g2_ref,
                 x1_out, h2_out, lg_out, cu_ref, wu_ref, wgz_ref, wph_ref, wpc_ref, wo_ref):
    i = pl.program_id(0)

    @pl.when(i == 0)
    def _():
        for dst, src in ((wu_ref, wu_f32), (wgz_ref, wgz_f32), (wph_ref, wph_f32),
                         (wpc_ref, wpc_f32), (wo_ref, wo_f32)):
            dst[...] = (src[0] if len(src.shape) == 3 else src[...]).astype(BF16)

    g1 = g1_ref[...]
    shift1 = mod_ref[0, 0:1, :]
    scale1 = mod_ref[0, 1:2, :]
    gate1 = mod_ref[0, 2:3, :]
    shift2 = mod_ref[0, 3:4, :]
    scale2 = mod_ref[0, 4:5, :]

    def glu(xv):
        h = _rms_mod(xv, g1, scale1, shift1).astype(BF16)
        u = _dot(h, wu_ref[...])
        return h, u[:, :CONV_DIM] * _sigmoid(u[:, CONV_DIM:])

    x = x_ref[...]
    h, glu_mid = glu(x)
    _, glu_prev = glu(xp_ref[...])
    _, glu_next = glu(xn_ref[...])
    cu_ref[0, 0:CONV_HALO, :] = glu_prev * pv_ref[i].astype(F32)
    cu_ref[0, CONV_HALO:CONV_HALO + POST_TILE, :] = glu_mid
    cu_ref[0, CONV_HALO + POST_TILE:, :] = glu_next * nv_ref[i].astype(F32)
    n_keep = POST_TILE + 2 * CONV_HALO - 8
    for k in range(1, 8):
        cu_ref[k, 0:n_keep, :] = cu_ref[0, k:k + n_keep, :]

    off = CONV_HALO - CONV_WIDTH // 2
    acc = jnp.zeros((POST_TILE, CONV_DIM), F32) + cb_ref[...]
    for j in range(CONV_WIDTH):
        base, k = divmod(off + j, 8)
        acc = acc + cu_ref[k, 8 * base:8 * base + POST_TILE, :] * cw_ref[j:j + 1, :]
    mu = jnp.mean(acc, axis=-1, keepdims=True)
    cen = acc - mu
    var = jnp.mean(cen * cen, axis=-1, keepdims=True)
    cv = cen * lax.rsqrt(var + EPS) * lg_ref[...] + lb_ref[...]
    y_b = _dot(_silu(cv).astype(BF16), wpc_ref[...])

    y_a = _dot(o_ref[...], wph_ref[...])
    gz = _sigmoid(_dot(h, wgz_ref[...]))
    merged = gz[:, :D_MODEL] * y_a + gz[:, D_MODEL:] * y_b
    x1 = x + gate1 * _dot(merged.astype(BF16), wo_ref[...])
    x1_out[...] = x1
    h2 = _rms_mod(x1, g2_ref[...], scale2, shift2)
    h2_out[...] = h2.astype(BF16)
    lg_out[...] = lax.dot_general(wr_ref[...], h2, (((1,), (1,)), ((), ())),
                                  preferred_element_type=F32, precision=HIGHEST)


def _post(pv, nv, x_all, mod_tile, o_all, w_in, layer, w_ph, w_pc, w_o, w_rt, g1, cw, cb, lg, lb, g2):
    n_tok = x_all.shape[0]
    n_tiles = n_tok // POST_TILE
    halo_per_tile = POST_TILE // CONV_HALO
    n_halo_blocks = n_tok // CONV_HALO
    glu_cols, gate_cols = 2 * CONV_DIM, 2 * D_MODEL
    glu_start = 5 * N_HEADS * HEAD_DIM
    assert glu_start % glu_cols == 0 and (glu_start + glu_cols) % gate_cols == 0

    def full(a):
        return pl.BlockSpec(a.shape, lambda i, pv, nv: (0,) * a.ndim)

    def resident(shape, index):
        return pl.BlockSpec(shape, lambda i, pv, nv: index, pipeline_mode=pl.Buffered(1))

    grid_spec = pltpu.PrefetchScalarGridSpec(
        num_scalar_prefetch=2,
        grid=(n_tiles,),
        in_specs=[
            pl.BlockSpec((POST_TILE, D_MODEL), lambda i, pv, nv: (i, 0)),
            pl.BlockSpec((CONV_HALO, D_MODEL), lambda i, pv, nv: (jnp.maximum(i * halo_per_tile - 1, 0), 0)),
            pl.BlockSpec((CONV_HALO, D_MODEL),
                         lambda i, pv, nv: (jnp.minimum((i + 1) * halo_per_tile, n_halo_blocks - 1), 0)),
            pl.BlockSpec((1, 8, D_MODEL), lambda i, pv, nv: (i * POST_TILE // STEP, 0, 0)),
            pl.BlockSpec((POST_TILE, D_MODEL), lambda i, pv, nv: (i, 0)),
            resident((1, D_MODEL, glu_cols), (layer, 0, glu_start // glu_cols)),
            resident((1, D_MODEL, gate_cols), (layer, 0, (glu_start + glu_cols) // gate_cols)),
            resident(w_ph.shape, (0, 0)), resident(w_pc.shape, (0, 0)), resident(w_o.shape, (0, 0)),
            full(w_rt), full(g1), full(cw), full(cb), full(lg), full(lb), full(g2),
        ],
        out_specs=[
            pl.BlockSpec((POST_TILE, D_MODEL), lambda i, pv, nv: (i, 0)),
            pl.BlockSpec((POST_TILE, D_MODEL), lambda i, pv, nv: (i, 0)),
            pl.BlockSpec((N_EXPERTS, POST_TILE), lambda i, pv, nv: (0, i)),
        ],
        scratch_shapes=[
            pltpu.VMEM((8, POST_TILE + 2 * CONV_HALO, CONV_DIM), F32),
            pltpu.VMEM((D_MODEL, glu_cols), BF16), pltpu.VMEM((D_MODEL, gate_cols), BF16),
            pltpu.VMEM(w_ph.shape, BF16), pltpu.VMEM(w_pc.shape, BF16), pltpu.VMEM(w_o.shape, BF16),
        ],
    )
    return pl.pallas_call(
        _post_kernel,
        grid_spec=grid_spec,
        out_shape=[
            jax.ShapeDtypeStruct((n_tok, D_MODEL), F32),
            jax.ShapeDtypeStruct((n_tok, D_MODEL), BF16),
            jax.ShapeDtypeStruct((N_EXPERTS, n_tok), F32),
        ],
        compiler_params=pltpu.CompilerParams(
            dimension_semantics=("arbitrary",), vmem_limit_bytes=VMEM_LIMIT),
        name="post_mixer",
    )(pv, nv, x_all, x_all, x_all, mod_tile, o_all, w_in, w_in, w_ph, w_pc, w_o, w_rt, g1, cw, cb, lg, lb, g2)


def _route_kernel(lg_ref, bias_ref, comb_out, chosen_out):
    n = lg_ref.shape[1]
    scores = _sigmoid(lg_ref[...])
    sel = scores + bias_ref[...]
    neg = jnp.float32(-jnp.inf)

    sel3 = sel.reshape(N_GROUPS, GROUP_SIZE, n)
    m1 = jnp.max(sel3, axis=1, keepdims=True)
    is_m1 = sel3 == m1
    n_m1 = jnp.sum(is_m1.astype(F32), axis=1, keepdims=True)
    m2 = jnp.max(jnp.where(is_m1, neg, sel3), axis=1, keepdims=True)
    grp = (m1 + jnp.where(n_m1 > 1.5, m1, m2)).reshape(N_GROUPS, n)

    gidx = lax.broadcasted_iota(jnp.int32, (N_GROUPS, n), 0)
    rank = jnp.zeros((N_GROUPS, n), F32)
    for g in range(N_GROUPS):
        other = grp[g:g + 1, :]
        ahead = (other > grp) | ((other == grp) & (g < gidx))
        rank = rank + ahead.astype(F32)
    keep_g = rank < TOPK_GROUPS - 0.5
    keep = jnp.broadcast_to(keep_g.reshape(N_GROUPS, 1, n), (N_GROUPS, GROUP_SIZE, n)).reshape(N_EXPERTS, n)
    cand = jnp.where(keep, sel, neg)

    eidx = lax.broadcasted_iota(jnp.int32, (N_EXPERTS, n), 0)
    chosen = jnp.zeros((N_EXPERTS, n), F32)
    for _ in range(TOP_K):
        best = jnp.max(cand, axis=0, keepdims=True)
        first = jnp.min(jnp.where(cand == best, eidx, N_EXPERTS), axis=0, keepdims=True)
        hit = eidx == first
        chosen = jnp.where(hit, 1.0, chosen)
        cand = jnp.where(hit, neg, cand)
    w = scores * chosen
    comb_out[...] = w / jnp.sum(w, axis=0, keepdims=True) * ROUTED_SCALE
    chosen_out[...] = chosen


def _route(logits_t, bias):
    n_tok = logits_t.shape[1]
    tile = 512
    return pl.pallas_call(
        _route_kernel,
        grid=(n_tok // tile,),
        in_specs=[
            pl.BlockSpec((N_EXPERTS, tile), lambda i: (0, i)),
            pl.BlockSpec((N_EXPERTS, 1), lambda i: (0, 0)),
        ],
        out_specs=[pl.BlockSpec((N_EXPERTS, tile), lambda i: (0, i)),
                   pl.BlockSpec((N_EXPERTS, tile), lambda i: (0, i))],
        out_shape=[jax.ShapeDtypeStruct((N_EXPERTS, n_tok), F32),
                   jax.ShapeDtypeStruct((N_EXPERTS, n_tok), F32)],
        name="router",
    )(logits_t, bias)


def _count_le(bounds, idx):
    return jnp.sum(jnp.where(bounds <= idx, 1.0, 0.0), axis=0, keepdims=True)


def _pick_row(onehot, col):
    return jnp.sum(jnp.where(onehot, col, 0.0), axis=0, keepdims=True)


def _plan_kernel(ch_ref, rankt_out, es_out, rs_out, src_out, et_out, ysrc_out, misc_out):
    nb = N_TOKENS // MOE_BLOCK
    n_src = N_TILES_MAX * UNITS_PER_TILE
    e_col = lax.broadcasted_iota(jnp.int32, (N_EXPERTS, 1), 0).astype(F32)
    blk_lane = lax.broadcasted_iota(jnp.int32, (N_EXPERTS, 128), 1)
    t_r = lax.broadcasted_iota(jnp.int32, (MOE_BLOCK, MOE_BLOCK), 0)
    t_c = lax.broadcasted_iota(jnp.int32, (MOE_BLOCK, MOE_BLOCK), 1)
    earlier = jnp.where(t_c < t_r, 1.0, 0.0).astype(BF16)
    e_r = lax.broadcasted_iota(jnp.int32, (N_EXPERTS, N_EXPERTS), 0)
    e_c = lax.broadcasted_iota(jnp.int32, (N_EXPERTS, N_EXPERTS), 1)
    upto = jnp.where(e_c <= e_r, 1.0, 0.0).astype(F32)
    slot = lax.broadcasted_iota(jnp.int32, (1, SLOT_MAX), 1).astype(F32)

    def col_cumsum(col):
        wide = jnp.broadcast_to(col, (N_EXPERTS, 128))
        return jnp.dot(upto, wide, preferred_element_type=F32, precision=HIGHEST)[:, 0:1]

    def block_stats(b, carry):
        units_mat, start_mat = carry
        cb = ch_ref[:, pl.ds(pl.multiple_of(b * MOE_BLOCK, MOE_BLOCK), MOE_BLOCK)]
        rank_t = _dot_nt(earlier, cb.astype(BF16))
        cb_t = cb.T
        rankt_out[b] = jnp.where(cb_t > 0, rank_t, -1.0).astype(BF16)
        cnt = jnp.sum(cb, axis=1, keepdims=True)
        units = jnp.floor((cnt + (UNIT - 1)) * (1.0 / UNIT))
        incl = col_cumsum(units)
        start = incl - units
        e_slot = _count_le(incl * UNIT, slot)
        onehot = e_col == e_slot
        r = slot - _pick_row(onehot, start * UNIT)
        valid = (e_slot < N_EXPERTS - 0.5) & (r < _pick_row(onehot, cnt))
        es_out[b] = jnp.where(valid, e_slot, -1.0).astype(jnp.int32)
        rs_out[b] = jnp.where(valid, r, -2.0)
        units_mat = jnp.where(blk_lane == b, units, units_mat)
        start_mat = jnp.where(blk_lane == b, start, start_mat)
        return units_mat, start_mat

    zeros = jnp.zeros((N_EXPERTS, 128), F32)
    units_mat, start_mat = lax.fori_loop(0, nb, block_stats, (zeros, zeros))

    tot = jnp.sum(units_mat, axis=1, keepdims=True)
    tiles_e = jnp.floor((tot + (UNITS_PER_TILE - 1)) * (1.0 / UNITS_PER_TILE))
    incl_t = col_cumsum(tiles_e)
    start_t = incl_t - tiles_e
    n_used = incl_t[N_EXPERTS - 1:N_EXPERTS, :]
    b_r = lax.broadcasted_iota(jnp.int32, (128, 128), 0)
    b_c = lax.broadcasted_iota(jnp.int32, (128, 128), 1)
    before = jnp.where(b_r < b_c, 1.0, 0.0).astype(F32)
    cum_b = jnp.dot(units_mat, before, preferred_element_type=F32, precision=HIGHEST)
    run_pos = start_t * UNITS_PER_TILE + cum_b

    tile_idx = lax.broadcasted_iota(jnp.int32, (1, 512), 1).astype(F32)
    et_out[...] = jnp.minimum(_count_le(incl_t, tile_idx), N_EXPERTS - 1.0).astype(jnp.int32)

    incl_ub_t = (cum_b + units_mat).T
    cum_b_t = cum_b.T
    start_mat_t = start_mat.T
    b_col = lax.broadcasted_iota(jnp.int32, (128, 1), 0)
    chunk = 1024

    def src_chunk(c, carry):
        off = pl.multiple_of(c * chunk, chunk)
        p = (lax.broadcasted_iota(jnp.int32, (1, chunk), 1) + off).astype(F32)
        e_p = jnp.minimum(_count_le(incl_t * UNITS_PER_TILE, p), N_EXPERTS - 1.0)
        onehot_e = e_col == e_p
        q = p - _pick_row(onehot_e, start_t * UNITS_PER_TILE)
        valid = (p < n_used * UNITS_PER_TILE) & (q < _pick_row(onehot_e, tot))
        sel = jnp.where(onehot_e, 1.0, 0.0).astype(F32)
        incl_sel = jnp.dot(incl_ub_t, sel, preferred_element_type=F32, precision=HIGHEST)
        b_p = jnp.sum(jnp.where((incl_sel <= q) & (b_col < nb), 1.0, 0.0), axis=0, keepdims=True)
        b_p = jnp.minimum(b_p, nb - 1.0)
        onehot_b = b_col.astype(F32) == b_p
        cum_sel = jnp.dot(cum_b_t, sel, preferred_element_type=F32, precision=HIGHEST)
        start_sel = jnp.dot(start_mat_t, sel, preferred_element_type=F32, precision=HIGHEST)
        unit = b_p * UNITS_PER_BLOCK + _pick_row(onehot_b, start_sel) + q - _pick_row(onehot_b, cum_sel)
        src_out[:, pl.ds(off, chunk)] = jnp.where(valid, unit, 0.0).astype(jnp.int32)
        return carry

    lax.fori_loop(0, n_src // chunk, src_chunk, 0)

    unit_idx = lax.broadcasted_iota(jnp.int32, (1, 256), 1).astype(F32)
    misc_lane = lax.broadcasted_iota(jnp.int32, (1, 128), 1)

    def block_units(b, used):
        here = blk_lane == b
        units = jnp.sum(jnp.where(here, units_mat, 0.0), axis=1, keepdims=True)
        start = jnp.sum(jnp.where(here, start_mat, 0.0), axis=1, keepdims=True)
        pos = jnp.sum(jnp.where(here, run_pos, 0.0), axis=1, keepdims=True)
        incl = start + units
        e_unit = _count_le(incl, unit_idx)
        onehot = e_col == e_unit
        val = _pick_row(onehot, pos - start) + unit_idx
        ysrc_out[b] = jnp.where(e_unit < N_EXPERTS - 0.5, val, 0.0).astype(jnp.int32)
        return jnp.where(misc_lane == b, incl[N_EXPERTS - 1:N_EXPERTS, :], used)

    used = lax.fori_loop(0, nb, block_units, jnp.zeros((1, 128), F32))
    misc_out[0:1, :] = used.astype(jnp.int32)
    misc_out[1:2, :] = jnp.broadcast_to(n_used, (1, 128)).astype(jnp.int32)
    misc_out[2:8, :] = jnp.zeros((6, 128), jnp.int32)


def _routing_plan(chosen):
    nb = N_TOKENS // MOE_BLOCK
    n_src = N_TILES_MAX * UNITS_PER_TILE
    assert n_src % 1024 == 0 and N_TILES_MAX <= 512 and nb <= 128 and UNITS_PER_BLOCK <= 256
    rank_bt, e_slot, r_slot, src, e_tile, ysrc, misc = pl.pallas_call(
        _plan_kernel,
        out_shape=[
            jax.ShapeDtypeStruct((nb, MOE_BLOCK, N_EXPERTS), BF16),
            jax.ShapeDtypeStruct((nb, 1, SLOT_MAX), jnp.int32),
            jax.ShapeDtypeStruct((nb, 1, SLOT_MAX), F32),
            jax.ShapeDtypeStruct((1, n_src), jnp.int32),
            jax.ShapeDtypeStruct((1, 512), jnp.int32),
            jax.ShapeDtypeStruct((nb, 1, 256), jnp.int32),
            jax.ShapeDtypeStruct((8, 128), jnp.int32),
        ],
        compiler_params=pltpu.CompilerParams(vmem_limit_bytes=VMEM_LIMIT),
        name="moe_plan",
    )(chosen)
    return dict(rank_bt=rank_bt, e_slot=e_slot, r_slot=r_slot, src=src.reshape(-1),
                e_tile=e_tile[0, :N_TILES_MAX], n_used=misc[1, :1],
                ysrc=ysrc[:, 0, :UNITS_PER_BLOCK].reshape(-1))


def _permute_kernel(h_ref, rank_ref, es_ref, rs_ref, xp_out):
    sub_e = lax.broadcasted_iota(jnp.int32, (N_EXPERTS, GROUP_SLOTS // 2), 0)
    tiles = [slice(i * GROUP_SLOTS // 2, (i + 1) * GROUP_SLOTS // 2) for i in range(2 * GROUPS_PER_BLOCK)]
    picks = []
    for rows in tiles:
        onehot_e = jnp.where(sub_e == es_ref[0, :, rows], 1.0, 0.0).astype(BF16)
        slot_rank = _dot(rank_ref[0], onehot_e)
        picks.append(jnp.where(slot_rank == rs_ref[0, :, rows], 1.0, 0.0).astype(BF16))
    for rows, pick in zip(tiles, picks):
        xp_out[0, rows, :] = _dot_tn(pick, h_ref[...]).astype(BF16)


def _permute(plan, h2):
    nb = h2.shape[0] // MOE_BLOCK
    return pl.pallas_call(
        _permute_kernel,
        grid=(nb,),
        in_specs=[
            pl.BlockSpec((MOE_BLOCK, D_MODEL), lambda b: (b, 0)),
            pl.BlockSpec((1, MOE_BLOCK, N_EXPERTS), lambda b: (b, 0, 0)),
            pl.BlockSpec((1, 1, SLOT_MAX), lambda b: (b, 0, 0)),
            pl.BlockSpec((1, 1, SLOT_MAX), lambda b: (b, 0, 0)),
        ],
        out_specs=pl.BlockSpec((1, SLOT_MAX, D_MODEL), lambda b: (b, 0, 0)),
        out_shape=jax.ShapeDtypeStruct((nb, SLOT_MAX, D_MODEL), BF16),
        compiler_params=pltpu.CompilerParams(
            dimension_semantics=("arbitrary",), vmem_limit_bytes=VMEM_LIMIT),
        name="moe_permute",
    )(h2, plan["rank_bt"], plan["e_slot"], plan["r_slot"])


def _unit_copies(table_ref, base, src_hbm, dst_ref, slot, sem, n_units, start):
    for u in range(n_units):
        copy = pltpu.make_async_copy(src_hbm.at[table_ref[base + u]],
                                     dst_ref.at[slot, pl.ds(u * UNIT, UNIT), :], sem.at[slot])
        if start:
            copy.start()
        else:
            copy.wait()


def _ffn_kernel(src_ref, et_ref, nu_ref, xp_hbm, *refs):
    w_refs = refs[:2 * TILES_PER_STEP]
    y_out = refs[2 * TILES_PER_STEP]
    scratch = refs[2 * TILES_PER_STEP + 1:]
    wb_refs, xt_ref, sem = scratch[:2 * TILES_PER_STEP], scratch[2 * TILES_PER_STEP], scratch[2 * TILES_PER_STEP + 1]
    n_units = TILES_PER_STEP * UNITS_PER_TILE
    step = pl.program_id(0)
    slot = lax.rem(step, 2)
    n_used = nu_ref[0]

    def copies(s, sl, start):
        _unit_copies(src_ref, s * n_units, xp_hbm, xt_ref, sl, sem, n_units, start)

    @pl.when((step == 0) & (n_used > 0))
    def _():
        copies(0, 0, True)

    @pl.when((step + 1 < pl.num_programs(0)) & ((step + 1) * TILES_PER_STEP < n_used))
    def _():
        copies(step + 1, 1 - slot, True)

    @pl.when(step * TILES_PER_STEP < n_used)
    def _():
        for t in range(TILES_PER_STEP):
            j = step * TILES_PER_STEP + t

            @pl.when((step == 0) | (et_ref[j] != et_ref[jnp.maximum(j - TILES_PER_STEP, 0)]))
            def _():
                wb_refs[2 * t][...] = w_refs[2 * t][0, 0].astype(BF16)
                wb_refs[2 * t + 1][...] = w_refs[2 * t + 1][0, 0].astype(BF16)

        copies(step, slot, False)
        tiles = [slice(t * FFN_TILE, (t + 1) * FFN_TILE) for t in range(TILES_PER_STEP)]
        gus = [_dot(xt_ref[slot, rows, :], wb_refs[2 * t][...]) for t, rows in enumerate(tiles)]
        acts = [(_silu(gu[:, :EXPERT_FF]) * gu[:, EXPERT_FF:]).astype(BF16) for gu in gus]
        for t, (rows, act) in enumerate(zip(tiles, acts)):
            y_out[rows, :] = _dot(act, wb_refs[2 * t + 1][...]).astype(BF16)

    @pl.when(step * TILES_PER_STEP >= n_used)
    def _():
        y_out[...] = jnp.zeros(y_out.shape, BF16)


def _expert_ffn(plan, xp, w_gu, w_d, layer):
    xp_units = xp.reshape(-1, UNIT, D_MODEL)
    assert N_TILES_MAX % TILES_PER_STEP == 0

    def weight_specs(t):
        return [pl.BlockSpec((1, 1, D_MODEL, 2 * EXPERT_FF),
                             lambda s, src, et, nu, t=t: (layer, et[s * TILES_PER_STEP + t], 0, 0)),
                pl.BlockSpec((1, 1, EXPERT_FF, D_MODEL),
                             lambda s, src, et, nu, t=t: (layer, et[s * TILES_PER_STEP + t], 0, 0))]

    grid_spec = pltpu.PrefetchScalarGridSpec(
        num_scalar_prefetch=3,
        grid=(N_TILES_MAX // TILES_PER_STEP,),
        in_specs=[pl.BlockSpec(memory_space=pl.ANY)]
        + [spec for t in range(TILES_PER_STEP) for spec in weight_specs(t)],
        out_specs=pl.BlockSpec((TILES_PER_STEP * FFN_TILE, D_MODEL), lambda s, src, et, nu: (s, 0)),
        scratch_shapes=[
            pltpu.VMEM((D_MODEL, 2 * EXPERT_FF), BF16) if i % 2 == 0 else pltpu.VMEM((EXPERT_FF, D_MODEL), BF16)
            for i in range(2 * TILES_PER_STEP)
        ] + [pltpu.VMEM((2, TILES_PER_STEP * FFN_TILE, D_MODEL), BF16), pltpu.SemaphoreType.DMA((2,))],
    )
    return pl.pallas_call(
        _ffn_kernel,
        grid_spec=grid_spec,
        out_shape=jax.ShapeDtypeStruct((N_TILES_MAX * FFN_TILE, D_MODEL), BF16),
        compiler_params=pltpu.CompilerParams(
            dimension_semantics=("arbitrary",), vmem_limit_bytes=VMEM_LIMIT),
        name="moe_expert_ffn",
    )(plan["src"], plan["e_tile"], plan["n_used"], xp_units, *([w_gu, w_d] * TILES_PER_STEP))


def _combine_kernel(ysrc_ref, y_hbm, rank_ref, comb_ref, er_ref, rr_ref, h_ref, x1_ref, mod_ref,
                    wsgu_ref, wsd_ref, fg_ref, o_ref, yt_ref, sem, *, final):
    b = pl.program_id(0)
    slot = lax.rem(b, 2)

    def copies(bb, sl, start):
        _unit_copies(ysrc_ref, bb * UNITS_PER_BLOCK, y_hbm, yt_ref, sl, sem, UNITS_PER_BLOCK, start)

    @pl.when(b == 0)
    def _():
        copies(0, 0, True)

    @pl.when(b + 1 < pl.num_programs(0))
    def _():
        copies(b + 1, 1 - slot, True)

    gu = _dot(h_ref[...], wsgu_ref[...])
    shared = _dot((_silu(gu[:, :EXPERT_FF]) * gu[:, EXPERT_FF:]).astype(BF16), wsd_ref[...])

    sub_e = lax.broadcasted_iota(jnp.int32, (N_EXPERTS, GROUP_SLOTS), 0)
    groups = [slice(g * GROUP_SLOTS, (g + 1) * GROUP_SLOTS) for g in range(GROUPS_PER_BLOCK)]
    onehots = [jnp.where(sub_e == er_ref[0, :, cols], 1.0, 0.0).astype(BF16) for cols in groups]
    slot_ranks = [_dot(rank_ref[0], onehot_e) for onehot_e in onehots]
    slot_ws = [_dot(comb_ref[0], onehot_e) for onehot_e in onehots]
    weights = jnp.concatenate(
        [jnp.where(slot_rank == rr_ref[0, :, cols], slot_w, 0.0).astype(BF16)
         for cols, slot_rank, slot_w in zip(groups, slot_ranks, slot_ws)], axis=1)

    copies(b, slot, False)
    x2 = x1_ref[...] + mod_ref[0, 5:6, :] * (shared + _dot(weights, yt_ref[slot]))
    if final:
        ms = jnp.mean(x2 * x2, axis=-1, keepdims=True)
        x2 = x2 * lax.rsqrt(ms + EPS) * fg_ref[...]
    o_ref[...] = x2


def _combine(plan, y_sorted, comb_bt, h2, x1, mod_tile, ws_gu, ws_d, final_g, final):
    n_tok = h2.shape[0]
    nb = n_tok // MOE_BLOCK
    y_units = y_sorted.reshape(-1, UNIT, D_MODEL)

    def full(a):
        return pl.BlockSpec(a.shape, lambda b, ysrc: (0,) * a.ndim)

    grid_spec = pltpu.PrefetchScalarGridSpec(
        num_scalar_prefetch=1,
        grid=(nb,),
        in_specs=[
            pl.BlockSpec(memory_space=pl.ANY),
            pl.BlockSpec((1, MOE_BLOCK, N_EXPERTS), lambda b, ysrc: (b, 0, 0)),
            pl.BlockSpec((1, MOE_BLOCK, N_EXPERTS), lambda b, ysrc: (b, 0, 0)),
            pl.BlockSpec((1, 1, SLOT_MAX), lambda b, ysrc: (b, 0, 0)),
            pl.BlockSpec((1, 1, SLOT_MAX), lambda b, ysrc: (b, 0, 0)),
            pl.BlockSpec((MOE_BLOCK, D_MODEL), lambda b, ysrc: (b, 0)),
            pl.BlockSpec((MOE_BLOCK, D_MODEL), lambda b, ysrc: (b, 0)),
            pl.BlockSpec((1, 8, D_MODEL), lambda b, ysrc: (b * MOE_BLOCK // STEP, 0, 0)),
            full(ws_gu), full(ws_d), full(final_g),
        ],
        out_specs=pl.BlockSpec((MOE_BLOCK, D_MODEL), lambda b, ysrc: (b, 0)),
        scratch_shapes=[
            pltpu.VMEM((2, SLOT_MAX, D_MODEL), BF16),
            pltpu.SemaphoreType.DMA((2,)),
        ],
    )
    return pl.pallas_call(
        functools.partial(_combine_kernel, final=final),
        grid_spec=grid_spec,
        out_shape=jax.ShapeDtypeStruct((n_tok, D_MODEL), F32),
        compiler_params=pltpu.CompilerParams(
            dimension_semantics=("arbitrary",), vmem_limit_bytes=VMEM_LIMIT),
        name="moe_combine",
    )(plan["ysrc"], y_units, plan["rank_bt"], comb_bt,
      plan["e_slot"], plan["r_slot"], h2, x1, mod_tile, ws_gu, ws_d, final_g)


def _level_ids():
    t = np.arange(CHUNK)
    x = t[:, None] ^ t[None, :]
    lvl = np.zeros((CHUNK, CHUNK), np.int32)
    nz = x > 0
    lvl[nz] = np.floor(np.log2(x[nz])).astype(np.int32) + 1
    past_f = t[:, None] >= t[None, :]
    return jnp.asarray(np.where(past_f, lvl, -1)), jnp.asarray(np.where(past_f.T, lvl, -1))


def kernel(x_prompt, x_sample, state_hgrn, c, c_ctx, w_ada, b_ada, norm1_g, w_in, hgrn_lb_logits, hgrn_norm_g, w_proj_hgrn, conv_dw_w, conv_dw_b, conv_norm_g, conv_norm_b, w_proj_conv, w_out, norm2_g, w_router, router_bias, w_expert_gate_up, w_expert_down, w_shared_gate_up, w_shared_down, final_norm_g):
    n_ctx, ctx_len, d = x_prompt.shape
    n_lat, lat_len, _ = x_sample.shape
    depth = w_ada.shape[0]
    assert d == D_MODEL and ctx_len * 4 == STEP and lat_len == STEP and n_ctx % 4 == 0
    ctx_steps = n_ctx * ctx_len // STEP
    n_steps = ctx_steps + n_lat
    n_tok = n_steps * STEP
    assert n_tok == N_TOKENS and STEP % POST_TILE == 0 and STEP % MOE_BLOCK == 0
    tiles_per_step = STEP // POST_TILE

    x_all = jnp.concatenate([x_prompt.reshape(-1, d), x_sample.reshape(-1, d)], axis=0)

    cc = jnp.zeros((8, d), F32).at[:n_lat].set(c.astype(F32)).at[n_lat].set(c_ctx.astype(F32))
    mod = _ada(cc, w_ada, b_ada).reshape(depth, 8, N_MOD, d)
    step_src = np.array([n_lat] * ctx_steps + list(range(n_lat)))
    mod_step = jnp.pad(mod[:, step_src], ((0, 0), (0, 0), (0, 8 - N_MOD), (0, 0)))

    nc = jnp.asarray([ctx_len // CHUNK] * ctx_steps + [lat_len // CHUNK] * n_lat, jnp.int32)
    tile_pos = np.arange(n_lat * tiles_per_step) % tiles_per_step
    pv = jnp.asarray(np.concatenate([np.zeros(ctx_steps * tiles_per_step), tile_pos > 0]), jnp.int32)
    nv = jnp.asarray(np.concatenate([np.zeros(ctx_steps * tiles_per_step), tile_pos < tiles_per_step - 1]), jnp.int32)

    lbv = jnp.cumsum(jax.nn.softmax(hgrn_lb_logits.astype(F32), axis=0), axis=0)
    lbv = (lbv - lbv[:1]).reshape(depth, 2, N_HEADS, HEAD_DIM)
    lvl_f, lvl_b = _level_ids()

    xs = x_all
    ctx_states = []
    for l in range(depth):
        lb = lbv[l]
        gp = jnp.stack([jnp.log(lb[0]), jnp.log1p(-lb[0]), 1.0 - lb[0],
                        jnp.log(lb[1]), jnp.log1p(-lb[1]), 1.0 - lb[1],
                        hgrn_norm_g[l].reshape(N_HEADS, HEAD_DIM).astype(F32),
                        jnp.zeros((N_HEADS, HEAD_DIM), F32)], axis=1)
        s0 = jnp.concatenate([jnp.zeros((ctx_steps, 2, N_HEADS, HEAD_DIM, HEAD_DIM), F32),
                              state_hgrn[:, l].astype(F32)], axis=0)
        g1 = norm1_g[l].reshape(1, d).astype(F32)

        o_all, states = _scan(nc, xs, mod_step[l], g1, w_in, l, gp, s0, lvl_f, lvl_b)
        ctx_states.append(states[:ctx_steps].reshape(n_ctx, 2, N_HEADS, HEAD_DIM, HEAD_DIM))

        cw = jnp.pad(conv_dw_w[l].astype(F32), ((0, 1), (0, 0)))
        x1, h2, logits_t = _post(
            pv, nv, xs, mod_step[l], o_all, w_in, l, w_proj_hgrn[l], w_proj_conv[l], w_out[l],
            w_router[l].T.astype(F32), g1, cw,
            conv_dw_b[l].reshape(1, -1).astype(F32), conv_norm_g[l].reshape(1, -1).astype(F32),
            conv_norm_b[l].reshape(1, -1).astype(F32), norm2_g[l].reshape(1, d).astype(F32))

        comb_t, chosen_t = _route(logits_t, router_bias[l].reshape(N_EXPERTS, 1).astype(F32))
        plan = _routing_plan(chosen_t)
        comb_bt = comb_t.reshape(N_EXPERTS, n_tok // MOE_BLOCK, MOE_BLOCK).transpose(1, 2, 0).astype(BF16)
        xp = _permute(plan, h2)
        y_sorted = _expert_ffn(plan, xp, w_expert_gate_up, w_expert_down, l)
        xs = _combine(plan, y_sorted, comb_bt, h2, x1, mod_step[l], w_shared_gate_up[l].astype(BF16),
                      w_shared_down[l].astype(BF16), final_norm_g.reshape(1, d).astype(F32),
                      final=(l == depth - 1))

    n_ctx_tok = n_ctx * ctx_len
    y_prompt = xs[:n_ctx_tok].reshape(x_prompt.shape).astype(x_prompt.dtype)
    y_sample = xs[n_ctx_tok:].reshape(x_sample.shape).astype(x_sample.dtype)
    new_state = jnp.stack(ctx_states, axis=1).astype(x_prompt.dtype)
    return (y_prompt, y_sample, new_state)
```

```python
import functools

import numpy as np
import jax
import jax.numpy as jnp
from jax import lax
from jax.experimental import pallas as pl
from jax.experimental.pallas import tpu as pltpu

F32 = jnp.float32
BF16 = jnp.bfloat16
HIGHEST = lax.Precision.HIGHEST

D_MODEL = 1024
N_HEADS = 8
HEAD_DIM = 128
CONV_DIM = 512
CONV_WIDTH = 31
CONV_HALO = 16
N_EXPERTS = 64
N_GROUPS = 8
GROUP_SIZE = N_EXPERTS // N_GROUPS
TOPK_GROUPS = 4
TOP_K = 8
EXPERT_FF = 256
ROUTED_SCALE = 2.5
N_MOD = 6
EPS = 1e-6

CHUNK = 128
N_LEVELS = 7
LOCKSTEP = 2
STEP = 1024
POST_TILE = 256
MOE_BLOCK = 256
UNIT = 16
FFN_TILE = 256
UNITS_PER_TILE = FFN_TILE // UNIT
TILES_PER_STEP = 2
SLOT_MAX = MOE_BLOCK * TOP_K + N_EXPERTS * UNIT
UNITS_PER_BLOCK = SLOT_MAX // UNIT
GROUP_UNITS = 32
GROUP_SLOTS = GROUP_UNITS * UNIT
GROUPS_PER_BLOCK = UNITS_PER_BLOCK // GROUP_UNITS
N_TOKENS = 8192
N_TILES_MAX = (N_TOKENS * TOP_K // UNIT + (N_TOKENS // MOE_BLOCK) * N_EXPERTS) // UNITS_PER_TILE + N_EXPERTS
HEAD_COLS = 5 * HEAD_DIM
VMEM_LIMIT = 52 * 1024 * 1024


def _dot(a, b):
    return jnp.dot(a, b, preferred_element_type=F32)


def _dot_nt(a, b):
    return lax.dot_general(a, b, (((1,), (1,)), ((), ())), preferred_element_type=F32)


def _dot_tn(a, b):
    return lax.dot_general(a, b, (((0,), (0,)), ((), ())), preferred_element_type=F32)


def _sigmoid(x):
    return 1.0 / (1.0 + jnp.exp(-x))


def _silu(x):
    return x * _sigmoid(x)


def _rms_mod(x, g, scale, shift):
    ms = jnp.mean(x * x, axis=-1, keepdims=True)
    return x * lax.rsqrt(ms + EPS) * g * (1.0 + scale) + shift


def _ada_kernel(c_ref, w_ref, b_ref, o_ref):
    cc = c_ref[...]
    o_ref[0] = jnp.dot(_silu(cc), w_ref[0], preferred_element_type=F32, precision=HIGHEST) + b_ref[0]


def _ada(cc, w_ada, b_ada):
    depth, d, n = w_ada.shape
    tn = 1536
    return pl.pallas_call(
        _ada_kernel,
        grid=(depth, n // tn),
        in_specs=[
            pl.BlockSpec((8, d), lambda l, j: (0, 0)),
            pl.BlockSpec((1, d, tn), lambda l, j: (l, 0, j)),
            pl.BlockSpec((1, 1, tn), lambda l, j: (l, 0, j)),
        ],
        out_specs=pl.BlockSpec((1, 8, tn), lambda l, j: (l, 0, j)),
        out_shape=jax.ShapeDtypeStruct((depth, 8, n), F32),
        compiler_params=pltpu.CompilerParams(vmem_limit_bytes=VMEM_LIMIT),
        name="ada_mod",
    )(cc, w_ada, b_ada.reshape(depth, 1, n))


def _forget_gate(z, log_lb, log1m_lb, one_m_lb):
    e = jnp.exp(-jnp.abs(z))
    r = 1.0 / (1.0 + e)
    log_sig = jnp.minimum(z, 0.0) - jnp.log(1.0 + e)
    k = one_m_lb * jnp.where(z > 0, e * r, r)
    b = log1m_lb + log_sig
    log_f = jnp.maximum(log_lb, b) + jnp.log(1.0 + jnp.exp(-jnp.abs(log_lb - b)))
    return log_f, k


def _level_reference(cum_ref, blk, fwd):
    half = blk // 2
    pieces = []
    if blk >= 8:
        for i in range(CHUNK // blk):
            row = i * blk + (half - 1 if fwd else half)
            pieces.append(jnp.broadcast_to(cum_ref[row:row + 1, :], (blk, HEAD_DIM)))
    else:
        sub = lax.broadcasted_iota(jnp.int32, (8, HEAD_DIM), 0)
        for i in range(CHUNK // 8):
            lo_row = 8 * i + (1 if fwd else 2)
            hi_row = 8 * i + (5 if fwd else 6)
            lo = jnp.broadcast_to(cum_ref[lo_row:lo_row + 1, :], (8, HEAD_DIM))
            hi = jnp.broadcast_to(cum_ref[hi_row:hi_row + 1, :], (8, HEAD_DIM))
            pieces.append(jnp.where(sub >= 4, hi, lo))
    return pieces[0] if len(pieces) == 1 else jnp.concatenate(pieces, axis=0)


def _chunk_steps(chains):
    r_idx = lax.broadcasted_iota(jnp.int32, (CHUNK, CHUNK), 0)
    c_idx = lax.broadcasted_iota(jnp.int32, (CHUNK, CHUNK), 1)
    work = []
    for q, k, v, log_f, lvl, cum_ref, fwd in chains:
        tri = jnp.where((r_idx >= c_idx) if fwd else (r_idx <= c_idx), 1.0, 0.0).astype(BF16)
        hi = log_f.astype(BF16)
        lo = (log_f - hi.astype(F32)).astype(BF16)
        both = _dot(tri, jnp.concatenate([hi, lo], axis=1))
        cum = both[:, :HEAD_DIM] + both[:, HEAD_DIM:]
        cum_ref[...] = cum
        q_b = q.astype(BF16)
        k_b = k.astype(BF16)
        work.append(dict(cum=cum, q_b=q_b, k_b=k_b, scores=jnp.where(lvl == 0, _dot_nt(q_b, k_b), 0.0)))

    for lev in range(1, N_LEVELS + 1):
        for (q, k, v, log_f, lvl, cum_ref, fwd), w in zip(chains, work):
            if lev == 1:
                qe = w["q_b"] * jnp.exp(log_f).astype(BF16)
                ke = w["k_b"]
            else:
                e = jnp.exp(-jnp.abs(w["cum"] - _level_reference(cum_ref, 1 << lev, fwd))).astype(BF16)
                qe = w["q_b"] * e
                ke = w["k_b"] * e
            w["scores"] = jnp.where(lvl == lev, _dot_nt(qe, ke), w["scores"])

    def finish(chain, w, st):
        q, k, v, log_f, lvl, cum_ref, fwd = chain
        total = cum_ref[CHUNK - 1:CHUNK, :] if fwd else cum_ref[0:1, :]
        o = (_dot(w["scores"].astype(BF16), v.astype(BF16))
             + _dot_nt((q * jnp.exp(w["cum"])).astype(BF16), st.astype(BF16)))
        k_st = k * jnp.exp(total - w["cum"])
        return o, st * jnp.exp(total) + _dot_tn(v.astype(BF16), k_st.astype(BF16))

    return [functools.partial(finish, chain, w) for chain, w in zip(chains, work)]


def _scan_kernel(nc_ref, x_ref, mod_ref, g1_ref, wq_ref, wv_ref, wf_ref, wb_ref, wg_ref, gp_ref, s0_ref,
                 lvlf_ref, lvlb_ref, o_out, st_out, h_ref, w_ref, z_ref, of_ref, ob_ref, cum_ref):
    pair = pl.program_id(0)
    step = pl.program_id(1)
    sub = pl.program_id(2)
    n_seq_chunks = nc_ref[step]
    n_chunks = STEP // CHUNK

    @pl.when((pair == 0) & (sub == 0))
    def _():
        g1 = g1_ref[...]
        shift = mod_ref[0, 0:1, :]
        scale = mod_ref[0, 1:2, :]

        for i in range(n_chunks):
            rows = pl.ds(i * CHUNK, CHUNK)
            h_ref[step, rows, :] = _rms_mod(x_ref[rows, :], g1, scale, shift).astype(BF16)

    @pl.when((step == 0) & (sub == 0))
    def _():
        for j, wj_ref in enumerate((wq_ref, wv_ref, wf_ref, wb_ref, wg_ref)):
            for hh in range(2):
                w_ref[:, hh * HEAD_COLS + j * HEAD_DIM:hh * HEAD_COLS + (j + 1) * HEAD_DIM] = (
                    wj_ref[0, :, hh * HEAD_DIM:(hh + 1) * HEAD_DIM].astype(BF16))

    @pl.when(sub == 0)
    def _():
        z = _dot(h_ref[step], w_ref[...])
        z_ref[0] = z[:, :HEAD_COLS]
        z_ref[1] = z[:, HEAD_COLS:]

    st_out[...] = jnp.zeros(st_out.shape, F32)
    st0 = (s0_ref[0, 0, 0].T, s0_ref[0, 1, 0].T)

    def chain(c, fwd, cum_ref):
        d = 0 if fwd else 1
        zcol = (2 if fwd else 3) * HEAD_DIM
        rows = pl.ds(c * CHUNK, CHUNK)
        log_f, k = _forget_gate(z_ref[sub, rows, zcol:zcol + HEAD_DIM], gp_ref[0, 3 * d:3 * d + 1, :],
                                gp_ref[0, 3 * d + 1:3 * d + 2, :], gp_ref[0, 3 * d + 2:3 * d + 3, :])
        return (z_ref[sub, rows, 0:HEAD_DIM], k, z_ref[sub, rows, HEAD_DIM:2 * HEAD_DIM], log_f,
                (lvlf_ref if fwd else lvlb_ref)[...], cum_ref, fwd)

    carry = list(st0)
    for i0 in range(0, n_chunks, LOCKSTEP):
        todo = [(i if fwd else n_chunks - 1 - i, fwd) for i in range(i0, i0 + LOCKSTEP) for fwd in (True, False)]
        finishers = _chunk_steps([chain(c, fwd, cum_ref.at[n]) for n, (c, fwd) in enumerate(todo)])
        for (c, fwd), finish_chain in zip(todo, finishers):
            d = 0 if fwd else 1
            pos = lax.rem(c, n_seq_chunks)
            first = (pos == 0) if fwd else (pos == n_seq_chunks - 1)
            o, carry[d] = finish_chain(jnp.where(first, st0[d], carry[d]))
            (of_ref if fwd else ob_ref)[pl.ds(c * CHUNK, CHUNK), :] = o
            st_out[0, lax.div(c, n_seq_chunks), d, 0] = carry[d].T

    norm_g = gp_ref[0, 6:7, :]

    chunk_rows = [pl.ds(i * CHUNK, CHUNK) for i in range(n_chunks)]
    os_ = [of_ref[rows, :] + ob_ref[rows, :] for rows in chunk_rows]
    invs = [lax.rsqrt(jnp.mean(o * o, axis=-1, keepdims=True) + EPS) for o in os_]
    for rows, o, inv in zip(chunk_rows, os_, invs):
        og = z_ref[sub, rows, 4 * HEAD_DIM:5 * HEAD_DIM]
        o_out[rows, :] = (o * inv * norm_g * _silu(og)).astype(BF16)


def _scan(nc, x_all, mod_step, g1, w_in, layer, gp, s0, lvl_f, lvl_b):
    n_tok = x_all.shape[0]
    n_steps = n_tok // STEP
    grid_spec = pltpu.PrefetchScalarGridSpec(
        num_scalar_prefetch=1,
        grid=(N_HEADS // 2, n_steps, 2),
        in_specs=[
            pl.BlockSpec((STEP, D_MODEL), lambda p, s, u, nc: (jnp.where((p == 0) & (u == 0), s, 0), 0)),
            pl.BlockSpec((1, 8, D_MODEL), lambda p, s, u, nc: (s, 0, 0)),
            pl.BlockSpec((1, D_MODEL), lambda p, s, u, nc: (0, 0)),
        ] + [
            pl.BlockSpec((1, D_MODEL, 2 * HEAD_DIM), lambda p, s, u, nc, j=j: (layer, 0, j * (N_HEADS // 2) + p))
            for j in range(5)
        ] + [
            pl.BlockSpec((1, 8, HEAD_DIM), lambda p, s, u, nc: (2 * p + u, 0, 0)),
            pl.BlockSpec((1, 2, 1, HEAD_DIM, HEAD_DIM), lambda p, s, u, nc: (s, 0, 2 * p + u, 0, 0)),
            pl.BlockSpec((CHUNK, CHUNK), lambda p, s, u, nc: (0, 0)),
            pl.BlockSpec((CHUNK, CHUNK), lambda p, s, u, nc: (0, 0)),
        ],
        out_specs=[
            pl.BlockSpec((STEP, HEAD_DIM), lambda p, s, u, nc: (s, 2 * p + u)),
            pl.BlockSpec((1, 4, 2, 1, HEAD_DIM, HEAD_DIM), lambda p, s, u, nc: (s, 0, 0, 2 * p + u, 0, 0)),
        ],
        scratch_shapes=[
            pltpu.VMEM((n_steps, STEP, D_MODEL), BF16),
            pltpu.VMEM((D_MODEL, 2 * HEAD_COLS), BF16),
            pltpu.VMEM((2, STEP, HEAD_COLS), F32),
            pltpu.VMEM((STEP, HEAD_DIM), F32),
            pltpu.VMEM((STEP, HEAD_DIM), F32),
            pltpu.VMEM((2 * LOCKSTEP, CHUNK, HEAD_DIM), F32),
        ],
    )
    return pl.pallas_call(
        _scan_kernel,
        grid_spec=grid_spec,
        out_shape=[
            jax.ShapeDtypeStruct((n_tok, D_MODEL), BF16),
            jax.ShapeDtypeStruct((n_steps, 4, 2, N_HEADS, HEAD_DIM, HEAD_DIM), F32),
        ],
        compiler_params=pltpu.CompilerParams(
            dimension_semantics=("arbitrary", "arbitrary", "arbitrary"), vmem_limit_bytes=VMEM_LIMIT),
        name="hgrn_scan",
    )(nc, x_all, mod_step, g1, *([w_in] * 5), gp, s0, lvl_f, lvl_b)


def _post_kernel(pv_ref, nv_ref, x_ref, xp_ref, xn_ref, mod_ref, o_ref, wu_f32, wgz_f32, wph_f32,
                 wpc_f32, wo_f32, wr_ref, g1_ref, cw_ref, cb_ref, lg_ref, lb_ref, g2_ref,
                 x1_out, h2_out, lg_out, cu_ref, wu_ref, wgz_ref, wph_ref, wpc_ref, wo_ref):
    i = pl.program_id(0)

    @pl.when(i == 0)
    def _():
        for dst, src in ((wu_ref, wu_f32), (wgz_ref, wgz_f32), (wph_ref, wph_f32),
                         (wpc_ref, wpc_f32), (wo_ref, wo_f32)):
            dst[...] = (src[0] if len(src.shape) == 3 else src[...]).astype(BF16)

    g1 = g1_ref[...]
    shift1 = mod_ref[0, 0:1, :]
    scale1 = mod_ref[0, 1:2, :]
    gate1 = mod_ref[0, 2:3, :]
    shift2 = mod_ref[0, 3:4, :]
    scale2 = mod_ref[0, 4:5, :]

    def glu(xv):
        h = _rms_mod(xv, g1, scale1, shift1).astype(BF16)
        u = _dot(h, wu_ref[...])
        return h, u[:, :CONV_DIM] * _sigmoid(u[:, CONV_DIM:])

    x = x_ref[...]
    h, glu_mid = glu(x)
    _, glu_prev = glu(xp_ref[...])
    _, glu_next = glu(xn_ref[...])
    cu_ref[0, 0:CONV_HALO, :] = glu_prev * pv_ref[i].astype(F32)
    cu_ref[0, CONV_HALO:CONV_HALO + POST_TILE, :] = glu_mid
    cu_ref[0, CONV_HALO + POST_TILE:, :] = glu_next * nv_ref[i].astype(F32)
    n_keep = POST_TILE + 2 * CONV_HALO - 8
    for k in range(1, 8):
        cu_ref[k, 0:n_keep, :] = cu_ref[0, k:k + n_keep, :]

    off = CONV_HALO - CONV_WIDTH // 2
    acc = jnp.zeros((POST_TILE, CONV_DIM), F32) + cb_ref[...]
    for j in range(CONV_WIDTH):
        base, k = divmod(off + j, 8)
        acc = acc + cu_ref[k, 8 * base:8 * base + POST_TILE, :] * cw_ref[j:j + 1, :]
    mu = jnp.mean(acc, axis=-1, keepdims=True)
    cen = acc - mu
    var = jnp.mean(cen * cen, axis=-1, keepdims=True)
    cv = cen * lax.rsqrt(var + EPS) * lg_ref[...] + lb_ref[...]
    y_b = _dot(_silu(cv).astype(BF16), wpc_ref[...])

    y_a = _dot(o_ref[...], wph_ref[...])
    gz = _sigmoid(_dot(h, wgz_ref[...]))
    merged = gz[:, :D_MODEL] * y_a + gz[:, D_MODEL:] * y_b
    x1 = x + gate1 * _dot(merged.astype(BF16), wo_ref[...])
    x1_out[...] = x1
    h2 = _rms_mod(x1, g2_ref[...], scale2, shift2)
    h2_out[...] = h2.astype(BF16)
    lg_out[...] = lax.dot_general(wr_ref[...], h2, (((1,), (1,)), ((), ())),
                                  preferred_element_type=F32, precision=HIGHEST)


def _post(pv, nv, x_all, mod_tile, o_all, w_in, layer, w_ph, w_pc, w_o, w_rt, g1, cw, cb, lg, lb, g2):
    n_tok = x_all.shape[0]
    n_tiles = n_tok // POST_TILE
    halo_per_tile = POST_TILE // CONV_HALO
    n_halo_blocks = n_tok // CONV_HALO
    glu_cols, gate_cols = 2 * CONV_DIM, 2 * D_MODEL
    glu_start = 5 * N_HEADS * HEAD_DIM
    assert glu_start % glu_cols == 0 and (glu_start + glu_cols) % gate_cols == 0

    def full(a):
        return pl.BlockSpec(a.shape, lambda i, pv, nv: (0,) * a.ndim)

    def resident(shape, index):
        return pl.BlockSpec(shape, lambda i, pv, nv: index, pipeline_mode=pl.Buffered(1))

    grid_spec = pltpu.PrefetchScalarGridSpec(
        num_scalar_prefetch=2,
        grid=(n_tiles,),
        in_specs=[
            pl.BlockSpec((POST_TILE, D_MODEL), lambda i, pv, nv: (i, 0)),
            pl.BlockSpec((CONV_HALO, D_MODEL), lambda i, pv, nv: (jnp.maximum(i * halo_per_tile - 1, 0), 0)),
            pl.BlockSpec((CONV_HALO, D_MODEL),
                         lambda i, pv, nv: (jnp.minimum((i + 1) * halo_per_tile, n_halo_blocks - 1), 0)),
            pl.BlockSpec((1, 8, D_MODEL), lambda i, pv, nv: (i * POST_TILE // STEP, 0, 0)),
            pl.BlockSpec((POST_TILE, D_MODEL), lambda i, pv, nv: (i, 0)),
            resident((1, D_MODEL, glu_cols), (layer, 0, glu_start // glu_cols)),
            resident((1, D_MODEL, gate_cols), (layer, 0, (glu_start + glu_cols) // gate_cols)),
            resident(w_ph.shape, (0, 0)), resident(w_pc.shape, (0, 0)), resident(w_o.shape, (0, 0)),
            full(w_rt), full(g1), full(cw), full(cb), full(lg), full(lb), full(g2),
        ],
        out_specs=[
            pl.BlockSpec((POST_TILE, D_MODEL), lambda i, pv, nv: (i, 0)),
            pl.BlockSpec((POST_TILE, D_MODEL), lambda i, pv, nv: (i, 0)),
            pl.BlockSpec((N_EXPERTS, POST_TILE), lambda i, pv, nv: (0, i)),
        ],
        scratch_shapes=[
            pltpu.VMEM((8, POST_TILE + 2 * CONV_HALO, CONV_DIM), F32),
            pltpu.VMEM((D_MODEL, glu_cols), BF16), pltpu.VMEM((D_MODEL, gate_cols), BF16),
            pltpu.VMEM(w_ph.shape, BF16), pltpu.VMEM(w_pc.shape, BF16), pltpu.VMEM(w_o.shape, BF16),
        ],
    )
    return pl.pallas_call(
        _post_kernel,
        grid_spec=grid_spec,
        out_shape=[
            jax.ShapeDtypeStruct((n_tok, D_MODEL), F32),
            jax.ShapeDtypeStruct((n_tok, D_MODEL), BF16),
            jax.ShapeDtypeStruct((N_EXPERTS, n_tok), F32),
        ],
        compiler_params=pltpu.CompilerParams(
            dimension_semantics=("arbitrary",), vmem_limit_bytes=VMEM_LIMIT),
        name="post_mixer",
    )(pv, nv, x_all, x_all, x_all, mod_tile, o_all, w_in, w_in, w_ph, w_pc, w_o, w_rt, g1, cw, cb, lg, lb, g2)


def _route_kernel(lg_ref, bias_ref, comb_out, chosen_out):
    n = lg_ref.shape[1]
    scores = _sigmoid(lg_ref[...])
    sel = scores + bias_ref[...]
    neg = jnp.float32(-jnp.inf)

    sel3 = sel.reshape(N_GROUPS, GROUP_SIZE, n)
    m1 = jnp.max(sel3, axis=1, keepdims=True)
    is_m1 = sel3 == m1
    n_m1 = jnp.sum(is_m1.astype(F32), axis=1, keepdims=True)
    m2 = jnp.max(jnp.where(is_m1, neg, sel3), axis=1, keepdims=True)
    grp = (m1 + jnp.where(n_m1 > 1.5, m1, m2)).reshape(N_GROUPS, n)

    gidx = lax.broadcasted_iota(jnp.int32, (N_GROUPS, n), 0)
    rank = jnp.zeros((N_GROUPS, n), F32)
    for g in range(N_GROUPS):
        other = grp[g:g + 1, :]
        ahead = (other > grp) | ((other == grp) & (g < gidx))
        rank = rank + ahead.astype(F32)
    keep_g = rank < TOPK_GROUPS - 0.5
    keep = jnp.broadcast_to(keep_g.reshape(N_GROUPS, 1, n), (N_GROUPS, GROUP_SIZE, n)).reshape(N_EXPERTS, n)
    cand = jnp.where(keep, sel, neg)

    eidx = lax.broadcasted_iota(jnp.int32, (N_EXPERTS, n), 0)
    chosen = jnp.zeros((N_EXPERTS, n), F32)
    for _ in range(TOP_K):
        best = jnp.max(cand, axis=0, keepdims=True)
        first = jnp.min(jnp.where(cand == best, eidx, N_EXPERTS), axis=0, keepdims=True)
        hit = eidx == first
        chosen = jnp.where(hit, 1.0, chosen)
        cand = jnp.where(hit, neg, cand)
    w = scores * chosen
    comb_out[...] = w / jnp.sum(w, axis=0, keepdims=True) * ROUTED_SCALE
    chosen_out[...] = chosen


def _route(logits_t, bias):
    n_tok = logits_t.shape[1]
    tile = 512
    return pl.pallas_call(
        _route_kernel,
        grid=(n_tok // tile,),
        in_specs=[
            pl.BlockSpec((N_EXPERTS, tile), lambda i: (0, i)),
            pl.BlockSpec((N_EXPERTS, 1), lambda i: (0, 0)),
        ],
        out_specs=[pl.BlockSpec((N_EXPERTS, tile), lambda i: (0, i)),
                   pl.BlockSpec((N_EXPERTS, tile), lambda i: (0, i))],
        out_shape=[jax.ShapeDtypeStruct((N_EXPERTS, n_tok), F32),
                   jax.ShapeDtypeStruct((N_EXPERTS, n_tok), F32)],
        name="router",
    )(logits_t, bias)


def _count_le(bounds, idx):
    return jnp.sum(jnp.where(bounds <= idx, 1.0, 0.0), axis=0, keepdims=True)


def _pick_row(onehot, col):
    return jnp.sum(jnp.where(onehot, col, 0.0), axis=0, keepdims=True)


def _plan_kernel(ch_ref, rankt_out, es_out, rs_out, src_out, et_out, ysrc_out, misc_out):
    nb = N_TOKENS // MOE_BLOCK
    n_src = N_TILES_MAX * UNITS_PER_TILE
    e_col = lax.broadcasted_iota(jnp.int32, (N_EXPERTS, 1), 0).astype(F32)
    blk_lane = lax.broadcasted_iota(jnp.int32, (N_EXPERTS, 128), 1)
    t_r = lax.broadcasted_iota(jnp.int32, (MOE_BLOCK, MOE_BLOCK), 0)
    t_c = lax.broadcasted_iota(jnp.int32, (MOE_BLOCK, MOE_BLOCK), 1)
    earlier = jnp.where(t_c < t_r, 1.0, 0.0).astype(BF16)
    e_r = lax.broadcasted_iota(jnp.int32, (N_EXPERTS, N_EXPERTS), 0)
    e_c = lax.broadcasted_iota(jnp.int32, (N_EXPERTS, N_EXPERTS), 1)
    upto = jnp.where(e_c <= e_r, 1.0, 0.0).astype(F32)
    slot = lax.broadcasted_iota(jnp.int32, (1, SLOT_MAX), 1).astype(F32)

    def col_cumsum(col):
        wide = jnp.broadcast_to(col, (N_EXPERTS, 128))
        return jnp.dot(upto, wide, preferred_element_type=F32, precision=HIGHEST)[:, 0:1]

    def block_stats(b, carry):
        units_mat, start_mat = carry
        cb = ch_ref[:, pl.ds(pl.multiple_of(b * MOE_BLOCK, MOE_BLOCK), MOE_BLOCK)]
        rank_t = _dot_nt(earlier, cb.astype(BF16))
        cb_t = cb.T
        rankt_out[b] = jnp.where(cb_t > 0, rank_t, -1.0).astype(BF16)
        cnt = jnp.sum(cb, axis=1, keepdims=True)
        units = jnp.floor((cnt + (UNIT - 1)) * (1.0 / UNIT))
        incl = col_cumsum(units)
        start = incl - units
        e_slot = _count_le(incl * UNIT, slot)
        onehot = e_col == e_slot
        r = slot - _pick_row(onehot, start * UNIT)
        valid = (e_slot < N_EXPERTS - 0.5) & (r < _pick_row(onehot, cnt))
        es_out[b] = jnp.where(valid, e_slot, -1.0).astype(jnp.int32)
        rs_out[b] = jnp.where(valid, r, -2.0)
        units_mat = jnp.where(blk_lane == b, units, units_mat)
        start_mat = jnp.where(blk_lane == b, start, start_mat)
        return units_mat, start_mat

    zeros = jnp.zeros((N_EXPERTS, 128), F32)
    units_mat, start_mat = lax.fori_loop(0, nb, block_stats, (zeros, zeros))

    tot = jnp.sum(units_mat, axis=1, keepdims=True)
    tiles_e = jnp.floor((tot + (UNITS_PER_TILE - 1)) * (1.0 / UNITS_PER_TILE))
    incl_t = col_cumsum(tiles_e)
    start_t = incl_t - tiles_e
    n_used = incl_t[N_EXPERTS - 1:N_EXPERTS, :]
    b_r = lax.broadcasted_iota(jnp.int32, (128, 128), 0)
    b_c = lax.broadcasted_iota(jnp.int32, (128, 128), 1)
    before = jnp.where(b_r < b_c, 1.0, 0.0).astype(F32)
    cum_b = jnp.dot(units_mat, before, preferred_element_type=F32, precision=HIGHEST)
    run_pos = start_t * UNITS_PER_TILE + cum_b

    tile_idx = lax.broadcasted_iota(jnp.int32, (1, 512), 1).astype(F32)
    et_out[...] = jnp.minimum(_count_le(incl_t, tile_idx), N_EXPERTS - 1.0).astype(jnp.int32)

    incl_ub_t = (cum_b + units_mat).T
    cum_b_t = cum_b.T
    start_mat_t = start_mat.T
    b_col = lax.broadcasted_iota(jnp.int32, (128, 1), 0)
    chunk = 1024

    def src_chunk(c, carry):
        off = pl.multiple_of(c * chunk, chunk)
        p = (lax.broadcasted_iota(jnp.int32, (1, chunk), 1) + off).astype(F32)
        e_p = jnp.minimum(_count_le(incl_t * UNITS_PER_TILE, p), N_EXPERTS - 1.0)
        onehot_e = e_col == e_p
        q = p - _pick_row(onehot_e, start_t * UNITS_PER_TILE)
        valid = (p < n_used * UNITS_PER_TILE) & (q < _pick_row(onehot_e, tot))
        sel = jnp.where(onehot_e, 1.0, 0.0).astype(F32)
        incl_sel = jnp.dot(incl_ub_t, sel, preferred_element_type=F32, precision=HIGHEST)
        b_p = jnp.sum(jnp.where((incl_sel <= q) & (b_col < nb), 1.0, 0.0), axis=0, keepdims=True)
        b_p = jnp.minimum(b_p, nb - 1.0)
        onehot_b = b_col.astype(F32) == b_p
        cum_sel = jnp.dot(cum_b_t, sel, preferred_element_type=F32, precision=HIGHEST)
        start_sel = jnp.dot(start_mat_t, sel, preferred_element_type=F32, precision=HIGHEST)
        unit = b_p * UNITS_PER_BLOCK + _pick_row(onehot_b, start_sel) + q - _pick_row(onehot_b, cum_sel)
        src_out[:, pl.ds(off, chunk)] = jnp.where(valid, unit, 0.0).astype(jnp.int32)
        return carry

    lax.fori_loop(0, n_src // chunk, src_chunk, 0)

    unit_idx = lax.broadcasted_iota(jnp.int32, (1, 256), 1).astype(F32)
    misc_lane = lax.broadcasted_iota(jnp.int32, (1, 128), 1)

    def block_units(b, used):
        here = blk_lane == b
        units = jnp.sum(jnp.where(here, units_mat, 0.0), axis=1, keepdims=True)
        start = jnp.sum(jnp.where(here, start_mat, 0.0), axis=1, keepdims=True)
        pos = jnp.sum(jnp.where(here, run_pos, 0.0), axis=1, keepdims=True)
        incl = start + units
        e_unit = _count_le(incl, unit_idx)
        onehot = e_col == e_unit
        val = _pick_row(onehot, pos - start) + unit_idx
        ysrc_out[b] = jnp.where(e_unit < N_EXPERTS - 0.5, val, 0.0).astype(jnp.int32)
        return jnp.where(misc_lane == b, incl[N_EXPERTS - 1:N_EXPERTS, :], used)

    used = lax.fori_loop(0, nb, block_units, jnp.zeros((1, 128), F32))
    misc_out[0:1, :] = used.astype(jnp.int32)
    misc_out[1:2, :] = jnp.broadcast_to(n_used, (1, 128)).astype(jnp.int32)
    misc_out[2:8, :] = jnp.zeros((6, 128), jnp.int32)


def _routing_plan(chosen):
    nb = N_TOKENS // MOE_BLOCK
    n_src = N_TILES_MAX * UNITS_PER_TILE
    assert n_src % 1024 == 0 and N_TILES_MAX <= 512 and nb <= 128 and UNITS_PER_BLOCK <= 256
    rank_bt, e_slot, r_slot, src, e_tile, ysrc, misc = pl.pallas_call(
        _plan_kernel,
        out_shape=[
            jax.ShapeDtypeStruct((nb, MOE_BLOCK, N_EXPERTS), BF16),
            jax.ShapeDtypeStruct((nb, 1, SLOT_MAX), jnp.int32),
            jax.ShapeDtypeStruct((nb, 1, SLOT_MAX), F32),
            jax.ShapeDtypeStruct((1, n_src), jnp.int32),
            jax.ShapeDtypeStruct((1, 512), jnp.int32),
            jax.ShapeDtypeStruct((nb, 1, 256), jnp.int32),
            jax.ShapeDtypeStruct((8, 128), jnp.int32),
        ],
        compiler_params=pltpu.CompilerParams(vmem_limit_bytes=VMEM_LIMIT),
        name="moe_plan",
    )(chosen)
    return dict(rank_bt=rank_bt, e_slot=e_slot, r_slot=r_slot, src=src.reshape(-1),
                e_tile=e_tile[0, :N_TILES_MAX], n_used=misc[1, :1],
                ysrc=ysrc[:, 0, :UNITS_PER_BLOCK].reshape(-1))


def _permute_kernel(h_ref, rank_ref, es_ref, rs_ref, xp_out):
    sub_e = lax.broadcasted_iota(jnp.int32, (N_EXPERTS, GROUP_SLOTS // 2), 0)
    tiles = [slice(i * GROUP_SLOTS // 2, (i + 1) * GROUP_SLOTS // 2) for i in range(2 * GROUPS_PER_BLOCK)]
    picks = []
    for rows in tiles:
        onehot_e = jnp.where(sub_e == es_ref[0, :, rows], 1.0, 0.0).astype(BF16)
        slot_rank = _dot(rank_ref[0], onehot_e)
        picks.append(jnp.where(slot_rank == rs_ref[0, :, rows], 1.0, 0.0).astype(BF16))
    for rows, pick in zip(tiles, picks):
        xp_out[0, rows, :] = _dot_tn(pick, h_ref[...]).astype(BF16)


def _permute(plan, h2):
    nb = h2.shape[0] // MOE_BLOCK
    return pl.pallas_call(
        _permute_kernel,
        grid=(nb,),
        in_specs=[
            pl.BlockSpec((MOE_BLOCK, D_MODEL), lambda b: (b, 0)),
            pl.BlockSpec((1, MOE_BLOCK, N_EXPERTS), lambda b: (b, 0, 0)),
            pl.BlockSpec((1, 1, SLOT_MAX), lambda b: (b, 0, 0)),
            pl.BlockSpec((1, 1, SLOT_MAX), lambda b: (b, 0, 0)),
        ],
        out_specs=pl.BlockSpec((1, SLOT_MAX, D_MODEL), lambda b: (b, 0, 0)),
        out_shape=jax.ShapeDtypeStruct((nb, SLOT_MAX, D_MODEL), BF16),
        compiler_params=pltpu.CompilerParams(
            dimension_semantics=("arbitrary",), vmem_limit_bytes=VMEM_LIMIT),
        name="moe_permute",
    )(h2, plan["rank_bt"], plan["e_slot"], plan["r_slot"])


def _unit_copies(table_ref, base, src_hbm, dst_ref, slot, sem, n_units, start):
    for u in range(n_units):
        copy = pltpu.make_async_copy(src_hbm.at[table_ref[base + u]],
                                     dst_ref.at[slot, pl.ds(u * UNIT, UNIT), :], sem.at[slot])
        if start:
            copy.start()
        else:
            copy.wait()


def _ffn_kernel(src_ref, et_ref, nu_ref, xp_hbm, *refs):
    w_refs = refs[:2 * TILES_PER_STEP]
    y_out = refs[2 * TILES_PER_STEP]
    scratch = refs[2 * TILES_PER_STEP + 1:]
    wb_refs, xt_ref, sem = scratch[:2 * TILES_PER_STEP], scratch[2 * TILES_PER_STEP], scratch[2 * TILES_PER_STEP + 1]
    n_units = TILES_PER_STEP * UNITS_PER_TILE
    step = pl.program_id(0)
    slot = lax.rem(step, 2)
    n_used = nu_ref[0]

    def copies(s, sl, start):
        _unit_copies(src_ref, s * n_units, xp_hbm, xt_ref, sl, sem, n_units, start)

    @pl.when((step == 0) & (n_used > 0))
    def _():
        copies(0, 0, True)

    @pl.when((step + 1 < pl.num_programs(0)) & ((step + 1) * TILES_PER_STEP < n_used))
    def _():
        copies(step + 1, 1 - slot, True)

    @pl.when(step * TILES_PER_STEP < n_used)
    def _():
        for t in range(TILES_PER_STEP):
            j = step * TILES_PER_STEP + t

            @pl.when((step == 0) | (et_ref[j] != et_ref[jnp.maximum(j - TILES_PER_STEP, 0)]))
            def _():
                wb_refs[2 * t][...] = w_refs[2 * t][0, 0].astype(BF16)
                wb_refs[2 * t + 1][...] = w_refs[2 * t + 1][0, 0].astype(BF16)

        copies(step, slot, False)
        tiles = [slice(t * FFN_TILE, (t + 1) * FFN_TILE) for t in range(TILES_PER_STEP)]
        gus = [_dot(xt_ref[slot, rows, :], wb_refs[2 * t][...]) for t, rows in enumerate(tiles)]
        acts = [(_silu(gu[:, :EXPERT_FF]) * gu[:, EXPERT_FF:]).astype(BF16) for gu in gus]
        for t, (rows, act) in enumerate(zip(tiles, acts)):
            y_out[rows, :] = _dot(act, wb_refs[2 * t + 1][...]).astype(BF16)

    @pl.when(step * TILES_PER_STEP >= n_used)
    def _():
        y_out[...] = jnp.zeros(y_out.shape, BF16)


def _expert_ffn(plan, xp, w_gu, w_d, layer):
    xp_units = xp.reshape(-1, UNIT, D_MODEL)
    assert N_TILES_MAX % TILES_PER_STEP == 0

    def weight_specs(t):
        return [pl.BlockSpec((1, 1, D_MODEL, 2 * EXPERT_FF),
                             lambda s, src, et, nu, t=t: (layer, et[s * TILES_PER_STEP + t], 0, 0)),
                pl.BlockSpec((1, 1, EXPERT_FF, D_MODEL),
                             lambda s, src, et, nu, t=t: (layer, et[s * TILES_PER_STEP + t], 0, 0))]

    grid_spec = pltpu.PrefetchScalarGridSpec(
        num_scalar_prefetch=3,
        grid=(N_TILES_MAX // TILES_PER_STEP,),
        in_specs=[pl.BlockSpec(memory_space=pl.ANY)]
        + [spec for t in range(TILES_PER_STEP) for spec in weight_specs(t)],
        out_specs=pl.BlockSpec((TILES_PER_STEP * FFN_TILE, D_MODEL), lambda s, src, et, nu: (s, 0)),
        scratch_shapes=[
            pltpu.VMEM((D_MODEL, 2 * EXPERT_FF), BF16) if i % 2 == 0 else pltpu.VMEM((EXPERT_FF, D_MODEL), BF16)
            for i in range(2 * TILES_PER_STEP)
        ] + [pltpu.VMEM((2, TILES_PER_STEP * FFN_TILE, D_MODEL), BF16), pltpu.SemaphoreType.DMA((2,))],
    )
    return pl.pallas_call(
        _ffn_kernel,
        grid_spec=grid_spec,
        out_shape=jax.ShapeDtypeStruct((N_TILES_MAX * FFN_TILE, D_MODEL), BF16),
        compiler_params=pltpu.CompilerParams(
            dimension_semantics=("arbitrary",), vmem_limit_bytes=VMEM_LIMIT),
        name="moe_expert_ffn",
    )(plan["src"], plan["e_tile"], plan["n_used"], xp_units, *([w_gu, w_d] * TILES_PER_STEP))


def _combine_kernel(ysrc_ref, y_hbm, rank_ref, comb_ref, er_ref, rr_ref, h_ref, x1_ref, mod_ref,
                    wsgu_ref, wsd_ref, fg_ref, o_ref, yt_ref, sem, *, final):
    b = pl.program_id(0)
    slot = lax.rem(b, 2)

    def copies(bb, sl, start):
        _unit_copies(ysrc_ref, bb * UNITS_PER_BLOCK, y_hbm, yt_ref, sl, sem, UNITS_PER_BLOCK, start)

    @pl.when(b == 0)
    def _():
        copies(0, 0, True)

    @pl.when(b + 1 < pl.num_programs(0))
    def _():
        copies(b + 1, 1 - slot, True)

    gu = _dot(h_ref[...], wsgu_ref[...])
    shared = _dot((_silu(gu[:, :EXPERT_FF]) * gu[:, EXPERT_FF:]).astype(BF16), wsd_ref[...])

    sub_e = lax.broadcasted_iota(jnp.int32, (N_EXPERTS, GROUP_SLOTS), 0)
    groups = [slice(g * GROUP_SLOTS, (g + 1) * GROUP_SLOTS) for g in range(GROUPS_PER_BLOCK)]
    onehots = [jnp.where(sub_e == er_ref[0, :, cols], 1.0, 0.0).astype(BF16) for cols in groups]
    slot_ranks = [_dot(rank_ref[0], onehot_e) for onehot_e in onehots]
    slot_ws = [_dot(comb_ref[0], onehot_e) for onehot_e in onehots]
    weights = jnp.concatenate(
        [jnp.where(slot_rank == rr_ref[0, :, cols], slot_w, 0.0).astype(BF16)
         for cols, slot_rank, slot_w in zip(groups, slot_ranks, slot_ws)], axis=1)

    copies(b, slot, False)
    x2 = x1_ref[...] + mod_ref[0, 5:6, :] * (shared + _dot(weights, yt_ref[slot]))
    if final:
        ms = jnp.mean(x2 * x2, axis=-1, keepdims=True)
        x2 = x2 * lax.rsqrt(ms + EPS) * fg_ref[...]
    o_ref[...] = x2


def _combine(plan, y_sorted, comb_bt, h2, x1, mod_tile, ws_gu, ws_d, final_g, final):
    n_tok = h2.shape[0]
    nb = n_tok // MOE_BLOCK
    y_units = y_sorted.reshape(-1, UNIT, D_MODEL)

    def full(a):
        return pl.BlockSpec(a.shape, lambda b, ysrc: (0,) * a.ndim)

    grid_spec = pltpu.PrefetchScalarGridSpec(
        num_scalar_prefetch=1,
        grid=(nb,),
        in_specs=[
            pl.BlockSpec(memory_space=pl.ANY),
            pl.BlockSpec((1, MOE_BLOCK, N_EXPERTS), lambda b, ysrc: (b, 0, 0)),
            pl.BlockSpec((1, MOE_BLOCK, N_EXPERTS), lambda b, ysrc: (b, 0, 0)),
            pl.BlockSpec((1, 1, SLOT_MAX), lambda b, ysrc: (b, 0, 0)),
            pl.BlockSpec((1, 1, SLOT_MAX), lambda b, ysrc: (b, 0, 0)),
            pl.BlockSpec((MOE_BLOCK, D_MODEL), lambda b, ysrc: (b, 0)),
            pl.BlockSpec((MOE_BLOCK, D_MODEL), lambda b, ysrc: (b, 0)),
            pl.BlockSpec((1, 8, D_MODEL), lambda b, ysrc: (b * MOE_BLOCK // STEP, 0, 0)),
            full(ws_gu), full(ws_d), full(final_g),
        ],
        out_specs=pl.BlockSpec((MOE_BLOCK, D_MODEL), lambda b, ysrc: (b, 0)),
        scratch_shapes=[
            pltpu.VMEM((2, SLOT_MAX, D_MODEL), BF16),
            pltpu.SemaphoreType.DMA((2,)),
        ],
    )
    return pl.pallas_call(
        functools.partial(_combine_kernel, final=final),
        grid_spec=grid_spec,
        out_shape=jax.ShapeDtypeStruct((n_tok, D_MODEL), F32),
        compiler_params=pltpu.CompilerParams(
            dimension_semantics=("arbitrary",), vmem_limit_bytes=VMEM_LIMIT),
        name="moe_combine",
    )(plan["ysrc"], y_units, plan["rank_bt"], comb_bt,
      plan["e_slot"], plan["r_slot"], h2, x1, mod_tile, ws_gu, ws_d, final_g)


def _level_ids():
    t = np.arange(CHUNK)
    x = t[:, None] ^ t[None, :]
    lvl = np.zeros((CHUNK, CHUNK), np.int32)
    nz = x > 0
    lvl[nz] = np.floor(np.log2(x[nz])).astype(np.int32) + 1
    past_f = t[:, None] >= t[None, :]
    return jnp.asarray(np.where(past_f, lvl, -1)), jnp.asarray(np.where(past_f.T, lvl, -1))


def kernel(x_prompt, x_sample, state_hgrn, c, c_ctx, w_ada, b_ada, norm1_g, w_in, hgrn_lb_logits, hgrn_norm_g, w_proj_hgrn, conv_dw_w, conv_dw_b, conv_norm_g, conv_norm_b, w_proj_conv, w_out, norm2_g, w_router, router_bias, w_expert_gate_up, w_expert_down, w_shared_gate_up, w_shared_down, final_norm_g):
    n_ctx, ctx_len, d = x_prompt.shape
    n_lat, lat_len, _ = x_sample.shape
    depth = w_ada.shape[0]
    assert d == D_MODEL and ctx_len * 4 == STEP and lat_len == STEP and n_ctx % 4 == 0
    ctx_steps = n_ctx * ctx_len // STEP
    n_steps = ctx_steps + n_lat
    n_tok = n_steps * STEP
    assert n_tok == N_TOKENS and STEP % POST_TILE == 0 and STEP % MOE_BLOCK == 0
    tiles_per_step = STEP // POST_TILE

    x_all = jnp.concatenate([x_prompt.reshape(-1, d), x_sample.reshape(-1, d)], axis=0)

    cc = jnp.zeros((8, d), F32).at[:n_lat].set(c.astype(F32)).at[n_lat].set(c_ctx.astype(F32))
    mod = _ada(cc, w_ada, b_ada).reshape(depth, 8, N_MOD, d)
    step_src = np.array([n_lat] * ctx_steps + list(range(n_lat)))
    mod_step = jnp.pad(mod[:, step_src], ((0, 0), (0, 0), (0, 8 - N_MOD), (0, 0)))

    nc = jnp.asarray([ctx_len // CHUNK] * ctx_steps + [lat_len // CHUNK] * n_lat, jnp.int32)
    tile_pos = np.arange(n_lat * tiles_per_step) % tiles_per_step
    pv = jnp.asarray(np.concatenate([np.zeros(ctx_steps * tiles_per_step), tile_pos > 0]), jnp.int32)
    nv = jnp.asarray(np.concatenate([np.zeros(ctx_steps * tiles_per_step), tile_pos < tiles_per_step - 1]), jnp.int32)

    lbv = jnp.cumsum(jax.nn.softmax(hgrn_lb_logits.astype(F32), axis=0), axis=0)
    lbv = (lbv - lbv[:1]).reshape(depth, 2, N_HEADS, HEAD_DIM)
    lvl_f, lvl_b = _level_ids()

    xs = x_all
    ctx_states = []
    for l in range(depth):
        lb = lbv[l]
        gp = jnp.stack([jnp.log(lb[0]), jnp.log1p(-lb[0]), 1.0 - lb[0],
                        jnp.log(lb[1]), jnp.log1p(-lb[1]), 1.0 - lb[1],
                        hgrn_norm_g[l].reshape(N_HEADS, HEAD_DIM).astype(F32),
                        jnp.zeros((N_HEADS, HEAD_DIM), F32)], axis=1)
        s0 = jnp.concatenate([jnp.zeros((ctx_steps, 2, N_HEADS, HEAD_DIM, HEAD_DIM), F32),
                              state_hgrn[:, l].astype(F32)], axis=0)
        g1 = norm1_g[l].reshape(1, d).astype(F32)

        o_all, states = _scan(nc, xs, mod_step[l], g1, w_in, l, gp, s0, lvl_f, lvl_b)
        ctx_states.append(states[:ctx_steps].reshape(n_ctx, 2, N_HEADS, HEAD_DIM, HEAD_DIM))

        cw = jnp.pad(conv_dw_w[l].astype(F32), ((0, 1), (0, 0)))
        x1, h2, logits_t = _post(
            pv, nv, xs, mod_step[l], o_all, w_in, l, w_proj_hgrn[l], w_proj_conv[l], w_out[l],
            w_router[l].T.astype(F32), g1, cw,
            conv_dw_b[l].reshape(1, -1).astype(F32), conv_norm_g[l].reshape(1, -1).astype(F32),
            conv_norm_b[l].reshape(1, -1).astype(F32), norm2_g[l].reshape(1, d).astype(F32))

        comb_t, chosen_t = _route(logits_t, router_bias[l].reshape(N_EXPERTS, 1).astype(F32))
        plan = _routing_plan(chosen_t)
        comb_bt = comb_t.reshape(N_EXPERTS, n_tok // MOE_BLOCK, MOE_BLOCK).transpose(1, 2, 0).astype(BF16)
        xp = _permute(plan, h2)
        y_sorted = _expert_ffn(plan, xp, w_expert_gate_up, w_expert_down, l)
        xs = _combine(plan, y_sorted, comb_bt, h2, x1, mod_step[l], w_shared_gate_up[l].astype(BF16),
                      w_shared_down[l].astype(BF16), final_norm_g.reshape(1, d).astype(F32),
                      final=(l == depth - 1))

    n_ctx_tok = n_ctx * ctx_len
    y_prompt = xs[:n_ctx_tok].reshape(x_prompt.shape).astype(x_prompt.dtype)
    y_sample = xs[n_ctx_tok:].reshape(x_sample.shape).astype(x_sample.dtype)
    new_state = jnp.stack(ctx_states, axis=1).astype(x_prompt.dtype)
    return (y_prompt, y_sample, new_state)
```

```python
import functools

import numpy as np
import jax
import jax.numpy as jnp
from jax import lax
from jax.experimental import pallas as pl
from jax.experimental.pallas import tpu as pltpu

F32 = jnp.float32
BF16 = jnp.bfloat16
HIGHEST = lax.Precision.HIGHEST

D_MODEL = 1024
N_HEADS = 8
HEAD_DIM = 128
CONV_DIM = 512
CONV_WIDTH = 31
CONV_HALO = 16
N_EXPERTS = 64
N_GROUPS = 8
GROUP_SIZE = N_EXPERTS // N_GROUPS
TOPK_GROUPS = 4
TOP_K = 8
EXPERT_FF = 256
ROUTED_SCALE = 2.5
N_MOD = 6
EPS = 1e-6

CHUNK = 128
N_LEVELS = 7
LOCKSTEP = 2
STEP = 1024
POST_TILE = 256
MOE_BLOCK = 256
UNIT = 16
FFN_TILE = 256
UNITS_PER_TILE = FFN_TILE // UNIT
TILES_PER_STEP = 2
SLOT_MAX = MOE_BLOCK * TOP_K + N_EXPERTS * UNIT
UNITS_PER_BLOCK = SLOT_MAX // UNIT
GROUP_UNITS = 32
GROUP_SLOTS = GROUP_UNITS * UNIT
GROUPS_PER_BLOCK = UNITS_PER_BLOCK // GROUP_UNITS
N_TOKENS = 8192
N_TILES_MAX = (N_TOKENS * TOP_K // UNIT + (N_TOKENS // MOE_BLOCK) * N_EXPERTS) // UNITS_PER_TILE + N_EXPERTS
HEAD_COLS = 5 * HEAD_DIM
VMEM_LIMIT = 52 * 1024 * 1024


def _dot(a, b):
    return jnp.dot(a, b, preferred_element_type=F32)


def _dot_nt(a, b):
    return lax.dot_general(a, b, (((1,), (1,)), ((), ())), preferred_element_type=F32)


def _dot_tn(a, b):
    return lax.dot_general(a, b, (((0,), (0,)), ((), ())), preferred_element_type=F32)


def _sigmoid(x):
    return 1.0 / (1.0 + jnp.exp(-x))


def _silu(x):
    return x * _sigmoid(x)


def _rms_mod(x, g, scale, shift):
    ms = jnp.mean(x * x, axis=-1, keepdims=True)
    return x * lax.rsqrt(ms + EPS) * g * (1.0 + scale) + shift


def _ada_kernel(c_ref, w_ref, b_ref, o_ref):
    cc = c_ref[...]
    o_ref[0] = jnp.dot(_silu(cc), w_ref[0], preferred_element_type=F32, precision=HIGHEST) + b_ref[0]


def _ada(cc, w_ada, b_ada):
    depth, d, n = w_ada.shape
    tn = 1536
    return pl.pallas_call(
        _ada_kernel,
        grid=(depth, n // tn),
        in_specs=[
            pl.BlockSpec((8, d), lambda l, j: (0, 0)),
            pl.BlockSpec((1, d, tn), lambda l, j: (l, 0, j)),
            pl.BlockSpec((1, 1, tn), lambda l, j: (l, 0, j)),
        ],
        out_specs=pl.BlockSpec((1, 8, tn), lambda l, j: (l, 0, j)),
        out_shape=jax.ShapeDtypeStruct((depth, 8, n), F32),
        compiler_params=pltpu.CompilerParams(vmem_limit_bytes=VMEM_LIMIT),
        name="ada_mod",
    )(cc, w_ada, b_ada.reshape(depth, 1, n))


def _forget_gate(z, log_lb, log1m_lb, one_m_lb):
    e = jnp.exp(-jnp.abs(z))
    r = 1.0 / (1.0 + e)
    log_sig = jnp.minimum(z, 0.0) - jnp.log(1.0 + e)
    k = one_m_lb * jnp.where(z > 0, e * r, r)
    b = log1m_lb + log_sig
    log_f = jnp.maximum(log_lb, b) + jnp.log(1.0 + jnp.exp(-jnp.abs(log_lb - b)))
    return log_f, k


def _level_reference(cum_ref, blk, fwd):
    half = blk // 2
    pieces = []
    if blk >= 8:
        for i in range(CHUNK // blk):
            row = i * blk + (half - 1 if fwd else half)
            pieces.append(jnp.broadcast_to(cum_ref[row:row + 1, :], (blk, HEAD_DIM)))
    else:
        sub = lax.broadcasted_iota(jnp.int32, (8, HEAD_DIM), 0)
        for i in range(CHUNK // 8):
            lo_row = 8 * i + (1 if fwd else 2)
            hi_row = 8 * i + (5 if fwd else 6)
            lo = jnp.broadcast_to(cum_ref[lo_row:lo_row + 1, :], (8, HEAD_DIM))
            hi = jnp.broadcast_to(cum_ref[hi_row:hi_row + 1, :], (8, HEAD_DIM))
            pieces.append(jnp.where(sub >= 4, hi, lo))
    return pieces[0] if len(pieces) == 1 else jnp.concatenate(pieces, axis=0)


def _chunk_steps(chains):
    r_idx = lax.broadcasted_iota(jnp.int32, (CHUNK, CHUNK), 0)
    c_idx = lax.broadcasted_iota(jnp.int32, (CHUNK, CHUNK), 1)
    work = []
    for q, k, v, log_f, lvl, cum_ref, fwd in chains:
        tri = jnp.where((r_idx >= c_idx) if fwd else (r_idx <= c_idx), 1.0, 0.0).astype(BF16)
        hi = log_f.astype(BF16)
        lo = (log_f - hi.astype(F32)).astype(BF16)
        both = _dot(tri, jnp.concatenate([hi, lo], axis=1))
        cum = both[:, :HEAD_DIM] + both[:, HEAD_DIM:]
        cum_ref[...] = cum
        q_b = q.astype(BF16)
        k_b = k.astype(BF16)
        work.append(dict(cum=cum, q_b=q_b, k_b=k_b, scores=jnp.where(lvl == 0, _dot_nt(q_b, k_b), 0.0)))

    for lev in range(1, N_LEVELS + 1):
        for (q, k, v, log_f, lvl, cum_ref, fwd), w in zip(chains, work):
            if lev == 1:
                qe = w["q_b"] * jnp.exp(log_f).astype(BF16)
                ke = w["k_b"]
            else:
                e = jnp.exp(-jnp.abs(w["cum"] - _level_reference(cum_ref, 1 << lev, fwd))).astype(BF16)
                qe = w["q_b"] * e
                ke = w["k_b"] * e
            w["scores"] = jnp.where(lvl == lev, _dot_nt(qe, ke), w["scores"])

    def finish(chain, w, st):
        q, k, v, log_f, lvl, cum_ref, fwd = chain
        total = cum_ref[CHUNK - 1:CHUNK, :] if fwd else cum_ref[0:1, :]
        o = (_dot(w["scores"].astype(BF16), v.astype(BF16))
             + _dot_nt((q * jnp.exp(w["cum"])).astype(BF16), st.astype(BF16)))
        k_st = k * jnp.exp(total - w["cum"])
        return o, st * jnp.exp(total) + _dot_tn(v.astype(BF16), k_st.astype(BF16))

    return [functools.partial(finish, chain, w) for chain, w in zip(chains, work)]


def _scan_kernel(nc_ref, x_ref, mod_ref, g1_ref, wq_ref, wv_ref, wf_ref, wb_ref, wg_ref, gp_ref, s0_ref,
                 lvlf_ref, lvlb_ref, o_out, st_out, h_ref, w_ref, z_ref, of_ref, ob_ref, cum_ref):
    pair = pl.program_id(0)
    step = pl.program_id(1)
    sub = pl.program_id(2)
    n_seq_chunks = nc_ref[step]
    n_chunks = STEP // CHUNK

    @pl.when((pair == 0) & (sub == 0))
    def _():
        g1 = g1_ref[...]
        shift = mod_ref[0, 0:1, :]
        scale = mod_ref[0, 1:2, :]

        for i in range(n_chunks):
            rows = pl.ds(i * CHUNK, CHUNK)
            h_ref[step, rows, :] = _rms_mod(x_ref[rows, :], g1, scale, shift).astype(BF16)

    @pl.when((step == 0) & (sub == 0))
    def _():
        for j, wj_ref in enumerate((wq_ref, wv_ref, wf_ref, wb_ref, wg_ref)):
            for hh in range(2):
                w_ref[:, hh * HEAD_COLS + j * HEAD_DIM:hh * HEAD_COLS + (j + 1) * HEAD_DIM] = (
                    wj_ref[0, :, hh * HEAD_DIM:(hh + 1) * HEAD_DIM].astype(BF16))

    @pl.when(sub == 0)
    def _():
        z = _dot(h_ref[step], w_ref[...])
        z_ref[0] = z[:, :HEAD_COLS]
        z_ref[1] = z[:, HEAD_COLS:]

    st_out[...] = jnp.zeros(st_out.shape, F32)
    st0 = (s0_ref[0, 0, 0].T, s0_ref[0, 1, 0].T)

    def chain(c, fwd, cum_ref):
        d = 0 if fwd else 1
        zcol = (2 if fwd else 3) * HEAD_DIM
        rows = pl.ds(c * CHUNK, CHUNK)
        log_f, k = _forget_gate(z_ref[sub, rows, zcol:zcol + HEAD_DIM], gp_ref[0, 3 * d:3 * d + 1, :],
                                gp_ref[0, 3 * d + 1:3 * d + 2, :], gp_ref[0, 3 * d + 2:3 * d + 3, :])
        return (z_ref[sub, rows, 0:HEAD_DIM], k, z_ref[sub, rows, HEAD_DIM:2 * HEAD_DIM], log_f,
                (lvlf_ref if fwd else lvlb_ref)[...], cum_ref, fwd)

    carry = list(st0)
    for i0 in range(0, n_chunks, LOCKSTEP):
        todo = [(i if fwd else n_chunks - 1 - i, fwd) for i in range(i0, i0 + LOCKSTEP) for fwd in (True, False)]
        finishers = _chunk_steps([chain(c, fwd, cum_ref.at[n]) for n, (c, fwd) in enumerate(todo)])
        for (c, fwd), finish_chain in zip(todo, finishers):
            d = 0 if fwd else 1
            pos = lax.rem(c, n_seq_chunks)
            first = (pos == 0) if fwd else (pos == n_seq_chunks - 1)
            o, carry[d] = finish_chain(jnp.where(first, st0[d], carry[d]))
            (of_ref if fwd else ob_ref)[pl.ds(c * CHUNK, CHUNK), :] = o
            st_out[0, lax.div(c, n_seq_chunks), d, 0] = carry[d].T

    norm_g = gp_ref[0, 6:7, :]

    chunk_rows = [pl.ds(i * CHUNK, CHUNK) for i in range(n_chunks)]
    os_ = [of_ref[rows, :] + ob_ref[rows, :] for rows in chunk_rows]
    invs = [lax.rsqrt(jnp.mean(o * o, axis=-1, keepdims=True) + EPS) for o in os_]
    for rows, o, inv in zip(chunk_rows, os_, invs):
        og = z_ref[sub, rows, 4 * HEAD_DIM:5 * HEAD_DIM]
        o_out[rows, :] = (o * inv * norm_g * _silu(og)).astype(BF16)


def _scan(nc, x_all, mod_step, g1, w_in, layer, gp, s0, lvl_f, lvl_b):
    n_tok = x_all.shape[0]
    n_steps = n_tok // STEP
    grid_spec = pltpu.PrefetchScalarGridSpec(
        num_scalar_prefetch=1,
        grid=(N_HEADS // 2, n_steps, 2),
        in_specs=[
            pl.BlockSpec((STEP, D_MODEL), lambda p, s, u, nc: (jnp.where((p == 0) & (u == 0), s, 0), 0)),
            pl.BlockSpec((1, 8, D_MODEL), lambda p, s, u, nc: (s, 0, 0)),
            pl.BlockSpec((1, D_MODEL), lambda p, s, u, nc: (0, 0)),
        ] + [
            pl.BlockSpec((1, D_MODEL, 2 * HEAD_DIM), lambda p, s, u, nc, j=j: (layer, 0, j * (N_HEADS // 2) + p))
            for j in range(5)
        ] + [
            pl.BlockSpec((1, 8, HEAD_DIM), lambda p, s, u, nc: (2 * p + u, 0, 0)),
            pl.BlockSpec((1, 2, 1, HEAD_DIM, HEAD_DIM), lambda p, s, u, nc: (s, 0, 2 * p + u, 0, 0)),
            pl.BlockSpec((CHUNK, CHUNK), lambda p, s, u, nc: (0, 0)),
            pl.BlockSpec((CHUNK, CHUNK), lambda p, s, u, nc: (0, 0)),
        ],
        out_specs=[
            pl.BlockSpec((STEP, HEAD_DIM), lambda p, s, u, nc: (s, 2 * p + u)),
            pl.BlockSpec((1, 4, 2, 1, HEAD_DIM, HEAD_DIM), lambda p, s, u, nc: (s, 0, 0, 2 * p + u, 0, 0)),
        ],
        scratch_shapes=[
            pltpu.VMEM((n_steps, STEP, D_MODEL), BF16),
            pltpu.VMEM((D_MODEL, 2 * HEAD_COLS), BF16),
            pltpu.VMEM((2, STEP, HEAD_COLS), F32),
            pltpu.VMEM((STEP, HEAD_DIM), F32),
            pltpu.VMEM((STEP, HEAD_DIM), F32),
            pltpu.VMEM((2 * LOCKSTEP, CHUNK, HEAD_DIM), F32),
        ],
    )
    return pl.pallas_call(
        _scan_kernel,
        grid_spec=grid_spec,
        out_shape=[
            jax.ShapeDtypeStruct((n_tok, D_MODEL), BF16),
            jax.ShapeDtypeStruct((n_steps, 4, 2, N_HEADS, HEAD_DIM, HEAD_DIM), F32),
        ],
        compiler_params=pltpu.CompilerParams(
            dimension_semantics=("arbitrary", "arbitrary", "arbitrary"), vmem_limit_bytes=VMEM_LIMIT),
        name="hgrn_scan",
    )(nc, x_all, mod_step, g1, *([w_in] * 5), gp, s0, lvl_f, lvl_b)


def _post_kernel(pv_ref, nv_ref, x_ref, xp_ref, xn_ref, mod_ref, o_ref, wu_f32, wgz_f32, wph_f32,
                 wpc_f32, wo_f32, wr_ref, g1_ref, cw_ref, cb_ref, lg_ref, lb_ref, g2_ref,
                 x1_out, h2_out, lg_out, cu_ref, wu_ref, wgz_ref, wph_ref, wpc_ref, wo_ref):
    i = pl.program_id(0)

    @pl.when(i == 0)
    def _():
        for dst, src in ((wu_ref, wu_f32), (wgz_ref, wgz_f32), (wph_ref, wph_f32),
                         (wpc_ref, wpc_f32), (wo_ref, wo_f32)):
            dst[...] = (src[0] if len(src.shape) == 3 else src[...]).astype(BF16)

    g1 = g1_ref[...]
    shift1 = mod_ref[0, 0:1, :]
    scale1 = mod_ref[0, 1:2, :]
    gate1 = mod_ref[0, 2:3, :]
    shift2 = mod_ref[0, 3:4, :]
    scale2 = mod_ref[0, 4:5, :]

    def glu(xv):
        h = _rms_mod(xv, g1, scale1, shift1).astype(BF16)
        u = _dot(h, wu_ref[...])
        return h, u[:, :CONV_DIM] * _sigmoid(u[:, CONV_DIM:])

    x = x_ref[...]
    h, glu_mid = glu(x)
    _, glu_prev = glu(xp_ref[...])
    _, glu_next = glu(xn_ref[...])
    cu_ref[0, 0:CONV_HALO, :] = glu_prev * pv_ref[i].astype(F32)
    cu_ref[0, CONV_HALO:CONV_HALO + POST_TILE, :] = glu_mid
    cu_ref[0, CONV_HALO + POST_TILE:, :] = glu_next * nv_ref[i].astype(F32)
    n_keep = POST_TILE + 2 * CONV_HALO - 8
    for k in range(1, 8):
        cu_ref[k, 0:n_keep, :] = cu_ref[0, k:k + n_keep, :]

    off = CONV_HALO - CONV_WIDTH // 2
    acc = jnp.zeros((POST_TILE, CONV_DIM), F32) + cb_ref[...]
    for j in range(CONV_WIDTH):
        base, k = divmod(off + j, 8)
        acc = acc + cu_ref[k, 8 * base:8 * base + POST_TILE, :] * cw_ref[j:j + 1, :]
    mu = jnp.mean(acc, axis=-1, keepdims=True)
    cen = acc - mu
    var = jnp.mean(cen * cen, axis=-1, keepdims=True)
    cv = cen * lax.rsqrt(var + EPS) * lg_ref[...] + lb_ref[...]
    y_b = _dot(_silu(cv).astype(BF16), wpc_ref[...])

    y_a = _dot(o_ref[...], wph_ref[...])
    gz = _sigmoid(_dot(h, wgz_ref[...]))
    merged = gz[:, :D_MODEL] * y_a + gz[:, D_MODEL:] * y_b
    x1 = x + gate1 * _dot(merged.astype(BF16), wo_ref[...])
    x1_out[...] = x1
    h2 = _rms_mod(x1, g2_ref[...], scale2, shift2)
    h2_hi = h2.astype(BF16)
    h2_out[...] = h2_hi
    h2_lo = (h2 - h2_hi.astype(F32)).astype(BF16)
    part = _dot_nt(wr_ref[...], h2_hi)
    lg_out[...] = part[:N_EXPERTS] + part[N_EXPERTS:] + _dot_nt(wr_ref[:N_EXPERTS, :], h2_lo)


def _post(pv, nv, x_all, mod_step, o_all, w_in, layer, w_ph, w_pc, w_o, w_rt, g1, cw, cb, lg, lb, g2):
    n_tok = x_all.shape[0]
    n_tiles = n_tok // POST_TILE
    halo_per_tile = POST_TILE // CONV_HALO
    n_halo_blocks = n_tok // CONV_HALO
    glu_cols, gate_cols = 2 * CONV_DIM, 2 * D_MODEL
    glu_start = 5 * N_HEADS * HEAD_DIM
    assert glu_start % glu_cols == 0 and (glu_start + glu_cols) % gate_cols == 0

    def full(a):
        return pl.BlockSpec(a.shape, lambda i, pv, nv: (0,) * a.ndim)

    def resident(shape, index):
        return pl.BlockSpec(shape, lambda i, pv, nv: index, pipeline_mode=pl.Buffered(1))

    grid_spec = pltpu.PrefetchScalarGridSpec(
        num_scalar_prefetch=2,
        grid=(n_tiles,),
        in_specs=[
            pl.BlockSpec((POST_TILE, D_MODEL), lambda i, pv, nv: (i, 0)),
            pl.BlockSpec((CONV_HALO, D_MODEL), lambda i, pv, nv: (jnp.maximum(i * halo_per_tile - 1, 0), 0)),
            pl.BlockSpec((CONV_HALO, D_MODEL),
                         lambda i, pv, nv: (jnp.minimum((i + 1) * halo_per_tile, n_halo_blocks - 1), 0)),
            pl.BlockSpec((1, 8, D_MODEL), lambda i, pv, nv: (i * POST_TILE // STEP, 0, 0)),
            pl.BlockSpec((POST_TILE, D_MODEL), lambda i, pv, nv: (i, 0)),
            resident((1, D_MODEL, glu_cols), (layer, 0, glu_start // glu_cols)),
            resident((1, D_MODEL, gate_cols), (layer, 0, (glu_start + glu_cols) // gate_cols)),
            resident(w_ph.shape, (0, 0)), resident(w_pc.shape, (0, 0)), resident(w_o.shape, (0, 0)),
            full(w_rt), full(g1), full(cw), full(cb), full(lg), full(lb), full(g2),
        ],
        out_specs=[
            pl.BlockSpec((POST_TILE, D_MODEL), lambda i, pv, nv: (i, 0)),
            pl.BlockSpec((POST_TILE, D_MODEL), lambda i, pv, nv: (i, 0)),
            pl.BlockSpec((N_EXPERTS, POST_TILE), lambda i, pv, nv: (0, i)),
        ],
        scratch_shapes=[
            pltpu.VMEM((8, POST_TILE + 2 * CONV_HALO, CONV_DIM), F32),
            pltpu.VMEM((D_MODEL, glu_cols), BF16), pltpu.VMEM((D_MODEL, gate_cols), BF16),
            pltpu.VMEM(w_ph.shape, BF16), pltpu.VMEM(w_pc.shape, BF16), pltpu.VMEM(w_o.shape, BF16),
        ],
    )
    return pl.pallas_call(
        _post_kernel,
        grid_spec=grid_spec,
        out_shape=[
            jax.ShapeDtypeStruct((n_tok, D_MODEL), F32),
            jax.ShapeDtypeStruct((n_tok, D_MODEL), BF16),
            jax.ShapeDtypeStruct((N_EXPERTS, n_tok), F32),
        ],
        compiler_params=pltpu.CompilerParams(
            dimension_semantics=("arbitrary",), vmem_limit_bytes=VMEM_LIMIT),
        name="post_mixer",
    )(pv, nv, x_all, x_all, x_all, mod_step, o_all, w_in, w_in, w_ph, w_pc, w_o, w_rt, g1, cw, cb, lg, lb, g2)


def _route_kernel(lg_ref, bias_ref, comb_out, chosen_out):
    n = lg_ref.shape[1]
    scores = _sigmoid(lg_ref[...])
    sel = scores + bias_ref[...]
    neg = jnp.float32(-jnp.inf)

    sel3 = sel.reshape(N_GROUPS, GROUP_SIZE, n)
    m1 = jnp.max(sel3, axis=1, keepdims=True)
    is_m1 = sel3 == m1
    n_m1 = jnp.sum(is_m1.astype(F32), axis=1, keepdims=True)
    m2 = jnp.max(jnp.where(is_m1, neg, sel3), axis=1, keepdims=True)
    grp = (m1 + jnp.where(n_m1 > 1.5, m1, m2)).reshape(N_GROUPS, n)

    gidx = lax.broadcasted_iota(jnp.int32, (N_GROUPS, n), 0)
    rank = jnp.zeros((N_GROUPS, n), F32)
    for g in range(N_GROUPS):
        other = grp[g:g + 1, :]
        ahead = (other > grp) | ((other == grp) & (g < gidx))
        rank = rank + ahead.astype(F32)
    keep_g = rank < TOPK_GROUPS - 0.5
    keep = jnp.broadcast_to(keep_g.reshape(N_GROUPS, 1, n), (N_GROUPS, GROUP_SIZE, n)).reshape(N_EXPERTS, n)
    cand = jnp.where(keep, sel, neg)

    eidx = lax.broadcasted_iota(jnp.int32, (N_EXPERTS, n), 0)
    chosen = jnp.zeros((N_EXPERTS, n), F32)
    for _ in range(TOP_K):
        best = jnp.max(cand, axis=0, keepdims=True)
        first = jnp.min(jnp.where(cand == best, eidx, N_EXPERTS), axis=0, keepdims=True)
        hit = eidx == first
        chosen = jnp.where(hit, 1.0, chosen)
        cand = jnp.where(hit, neg, cand)
    w = scores * chosen
    comb_out[...] = w / jnp.sum(w, axis=0, keepdims=True) * ROUTED_SCALE
    chosen_out[...] = chosen


def _route(logits_t, bias):
    n_tok = logits_t.shape[1]
    tile = 512
    return pl.pallas_call(
        _route_kernel,
        grid=(n_tok // tile,),
        in_specs=[
            pl.BlockSpec((N_EXPERTS, tile), lambda i: (0, i)),
            pl.BlockSpec((N_EXPERTS, 1), lambda i: (0, 0)),
        ],
        out_specs=[pl.BlockSpec((N_EXPERTS, tile), lambda i: (0, i)),
                   pl.BlockSpec((N_EXPERTS, tile), lambda i: (0, i))],
        out_shape=[jax.ShapeDtypeStruct((N_EXPERTS, n_tok), F32),
                   jax.ShapeDtypeStruct((N_EXPERTS, n_tok), F32)],
        name="router",
    )(logits_t, bias)


def _count_le(bounds, idx):
    return jnp.sum(jnp.where(bounds <= idx, 1.0, 0.0), axis=0, keepdims=True)


def _pick_row(onehot, col):
    return jnp.sum(jnp.where(onehot, col, 0.0), axis=0, keepdims=True)


def _plan_kernel(ch_ref, rankt_out, es_out, rs_out, src_out, et_out, ysrc_out, misc_out):
    nb = N_TOKENS // MOE_BLOCK
    n_src = N_TILES_MAX * UNITS_PER_TILE
    e_col = lax.broadcasted_iota(jnp.int32, (N_EXPERTS, 1), 0).astype(F32)
    blk_lane = lax.broadcasted_iota(jnp.int32, (N_EXPERTS, 128), 1)
    t_r = lax.broadcasted_iota(jnp.int32, (MOE_BLOCK, MOE_BLOCK), 0)
    t_c = lax.broadcasted_iota(jnp.int32, (MOE_BLOCK, MOE_BLOCK), 1)
    earlier = jnp.where(t_c < t_r, 1.0, 0.0).astype(BF16)
    e_r = lax.broadcasted_iota(jnp.int32, (N_EXPERTS, N_EXPERTS), 0)
    e_c = lax.broadcasted_iota(jnp.int32, (N_EXPERTS, N_EXPERTS), 1)
    upto = jnp.where(e_c <= e_r, 1.0, 0.0).astype(F32)
    slot = lax.broadcasted_iota(jnp.int32, (1, SLOT_MAX), 1).astype(F32)

    def col_cumsum(col):
        wide = jnp.broadcast_to(col, (N_EXPERTS, 128))
        return jnp.dot(upto, wide, preferred_element_type=F32, precision=HIGHEST)[:, 0:1]

    def block_stats(b, carry):
        units_mat, start_mat = carry
        cb = ch_ref[:, pl.ds(pl.multiple_of(b * MOE_BLOCK, MOE_BLOCK), MOE_BLOCK)]
        rank_t = _dot_nt(earlier, cb.astype(BF16))
        cb_t = cb.T
        rankt_out[b] = jnp.where(cb_t > 0, rank_t, -1.0).astype(BF16)
        cnt = jnp.sum(cb, axis=1, keepdims=True)
        units = jnp.floor((cnt + (UNIT - 1)) * (1.0 / UNIT))
        incl = col_cumsum(units)
        start = incl - units
        e_slot = _count_le(incl * UNIT, slot)
        onehot = e_col == e_slot
        r = slot - _pick_row(onehot, start * UNIT)
        valid = (e_slot < N_EXPERTS - 0.5) & (r < _pick_row(onehot, cnt))
        es_out[b] = jnp.where(valid, e_slot, -1.0).astype(jnp.int32)
        rs_out[b] = jnp.where(valid, r, -2.0)
        units_mat = jnp.where(blk_lane == b, units, units_mat)
        start_mat = jnp.where(blk_lane == b, start, start_mat)
        return units_mat, start_mat

    zeros = jnp.zeros((N_EXPERTS, 128), F32)
    units_mat, start_mat = lax.fori_loop(0, nb, block_stats, (zeros, zeros))

    tot = jnp.sum(units_mat, axis=1, keepdims=True)
    tiles_e = jnp.floor((tot + (UNITS_PER_TILE - 1)) * (1.0 / UNITS_PER_TILE))
    incl_t = col_cumsum(tiles_e)
    start_t = incl_t - tiles_e
    n_used = incl_t[N_EXPERTS - 1:N_EXPERTS, :]
    b_r = lax.broadcasted_iota(jnp.int32, (128, 128), 0)
    b_c = lax.broadcasted_iota(jnp.int32, (128, 128), 1)
    before = jnp.where(b_r < b_c, 1.0, 0.0).astype(F32)
    cum_b = jnp.dot(units_mat, before, preferred_element_type=F32, precision=HIGHEST)
    run_pos = start_t * UNITS_PER_TILE + cum_b

    tile_idx = lax.broadcasted_iota(jnp.int32, (1, 512), 1).astype(F32)
    et_out[...] = jnp.minimum(_count_le(incl_t, tile_idx), N_EXPERTS - 1.0).astype(jnp.int32)

    def split_rows(m):
        hi = m.astype(BF16)
        return [hi, (m - hi.astype(F32)).astype(BF16)]

    tables = jnp.concatenate(split_rows((cum_b + units_mat).T) + split_rows(cum_b.T) + split_rows(start_mat.T),
                             axis=0)
    b_col = lax.broadcasted_iota(jnp.int32, (128, 1), 0)
    chunk = 1024

    def src_chunk(c, carry):
        off = pl.multiple_of(c * chunk, chunk)
        p = (lax.broadcasted_iota(jnp.int32, (1, chunk), 1) + off).astype(F32)
        e_p = jnp.minimum(_count_le(incl_t * UNITS_PER_TILE, p), N_EXPERTS - 1.0)
        onehot_e = e_col == e_p
        q = p - _pick_row(onehot_e, start_t * UNITS_PER_TILE)
        valid = (p < n_used * UNITS_PER_TILE) & (q < _pick_row(onehot_e, tot))
        picked = _dot(tables, jnp.where(onehot_e, 1.0, 0.0).astype(BF16))
        incl_sel = picked[0:128] + picked[128:256]
        cum_sel = picked[256:384] + picked[384:512]
        start_sel = picked[512:640] + picked[640:768]
        b_p = jnp.sum(jnp.where((incl_sel <= q) & (b_col < nb), 1.0, 0.0), axis=0, keepdims=True)
        b_p = jnp.minimum(b_p, nb - 1.0)
        onehot_b = b_col.astype(F32) == b_p
        unit = b_p * UNITS_PER_BLOCK + _pick_row(onehot_b, start_sel) + q - _pick_row(onehot_b, cum_sel)
        src_out[:, pl.ds(off, chunk)] = jnp.where(valid, unit, 0.0).astype(jnp.int32)
        return carry

    lax.fori_loop(0, n_src // chunk, src_chunk, 0)

    unit_idx = lax.broadcasted_iota(jnp.int32, (1, 256), 1).astype(F32)
    misc_lane = lax.broadcasted_iota(jnp.int32, (1, 128), 1)

    def block_units(b, used):
        here = blk_lane == b
        units = jnp.sum(jnp.where(here, units_mat, 0.0), axis=1, keepdims=True)
        start = jnp.sum(jnp.where(here, start_mat, 0.0), axis=1, keepdims=True)
        pos = jnp.sum(jnp.where(here, run_pos, 0.0), axis=1, keepdims=True)
        incl = start + units
        e_unit = _count_le(incl, unit_idx)
        onehot = e_col == e_unit
        val = _pick_row(onehot, pos - start) + unit_idx
        ysrc_out[b] = jnp.where(e_unit < N_EXPERTS - 0.5, val, 0.0).astype(jnp.int32)
        return jnp.where(misc_lane == b, incl[N_EXPERTS - 1:N_EXPERTS, :], used)

    used = lax.fori_loop(0, nb, block_units, jnp.zeros((1, 128), F32))
    misc_out[0:1, :] = used.astype(jnp.int32)
    misc_out[1:2, :] = jnp.broadcast_to(n_used, (1, 128)).astype(jnp.int32)
    misc_out[2:8, :] = jnp.zeros((6, 128), jnp.int32)


def _routing_plan(chosen):
    nb = N_TOKENS // MOE_BLOCK
    n_src = N_TILES_MAX * UNITS_PER_TILE
    assert n_src % 1024 == 0 and N_TILES_MAX <= 512 and nb <= 128 and UNITS_PER_BLOCK <= 256
    rank_bt, e_slot, r_slot, src, e_tile, ysrc, misc = pl.pallas_call(
        _plan_kernel,
        out_shape=[
            jax.ShapeDtypeStruct((nb, MOE_BLOCK, N_EXPERTS), BF16),
            jax.ShapeDtypeStruct((nb, 1, SLOT_MAX), jnp.int32),
            jax.ShapeDtypeStruct((nb, 1, SLOT_MAX), F32),
            jax.ShapeDtypeStruct((1, n_src), jnp.int32),
            jax.ShapeDtypeStruct((1, 512), jnp.int32),
            jax.ShapeDtypeStruct((nb, 1, 256), jnp.int32),
            jax.ShapeDtypeStruct((8, 128), jnp.int32),
        ],
        compiler_params=pltpu.CompilerParams(vmem_limit_bytes=VMEM_LIMIT),
        name="moe_plan",
    )(chosen)
    return dict(rank_bt=rank_bt, e_slot=e_slot, r_slot=r_slot, src=src.reshape(-1),
                e_tile=e_tile[0, :N_TILES_MAX], n_used=misc[1, :1],
                ysrc=ysrc[:, 0, :UNITS_PER_BLOCK].reshape(-1))


def _permute_kernel(h_ref, rank_ref, es_ref, rs_ref, xp_out):
    sub_e = lax.broadcasted_iota(jnp.int32, (N_EXPERTS, GROUP_SLOTS // 2), 0)
    tiles = [slice(i * GROUP_SLOTS // 2, (i + 1) * GROUP_SLOTS // 2) for i in range(2 * GROUPS_PER_BLOCK)]
    picks = []
    for rows in tiles:
        onehot_e = jnp.where(sub_e == es_ref[0, :, rows], 1.0, 0.0).astype(BF16)
        slot_rank = _dot(rank_ref[0], onehot_e)
        picks.append(jnp.where(slot_rank == rs_ref[0, :, rows], 1.0, 0.0).astype(BF16))
    for rows, pick in zip(tiles, picks):
        xp_out[0, rows, :] = _dot_tn(pick, h_ref[...]).astype(BF16)


def _permute(plan, h2):
    nb = h2.shape[0] // MOE_BLOCK
    return pl.pallas_call(
        _permute_kernel,
        grid=(nb,),
        in_specs=[
            pl.BlockSpec((MOE_BLOCK, D_MODEL), lambda b: (b, 0)),
            pl.BlockSpec((1, MOE_BLOCK, N_EXPERTS), lambda b: (b, 0, 0)),
            pl.BlockSpec((1, 1, SLOT_MAX), lambda b: (b, 0, 0)),
            pl.BlockSpec((1, 1, SLOT_MAX), lambda b: (b, 0, 0)),
        ],
        out_specs=pl.BlockSpec((1, SLOT_MAX, D_MODEL), lambda b: (b, 0, 0)),
        out_shape=jax.ShapeDtypeStruct((nb, SLOT_MAX, D_MODEL), BF16),
        compiler_params=pltpu.CompilerParams(
            dimension_semantics=("arbitrary",), vmem_limit_bytes=VMEM_LIMIT),
        name="moe_permute",
    )(h2, plan["rank_bt"], plan["e_slot"], plan["r_slot"])


def _unit_copies(table_ref, base, src_hbm, dst_ref, slot, sem, n_units, start):
    for u in range(n_units):
        copy = pltpu.make_async_copy(src_hbm.at[table_ref[base + u]],
                                     dst_ref.at[slot, pl.ds(u * UNIT, UNIT), :], sem.at[slot])
        if start:
            copy.start()
        else:
            copy.wait()


def _ffn_kernel(src_ref, et_ref, nu_ref, xp_hbm, *refs):
    w_refs = refs[:2 * TILES_PER_STEP]
    y_out = refs[2 * TILES_PER_STEP]
    scratch = refs[2 * TILES_PER_STEP + 1:]
    wb_refs, xt_ref, sem = scratch[:2 * TILES_PER_STEP], scratch[2 * TILES_PER_STEP], scratch[2 * TILES_PER_STEP + 1]
    n_units = TILES_PER_STEP * UNITS_PER_TILE
    step = pl.program_id(0)
    slot = lax.rem(step, 2)
    n_used = nu_ref[0]

    def copies(s, sl, start):
        _unit_copies(src_ref, s * n_units, xp_hbm, xt_ref, sl, sem, n_units, start)

    @pl.when((step == 0) & (n_used > 0))
    def _():
        copies(0, 0, True)

    @pl.when((step + 1 < pl.num_programs(0)) & ((step + 1) * TILES_PER_STEP < n_used))
    def _():
        copies(step + 1, 1 - slot, True)

    @pl.when(step * TILES_PER_STEP < n_used)
    def _():
        for t in range(TILES_PER_STEP):
            j = step * TILES_PER_STEP + t

            @pl.when((step == 0) | (et_ref[j] != et_ref[jnp.maximum(j - TILES_PER_STEP, 0)]))
            def _():
                wb_refs[2 * t][...] = w_refs[2 * t][0, 0].astype(BF16)
                wb_refs[2 * t + 1][...] = w_refs[2 * t + 1][0, 0].astype(BF16)

        copies(step, slot, False)
        tiles = [slice(t * FFN_TILE, (t + 1) * FFN_TILE) for t in range(TILES_PER_STEP)]
        gus = [_dot(xt_ref[slot, rows, :], wb_refs[2 * t][...]) for t, rows in enumerate(tiles)]
        acts = [(_silu(gu[:, :EXPERT_FF]) * gu[:, EXPERT_FF:]).astype(BF16) for gu in gus]
        for t, (rows, act) in enumerate(zip(tiles, acts)):
            y_out[rows, :] = _dot(act, wb_refs[2 * t + 1][...]).astype(BF16)

    @pl.when(step * TILES_PER_STEP >= n_used)
    def _():
        y_out[...] = jnp.zeros(y_out.shape, BF16)


def _expert_ffn(plan, xp, w_gu, w_d, layer):
    xp_units = xp.reshape(-1, UNIT, D_MODEL)
    assert N_TILES_MAX % TILES_PER_STEP == 0

    def weight_specs(t):
        return [pl.BlockSpec((1, 1, D_MODEL, 2 * EXPERT_FF),
                             lambda s, src, et, nu, t=t: (layer, et[s * TILES_PER_STEP + t], 0, 0)),
                pl.BlockSpec((1, 1, EXPERT_FF, D_MODEL),
                             lambda s, src, et, nu, t=t: (layer, et[s * TILES_PER_STEP + t], 0, 0))]

    grid_spec = pltpu.PrefetchScalarGridSpec(
        num_scalar_prefetch=3,
        grid=(N_TILES_MAX // TILES_PER_STEP,),
        in_specs=[pl.BlockSpec(memory_space=pl.ANY)]
        + [spec for t in range(TILES_PER_STEP) for spec in weight_specs(t)],
        out_specs=pl.BlockSpec((TILES_PER_STEP * FFN_TILE, D_MODEL), lambda s, src, et, nu: (s, 0)),
        scratch_shapes=[
            pltpu.VMEM((D_MODEL, 2 * EXPERT_FF), BF16) if i % 2 == 0 else pltpu.VMEM((EXPERT_FF, D_MODEL), BF16)
            for i in range(2 * TILES_PER_STEP)
        ] + [pltpu.VMEM((2, TILES_PER_STEP * FFN_TILE, D_MODEL), BF16), pltpu.SemaphoreType.DMA((2,))],
    )
    return pl.pallas_call(
        _ffn_kernel,
        grid_spec=grid_spec,
        out_shape=jax.ShapeDtypeStruct((N_TILES_MAX * FFN_TILE, D_MODEL), BF16),
        compiler_params=pltpu.CompilerParams(
            dimension_semantics=("arbitrary",), vmem_limit_bytes=VMEM_LIMIT),
        name="moe_expert_ffn",
    )(plan["src"], plan["e_tile"], plan["n_used"], xp_units, *([w_gu, w_d] * TILES_PER_STEP))


def _combine_kernel(ysrc_ref, y_hbm, rank_ref, comb_ref, er_ref, rr_ref, h_ref, x1_ref, mod_ref,
                    wsgu_ref, wsd_ref, fg_ref, o_ref, yt_ref, sem, *, final):
    b = pl.program_id(0)
    slot = lax.rem(b, 2)

    def copies(bb, sl, start):
        _unit_copies(ysrc_ref, bb * UNITS_PER_BLOCK, y_hbm, yt_ref, sl, sem, UNITS_PER_BLOCK, start)

    @pl.when(b == 0)
    def _():
        copies(0, 0, True)

    @pl.when(b + 1 < pl.num_programs(0))
    def _():
        copies(b + 1, 1 - slot, True)

    gu = _dot(h_ref[...], wsgu_ref[...])
    shared = _dot((_silu(gu[:, :EXPERT_FF]) * gu[:, EXPERT_FF:]).astype(BF16), wsd_ref[...])

    sub_e = lax.broadcasted_iota(jnp.int32, (N_EXPERTS, GROUP_SLOTS), 0)
    groups = [slice(g * GROUP_SLOTS, (g + 1) * GROUP_SLOTS) for g in range(GROUPS_PER_BLOCK)]
    onehots = [jnp.where(sub_e == er_ref[0, :, cols], 1.0, 0.0).astype(BF16) for cols in groups]
    slot_ranks = [_dot(rank_ref[0], onehot_e) for onehot_e in onehots]
    slot_ws = [_dot(comb_ref[0], onehot_e) for onehot_e in onehots]
    weights = jnp.concatenate(
        [jnp.where(slot_rank == rr_ref[0, :, cols], slot_w, 0.0).astype(BF16)
         for cols, slot_rank, slot_w in zip(groups, slot_ranks, slot_ws)], axis=1)

    copies(b, slot, False)
    x2 = x1_ref[...] + mod_ref[0, 5:6, :] * (shared + _dot(weights, yt_ref[slot]))
    if final:
        ms = jnp.mean(x2 * x2, axis=-1, keepdims=True)
        x2 = x2 * lax.rsqrt(ms + EPS) * fg_ref[...]
    o_ref[...] = x2


def _combine(plan, y_sorted, comb_bt, h2, x1, mod_step, ws_gu, ws_d, final_g, final):
    n_tok = h2.shape[0]
    nb = n_tok // MOE_BLOCK
    y_units = y_sorted.reshape(-1, UNIT, D_MODEL)

    def full(a):
        return pl.BlockSpec(a.shape, lambda b, ysrc: (0,) * a.ndim)

    grid_spec = pltpu.PrefetchScalarGridSpec(
        num_scalar_prefetch=1,
        grid=(nb,),
        in_specs=[
            pl.BlockSpec(memory_space=pl.ANY),
            pl.BlockSpec((1, MOE_BLOCK, N_EXPERTS), lambda b, ysrc: (b, 0, 0)),
            pl.BlockSpec((1, MOE_BLOCK, N_EXPERTS), lambda b, ysrc: (b, 0, 0)),
            pl.BlockSpec((1, 1, SLOT_MAX), lambda b, ysrc: (b, 0, 0)),
            pl.BlockSpec((1, 1, SLOT_MAX), lambda b, ysrc: (b, 0, 0)),
            pl.BlockSpec((MOE_BLOCK, D_MODEL), lambda b, ysrc: (b, 0)),
            pl.BlockSpec((MOE_BLOCK, D_MODEL), lambda b, ysrc: (b, 0)),
            pl.BlockSpec((1, 8, D_MODEL), lambda b, ysrc: (b * MOE_BLOCK // STEP, 0, 0)),
            full(ws_gu), full(ws_d), full(final_g),
        ],
        out_specs=pl.BlockSpec((MOE_BLOCK, D_MODEL), lambda b, ysrc: (b, 0)),
        scratch_shapes=[
            pltpu.VMEM((2, SLOT_MAX, D_MODEL), BF16),
            pltpu.SemaphoreType.DMA((2,)),
        ],
    )
    return pl.pallas_call(
        functools.partial(_combine_kernel, final=final),
        grid_spec=grid_spec,
        out_shape=jax.ShapeDtypeStruct((n_tok, D_MODEL), F32),
        compiler_params=pltpu.CompilerParams(
            dimension_semantics=("arbitrary",), vmem_limit_bytes=VMEM_LIMIT),
        name="moe_combine",
    )(plan["ysrc"], y_units, plan["rank_bt"], comb_bt,
      plan["e_slot"], plan["r_slot"], h2, x1, mod_step, ws_gu, ws_d, final_g)


def _split_bf16(w):
    hi = w.astype(BF16)
    return jnp.concatenate([hi, (w - hi.astype(F32)).astype(BF16)], axis=0)


def _level_ids():
    t = np.arange(CHUNK)
    x = t[:, None] ^ t[None, :]
    lvl = np.zeros((CHUNK, CHUNK), np.int32)
    nz = x > 0
    lvl[nz] = np.floor(np.log2(x[nz])).astype(np.int32) + 1
    past_f = t[:, None] >= t[None, :]
    return jnp.asarray(np.where(past_f, lvl, -1)), jnp.asarray(np.where(past_f.T, lvl, -1))


def kernel(x_prompt, x_sample, state_hgrn, c, c_ctx, w_ada, b_ada, norm1_g, w_in, hgrn_lb_logits, hgrn_norm_g, w_proj_hgrn, conv_dw_w, conv_dw_b, conv_norm_g, conv_norm_b, w_proj_conv, w_out, norm2_g, w_router, router_bias, w_expert_gate_up, w_expert_down, w_shared_gate_up, w_shared_down, final_norm_g):
    n_ctx, ctx_len, d = x_prompt.shape
    n_lat, lat_len, _ = x_sample.shape
    depth = w_ada.shape[0]
    assert d == D_MODEL and ctx_len * 4 == STEP and lat_len == STEP and n_ctx % 4 == 0
    ctx_steps = n_ctx * ctx_len // STEP
    n_steps = ctx_steps + n_lat
    n_tok = n_steps * STEP
    assert n_tok == N_TOKENS and STEP % POST_TILE == 0 and STEP % MOE_BLOCK == 0
    tiles_per_step = STEP // POST_TILE

    x_all = jnp.concatenate([x_prompt.reshape(-1, d), x_sample.reshape(-1, d)], axis=0)

    cc = jnp.zeros((8, d), F32).at[:n_lat].set(c.astype(F32)).at[n_lat].set(c_ctx.astype(F32))
    mod = _ada(cc, w_ada, b_ada).reshape(depth, 8, N_MOD, d)
    step_src = np.array([n_lat] * ctx_steps + list(range(n_lat)))
    mod_step = jnp.pad(mod[:, step_src], ((0, 0), (0, 0), (0, 8 - N_MOD), (0, 0)))

    nc = jnp.asarray([ctx_len // CHUNK] * ctx_steps + [lat_len // CHUNK] * n_lat, jnp.int32)
    tile_pos = np.arange(n_lat * tiles_per_step) % tiles_per_step
    pv = jnp.asarray(np.concatenate([np.zeros(ctx_steps * tiles_per_step), tile_pos > 0]), jnp.int32)
    nv = jnp.asarray(np.concatenate([np.zeros(ctx_steps * tiles_per_step), tile_pos < tiles_per_step - 1]), jnp.int32)

    lbv = jnp.cumsum(jax.nn.softmax(hgrn_lb_logits.astype(F32), axis=0), axis=0)
    lbv = (lbv - lbv[:1]).reshape(depth, 2, N_HEADS, HEAD_DIM)
    lvl_f, lvl_b = _level_ids()

    xs = x_all
    ctx_states = []
    for l in range(depth):
        lb = lbv[l]
        gp = jnp.stack([jnp.log(lb[0]), jnp.log1p(-lb[0]), 1.0 - lb[0],
                        jnp.log(lb[1]), jnp.log1p(-lb[1]), 1.0 - lb[1],
                        hgrn_norm_g[l].reshape(N_HEADS, HEAD_DIM).astype(F32),
                        jnp.zeros((N_HEADS, HEAD_DIM), F32)], axis=1)
        s0 = jnp.concatenate([jnp.zeros((ctx_steps, 2, N_HEADS, HEAD_DIM, HEAD_DIM), F32),
                              state_hgrn[:, l].astype(F32)], axis=0)
        g1 = norm1_g[l].reshape(1, d).astype(F32)

        o_all, states = _scan(nc, xs, mod_step[l], g1, w_in, l, gp, s0, lvl_f, lvl_b)
        ctx_states.append(states[:ctx_steps].reshape(n_ctx, 2, N_HEADS, HEAD_DIM, HEAD_DIM))

        cw = jnp.pad(conv_dw_w[l].astype(F32), ((0, 1), (0, 0)))
        x1, h2, logits_t = _post(
            pv, nv, xs, mod_step[l], o_all, w_in, l, w_proj_hgrn[l], w_proj_conv[l], w_out[l],
            _split_bf16(w_router[l].T.astype(F32)), g1, cw,
            conv_dw_b[l].reshape(1, -1).astype(F32), conv_norm_g[l].reshape(1, -1).astype(F32),
            conv_norm_b[l].reshape(1, -1).astype(F32), norm2_g[l].reshape(1, d).astype(F32))

        comb_t, chosen_t = _route(logits_t, router_bias[l].reshape(N_EXPERTS, 1).astype(F32))
        plan = _routing_plan(chosen_t)
        comb_bt = comb_t.reshape(N_EXPERTS, n_tok // MOE_BLOCK, MOE_BLOCK).transpose(1, 2, 0).astype(BF16)
        xp = _permute(plan, h2)
        y_sorted = _expert_ffn(plan, xp, w_expert_gate_up, w_expert_down, l)
        xs = _combine(plan, y_sorted, comb_bt, h2, x1, mod_step[l], w_shared_gate_up[l].astype(BF16),
                      w_shared_down[l].astype(BF16), final_norm_g.reshape(1, d).astype(F32),
                      final=(l == depth - 1))

    n_ctx_tok = n_ctx * ctx_len
    y_prompt = xs[:n_ctx_tok].reshape(x_prompt.shape).astype(x_prompt.dtype)
    y_sample = xs[n_ctx_tok:].reshape(x_sample.shape).astype(x_sample.dtype)
    new_state = jnp.stack(ctx_states, axis=1).astype(x_prompt.dtype)
    return (y_prompt, y_sample, new_state)
```

```python
import functools

import numpy as np
import jax
import jax.numpy as jnp
from jax import lax
from jax.experimental import pallas as pl
from jax.experimental.pallas import tpu as pltpu

F32 = jnp.float32
BF16 = jnp.bfloat16
HIGHEST = lax.Precision.HIGHEST

D_MODEL = 1024
N_HEADS = 8
HEAD_DIM = 128
CONV_DIM = 512
CONV_WIDTH = 31
CONV_HALO = 16
N_EXPERTS = 64
N_GROUPS = 8
GROUP_SIZE = N_EXPERTS // N_GROUPS
TOPK_GROUPS = 4
TOP_K = 8
EXPERT_FF = 256
ROUTED_SCALE = 2.5
N_MOD = 6
EPS = 1e-6

CHUNK = 128
N_LEVELS = 7
LOCKSTEP = 2
STEP = 1024
POST_TILE = 256
MOE_BLOCK = 256
UNIT = 16
FFN_TILE = 256
UNITS_PER_TILE = FFN_TILE // UNIT
TILES_PER_STEP = 2
SLOT_MAX = MOE_BLOCK * TOP_K + N_EXPERTS * UNIT
UNITS_PER_BLOCK = SLOT_MAX // UNIT
GROUP_UNITS = 32
GROUP_SLOTS = GROUP_UNITS * UNIT
GROUPS_PER_BLOCK = UNITS_PER_BLOCK // GROUP_UNITS
N_TOKENS = 8192
N_TILES_MAX = (N_TOKENS * TOP_K // UNIT + (N_TOKENS // MOE_BLOCK) * N_EXPERTS) // UNITS_PER_TILE + N_EXPERTS
HEAD_COLS = 5 * HEAD_DIM
VMEM_LIMIT = 52 * 1024 * 1024


def _dot(a, b):
    return jnp.dot(a, b, preferred_element_type=F32)


def _dot_nt(a, b):
    return lax.dot_general(a, b, (((1,), (1,)), ((), ())), preferred_element_type=F32)


def _dot_tn(a, b):
    return lax.dot_general(a, b, (((0,), (0,)), ((), ())), preferred_element_type=F32)


def _sigmoid(x):
    return 1.0 / (1.0 + jnp.exp(-x))


def _silu(x):
    return x * _sigmoid(x)


def _rms_mod(x, g, scale, shift):
    ms = jnp.mean(x * x, axis=-1, keepdims=True)
    return x * lax.rsqrt(ms + EPS) * g * (1.0 + scale) + shift


def _ada_kernel(c_ref, w_ref, b_ref, o_ref):
    cc = c_ref[...]
    o_ref[0] = jnp.dot(_silu(cc), w_ref[0], preferred_element_type=F32, precision=HIGHEST) + b_ref[0]


def _ada(cc, w_ada, b_ada):
    depth, d, n = w_ada.shape
    tn = 1536
    return pl.pallas_call(
        _ada_kernel,
        grid=(depth, n // tn),
        in_specs=[
            pl.BlockSpec((8, d), lambda l, j: (0, 0)),
            pl.BlockSpec((1, d, tn), lambda l, j: (l, 0, j)),
            pl.BlockSpec((1, 1, tn), lambda l, j: (l, 0, j)),
        ],
        out_specs=pl.BlockSpec((1, 8, tn), lambda l, j: (l, 0, j)),
        out_shape=jax.ShapeDtypeStruct((depth, 8, n), F32),
        compiler_params=pltpu.CompilerParams(vmem_limit_bytes=VMEM_LIMIT),
        name="ada_mod",
    )(cc, w_ada, b_ada.reshape(depth, 1, n))


def _forget_gate(z, log_lb, log1m_lb, one_m_lb):
    e = jnp.exp(-jnp.abs(z))
    r = 1.0 / (1.0 + e)
    log_sig = jnp.minimum(z, 0.0) - jnp.log(1.0 + e)
    k = one_m_lb * jnp.where(z > 0, e * r, r)
    b = log1m_lb + log_sig
    log_f = jnp.maximum(log_lb, b) + jnp.log(1.0 + jnp.exp(-jnp.abs(log_lb - b)))
    return log_f, k


def _level_reference(cum_ref, blk, fwd):
    half = blk // 2
    pieces = []
    if blk >= 8:
        for i in range(CHUNK // blk):
            row = i * blk + (half - 1 if fwd else half)
            pieces.append(jnp.broadcast_to(cum_ref[row:row + 1, :], (blk, HEAD_DIM)))
    else:
        sub = lax.broadcasted_iota(jnp.int32, (8, HEAD_DIM), 0)
        for i in range(CHUNK // 8):
            lo_row = 8 * i + (1 if fwd else 2)
            hi_row = 8 * i + (5 if fwd else 6)
            lo = jnp.broadcast_to(cum_ref[lo_row:lo_row + 1, :], (8, HEAD_DIM))
            hi = jnp.broadcast_to(cum_ref[hi_row:hi_row + 1, :], (8, HEAD_DIM))
            pieces.append(jnp.where(sub >= 4, hi, lo))
    return pieces[0] if len(pieces) == 1 else jnp.concatenate(pieces, axis=0)


def _chunk_steps(chains):
    r_idx = lax.broadcasted_iota(jnp.int32, (CHUNK, CHUNK), 0)
    c_idx = lax.broadcasted_iota(jnp.int32, (CHUNK, CHUNK), 1)
    work = []
    for q, k, v, log_f, lvl, cum_ref, fwd in chains:
        tri = jnp.where((r_idx >= c_idx) if fwd else (r_idx <= c_idx), 1.0, 0.0).astype(BF16)
        hi = log_f.astype(BF16)
        lo = (log_f - hi.astype(F32)).astype(BF16)
        both = _dot(tri, jnp.concatenate([hi, lo], axis=1))
        cum = both[:, :HEAD_DIM] + both[:, HEAD_DIM:]
        cum_ref[...] = cum
        q_b = q.astype(BF16)
        k_b = k.astype(BF16)
        work.append(dict(cum=cum, q_b=q_b, k_b=k_b, scores=jnp.where(lvl == 0, _dot_nt(q_b, k_b), 0.0)))

    for lev in range(1, N_LEVELS + 1):
        for (q, k, v, log_f, lvl, cum_ref, fwd), w in zip(chains, work):
            if lev == 1:
                qe = w["q_b"] * jnp.exp(log_f).astype(BF16)
                ke = w["k_b"]
            else:
                e = jnp.exp(-jnp.abs(w["cum"] - _level_reference(cum_ref, 1 << lev, fwd))).astype(BF16)
                qe = w["q_b"] * e
                ke = w["k_b"] * e
            w["scores"] = jnp.where(lvl == lev, _dot_nt(qe, ke), w["scores"])

    def finish(chain, w, st):
        q, k, v, log_f, lvl, cum_ref, fwd = chain
        total = cum_ref[CHUNK - 1:CHUNK, :] if fwd else cum_ref[0:1, :]
        o = (_dot(w["scores"].astype(BF16), v.astype(BF16))
             + _dot_nt((q * jnp.exp(w["cum"])).astype(BF16), st.astype(BF16)))
        k_st = k * jnp.exp(total - w["cum"])
        return o, st * jnp.exp(total) + _dot_tn(v.astype(BF16), k_st.astype(BF16))

    return [functools.partial(finish, chain, w) for chain, w in zip(chains, work)]


def _scan_kernel(nc_ref, x_ref, mod_ref, g1_ref, wq_ref, wv_ref, wf_ref, wb_ref, wg_ref, gp_ref, s0_ref,
                 lvlf_ref, lvlb_ref, o_out, st_out, h_ref, w_ref, z_ref, of_ref, ob_ref, cum_ref):
    pair = pl.program_id(0)
    step = pl.program_id(1)
    sub = pl.program_id(2)
    n_seq_chunks = nc_ref[step]
    n_chunks = STEP // CHUNK

    @pl.when((pair == 0) & (sub == 0))
    def _():
        g1 = g1_ref[...]
        shift = mod_ref[0, 0:1, :]
        scale = mod_ref[0, 1:2, :]

        for i in range(n_chunks):
            rows = pl.ds(i * CHUNK, CHUNK)
            h_ref[step, rows, :] = _rms_mod(x_ref[rows, :], g1, scale, shift).astype(BF16)

    @pl.when((step == 0) & (sub == 0))
    def _():
        for j, wj_ref in enumerate((wq_ref, wv_ref, wf_ref, wb_ref, wg_ref)):
            for hh in range(2):
                w_ref[:, hh * HEAD_COLS + j * HEAD_DIM:hh * HEAD_COLS + (j + 1) * HEAD_DIM] = (
                    wj_ref[0, :, hh * HEAD_DIM:(hh + 1) * HEAD_DIM].astype(BF16))

    @pl.when(sub == 0)
    def _():
        z = _dot(h_ref[step], w_ref[...])
        z_ref[0] = z[:, :HEAD_COLS]
        z_ref[1] = z[:, HEAD_COLS:]

    st_out[...] = jnp.zeros(st_out.shape, F32)
    st0 = (s0_ref[0, 0, 0].T, s0_ref[0, 1, 0].T)

    def chain(c, fwd, cum_ref):
        d = 0 if fwd else 1
        zcol = (2 if fwd else 3) * HEAD_DIM
        rows = pl.ds(c * CHUNK, CHUNK)
        log_f, k = _forget_gate(z_ref[sub, rows, zcol:zcol + HEAD_DIM], gp_ref[0, 3 * d:3 * d + 1, :],
                                gp_ref[0, 3 * d + 1:3 * d + 2, :], gp_ref[0, 3 * d + 2:3 * d + 3, :])
        return (z_ref[sub, rows, 0:HEAD_DIM], k, z_ref[sub, rows, HEAD_DIM:2 * HEAD_DIM], log_f,
                (lvlf_ref if fwd else lvlb_ref)[...], cum_ref, fwd)

    carry = list(st0)
    for i0 in range(0, n_chunks, LOCKSTEP):
        todo = [(i if fwd else n_chunks - 1 - i, fwd) for i in range(i0, i0 + LOCKSTEP) for fwd in (True, False)]
        finishers = _chunk_steps([chain(c, fwd, cum_ref.at[n]) for n, (c, fwd) in enumerate(todo)])
        for (c, fwd), finish_chain in zip(todo, finishers):
            d = 0 if fwd else 1
            pos = lax.rem(c, n_seq_chunks)
            first = (pos == 0) if fwd else (pos == n_seq_chunks - 1)
            o, carry[d] = finish_chain(jnp.where(first, st0[d], carry[d]))
            (of_ref if fwd else ob_ref)[pl.ds(c * CHUNK, CHUNK), :] = o
            st_out[0, lax.div(c, n_seq_chunks), d, 0] = carry[d].T

    norm_g = gp_ref[0, 6:7, :]

    chunk_rows = [pl.ds(i * CHUNK, CHUNK) for i in range(n_chunks)]
    os_ = [of_ref[rows, :] + ob_ref[rows, :] for rows in chunk_rows]
    invs = [lax.rsqrt(jnp.mean(o * o, axis=-1, keepdims=True) + EPS) for o in os_]
    for rows, o, inv in zip(chunk_rows, os_, invs):
        og = z_ref[sub, rows, 4 * HEAD_DIM:5 * HEAD_DIM]
        o_out[rows, :] = (o * inv * norm_g * _silu(og)).astype(BF16)


def _scan(nc, x_all, mod_step, g1, w_in, layer, gp, s0, lvl_f, lvl_b):
    n_tok = x_all.shape[0]
    n_steps = n_tok // STEP
    grid_spec = pltpu.PrefetchScalarGridSpec(
        num_scalar_prefetch=1,
        grid=(N_HEADS // 2, n_steps, 2),
        in_specs=[
            pl.BlockSpec((STEP, D_MODEL), lambda p, s, u, nc: (jnp.where((p == 0) & (u == 0), s, 0), 0)),
            pl.BlockSpec((1, 8, D_MODEL), lambda p, s, u, nc: (s, 0, 0)),
            pl.BlockSpec((1, D_MODEL), lambda p, s, u, nc: (0, 0)),
        ] + [
            pl.BlockSpec((1, D_MODEL, 2 * HEAD_DIM), lambda p, s, u, nc, j=j: (layer, 0, j * (N_HEADS // 2) + p))
            for j in range(5)
        ] + [
            pl.BlockSpec((1, 8, HEAD_DIM), lambda p, s, u, nc: (2 * p + u, 0, 0)),
            pl.BlockSpec((1, 2, 1, HEAD_DIM, HEAD_DIM), lambda p, s, u, nc: (s, 0, 2 * p + u, 0, 0)),
            pl.BlockSpec((CHUNK, CHUNK), lambda p, s, u, nc: (0, 0)),
            pl.BlockSpec((CHUNK, CHUNK), lambda p, s, u, nc: (0, 0)),
        ],
        out_specs=[
            pl.BlockSpec((STEP, HEAD_DIM), lambda p, s, u, nc: (s, 2 * p + u)),
            pl.BlockSpec((1, 4, 2, 1, HEAD_DIM, HEAD_DIM), lambda p, s, u, nc: (s, 0, 0, 2 * p + u, 0, 0)),
        ],
        scratch_shapes=[
            pltpu.VMEM((n_steps, STEP, D_MODEL), BF16),
            pltpu.VMEM((D_MODEL, 2 * HEAD_COLS), BF16),
            pltpu.VMEM((2, STEP, HEAD_COLS), F32),
            pltpu.VMEM((STEP, HEAD_DIM), F32),
            pltpu.VMEM((STEP, HEAD_DIM), F32),
            pltpu.VMEM((2 * LOCKSTEP, CHUNK, HEAD_DIM), F32),
        ],
    )
    return pl.pallas_call(
        _scan_kernel,
        grid_spec=grid_spec,
        out_shape=[
            jax.ShapeDtypeStruct((n_tok, D_MODEL), BF16),
            jax.ShapeDtypeStruct((n_steps, 4, 2, N_HEADS, HEAD_DIM, HEAD_DIM), F32),
        ],
        compiler_params=pltpu.CompilerParams(
            dimension_semantics=("arbitrary", "arbitrary", "arbitrary"), vmem_limit_bytes=VMEM_LIMIT),
        name="hgrn_scan",
    )(nc, x_all, mod_step, g1, *([w_in] * 5), gp, s0, lvl_f, lvl_b)


def _post_kernel(pv_ref, nv_ref, x_ref, xp_ref, xn_ref, mod_ref, o_ref, wu_f32, wgz_f32, wph_f32,
                 wpc_f32, wo_f32, wr_ref, g1_ref, cw_ref, cb_ref, lg_ref, lb_ref, g2_ref,
                 x1_out, h2_out, lg_out, cu_ref, wu_ref, wgz_ref, wph_ref, wpc_ref, wo_ref):
    i = pl.program_id(0)

    @pl.when(i == 0)
    def _():
        for dst, src in ((wu_ref, wu_f32), (wgz_ref, wgz_f32), (wph_ref, wph_f32),
                         (wpc_ref, wpc_f32), (wo_ref, wo_f32)):
            dst[...] = (src[0] if len(src.shape) == 3 else src[...]).astype(BF16)

    g1 = g1_ref[...]
    shift1 = mod_ref[0, 0:1, :]
    scale1 = mod_ref[0, 1:2, :]
    gate1 = mod_ref[0, 2:3, :]
    shift2 = mod_ref[0, 3:4, :]
    scale2 = mod_ref[0, 4:5, :]

    def glu(xv):
        h = _rms_mod(xv, g1, scale1, shift1).astype(BF16)
        u = _dot(h, wu_ref[...])
        return h, u[:, :CONV_DIM] * _sigmoid(u[:, CONV_DIM:])

    x = x_ref[...]
    h, glu_mid = glu(x)
    _, glu_prev = glu(xp_ref[...])
    _, glu_next = glu(xn_ref[...])
    cu_ref[0, 0:CONV_HALO, :] = glu_prev * pv_ref[i].astype(F32)
    cu_ref[0, CONV_HALO:CONV_HALO + POST_TILE, :] = glu_mid
    cu_ref[0, CONV_HALO + POST_TILE:, :] = glu_next * nv_ref[i].astype(F32)
    n_keep = POST_TILE + 2 * CONV_HALO - 8
    for k in range(1, 8):
        cu_ref[k, 0:n_keep, :] = cu_ref[0, k:k + n_keep, :]

    off = CONV_HALO - CONV_WIDTH // 2
    acc = jnp.zeros((POST_TILE, CONV_DIM), F32) + cb_ref[...]
    for j in range(CONV_WIDTH):
        base, k = divmod(off + j, 8)
        acc = acc + cu_ref[k, 8 * base:8 * base + POST_TILE, :] * cw_ref[j:j + 1, :]
    mu = jnp.mean(acc, axis=-1, keepdims=True)
    cen = acc - mu
    var = jnp.mean(cen * cen, axis=-1, keepdims=True)
    cv = cen * lax.rsqrt(var + EPS) * lg_ref[...] + lb_ref[...]
    y_b = _dot(_silu(cv).astype(BF16), wpc_ref[...])

    y_a = _dot(o_ref[...], wph_ref[...])
    gz = _sigmoid(_dot(h, wgz_ref[...]))
    merged = gz[:, :D_MODEL] * y_a + gz[:, D_MODEL:] * y_b
    x1 = x + gate1 * _dot(merged.astype(BF16), wo_ref[...])
    x1_out[...] = x1
    h2 = _rms_mod(x1, g2_ref[...], scale2, shift2)
    h2_hi = h2.astype(BF16)
    h2_out[...] = h2_hi
    h2_lo = (h2 - h2_hi.astype(F32)).astype(BF16)
    part = _dot_nt(wr_ref[...], h2_hi)
    lg_out[...] = part[:N_EXPERTS] + part[N_EXPERTS:] + _dot_nt(wr_ref[:N_EXPERTS, :], h2_lo)


def _post(pv, nv, x_all, mod_step, o_all, w_in, layer, w_ph, w_pc, w_o, w_rt, g1, cw, cb, lg, lb, g2):
    n_tok = x_all.shape[0]
    n_tiles = n_tok // POST_TILE
    halo_per_tile = POST_TILE // CONV_HALO
    n_halo_blocks = n_tok // CONV_HALO
    glu_cols, gate_cols = 2 * CONV_DIM, 2 * D_MODEL
    glu_start = 5 * N_HEADS * HEAD_DIM
    assert glu_start % glu_cols == 0 and (glu_start + glu_cols) % gate_cols == 0

    def full(a):
        return pl.BlockSpec(a.shape, lambda i, pv, nv: (0,) * a.ndim)

    def resident(shape, index):
        return pl.BlockSpec(shape, lambda i, pv, nv: index, pipeline_mode=pl.Buffered(1))

    grid_spec = pltpu.PrefetchScalarGridSpec(
        num_scalar_prefetch=2,
        grid=(n_tiles,),
        in_specs=[
            pl.BlockSpec((POST_TILE, D_MODEL), lambda i, pv, nv: (i, 0)),
            pl.BlockSpec((CONV_HALO, D_MODEL), lambda i, pv, nv: (jnp.maximum(i * halo_per_tile - 1, 0), 0)),
            pl.BlockSpec((CONV_HALO, D_MODEL),
                         lambda i, pv, nv: (jnp.minimum((i + 1) * halo_per_tile, n_halo_blocks - 1), 0)),
            pl.BlockSpec((1, 8, D_MODEL), lambda i, pv, nv: (i, 0, 0)),
            pl.BlockSpec((POST_TILE, D_MODEL), lambda i, pv, nv: (i, 0)),
            resident((1, D_MODEL, glu_cols), (layer, 0, glu_start // glu_cols)),
            resident((1, D_MODEL, gate_cols), (layer, 0, (glu_start + glu_cols) // gate_cols)),
            resident(w_ph.shape, (0, 0)), resident(w_pc.shape, (0, 0)), resident(w_o.shape, (0, 0)),
            full(w_rt), full(g1), full(cw), full(cb), full(lg), full(lb), full(g2),
        ],
        out_specs=[
            pl.BlockSpec((POST_TILE, D_MODEL), lambda i, pv, nv: (i, 0)),
            pl.BlockSpec((POST_TILE, D_MODEL), lambda i, pv, nv: (i, 0)),
            pl.BlockSpec((N_EXPERTS, POST_TILE), lambda i, pv, nv: (0, i)),
        ],
        scratch_shapes=[
            pltpu.VMEM((8, POST_TILE + 2 * CONV_HALO, CONV_DIM), F32),
            pltpu.VMEM((D_MODEL, glu_cols), BF16), pltpu.VMEM((D_MODEL, gate_cols), BF16),
            pltpu.VMEM(w_ph.shape, BF16), pltpu.VMEM(w_pc.shape, BF16), pltpu.VMEM(w_o.shape, BF16),
        ],
    )
    return pl.pallas_call(
        _post_kernel,
        grid_spec=grid_spec,
        out_shape=[
            jax.ShapeDtypeStruct((n_tok, D_MODEL), F32),
            jax.ShapeDtypeStruct((n_tok, D_MODEL), BF16),
            jax.ShapeDtypeStruct((N_EXPERTS, n_tok), F32),
        ],
        compiler_params=pltpu.CompilerParams(
            dimension_semantics=("arbitrary",), vmem_limit_bytes=VMEM_LIMIT),
        name="post_mixer",
    )(pv, nv, x_all, x_all, x_all, mod_step, o_all, w_in, w_in, w_ph, w_pc, w_o, w_rt, g1, cw, cb, lg, lb, g2)


def _route_kernel(lg_ref, bias_ref, comb_out, chosen_out):
    n = lg_ref.shape[1]
    scores = _sigmoid(lg_ref[...])
    sel = scores + bias_ref[...]
    neg = jnp.float32(-jnp.inf)

    sel3 = sel.reshape(N_GROUPS, GROUP_SIZE, n)
    m1 = jnp.max(sel3, axis=1, keepdims=True)
    is_m1 = sel3 == m1
    n_m1 = jnp.sum(is_m1.astype(F32), axis=1, keepdims=True)
    m2 = jnp.max(jnp.where(is_m1, neg, sel3), axis=1, keepdims=True)
    grp = (m1 + jnp.where(n_m1 > 1.5, m1, m2)).reshape(N_GROUPS, n)

    gidx = lax.broadcasted_iota(jnp.int32, (N_GROUPS, n), 0)
    rank = jnp.zeros((N_GROUPS, n), F32)
    for g in range(N_GROUPS):
        other = grp[g:g + 1, :]
        ahead = (other > grp) | ((other == grp) & (g < gidx))
        rank = rank + ahead.astype(F32)
    keep_g = rank < TOPK_GROUPS - 0.5
    keep = jnp.broadcast_to(keep_g.reshape(N_GROUPS, 1, n), (N_GROUPS, GROUP_SIZE, n)).reshape(N_EXPERTS, n)
    cand = jnp.where(keep, sel, neg)

    eidx = lax.broadcasted_iota(jnp.int32, (N_EXPERTS, n), 0)
    chosen = jnp.zeros((N_EXPERTS, n), F32)
    for _ in range(TOP_K):
        best = jnp.max(cand, axis=0, keepdims=True)
        first = jnp.min(jnp.where(cand == best, eidx, N_EXPERTS), axis=0, keepdims=True)
        hit = eidx == first
        chosen = jnp.where(hit, 1.0, chosen)
        cand = jnp.where(hit, neg, cand)
    w = scores * chosen
    comb_out[...] = w / jnp.sum(w, axis=0, keepdims=True) * ROUTED_SCALE
    chosen_out[...] = chosen


def _route(logits_t, bias):
    n_tok = logits_t.shape[1]
    tile = 512
    return pl.pallas_call(
        _route_kernel,
        grid=(n_tok // tile,),
        in_specs=[
            pl.BlockSpec((N_EXPERTS, tile), lambda i: (0, i)),
            pl.BlockSpec((N_EXPERTS, 1), lambda i: (0, 0)),
        ],
        out_specs=[pl.BlockSpec((N_EXPERTS, tile), lambda i: (0, i)),
                   pl.BlockSpec((N_EXPERTS, tile), lambda i: (0, i))],
        out_shape=[jax.ShapeDtypeStruct((N_EXPERTS, n_tok), F32),
                   jax.ShapeDtypeStruct((N_EXPERTS, n_tok), F32)],
        name="router",
    )(logits_t, bias)


def _count_le(bounds, idx):
    return jnp.sum(jnp.where(bounds <= idx, 1.0, 0.0), axis=0, keepdims=True)


def _pick_row(onehot, col):
    return jnp.sum(jnp.where(onehot, col, 0.0), axis=0, keepdims=True)


def _plan_kernel(ch_ref, rankt_out, es_out, rs_out, src_out, et_out, ysrc_out, misc_out):
    nb = N_TOKENS // MOE_BLOCK
    n_src = N_TILES_MAX * UNITS_PER_TILE
    e_col = lax.broadcasted_iota(jnp.int32, (N_EXPERTS, 1), 0).astype(F32)
    blk_lane = lax.broadcasted_iota(jnp.int32, (N_EXPERTS, 128), 1)
    t_r = lax.broadcasted_iota(jnp.int32, (MOE_BLOCK, MOE_BLOCK), 0)
    t_c = lax.broadcasted_iota(jnp.int32, (MOE_BLOCK, MOE_BLOCK), 1)
    earlier = jnp.where(t_c < t_r, 1.0, 0.0).astype(BF16)
    e_r = lax.broadcasted_iota(jnp.int32, (N_EXPERTS, N_EXPERTS), 0)
    e_c = lax.broadcasted_iota(jnp.int32, (N_EXPERTS, N_EXPERTS), 1)
    upto = jnp.where(e_c <= e_r, 1.0, 0.0).astype(F32)
    slot = lax.broadcasted_iota(jnp.int32, (1, SLOT_MAX), 1).astype(F32)

    def col_cumsum(col):
        wide = jnp.broadcast_to(col, (N_EXPERTS, 128))
        return jnp.dot(upto, wide, preferred_element_type=F32, precision=HIGHEST)[:, 0:1]

    def block_stats(b, carry):
        units_mat, start_mat = carry
        cb = ch_ref[:, pl.ds(pl.multiple_of(b * MOE_BLOCK, MOE_BLOCK), MOE_BLOCK)]
        rank_t = _dot_nt(earlier, cb.astype(BF16))
        cb_t = cb.T
        rankt_out[b] = jnp.where(cb_t > 0, rank_t, -1.0).astype(BF16)
        cnt = jnp.sum(cb, axis=1, keepdims=True)
        units = jnp.floor((cnt + (UNIT - 1)) * (1.0 / UNIT))
        incl = col_cumsum(units)
        start = incl - units
        e_slot = _count_le(incl * UNIT, slot)
        onehot = e_col == e_slot
        r = slot - _pick_row(onehot, start * UNIT)
        valid = (e_slot < N_EXPERTS - 0.5) & (r < _pick_row(onehot, cnt))
        es_out[b] = jnp.where(valid, e_slot, -1.0).astype(jnp.int32)
        rs_out[b] = jnp.where(valid, r, -2.0)
        units_mat = jnp.where(blk_lane == b, units, units_mat)
        start_mat = jnp.where(blk_lane == b, start, start_mat)
        return units_mat, start_mat

    zeros = jnp.zeros((N_EXPERTS, 128), F32)
    units_mat, start_mat = lax.fori_loop(0, nb, block_stats, (zeros, zeros))

    tot = jnp.sum(units_mat, axis=1, keepdims=True)
    tiles_e = jnp.floor((tot + (UNITS_PER_TILE - 1)) * (1.0 / UNITS_PER_TILE))
    incl_t = col_cumsum(tiles_e)
    start_t = incl_t - tiles_e
    n_used = incl_t[N_EXPERTS - 1:N_EXPERTS, :]
    b_r = lax.broadcasted_iota(jnp.int32, (128, 128), 0)
    b_c = lax.broadcasted_iota(jnp.int32, (128, 128), 1)
    before = jnp.where(b_r < b_c, 1.0, 0.0).astype(F32)
    cum_b = jnp.dot(units_mat, before, preferred_element_type=F32, precision=HIGHEST)
    run_pos = start_t * UNITS_PER_TILE + cum_b

    tile_idx = lax.broadcasted_iota(jnp.int32, (1, 512), 1).astype(F32)
    et_out[...] = jnp.minimum(_count_le(incl_t, tile_idx), N_EXPERTS - 1.0).astype(jnp.int32)

    def split_rows(m):
        hi = m.astype(BF16)
        return [hi, (m - hi.astype(F32)).astype(BF16)]

    tables = jnp.concatenate(split_rows((cum_b + units_mat).T) + split_rows(cum_b.T) + split_rows(start_mat.T),
                             axis=0)
    b_col = lax.broadcasted_iota(jnp.int32, (128, 1), 0)
    chunk = 1024

    def src_chunk(c, carry):
        off = pl.multiple_of(c * chunk, chunk)
        p = (lax.broadcasted_iota(jnp.int32, (1, chunk), 1) + off).astype(F32)
        e_p = jnp.minimum(_count_le(incl_t * UNITS_PER_TILE, p), N_EXPERTS - 1.0)
        onehot_e = e_col == e_p
        q = p - _pick_row(onehot_e, start_t * UNITS_PER_TILE)
        valid = (p < n_used * UNITS_PER_TILE) & (q < _pick_row(onehot_e, tot))
        picked = _dot(tables, jnp.where(onehot_e, 1.0, 0.0).astype(BF16))
        incl_sel = picked[0:128] + picked[128:256]
        cum_sel = picked[256:384] + picked[384:512]
        start_sel = picked[512:640] + picked[640:768]
        b_p = jnp.sum(jnp.where((incl_sel <= q) & (b_col < nb), 1.0, 0.0), axis=0, keepdims=True)
        b_p = jnp.minimum(b_p, nb - 1.0)
        onehot_b = b_col.astype(F32) == b_p
        unit = b_p * UNITS_PER_BLOCK + _pick_row(onehot_b, start_sel) + q - _pick_row(onehot_b, cum_sel)
        src_out[:, pl.ds(off, chunk)] = jnp.where(valid, unit, 0.0).astype(jnp.int32)
        return carry

    lax.fori_loop(0, n_src // chunk, src_chunk, 0)

    unit_idx = lax.broadcasted_iota(jnp.int32, (1, 256), 1).astype(F32)
    misc_lane = lax.broadcasted_iota(jnp.int32, (1, 128), 1)

    def block_units(b, used):
        here = blk_lane == b
        units = jnp.sum(jnp.where(here, units_mat, 0.0), axis=1, keepdims=True)
        start = jnp.sum(jnp.where(here, start_mat, 0.0), axis=1, keepdims=True)
        pos = jnp.sum(jnp.where(here, run_pos, 0.0), axis=1, keepdims=True)
        incl = start + units
        e_unit = _count_le(incl, unit_idx)
        onehot = e_col == e_unit
        val = _pick_row(onehot, pos - start) + unit_idx
        ysrc_out[b] = jnp.where(e_unit < N_EXPERTS - 0.5, val, 0.0).astype(jnp.int32)
        return jnp.where(misc_lane == b, incl[N_EXPERTS - 1:N_EXPERTS, :], used)

    used = lax.fori_loop(0, nb, block_units, jnp.zeros((1, 128), F32))
    misc_out[0:1, :] = used.astype(jnp.int32)
    misc_out[1:2, :] = jnp.broadcast_to(n_used, (1, 128)).astype(jnp.int32)
    misc_out[2:8, :] = jnp.zeros((6, 128), jnp.int32)


def _routing_plan(chosen):
    nb = N_TOKENS // MOE_BLOCK
    n_src = N_TILES_MAX * UNITS_PER_TILE
    assert n_src % 1024 == 0 and N_TILES_MAX <= 512 and nb <= 128 and UNITS_PER_BLOCK <= 256
    rank_bt, e_slot, r_slot, src, e_tile, ysrc, misc = pl.pallas_call(
        _plan_kernel,
        out_shape=[
            jax.ShapeDtypeStruct((nb, MOE_BLOCK, N_EXPERTS), BF16),
            jax.ShapeDtypeStruct((nb, 1, SLOT_MAX), jnp.int32),
            jax.ShapeDtypeStruct((nb, 1, SLOT_MAX), F32),
            jax.ShapeDtypeStruct((1, n_src), jnp.int32),
            jax.ShapeDtypeStruct((1, 512), jnp.int32),
            jax.ShapeDtypeStruct((nb, 1, 256), jnp.int32),
            jax.ShapeDtypeStruct((8, 128), jnp.int32),
        ],
        compiler_params=pltpu.CompilerParams(vmem_limit_bytes=VMEM_LIMIT),
        name="moe_plan",
    )(chosen)
    return dict(rank_bt=rank_bt, e_slot=e_slot, r_slot=r_slot, src=src.reshape(-1),
                e_tile=e_tile[0, :N_TILES_MAX], n_used=misc[1, :1],
                ysrc=ysrc[:, 0, :UNITS_PER_BLOCK].reshape(-1))


def _permute_kernel(h_ref, rank_ref, es_ref, rs_ref, xp_out):
    sub_e = lax.broadcasted_iota(jnp.int32, (N_EXPERTS, GROUP_SLOTS // 2), 0)
    tiles = [slice(i * GROUP_SLOTS // 2, (i + 1) * GROUP_SLOTS // 2) for i in range(2 * GROUPS_PER_BLOCK)]
    picks = []
    for rows in tiles:
        onehot_e = jnp.where(sub_e == es_ref[0, :, rows], 1.0, 0.0).astype(BF16)
        slot_rank = _dot(rank_ref[0], onehot_e)
        picks.append(jnp.where(slot_rank == rs_ref[0, :, rows], 1.0, 0.0).astype(BF16))
    for rows, pick in zip(tiles, picks):
        xp_out[0, rows, :] = _dot_tn(pick, h_ref[...]).astype(BF16)


def _permute(plan, h2):
    nb = h2.shape[0] // MOE_BLOCK
    return pl.pallas_call(
        _permute_kernel,
        grid=(nb,),
        in_specs=[
            pl.BlockSpec((MOE_BLOCK, D_MODEL), lambda b: (b, 0)),
            pl.BlockSpec((1, MOE_BLOCK, N_EXPERTS), lambda b: (b, 0, 0)),
            pl.BlockSpec((1, 1, SLOT_MAX), lambda b: (b, 0, 0)),
            pl.BlockSpec((1, 1, SLOT_MAX), lambda b: (b, 0, 0)),
        ],
        out_specs=pl.BlockSpec((1, SLOT_MAX, D_MODEL), lambda b: (b, 0, 0)),
        out_shape=jax.ShapeDtypeStruct((nb, SLOT_MAX, D_MODEL), BF16),
        compiler_params=pltpu.CompilerParams(
            dimension_semantics=("arbitrary",), vmem_limit_bytes=VMEM_LIMIT),
        name="moe_permute",
    )(h2, plan["rank_bt"], plan["e_slot"], plan["r_slot"])


def _unit_copies(table_ref, base, src_hbm, dst_ref, slot, sem, n_units, start):
    for u in range(n_units):
        copy = pltpu.make_async_copy(src_hbm.at[table_ref[base + u]],
                                     dst_ref.at[slot, pl.ds(u * UNIT, UNIT), :], sem.at[slot])
        if start:
            copy.start()
        else:
            copy.wait()


def _ffn_kernel(src_ref, et_ref, nu_ref, xp_hbm, *refs):
    w_refs = refs[:2 * TILES_PER_STEP]
    y_out = refs[2 * TILES_PER_STEP]
    scratch = refs[2 * TILES_PER_STEP + 1:]
    wb_refs, xt_ref, sem = scratch[:2 * TILES_PER_STEP], scratch[2 * TILES_PER_STEP], scratch[2 * TILES_PER_STEP + 1]
    n_units = TILES_PER_STEP * UNITS_PER_TILE
    step = pl.program_id(0)
    slot = lax.rem(step, 2)
    n_used = nu_ref[0]

    def copies(s, sl, start):
        _unit_copies(src_ref, s * n_units, xp_hbm, xt_ref, sl, sem, n_units, start)

    @pl.when((step == 0) & (n_used > 0))
    def _():
        copies(0, 0, True)

    @pl.when((step + 1 < pl.num_programs(0)) & ((step + 1) * TILES_PER_STEP < n_used))
    def _():
        copies(step + 1, 1 - slot, True)

    @pl.when(step * TILES_PER_STEP < n_used)
    def _():
        for t in range(TILES_PER_STEP):
            j = step * TILES_PER_STEP + t

            @pl.when((step == 0) | (et_ref[j] != et_ref[jnp.maximum(j - TILES_PER_STEP, 0)]))
            def _():
                wb_refs[2 * t][...] = w_refs[2 * t][0, 0].astype(BF16)
                wb_refs[2 * t + 1][...] = w_refs[2 * t + 1][0, 0].astype(BF16)

        copies(step, slot, False)
        tiles = [slice(t * FFN_TILE, (t + 1) * FFN_TILE) for t in range(TILES_PER_STEP)]
        gus = [_dot(xt_ref[slot, rows, :], wb_refs[2 * t][...]) for t, rows in enumerate(tiles)]
        acts = [(_silu(gu[:, :EXPERT_FF]) * gu[:, EXPERT_FF:]).astype(BF16) for gu in gus]
        for t, (rows, act) in enumerate(zip(tiles, acts)):
            y_out[rows, :] = _dot(act, wb_refs[2 * t + 1][...]).astype(BF16)

    @pl.when(step * TILES_PER_STEP >= n_used)
    def _():
        y_out[...] = jnp.zeros(y_out.shape, BF16)


def _expert_ffn(plan, xp, w_gu, w_d, layer):
    xp_units = xp.reshape(-1, UNIT, D_MODEL)
    assert N_TILES_MAX % TILES_PER_STEP == 0

    def weight_specs(t):
        return [pl.BlockSpec((1, 1, D_MODEL, 2 * EXPERT_FF),
                             lambda s, src, et, nu, t=t: (layer, et[s * TILES_PER_STEP + t], 0, 0)),
                pl.BlockSpec((1, 1, EXPERT_FF, D_MODEL),
                             lambda s, src, et, nu, t=t: (layer, et[s * TILES_PER_STEP + t], 0, 0))]

    grid_spec = pltpu.PrefetchScalarGridSpec(
        num_scalar_prefetch=3,
        grid=(N_TILES_MAX // TILES_PER_STEP,),
        in_specs=[pl.BlockSpec(memory_space=pl.ANY)]
        + [spec for t in range(TILES_PER_STEP) for spec in weight_specs(t)],
        out_specs=pl.BlockSpec((TILES_PER_STEP * FFN_TILE, D_MODEL), lambda s, src, et, nu: (s, 0)),
        scratch_shapes=[
            pltpu.VMEM((D_MODEL, 2 * EXPERT_FF), BF16) if i % 2 == 0 else pltpu.VMEM((EXPERT_FF, D_MODEL), BF16)
            for i in range(2 * TILES_PER_STEP)
        ] + [pltpu.VMEM((2, TILES_PER_STEP * FFN_TILE, D_MODEL), BF16), pltpu.SemaphoreType.DMA((2,))],
    )
    return pl.pallas_call(
        _ffn_kernel,
        grid_spec=grid_spec,
        out_shape=jax.ShapeDtypeStruct((N_TILES_MAX * FFN_TILE, D_MODEL), BF16),
        compiler_params=pltpu.CompilerParams(
            dimension_semantics=("arbitrary",), vmem_limit_bytes=VMEM_LIMIT),
        name="moe_expert_ffn",
    )(plan["src"], plan["e_tile"], plan["n_used"], xp_units, *([w_gu, w_d] * TILES_PER_STEP))


def _combine_kernel(ysrc_ref, y_hbm, rank_ref, comb_ref, er_ref, rr_ref, h_ref, x1_ref, mod_ref,
                    wsgu_ref, wsd_ref, fg_ref, o_ref, yt_ref, sem, *, final):
    b = pl.program_id(0)
    slot = lax.rem(b, 2)

    def copies(bb, sl, start):
        _unit_copies(ysrc_ref, bb * UNITS_PER_BLOCK, y_hbm, yt_ref, sl, sem, UNITS_PER_BLOCK, start)

    @pl.when(b == 0)
    def _():
        copies(0, 0, True)

    @pl.when(b + 1 < pl.num_programs(0))
    def _():
        copies(b + 1, 1 - slot, True)

    gu = _dot(h_ref[...], wsgu_ref[...])
    shared = _dot((_silu(gu[:, :EXPERT_FF]) * gu[:, EXPERT_FF:]).astype(BF16), wsd_ref[...])

    sub_e = lax.broadcasted_iota(jnp.int32, (N_EXPERTS, GROUP_SLOTS), 0)
    groups = [slice(g * GROUP_SLOTS, (g + 1) * GROUP_SLOTS) for g in range(GROUPS_PER_BLOCK)]
    onehots = [jnp.where(sub_e == er_ref[0, :, cols], 1.0, 0.0).astype(BF16) for cols in groups]
    slot_ranks = [_dot(rank_ref[0], onehot_e) for onehot_e in onehots]
    slot_ws = [_dot(comb_ref[0], onehot_e) for onehot_e in onehots]
    weights = jnp.concatenate(
        [jnp.where(slot_rank == rr_ref[0, :, cols], slot_w, 0.0).astype(BF16)
         for cols, slot_rank, slot_w in zip(groups, slot_ranks, slot_ws)], axis=1)

    copies(b, slot, False)
    x2 = x1_ref[...] + mod_ref[0, 5:6, :] * (shared + _dot(weights, yt_ref[slot]))
    if final:
        ms = jnp.mean(x2 * x2, axis=-1, keepdims=True)
        x2 = x2 * lax.rsqrt(ms + EPS) * fg_ref[...]
    o_ref[...] = x2


def _combine(plan, y_sorted, comb_bt, h2, x1, mod_step, ws_gu, ws_d, final_g, final):
    n_tok = h2.shape[0]
    nb = n_tok // MOE_BLOCK
    y_units = y_sorted.reshape(-1, UNIT, D_MODEL)

    def full(a):
        return pl.BlockSpec(a.shape, lambda b, ysrc: (0,) * a.ndim)

    grid_spec = pltpu.PrefetchScalarGridSpec(
        num_scalar_prefetch=1,
        grid=(nb,),
        in_specs=[
            pl.BlockSpec(memory_space=pl.ANY),
            pl.BlockSpec((1, MOE_BLOCK, N_EXPERTS), lambda b, ysrc: (b, 0, 0)),
            pl.BlockSpec((1, MOE_BLOCK, N_EXPERTS), lambda b, ysrc: (b, 0, 0)),
            pl.BlockSpec((1, 1, SLOT_MAX), lambda b, ysrc: (b, 0, 0)),
            pl.BlockSpec((1, 1, SLOT_MAX), lambda b, ysrc: (b, 0, 0)),
            pl.BlockSpec((MOE_BLOCK, D_MODEL), lambda b, ysrc: (b, 0)),
            pl.BlockSpec((MOE_BLOCK, D_MODEL), lambda b, ysrc: (b, 0)),
            pl.BlockSpec((1, 8, D_MODEL), lambda b, ysrc: (b, 0, 0)),
            full(ws_gu), full(ws_d), full(final_g),
        ],
        out_specs=pl.BlockSpec((MOE_BLOCK, D_MODEL), lambda b, ysrc: (b, 0)),
        scratch_shapes=[
            pltpu.VMEM((2, SLOT_MAX, D_MODEL), BF16),
            pltpu.SemaphoreType.DMA((2,)),
        ],
    )
    return pl.pallas_call(
        functools.partial(_combine_kernel, final=final),
        grid_spec=grid_spec,
        out_shape=jax.ShapeDtypeStruct((n_tok, D_MODEL), F32),
        compiler_params=pltpu.CompilerParams(
            dimension_semantics=("arbitrary",), vmem_limit_bytes=VMEM_LIMIT),
        name="moe_combine",
    )(plan["ysrc"], y_units, plan["rank_bt"], comb_bt,
      plan["e_slot"], plan["r_slot"], h2, x1, mod_step, ws_gu, ws_d, final_g)


def _split_bf16(w):
    hi = w.astype(BF16)
    return jnp.concatenate([hi, (w - hi.astype(F32)).astype(BF16)], axis=0)


def _level_ids():
    t = np.arange(CHUNK)
    x = t[:, None] ^ t[None, :]
    lvl = np.zeros((CHUNK, CHUNK), np.int32)
    nz = x > 0
    lvl[nz] = np.floor(np.log2(x[nz])).astype(np.int32) + 1
    past_f = t[:, None] >= t[None, :]
    return jnp.asarray(np.where(past_f, lvl, -1)), jnp.asarray(np.where(past_f.T, lvl, -1))


def kernel(x_prompt, x_sample, state_hgrn, c, c_ctx, w_ada, b_ada, norm1_g, w_in, hgrn_lb_logits, hgrn_norm_g, w_proj_hgrn, conv_dw_w, conv_dw_b, conv_norm_g, conv_norm_b, w_proj_conv, w_out, norm2_g, w_router, router_bias, w_expert_gate_up, w_expert_down, w_shared_gate_up, w_shared_down, final_norm_g):
    n_ctx, ctx_len, d = x_prompt.shape
    n_lat, lat_len, _ = x_sample.shape
    depth = w_ada.shape[0]
    assert d == D_MODEL and ctx_len * 4 == STEP and lat_len == STEP and n_ctx % 4 == 0
    ctx_steps = n_ctx * ctx_len // STEP
    n_steps = ctx_steps + n_lat
    n_tok = n_steps * STEP
    assert n_tok == N_TOKENS and STEP % POST_TILE == 0 and STEP % MOE_BLOCK == 0
    tiles_per_step = STEP // POST_TILE

    x_all = jnp.concatenate([x_prompt.reshape(-1, d), x_sample.reshape(-1, d)], axis=0)

    cc = jnp.zeros((8, d), F32).at[:n_lat].set(c.astype(F32)).at[n_lat].set(c_ctx.astype(F32))
    mod = _ada(cc, w_ada, b_ada).reshape(depth, 8, N_MOD, d)
    step_src = np.array([n_lat] * ctx_steps + list(range(n_lat)))
    mod_step = jnp.pad(mod[:, step_src], ((0, 0), (0, 0), (0, 8 - N_MOD), (0, 0)))
    mod_post = jnp.repeat(mod_step, STEP // POST_TILE, axis=1)
    mod_block = jnp.repeat(mod_step, STEP // MOE_BLOCK, axis=1)

    nc = jnp.asarray([ctx_len // CHUNK] * ctx_steps + [lat_len // CHUNK] * n_lat, jnp.int32)
    tile_pos = np.arange(n_lat * tiles_per_step) % tiles_per_step
    pv = jnp.asarray(np.concatenate([np.zeros(ctx_steps * tiles_per_step), tile_pos > 0]), jnp.int32)
    nv = jnp.asarray(np.concatenate([np.zeros(ctx_steps * tiles_per_step), tile_pos < tiles_per_step - 1]), jnp.int32)

    lbv = jnp.cumsum(jax.nn.softmax(hgrn_lb_logits.astype(F32), axis=0), axis=0)
    lbv = (lbv - lbv[:1]).reshape(depth, 2, N_HEADS, HEAD_DIM)
    lvl_f, lvl_b = _level_ids()

    xs = x_all
    ctx_states = []
    for l in range(depth):
        lb = lbv[l]
        gp = jnp.stack([jnp.log(lb[0]), jnp.log1p(-lb[0]), 1.0 - lb[0],
                        jnp.log(lb[1]), jnp.log1p(-lb[1]), 1.0 - lb[1],
                        hgrn_norm_g[l].reshape(N_HEADS, HEAD_DIM).astype(F32),
                        jnp.zeros((N_HEADS, HEAD_DIM), F32)], axis=1)
        s0 = jnp.concatenate([jnp.zeros((ctx_steps, 2, N_HEADS, HEAD_DIM, HEAD_DIM), F32),
                              state_hgrn[:, l].astype(F32)], axis=0)
        g1 = norm1_g[l].reshape(1, d).astype(F32)

        o_all, states = _scan(nc, xs, mod_step[l], g1, w_in, l, gp, s0, lvl_f, lvl_b)
        ctx_states.append(states[:ctx_steps].reshape(n_ctx, 2, N_HEADS, HEAD_DIM, HEAD_DIM))

        cw = jnp.pad(conv_dw_w[l].astype(F32), ((0, 1), (0, 0)))
        x1, h2, logits_t = _post(
            pv, nv, xs, mod_post[l], o_all, w_in, l, w_proj_hgrn[l], w_proj_conv[l], w_out[l],
            _split_bf16(w_router[l].T.astype(F32)), g1, cw,
            conv_dw_b[l].reshape(1, -1).astype(F32), conv_norm_g[l].reshape(1, -1).astype(F32),
            conv_norm_b[l].reshape(1, -1).astype(F32), norm2_g[l].reshape(1, d).astype(F32))

        comb_t, chosen_t = _route(logits_t, router_bias[l].reshape(N_EXPERTS, 1).astype(F32))
        plan = _routing_plan(chosen_t)
        comb_bt = comb_t.reshape(N_EXPERTS, n_tok // MOE_BLOCK, MOE_BLOCK).transpose(1, 2, 0).astype(BF16)
        xp = _permute(plan, h2)
        y_sorted = _expert_ffn(plan, xp, w_expert_gate_up, w_expert_down, l)
        xs = _combine(plan, y_sorted, comb_bt, h2, x1, mod_block[l], w_shared_gate_up[l].astype(BF16),
                      w_shared_down[l].astype(BF16), final_norm_g.reshape(1, d).astype(F32),
                      final=(l == depth - 1))

    n_ctx_tok = n_ctx * ctx_len
    y_prompt = xs[:n_ctx_tok].reshape(x_prompt.shape).astype(x_prompt.dtype)
    y_sample = xs[n_ctx_tok:].reshape(x_sample.shape).astype(x_sample.dtype)
    new_state = jnp.stack(ctx_states, axis=1).astype(x_prompt.dtype)
    return (y_prompt, y_sample, new_state)
```

```python
import functools

import numpy as np
import jax
import jax.numpy as jnp
from jax import lax
from jax.experimental import pallas as pl
from jax.experimental.pallas import tpu as pltpu

F32 = jnp.float32
BF16 = jnp.bfloat16
HIGHEST = lax.Precision.HIGHEST

D_MODEL = 1024
N_HEADS = 8
HEAD_DIM = 128
CONV_DIM = 512
CONV_WIDTH = 31
CONV_HALO = 16
N_EXPERTS = 64
N_GROUPS = 8
GROUP_SIZE = N_EXPERTS // N_GROUPS
TOPK_GROUPS = 4
TOP_K = 8
EXPERT_FF = 256
ROUTED_SCALE = 2.5
N_MOD = 6
EPS = 1e-6

CHUNK = 128
N_LEVELS = 7
LOCKSTEP = 2
STEP = 1024
POST_TILE = 256
MOE_BLOCK = 256
UNIT = 16
FFN_TILE = 256
UNITS_PER_TILE = FFN_TILE // UNIT
TILES_PER_STEP = 2
SLOT_MAX = MOE_BLOCK * TOP_K + N_EXPERTS * UNIT
UNITS_PER_BLOCK = SLOT_MAX // UNIT
GROUP_UNITS = 32
GROUP_SLOTS = GROUP_UNITS * UNIT
GROUPS_PER_BLOCK = UNITS_PER_BLOCK // GROUP_UNITS
N_TOKENS = 8192
N_TILES_MAX = (N_TOKENS * TOP_K // UNIT + (N_TOKENS // MOE_BLOCK) * N_EXPERTS) // UNITS_PER_TILE + N_EXPERTS
HEAD_COLS = 5 * HEAD_DIM
VMEM_LIMIT = 52 * 1024 * 1024


def _dot(a, b):
    return jnp.dot(a, b, preferred_element_type=F32)


def _dot_nt(a, b):
    return lax.dot_general(a, b, (((1,), (1,)), ((), ())), preferred_element_type=F32)


def _dot_tn(a, b):
    return lax.dot_general(a, b, (((0,), (0,)), ((), ())), preferred_element_type=F32)


def _sigmoid(x):
    return 1.0 / (1.0 + jnp.exp(-x))


def _silu(x):
    return x * _sigmoid(x)


def _rms_mod(x, g, scale, shift):
    ms = jnp.mean(x * x, axis=-1, keepdims=True)
    return x * lax.rsqrt(ms + EPS) * g * (1.0 + scale) + shift


def _ada_kernel(c_ref, w_ref, b_ref, o_ref):
    cc = c_ref[...]
    o_ref[0] = jnp.dot(_silu(cc), w_ref[0], preferred_element_type=F32, precision=HIGHEST) + b_ref[0]


def _ada(cc, w_ada, b_ada):
    depth, d, n = w_ada.shape
    tn = 1536
    return pl.pallas_call(
        _ada_kernel,
        grid=(depth, n // tn),
        in_specs=[
            pl.BlockSpec((8, d), lambda l, j: (0, 0)),
            pl.BlockSpec((1, d, tn), lambda l, j: (l, 0, j)),
            pl.BlockSpec((1, 1, tn), lambda l, j: (l, 0, j)),
        ],
        out_specs=pl.BlockSpec((1, 8, tn), lambda l, j: (l, 0, j)),
        out_shape=jax.ShapeDtypeStruct((depth, 8, n), F32),
        compiler_params=pltpu.CompilerParams(vmem_limit_bytes=VMEM_LIMIT),
        name="ada_mod",
    )(cc, w_ada, b_ada.reshape(depth, 1, n))


def _forget_gate(z, log_lb, log1m_lb, one_m_lb):
    e = jnp.exp(-jnp.abs(z))
    r = 1.0 / (1.0 + e)
    log_sig = jnp.minimum(z, 0.0) - jnp.log(1.0 + e)
    k = one_m_lb * jnp.where(z > 0, e * r, r)
    b = log1m_lb + log_sig
    log_f = jnp.maximum(log_lb, b) + jnp.log(1.0 + jnp.exp(-jnp.abs(log_lb - b)))
    return log_f, k


def _level_reference(cum_ref, blk, fwd):
    half = blk // 2
    pieces = []
    if blk >= 8:
        for i in range(CHUNK // blk):
            row = i * blk + (half - 1 if fwd else half)
            pieces.append(jnp.broadcast_to(cum_ref[row:row + 1, :], (blk, HEAD_DIM)))
    else:
        sub = lax.broadcasted_iota(jnp.int32, (8, HEAD_DIM), 0)
        for i in range(CHUNK // 8):
            lo_row = 8 * i + (1 if fwd else 2)
            hi_row = 8 * i + (5 if fwd else 6)
            lo = jnp.broadcast_to(cum_ref[lo_row:lo_row + 1, :], (8, HEAD_DIM))
            hi = jnp.broadcast_to(cum_ref[hi_row:hi_row + 1, :], (8, HEAD_DIM))
            pieces.append(jnp.where(sub >= 4, hi, lo))
    return pieces[0] if len(pieces) == 1 else jnp.concatenate(pieces, axis=0)


def _chunk_steps(chains):
    r_idx = lax.broadcasted_iota(jnp.int32, (CHUNK, CHUNK), 0)
    c_idx = lax.broadcasted_iota(jnp.int32, (CHUNK, CHUNK), 1)
    work = []
    for q, k, v, log_f, lvl, cum_ref, fwd in chains:
        tri = jnp.where((r_idx >= c_idx) if fwd else (r_idx <= c_idx), 1.0, 0.0).astype(BF16)
        hi = log_f.astype(BF16)
        lo = (log_f - hi.astype(F32)).astype(BF16)
        both = _dot(tri, jnp.concatenate([hi, lo], axis=1))
        cum = both[:, :HEAD_DIM] + both[:, HEAD_DIM:]
        cum_ref[...] = cum
        q_b = q.astype(BF16)
        k_b = k.astype(BF16)
        work.append(dict(cum=cum, q_b=q_b, k_b=k_b, scores=jnp.where(lvl == 0, _dot_nt(q_b, k_b), 0.0)))

    for lev in range(1, N_LEVELS + 1):
        for (q, k, v, log_f, lvl, cum_ref, fwd), w in zip(chains, work):
            if lev == 1:
                qe = w["q_b"] * jnp.exp(log_f).astype(BF16)
                ke = w["k_b"]
            else:
                e = jnp.exp(-jnp.abs(w["cum"] - _level_reference(cum_ref, 1 << lev, fwd))).astype(BF16)
                qe = w["q_b"] * e
                ke = w["k_b"] * e
            w["scores"] = jnp.where(lvl == lev, _dot_nt(qe, ke), w["scores"])

    def finish(chain, w, st):
        q, k, v, log_f, lvl, cum_ref, fwd = chain
        total = cum_ref[CHUNK - 1:CHUNK, :] if fwd else cum_ref[0:1, :]
        o = (_dot(w["scores"].astype(BF16), v.astype(BF16))
             + _dot_nt((q * jnp.exp(w["cum"])).astype(BF16), st.astype(BF16)))
        k_st = k * jnp.exp(total - w["cum"])
        return o, st * jnp.exp(total) + _dot_tn(v.astype(BF16), k_st.astype(BF16))

    return [functools.partial(finish, chain, w) for chain, w in zip(chains, work)]


def _scan_kernel(nc_ref, x_ref, mod_ref, g1_ref, wq_ref, wv_ref, wf_ref, wb_ref, wg_ref, gp_ref, s0_ref,
                 lvlf_ref, lvlb_ref, o_out, st_out, h_ref, w_ref, z_ref, of_ref, ob_ref, cum_ref):
    pair = pl.program_id(0)
    step = pl.program_id(1)
    sub = pl.program_id(2)
    n_seq_chunks = nc_ref[step]
    n_chunks = STEP // CHUNK

    @pl.when((pair == 0) & (sub == 0))
    def _():
        g1 = g1_ref[...]
        shift = mod_ref[0, 0:1, :]
        scale = mod_ref[0, 1:2, :]

        for i in range(n_chunks):
            rows = pl.ds(i * CHUNK, CHUNK)
            h_ref[step, rows, :] = _rms_mod(x_ref[rows, :], g1, scale, shift).astype(BF16)

    @pl.when((step == 0) & (sub == 0))
    def _():
        for j, wj_ref in enumerate((wq_ref, wv_ref, wf_ref, wb_ref, wg_ref)):
            for hh in range(2):
                w_ref[:, hh * HEAD_COLS + j * HEAD_DIM:hh * HEAD_COLS + (j + 1) * HEAD_DIM] = (
                    wj_ref[0, :, hh * HEAD_DIM:(hh + 1) * HEAD_DIM].astype(BF16))

    @pl.when(sub == 0)
    def _():
        z = _dot(h_ref[step], w_ref[...])
        z_ref[0] = z[:, :HEAD_COLS]
        z_ref[1] = z[:, HEAD_COLS:]

    st_out[...] = jnp.zeros(st_out.shape, F32)
    st0 = (s0_ref[0, 0, 0].T, s0_ref[0, 1, 0].T)

    def chain(c, fwd, cum_ref):
        d = 0 if fwd else 1
        zcol = (2 if fwd else 3) * HEAD_DIM
        rows = pl.ds(c * CHUNK, CHUNK)
        log_f, k = _forget_gate(z_ref[sub, rows, zcol:zcol + HEAD_DIM], gp_ref[0, 3 * d:3 * d + 1, :],
                                gp_ref[0, 3 * d + 1:3 * d + 2, :], gp_ref[0, 3 * d + 2:3 * d + 3, :])
        return (z_ref[sub, rows, 0:HEAD_DIM], k, z_ref[sub, rows, HEAD_DIM:2 * HEAD_DIM], log_f,
                (lvlf_ref if fwd else lvlb_ref)[...], cum_ref, fwd)

    carry = list(st0)
    for i0 in range(0, n_chunks, LOCKSTEP):
        todo = [(i if fwd else n_chunks - 1 - i, fwd) for i in range(i0, i0 + LOCKSTEP) for fwd in (True, False)]
        finishers = _chunk_steps([chain(c, fwd, cum_ref.at[n]) for n, (c, fwd) in enumerate(todo)])
        for (c, fwd), finish_chain in zip(todo, finishers):
            d = 0 if fwd else 1
            pos = lax.rem(c, n_seq_chunks)
            first = (pos == 0) if fwd else (pos == n_seq_chunks - 1)
            o, carry[d] = finish_chain(jnp.where(first, st0[d], carry[d]))
            (of_ref if fwd else ob_ref)[pl.ds(c * CHUNK, CHUNK), :] = o
            st_out[0, lax.div(c, n_seq_chunks), d, 0] = carry[d].T

    norm_g = gp_ref[0, 6:7, :]

    chunk_rows = [pl.ds(i * CHUNK, CHUNK) for i in range(n_chunks)]
    os_ = [of_ref[rows, :] + ob_ref[rows, :] for rows in chunk_rows]
    invs = [lax.rsqrt(jnp.mean(o * o, axis=-1, keepdims=True) + EPS) for o in os_]
    for rows, o, inv in zip(chunk_rows, os_, invs):
        og = z_ref[sub, rows, 4 * HEAD_DIM:5 * HEAD_DIM]
        o_out[rows, :] = (o * inv * norm_g * _silu(og)).astype(BF16)


def _scan(nc, x_all, mod_step, g1, w_in, layer, gp, s0, lvl_f, lvl_b):
    n_tok = x_all.shape[0]
    n_steps = n_tok // STEP
    grid_spec = pltpu.PrefetchScalarGridSpec(
        num_scalar_prefetch=1,
        grid=(N_HEADS // 2, n_steps, 2),
        in_specs=[
            pl.BlockSpec((STEP, D_MODEL), lambda p, s, u, nc: (jnp.where((p == 0) & (u == 0), s, 0), 0)),
            pl.BlockSpec((1, 8, D_MODEL), lambda p, s, u, nc: (s, 0, 0)),
            pl.BlockSpec((1, D_MODEL), lambda p, s, u, nc: (0, 0)),
        ] + [
            pl.BlockSpec((1, D_MODEL, 2 * HEAD_DIM), lambda p, s, u, nc, j=j: (layer, 0, j * (N_HEADS // 2) + p))
            for j in range(5)
        ] + [
            pl.BlockSpec((1, 8, HEAD_DIM), lambda p, s, u, nc: (2 * p + u, 0, 0)),
            pl.BlockSpec((1, 2, 1, HEAD_DIM, HEAD_DIM), lambda p, s, u, nc: (s, 0, 2 * p + u, 0, 0)),
            pl.BlockSpec((CHUNK, CHUNK), lambda p, s, u, nc: (0, 0)),
            pl.BlockSpec((CHUNK, CHUNK), lambda p, s, u, nc: (0, 0)),
        ],
        out_specs=[
            pl.BlockSpec((STEP, HEAD_DIM), lambda p, s, u, nc: (s, 2 * p + u)),
            pl.BlockSpec((1, 4, 2, 1, HEAD_DIM, HEAD_DIM), lambda p, s, u, nc: (s, 0, 0, 2 * p + u, 0, 0)),
        ],
        scratch_shapes=[
            pltpu.VMEM((n_steps, STEP, D_MODEL), BF16),
            pltpu.VMEM((D_MODEL, 2 * HEAD_COLS), BF16),
            pltpu.VMEM((2, STEP, HEAD_COLS), F32),
            pltpu.VMEM((STEP, HEAD_DIM), F32),
            pltpu.VMEM((STEP, HEAD_DIM), F32),
            pltpu.VMEM((2 * LOCKSTEP, CHUNK, HEAD_DIM), F32),
        ],
    )
    return pl.pallas_call(
        _scan_kernel,
        grid_spec=grid_spec,
        out_shape=[
            jax.ShapeDtypeStruct((n_tok, D_MODEL), BF16),
            jax.ShapeDtypeStruct((n_steps, 4, 2, N_HEADS, HEAD_DIM, HEAD_DIM), F32),
        ],
        compiler_params=pltpu.CompilerParams(
            dimension_semantics=("arbitrary", "arbitrary", "arbitrary"), vmem_limit_bytes=VMEM_LIMIT),
        name="hgrn_scan",
    )(nc, x_all, mod_step, g1, *([w_in] * 5), gp, s0, lvl_f, lvl_b)


def _post_kernel(pv_ref, nv_ref, x_ref, xp_ref, xn_ref, mod_ref, o_ref, wu_f32, wgz_f32, wph_f32,
                 wpc_f32, wo_f32, wr_ref, g1_ref, cw_ref, cb_ref, lg_ref, lb_ref, g2_ref,
                 x1_out, h2_out, lg_out, cu_ref, wu_ref, wgz_ref, wph_ref, wpc_ref, wo_ref):
    i = pl.program_id(0)

    @pl.when(i == 0)
    def _():
        for dst, src in ((wu_ref, wu_f32), (wgz_ref, wgz_f32), (wph_ref, wph_f32),
                         (wpc_ref, wpc_f32), (wo_ref, wo_f32)):
            dst[...] = (src[0] if len(src.shape) == 3 else src[...]).astype(BF16)

    g1 = g1_ref[...]
    shift1 = mod_ref[0, 0:1, :]
    scale1 = mod_ref[0, 1:2, :]
    gate1 = mod_ref[0, 2:3, :]
    shift2 = mod_ref[0, 3:4, :]
    scale2 = mod_ref[0, 4:5, :]

    def glu(xv):
        h = _rms_mod(xv, g1, scale1, shift1).astype(BF16)
        u = _dot(h, wu_ref[...])
        return h, u[:, :CONV_DIM] * _sigmoid(u[:, CONV_DIM:])

    x = x_ref[...]
    h, glu_mid = glu(x)
    _, glu_prev = glu(xp_ref[...])
    _, glu_next = glu(xn_ref[...])
    cu_ref[0, 0:CONV_HALO, :] = glu_prev * pv_ref[i].astype(F32)
    cu_ref[0, CONV_HALO:CONV_HALO + POST_TILE, :] = glu_mid
    cu_ref[0, CONV_HALO + POST_TILE:, :] = glu_next * nv_ref[i].astype(F32)
    n_keep = POST_TILE + 2 * CONV_HALO - 8
    for k in range(1, 8):
        cu_ref[k, 0:n_keep, :] = cu_ref[0, k:k + n_keep, :]

    off = CONV_HALO - CONV_WIDTH // 2
    acc = jnp.zeros((POST_TILE, CONV_DIM), F32) + cb_ref[...]
    for j in range(CONV_WIDTH):
        base, k = divmod(off + j, 8)
        acc = acc + cu_ref[k, 8 * base:8 * base + POST_TILE, :] * cw_ref[j:j + 1, :]
    mu = jnp.mean(acc, axis=-1, keepdims=True)
    cen = acc - mu
    var = jnp.mean(cen * cen, axis=-1, keepdims=True)
    cv = cen * lax.rsqrt(var + EPS) * lg_ref[...] + lb_ref[...]
    y_b = _dot(_silu(cv).astype(BF16), wpc_ref[...])

    y_a = _dot(o_ref[...], wph_ref[...])
    gz = _sigmoid(_dot(h, wgz_ref[...]))
    merged = gz[:, :D_MODEL] * y_a + gz[:, D_MODEL:] * y_b
    x1 = x + gate1 * _dot(merged.astype(BF16), wo_ref[...])
    x1_out[...] = x1
    h2 = _rms_mod(x1, g2_ref[...], scale2, shift2)
    h2_hi = h2.astype(BF16)
    h2_out[...] = h2_hi
    h2_lo = (h2 - h2_hi.astype(F32)).astype(BF16)
    part = _dot_nt(wr_ref[...], h2_hi)
    lg_out[...] = part[:N_EXPERTS] + part[N_EXPERTS:] + _dot_nt(wr_ref[:N_EXPERTS, :], h2_lo)


def _post(pv, nv, x_all, mod_rows, o_all, w_in, layer, w_ph, w_pc, w_o, w_rt, g1, cw, cb, lg, lb, g2):
    n_tok = x_all.shape[0]
    n_tiles = n_tok // POST_TILE
    halo_per_tile = POST_TILE // CONV_HALO
    n_halo_blocks = n_tok // CONV_HALO
    glu_cols, gate_cols = 2 * CONV_DIM, 2 * D_MODEL
    glu_start = 5 * N_HEADS * HEAD_DIM
    assert glu_start % glu_cols == 0 and (glu_start + glu_cols) % gate_cols == 0

    def full(a):
        return pl.BlockSpec(a.shape, lambda i, pv, nv: (0,) * a.ndim)

    def resident(shape, index):
        return pl.BlockSpec(shape, lambda i, pv, nv: index, pipeline_mode=pl.Buffered(1))

    grid_spec = pltpu.PrefetchScalarGridSpec(
        num_scalar_prefetch=2,
        grid=(n_tiles,),
        in_specs=[
            pl.BlockSpec((POST_TILE, D_MODEL), lambda i, pv, nv: (i, 0)),
            pl.BlockSpec((CONV_HALO, D_MODEL), lambda i, pv, nv: (jnp.maximum(i * halo_per_tile - 1, 0), 0)),
            pl.BlockSpec((CONV_HALO, D_MODEL),
                         lambda i, pv, nv: (jnp.minimum((i + 1) * halo_per_tile, n_halo_blocks - 1), 0)),
            pl.BlockSpec((1, 8, D_MODEL), lambda i, pv, nv: (i, 0, 0)),
            pl.BlockSpec((POST_TILE, D_MODEL), lambda i, pv, nv: (i, 0)),
            resident((1, D_MODEL, glu_cols), (layer, 0, glu_start // glu_cols)),
            resident((1, D_MODEL, gate_cols), (layer, 0, (glu_start + glu_cols) // gate_cols)),
            resident(w_ph.shape, (0, 0)), resident(w_pc.shape, (0, 0)), resident(w_o.shape, (0, 0)),
            full(w_rt), full(g1), full(cw), full(cb), full(lg), full(lb), full(g2),
        ],
        out_specs=[
            pl.BlockSpec((POST_TILE, D_MODEL), lambda i, pv, nv: (i, 0)),
            pl.BlockSpec((POST_TILE, D_MODEL), lambda i, pv, nv: (i, 0)),
            pl.BlockSpec((N_EXPERTS, POST_TILE), lambda i, pv, nv: (0, i)),
        ],
        scratch_shapes=[
            pltpu.VMEM((8, POST_TILE + 2 * CONV_HALO, CONV_DIM), F32),
            pltpu.VMEM((D_MODEL, glu_cols), BF16), pltpu.VMEM((D_MODEL, gate_cols), BF16),
            pltpu.VMEM(w_ph.shape, BF16), pltpu.VMEM(w_pc.shape, BF16), pltpu.VMEM(w_o.shape, BF16),
        ],
    )
    return pl.pallas_call(
        _post_kernel,
        grid_spec=grid_spec,
        out_shape=[
            jax.ShapeDtypeStruct((n_tok, D_MODEL), F32),
            jax.ShapeDtypeStruct((n_tok, D_MODEL), BF16),
            jax.ShapeDtypeStruct((N_EXPERTS, n_tok), F32),
        ],
        compiler_params=pltpu.CompilerParams(
            dimension_semantics=("arbitrary",), vmem_limit_bytes=VMEM_LIMIT),
        name="post_mixer",
    )(pv, nv, x_all, x_all, x_all, mod_rows, o_all, w_in, w_in, w_ph, w_pc, w_o, w_rt, g1, cw, cb, lg, lb, g2)


def _route_kernel(lg_ref, bias_ref, comb_out, chosen_out):
    n = lg_ref.shape[1]
    scores = _sigmoid(lg_ref[...])
    sel = scores + bias_ref[...]
    neg = jnp.float32(-jnp.inf)

    sel3 = sel.reshape(N_GROUPS, GROUP_SIZE, n)
    m1 = jnp.max(sel3, axis=1, keepdims=True)
    is_m1 = sel3 == m1
    n_m1 = jnp.sum(is_m1.astype(F32), axis=1, keepdims=True)
    m2 = jnp.max(jnp.where(is_m1, neg, sel3), axis=1, keepdims=True)
    grp = (m1 + jnp.where(n_m1 > 1.5, m1, m2)).reshape(N_GROUPS, n)

    gidx = lax.broadcasted_iota(jnp.int32, (N_GROUPS, n), 0)
    rank = jnp.zeros((N_GROUPS, n), F32)
    for g in range(N_GROUPS):
        other = grp[g:g + 1, :]
        ahead = (other > grp) | ((other == grp) & (g < gidx))
        rank = rank + ahead.astype(F32)
    keep_g = rank < TOPK_GROUPS - 0.5
    keep = jnp.broadcast_to(keep_g.reshape(N_GROUPS, 1, n), (N_GROUPS, GROUP_SIZE, n)).reshape(N_EXPERTS, n)
    cand = jnp.where(keep, sel, neg)

    eidx = lax.broadcasted_iota(jnp.int32, (N_EXPERTS, n), 0)
    chosen = jnp.zeros((N_EXPERTS, n), F32)
    for _ in range(TOP_K):
        best = jnp.max(cand, axis=0, keepdims=True)
        first = jnp.min(jnp.where(cand == best, eidx, N_EXPERTS), axis=0, keepdims=True)
        hit = eidx == first
        chosen = jnp.where(hit, 1.0, chosen)
        cand = jnp.where(hit, neg, cand)
    w = scores * chosen
    comb_out[...] = w / jnp.sum(w, axis=0, keepdims=True) * ROUTED_SCALE
    chosen_out[...] = chosen


def _route(logits_t, bias):
    n_tok = logits_t.shape[1]
    tile = 512
    return pl.pallas_call(
        _route_kernel,
        grid=(n_tok // tile,),
        in_specs=[
            pl.BlockSpec((N_EXPERTS, tile), lambda i: (0, i)),
            pl.BlockSpec((N_EXPERTS, 1), lambda i: (0, 0)),
        ],
        out_specs=[pl.BlockSpec((N_EXPERTS, tile), lambda i: (0, i)),
                   pl.BlockSpec((N_EXPERTS, tile), lambda i: (0, i))],
        out_shape=[jax.ShapeDtypeStruct((N_EXPERTS, n_tok), F32),
                   jax.ShapeDtypeStruct((N_EXPERTS, n_tok), F32)],
        name="router",
    )(logits_t, bias)


def _count_le(bounds, idx):
    return jnp.sum(jnp.where(bounds <= idx, 1.0, 0.0), axis=0, keepdims=True)


def _pick_row(onehot, col):
    return jnp.sum(jnp.where(onehot, col, 0.0), axis=0, keepdims=True)


def _plan_kernel(ch_ref, rankt_out, es_out, rs_out, src_out, et_out, ysrc_out, misc_out):
    nb = N_TOKENS // MOE_BLOCK
    n_src = N_TILES_MAX * UNITS_PER_TILE
    e_col = lax.broadcasted_iota(jnp.int32, (N_EXPERTS, 1), 0).astype(F32)
    blk_lane = lax.broadcasted_iota(jnp.int32, (N_EXPERTS, 128), 1)
    t_r = lax.broadcasted_iota(jnp.int32, (MOE_BLOCK, MOE_BLOCK), 0)
    t_c = lax.broadcasted_iota(jnp.int32, (MOE_BLOCK, MOE_BLOCK), 1)
    earlier = jnp.where(t_c < t_r, 1.0, 0.0).astype(BF16)
    e_r = lax.broadcasted_iota(jnp.int32, (N_EXPERTS, N_EXPERTS), 0)
    e_c = lax.broadcasted_iota(jnp.int32, (N_EXPERTS, N_EXPERTS), 1)
    upto = jnp.where(e_c <= e_r, 1.0, 0.0).astype(F32)
    slot = lax.broadcasted_iota(jnp.int32, (1, SLOT_MAX), 1).astype(F32)

    def col_cumsum(col):
        wide = jnp.broadcast_to(col, (N_EXPERTS, 128))
        return jnp.dot(upto, wide, preferred_element_type=F32, precision=HIGHEST)[:, 0:1]

    def block_stats(b, carry):
        units_mat, start_mat = carry
        cb = ch_ref[:, pl.ds(pl.multiple_of(b * MOE_BLOCK, MOE_BLOCK), MOE_BLOCK)]
        rank_t = _dot_nt(earlier, cb.astype(BF16))
        cb_t = cb.T
        rankt_out[b] = jnp.where(cb_t > 0, rank_t, -1.0).astype(BF16)
        cnt = jnp.sum(cb, axis=1, keepdims=True)
        units = jnp.floor((cnt + (UNIT - 1)) * (1.0 / UNIT))
        incl = col_cumsum(units)
        start = incl - units
        e_slot = _count_le(incl * UNIT, slot)
        onehot = e_col == e_slot
        r = slot - _pick_row(onehot, start * UNIT)
        valid = (e_slot < N_EXPERTS - 0.5) & (r < _pick_row(onehot, cnt))
        es_out[b] = jnp.where(valid, e_slot, -1.0).astype(jnp.int32)
        rs_out[b] = jnp.where(valid, r, -2.0)
        units_mat = jnp.where(blk_lane == b, units, units_mat)
        start_mat = jnp.where(blk_lane == b, start, start_mat)
        return units_mat, start_mat

    zeros = jnp.zeros((N_EXPERTS, 128), F32)
    units_mat, start_mat = lax.fori_loop(0, nb, block_stats, (zeros, zeros))

    tot = jnp.sum(units_mat, axis=1, keepdims=True)
    tiles_e = jnp.floor((tot + (UNITS_PER_TILE - 1)) * (1.0 / UNITS_PER_TILE))
    incl_t = col_cumsum(tiles_e)
    start_t = incl_t - tiles_e
    n_used = incl_t[N_EXPERTS - 1:N_EXPERTS, :]
    b_r = lax.broadcasted_iota(jnp.int32, (128, 128), 0)
    b_c = lax.broadcasted_iota(jnp.int32, (128, 128), 1)
    before = jnp.where(b_r < b_c, 1.0, 0.0).astype(F32)
    cum_b = jnp.dot(units_mat, before, preferred_element_type=F32, precision=HIGHEST)
    run_pos = start_t * UNITS_PER_TILE + cum_b

    tile_idx = lax.broadcasted_iota(jnp.int32, (1, 512), 1).astype(F32)
    et_out[...] = jnp.minimum(_count_le(incl_t, tile_idx), N_EXPERTS - 1.0).astype(jnp.int32)

    def split_rows(m):
        hi = m.astype(BF16)
        return [hi, (m - hi.astype(F32)).astype(BF16)]

    tables = jnp.concatenate(split_rows((cum_b + units_mat).T) + split_rows(cum_b.T) + split_rows(start_mat.T),
                             axis=0)
    b_col = lax.broadcasted_iota(jnp.int32, (128, 1), 0)
    chunk = 1024

    def src_chunk(c, carry):
        off = pl.multiple_of(c * chunk, chunk)
        p = (lax.broadcasted_iota(jnp.int32, (1, chunk), 1) + off).astype(F32)
        e_p = jnp.minimum(_count_le(incl_t * UNITS_PER_TILE, p), N_EXPERTS - 1.0)
        onehot_e = e_col == e_p
        q = p - _pick_row(onehot_e, start_t * UNITS_PER_TILE)
        valid = (p < n_used * UNITS_PER_TILE) & (q < _pick_row(onehot_e, tot))
        picked = _dot(tables, jnp.where(onehot_e, 1.0, 0.0).astype(BF16))
        incl_sel = picked[0:128] + picked[128:256]
        cum_sel = picked[256:384] + picked[384:512]
        start_sel = picked[512:640] + picked[640:768]
        b_p = jnp.sum(jnp.where((incl_sel <= q) & (b_col < nb), 1.0, 0.0), axis=0, keepdims=True)
        b_p = jnp.minimum(b_p, nb - 1.0)
        onehot_b = b_col.astype(F32) == b_p
        unit = b_p * UNITS_PER_BLOCK + _pick_row(onehot_b, start_sel) + q - _pick_row(onehot_b, cum_sel)
        src_out[:, pl.ds(off, chunk)] = jnp.where(valid, unit, 0.0).astype(jnp.int32)
        return carry

    lax.fori_loop(0, n_src // chunk, src_chunk, 0)

    unit_idx = lax.broadcasted_iota(jnp.int32, (1, 256), 1).astype(F32)
    misc_lane = lax.broadcasted_iota(jnp.int32, (1, 128), 1)

    def block_units(b, used):
        here = blk_lane == b
        units = jnp.sum(jnp.where(here, units_mat, 0.0), axis=1, keepdims=True)
        start = jnp.sum(jnp.where(here, start_mat, 0.0), axis=1, keepdims=True)
        pos = jnp.sum(jnp.where(here, run_pos, 0.0), axis=1, keepdims=True)
        incl = start + units
        e_unit = _count_le(incl, unit_idx)
        onehot = e_col == e_unit
        val = _pick_row(onehot, pos - start) + unit_idx
        ysrc_out[b] = jnp.where(e_unit < N_EXPERTS - 0.5, val, 0.0).astype(jnp.int32)
        return jnp.where(misc_lane == b, incl[N_EXPERTS - 1:N_EXPERTS, :], used)

    used = lax.fori_loop(0, nb, block_units, jnp.zeros((1, 128), F32))
    misc_out[0:1, :] = used.astype(jnp.int32)
    misc_out[1:2, :] = jnp.broadcast_to(n_used, (1, 128)).astype(jnp.int32)
    misc_out[2:8, :] = jnp.zeros((6, 128), jnp.int32)


def _routing_plan(chosen):
    nb = N_TOKENS // MOE_BLOCK
    n_src = N_TILES_MAX * UNITS_PER_TILE
    assert n_src % 1024 == 0 and N_TILES_MAX <= 512 and nb <= 128 and UNITS_PER_BLOCK <= 256
    rank_bt, e_slot, r_slot, src, e_tile, ysrc, misc = pl.pallas_call(
        _plan_kernel,
        out_shape=[
            jax.ShapeDtypeStruct((nb, MOE_BLOCK, N_EXPERTS), BF16),
            jax.ShapeDtypeStruct((nb, 1, SLOT_MAX), jnp.int32),
            jax.ShapeDtypeStruct((nb, 1, SLOT_MAX), F32),
            jax.ShapeDtypeStruct((1, n_src), jnp.int32),
            jax.ShapeDtypeStruct((1, 512), jnp.int32),
            jax.ShapeDtypeStruct((nb, 1, 256), jnp.int32),
            jax.ShapeDtypeStruct((8, 128), jnp.int32),
        ],
        compiler_params=pltpu.CompilerParams(vmem_limit_bytes=VMEM_LIMIT),
        name="moe_plan",
    )(chosen)
    return dict(rank_bt=rank_bt, e_slot=e_slot, r_slot=r_slot, src=src.reshape(-1),
                e_tile=e_tile[0, :N_TILES_MAX], n_used=misc[1, :1],
                ysrc=ysrc[:, 0, :UNITS_PER_BLOCK].reshape(-1))


def _permute_kernel(h_ref, rank_ref, es_ref, rs_ref, xp_out):
    sub_e = lax.broadcasted_iota(jnp.int32, (N_EXPERTS, GROUP_SLOTS // 2), 0)
    tiles = [slice(i * GROUP_SLOTS // 2, (i + 1) * GROUP_SLOTS // 2) for i in range(2 * GROUPS_PER_BLOCK)]
    picks = []
    for rows in tiles:
        onehot_e = jnp.where(sub_e == es_ref[0, :, rows], 1.0, 0.0).astype(BF16)
        slot_rank = _dot(rank_ref[0], onehot_e)
        picks.append(jnp.where(slot_rank == rs_ref[0, :, rows], 1.0, 0.0).astype(BF16))
    for rows, pick in zip(tiles, picks):
        xp_out[0, rows, :] = _dot_tn(pick, h_ref[...]).astype(BF16)


def _permute(plan, h2):
    nb = h2.shape[0] // MOE_BLOCK
    return pl.pallas_call(
        _permute_kernel,
        grid=(nb,),
        in_specs=[
            pl.BlockSpec((MOE_BLOCK, D_MODEL), lambda b: (b, 0)),
            pl.BlockSpec((1, MOE_BLOCK, N_EXPERTS), lambda b: (b, 0, 0)),
            pl.BlockSpec((1, 1, SLOT_MAX), lambda b: (b, 0, 0)),
            pl.BlockSpec((1, 1, SLOT_MAX), lambda b: (b, 0, 0)),
        ],
        out_specs=pl.BlockSpec((1, SLOT_MAX, D_MODEL), lambda b: (b, 0, 0)),
        out_shape=jax.ShapeDtypeStruct((nb, SLOT_MAX, D_MODEL), BF16),
        compiler_params=pltpu.CompilerParams(
            dimension_semantics=("arbitrary",), vmem_limit_bytes=VMEM_LIMIT),
        name="moe_permute",
    )(h2, plan["rank_bt"], plan["e_slot"], plan["r_slot"])


def _unit_copies(table_ref, base, src_hbm, dst_ref, slot, sem, n_units, start):
    for u in range(n_units):
        copy = pltpu.make_async_copy(src_hbm.at[table_ref[base + u]],
                                     dst_ref.at[slot, pl.ds(u * UNIT, UNIT), :], sem.at[slot])
        if start:
            copy.start()
        else:
            copy.wait()


def _ffn_kernel(src_ref, et_ref, nu_ref, xp_hbm, *refs):
    w_refs = refs[:2 * TILES_PER_STEP]
    y_out = refs[2 * TILES_PER_STEP]
    scratch = refs[2 * TILES_PER_STEP + 1:]
    wb_refs, xt_ref, sem = scratch[:2 * TILES_PER_STEP], scratch[2 * TILES_PER_STEP], scratch[2 * TILES_PER_STEP + 1]
    n_units = TILES_PER_STEP * UNITS_PER_TILE
    step = pl.program_id(0)
    slot = lax.rem(step, 2)
    n_used = nu_ref[0]

    def copies(s, sl, start):
        _unit_copies(src_ref, s * n_units, xp_hbm, xt_ref, sl, sem, n_units, start)

    @pl.when((step == 0) & (n_used > 0))
    def _():
        copies(0, 0, True)

    @pl.when((step + 1 < pl.num_programs(0)) & ((step + 1) * TILES_PER_STEP < n_used))
    def _():
        copies(step + 1, 1 - slot, True)

    @pl.when(step * TILES_PER_STEP < n_used)
    def _():
        for t in range(TILES_PER_STEP):
            j = step * TILES_PER_STEP + t

            @pl.when((step == 0) | (et_ref[j] != et_ref[jnp.maximum(j - TILES_PER_STEP, 0)]))
            def _():
                wb_refs[2 * t][...] = w_refs[2 * t][0, 0].astype(BF16)
                wb_refs[2 * t + 1][...] = w_refs[2 * t + 1][0, 0].astype(BF16)

        copies(step, slot, False)
        tiles = [slice(t * FFN_TILE, (t + 1) * FFN_TILE) for t in range(TILES_PER_STEP)]
        gus = [_dot(xt_ref[slot, rows, :], wb_refs[2 * t][...]) for t, rows in enumerate(tiles)]
        acts = [(_silu(gu[:, :EXPERT_FF]) * gu[:, EXPERT_FF:]).astype(BF16) for gu in gus]
        for t, (rows, act) in enumerate(zip(tiles, acts)):
            y_out[rows, :] = _dot(act, wb_refs[2 * t + 1][...]).astype(BF16)

    @pl.when(step * TILES_PER_STEP >= n_used)
    def _():
        y_out[...] = jnp.zeros(y_out.shape, BF16)


def _expert_ffn(plan, xp, w_gu, w_d, layer):
    xp_units = xp.reshape(-1, UNIT, D_MODEL)
    assert N_TILES_MAX % TILES_PER_STEP == 0

    def weight_specs(t):
        return [pl.BlockSpec((1, 1, D_MODEL, 2 * EXPERT_FF),
                             lambda s, src, et, nu, t=t: (layer, et[s * TILES_PER_STEP + t], 0, 0)),
                pl.BlockSpec((1, 1, EXPERT_FF, D_MODEL),
                             lambda s, src, et, nu, t=t: (layer, et[s * TILES_PER_STEP + t], 0, 0))]

    grid_spec = pltpu.PrefetchScalarGridSpec(
        num_scalar_prefetch=3,
        grid=(N_TILES_MAX // TILES_PER_STEP,),
        in_specs=[pl.BlockSpec(memory_space=pl.ANY)]
        + [spec for t in range(TILES_PER_STEP) for spec in weight_specs(t)],
        out_specs=pl.BlockSpec((TILES_PER_STEP * FFN_TILE, D_MODEL), lambda s, src, et, nu: (s, 0)),
        scratch_shapes=[
            pltpu.VMEM((D_MODEL, 2 * EXPERT_FF), BF16) if i % 2 == 0 else pltpu.VMEM((EXPERT_FF, D_MODEL), BF16)
            for i in range(2 * TILES_PER_STEP)
        ] + [pltpu.VMEM((2, TILES_PER_STEP * FFN_TILE, D_MODEL), BF16), pltpu.SemaphoreType.DMA((2,))],
    )
    return pl.pallas_call(
        _ffn_kernel,
        grid_spec=grid_spec,
        out_shape=jax.ShapeDtypeStruct((N_TILES_MAX * FFN_TILE, D_MODEL), BF16),
        compiler_params=pltpu.CompilerParams(
            dimension_semantics=("arbitrary",), vmem_limit_bytes=VMEM_LIMIT),
        name="moe_expert_ffn",
    )(plan["src"], plan["e_tile"], plan["n_used"], xp_units, *([w_gu, w_d] * TILES_PER_STEP))


def _combine_kernel(ysrc_ref, y_hbm, rank_ref, comb_ref, er_ref, rr_ref, h_ref, x1_ref, mod_ref,
                    wsgu_ref, wsd_ref, fg_ref, o_ref, yt_ref, sem, *, final):
    b = pl.program_id(0)
    slot = lax.rem(b, 2)

    def copies(bb, sl, start):
        _unit_copies(ysrc_ref, bb * UNITS_PER_BLOCK, y_hbm, yt_ref, sl, sem, UNITS_PER_BLOCK, start)

    @pl.when(b == 0)
    def _():
        copies(0, 0, True)

    @pl.when(b + 1 < pl.num_programs(0))
    def _():
        copies(b + 1, 1 - slot, True)

    gu = _dot(h_ref[...], wsgu_ref[...])
    shared = _dot((_silu(gu[:, :EXPERT_FF]) * gu[:, EXPERT_FF:]).astype(BF16), wsd_ref[...])

    sub_e = lax.broadcasted_iota(jnp.int32, (N_EXPERTS, GROUP_SLOTS), 0)
    groups = [slice(g * GROUP_SLOTS, (g + 1) * GROUP_SLOTS) for g in range(GROUPS_PER_BLOCK)]
    onehots = [jnp.where(sub_e == er_ref[0, :, cols], 1.0, 0.0).astype(BF16) for cols in groups]
    slot_ranks = [_dot(rank_ref[0], onehot_e) for onehot_e in onehots]
    slot_ws = [_dot(comb_ref[0], onehot_e) for onehot_e in onehots]
    weights = jnp.concatenate(
        [jnp.where(slot_rank == rr_ref[0, :, cols], slot_w, 0.0).astype(BF16)
         for cols, slot_rank, slot_w in zip(groups, slot_ranks, slot_ws)], axis=1)

    copies(b, slot, False)
    x2 = x1_ref[...] + mod_ref[0, 5:6, :] * (shared + _dot(weights, yt_ref[slot]))
    if final:
        ms = jnp.mean(x2 * x2, axis=-1, keepdims=True)
        x2 = x2 * lax.rsqrt(ms + EPS) * fg_ref[...]
    o_ref[...] = x2


def _combine(plan, y_sorted, comb_bt, h2, x1, mod_rows, ws_gu, ws_d, final_g, final):
    n_tok = h2.shape[0]
    nb = n_tok // MOE_BLOCK
    y_units = y_sorted.reshape(-1, UNIT, D_MODEL)

    def full(a):
        return pl.BlockSpec(a.shape, lambda b, ysrc: (0,) * a.ndim)

    grid_spec = pltpu.PrefetchScalarGridSpec(
        num_scalar_prefetch=1,
        grid=(nb,),
        in_specs=[
            pl.BlockSpec(memory_space=pl.ANY),
            pl.BlockSpec((1, MOE_BLOCK, N_EXPERTS), lambda b, ysrc: (b, 0, 0)),
            pl.BlockSpec((1, MOE_BLOCK, N_EXPERTS), lambda b, ysrc: (b, 0, 0)),
            pl.BlockSpec((1, 1, SLOT_MAX), lambda b, ysrc: (b, 0, 0)),
            pl.BlockSpec((1, 1, SLOT_MAX), lambda b, ysrc: (b, 0, 0)),
            pl.BlockSpec((MOE_BLOCK, D_MODEL), lambda b, ysrc: (b, 0)),
            pl.BlockSpec((MOE_BLOCK, D_MODEL), lambda b, ysrc: (b, 0)),
            pl.BlockSpec((1, 8, D_MODEL), lambda b, ysrc: (b, 0, 0)),
            full(ws_gu), full(ws_d), full(final_g),
        ],
        out_specs=pl.BlockSpec((MOE_BLOCK, D_MODEL), lambda b, ysrc: (b, 0)),
        scratch_shapes=[
            pltpu.VMEM((2, SLOT_MAX, D_MODEL), BF16),
            pltpu.SemaphoreType.DMA((2,)),
        ],
    )
    return pl.pallas_call(
        functools.partial(_combine_kernel, final=final),
        grid_spec=grid_spec,
        out_shape=jax.ShapeDtypeStruct((n_tok, D_MODEL), F32),
        compiler_params=pltpu.CompilerParams(
            dimension_semantics=("arbitrary",), vmem_limit_bytes=VMEM_LIMIT),
        name="moe_combine",
    )(plan["ysrc"], y_units, plan["rank_bt"], comb_bt,
      plan["e_slot"], plan["r_slot"], h2, x1, mod_rows, ws_gu, ws_d, final_g)


def _split_bf16(w):
    hi = w.astype(BF16)
    return jnp.concatenate([hi, (w - hi.astype(F32)).astype(BF16)], axis=0)


def _level_ids():
    t = np.arange(CHUNK)
    x = t[:, None] ^ t[None, :]
    lvl = np.zeros((CHUNK, CHUNK), np.int32)
    nz = x > 0
    lvl[nz] = np.floor(np.log2(x[nz])).astype(np.int32) + 1
    past_f = t[:, None] >= t[None, :]
    return jnp.asarray(np.where(past_f, lvl, -1)), jnp.asarray(np.where(past_f.T, lvl, -1))


def kernel(x_prompt, x_sample, state_hgrn, c, c_ctx, w_ada, b_ada, norm1_g, w_in, hgrn_lb_logits, hgrn_norm_g, w_proj_hgrn, conv_dw_w, conv_dw_b, conv_norm_g, conv_norm_b, w_proj_conv, w_out, norm2_g, w_router, router_bias, w_expert_gate_up, w_expert_down, w_shared_gate_up, w_shared_down, final_norm_g):
    n_ctx, ctx_len, d = x_prompt.shape
    n_lat, lat_len, _ = x_sample.shape
    depth = w_ada.shape[0]
    assert d == D_MODEL and ctx_len * 4 == STEP and lat_len == STEP and n_ctx % 4 == 0
    ctx_steps = n_ctx * ctx_len // STEP
    n_steps = ctx_steps + n_lat
    n_tok = n_steps * STEP
    assert n_tok == N_TOKENS and STEP % POST_TILE == 0 and STEP % MOE_BLOCK == 0
    tiles_per_step = STEP // POST_TILE

    x_all = jnp.concatenate([x_prompt.reshape(-1, d), x_sample.reshape(-1, d)], axis=0)

    cc = jnp.zeros((8, d), F32).at[:n_lat].set(c.astype(F32)).at[n_lat].set(c_ctx.astype(F32))
    mod = _ada(cc, w_ada, b_ada).reshape(depth, 8, N_MOD, d)
    step_src = np.array([n_lat] * ctx_steps + list(range(n_lat)))
    mod_step = jnp.pad(mod[:, step_src], ((0, 0), (0, 0), (0, 8 - N_MOD), (0, 0)))
    mod_post = jnp.repeat(mod_step, STEP // POST_TILE, axis=1)
    mod_block = jnp.repeat(mod_step, STEP // MOE_BLOCK, axis=1)

    nc = jnp.asarray([ctx_len // CHUNK] * ctx_steps + [lat_len // CHUNK] * n_lat, jnp.int32)
    tile_pos = np.arange(n_lat * tiles_per_step) % tiles_per_step
    pv = jnp.asarray(np.concatenate([np.zeros(ctx_steps * tiles_per_step), tile_pos > 0]), jnp.int32)
    nv = jnp.asarray(np.concatenate([np.zeros(ctx_steps * tiles_per_step), tile_pos < tiles_per_step - 1]), jnp.int32)

    lbv = jnp.cumsum(jax.nn.softmax(hgrn_lb_logits.astype(F32), axis=0), axis=0)
    lbv = (lbv - lbv[:1]).reshape(depth, 2, N_HEADS, HEAD_DIM)
    lvl_f, lvl_b = _level_ids()

    xs = x_all
    ctx_states = []
    for l in range(depth):
        lb = lbv[l]
        gp = jnp.stack([jnp.log(lb[0]), jnp.log1p(-lb[0]), 1.0 - lb[0],
                        jnp.log(lb[1]), jnp.log1p(-lb[1]), 1.0 - lb[1],
                        hgrn_norm_g[l].reshape(N_HEADS, HEAD_DIM).astype(F32),
                        jnp.zeros((N_HEADS, HEAD_DIM), F32)], axis=1)
        s0 = jnp.concatenate([jnp.zeros((ctx_steps, 2, N_HEADS, HEAD_DIM, HEAD_DIM), F32),
                              state_hgrn[:, l].astype(F32)], axis=0)
        g1 = norm1_g[l].reshape(1, d).astype(F32)

        o_all, states = _scan(nc, xs, mod_step[l], g1, w_in, l, gp, s0, lvl_f, lvl_b)
        ctx_states.append(states[:ctx_steps].reshape(n_ctx, 2, N_HEADS, HEAD_DIM, HEAD_DIM))

        cw = jnp.pad(conv_dw_w[l].astype(F32), ((0, 1), (0, 0)))
        x1, h2, logits_t = _post(
            pv, nv, xs, mod_post[l], o_all, w_in, l, w_proj_hgrn[l], w_proj_conv[l], w_out[l],
            _split_bf16(w_router[l].T.astype(F32)), g1, cw,
            conv_dw_b[l].reshape(1, -1).astype(F32), conv_norm_g[l].reshape(1, -1).astype(F32),
            conv_norm_b[l].reshape(1, -1).astype(F32), norm2_g[l].reshape(1, d).astype(F32))

        comb_t, chosen_t = _route(logits_t, router_bias[l].reshape(N_EXPERTS, 1).astype(F32))
        plan = _routing_plan(chosen_t)
        comb_bt = comb_t.reshape(N_EXPERTS, n_tok // MOE_BLOCK, MOE_BLOCK).transpose(1, 2, 0).astype(BF16)
        xp = _permute(plan, h2)
        y_sorted = _expert_ffn(plan, xp, w_expert_gate_up, w_expert_down, l)
        xs = _combine(plan, y_sorted, comb_bt, h2, x1, mod_block[l], w_shared_gate_up[l].astype(BF16),
                      w_shared_down[l].astype(BF16), final_norm_g.reshape(1, d).astype(F32),
                      final=(l == depth - 1))

    n_ctx_tok = n_ctx * ctx_len
    y_prompt = xs[:n_ctx_tok].reshape(x_prompt.shape).astype(x_prompt.dtype)
    y_sample = xs[n_ctx_tok:].reshape(x_sample.shape).astype(x_sample.dtype)
    new_state = jnp.stack(ctx_states, axis=1).astype(x_prompt.dtype)
    return (y_prompt, y_sample, new_state)
```
